```python
import jax, jax.numpy as jnp
from jax import lax
import numpy as np

D_MODEL = 1024
BATCH = 4
SEQ = 8192
DEPTH = 2

D_MIX = D_MODEL
LRU_WIDTH = D_MIX // 4
LRU_BLOCKS = 4
LRU_BLOCK_W = LRU_WIDTH // LRU_BLOCKS
CONV_WIDTH = 4
LRU_C = 8.0
MLA_HEADS = 6
MLA_NOPE = 64
MLA_ROPE = 32
MLA_V = 64
MLA_Q_RANK = 192
MLA_KV_RANK = 128
NSA_HEADS = 6
NSA_KV_HEADS = 2
NSA_GROUP = NSA_HEADS // NSA_KV_HEADS
NSA_HEAD_DIM = 64
NSA_KV = NSA_KV_HEADS * NSA_HEAD_DIM
CMP_LEN = 32
CMP_STRIDE = 16
CMP_HIDDEN = 128
SEL_LEN = 64
SEL_TOPK = 16
WINDOW = 512
FORCE_BONUS = 1.0e3
Q_BLOCK = 128
D_FF = 2816
N_EXPERTS = 8
TOP_K = 2
D_FF_EXPERT = 1408
N_DENSE = (DEPTH + 1) // 2
N_MOE = DEPTH // 2
ROPE_THETA = 10000.0
EPS = 1e-6
NEG_INF = -1.0e30
IN_SPLITS = (LRU_WIDTH, LRU_WIDTH,
             MLA_Q_RANK, MLA_KV_RANK, MLA_ROPE,
             NSA_HEADS * NSA_HEAD_DIM, NSA_KV, NSA_KV, NSA_KV, NSA_KV, NSA_KV, NSA_KV, 3 * NSA_HEADS)
IN_COLS = sum(IN_SPLITS)

kernel_name = 'hymba_style_lru_mla_nsa_moe_block'


def rmsnorm(x, g):
    xf = x.astype(jnp.float32)
    y = xf * lax.rsqrt(jnp.mean(xf * xf, axis=-1, keepdims=True) + EPS)
    return (y * g.astype(jnp.float32)).astype(x.dtype)


def masked_softmax(s, mask):
    s = jnp.where(mask, s, NEG_INF)
    m = jnp.max(s, axis=-1, keepdims=True)
    e = jnp.where(mask, jnp.exp(s - m), 0.0)
    return e / jnp.maximum(jnp.sum(e, axis=-1, keepdims=True), 1e-30)


def rope(x, pos):
    d = x.shape[-1]
    half = d // 2
    inv = ROPE_THETA ** (-jnp.arange(half, dtype=jnp.float32) * 2.0 / d)
    ang = pos.astype(jnp.float32)[:, None] * inv[None, :]
    cos = jnp.cos(ang)[None, :, None, :].astype(x.dtype)
    sin = jnp.sin(ang)[None, :, None, :].astype(x.dtype)
    x1, x2 = x[..., :half], x[..., half:]
    return jnp.concatenate([x1 * cos - x2 * sin, x2 * cos + x1 * sin], axis=-1)


def causal_conv(x, w, b):
    S = x.shape[1]
    xp = jnp.pad(x, ((0, 0), (CONV_WIDTH - 1, 0), (0, 0)))
    out = b
    for k in range(CONV_WIDTH):
        out = out + xp[:, k:k + S] * w[k]
    return out


def block_diag(x, w, b):
    B, S, _ = x.shape
    y = jnp.einsum('bsnc,ncd->bsnd', x.reshape(B, S, LRU_BLOCKS, LRU_BLOCK_W), w)
    return y.reshape(B, S, LRU_WIDTH) + b


def rg_lru(x, wa, ba, wx, bx, lam):
    r = jax.nn.sigmoid(block_diag(x, wa, ba)).astype(jnp.float32)
    i = jax.nn.sigmoid(block_diag(x, wx, bx)).astype(jnp.float32)
    log_a = -LRU_C * r * jax.nn.softplus(-lam.astype(jnp.float32))
    a = jnp.exp(log_a)
    b = jnp.sqrt(-jnp.expm1(2.0 * log_a)) * (i * x.astype(jnp.float32))

    def combine(lhs, rhs):
        a1, b1 = lhs
        a2, b2 = rhs
        return a1 * a2, a2 * b1 + b2

    _, h = lax.associative_scan(combine, (a, b), axis=1)
    return h.astype(x.dtype)


def recurrent_group(x_in, y_in, conv_w, conv_b, wa, ba, wx, bx, lam):
    xc = causal_conv(x_in, conv_w, conv_b)
    return rg_lru(xc, wa, ba, wx, bx, lam) * jax.nn.gelu(y_in)


def causal_block_attention(q, k, v, scale):
    B, S, H, Dk = q.shape
    Dv = v.shape[-1]
    nb = S // Q_BLOCK
    qb = q.reshape(B, nb, Q_BLOCK, H, Dk).transpose(1, 0, 2, 3, 4)
    kpos = jnp.arange(S)

    def one(args):
        qi, bidx = args
        qpos = bidx * Q_BLOCK + jnp.arange(Q_BLOCK)
        s = jnp.einsum('bqhd,bkhd->bhqk', qi, k).astype(jnp.float32) * scale
        p = masked_softmax(s, (kpos[None, :] <= qpos[:, None])[None, None])
        return jnp.einsum('bhqk,bkhd->bqhd', p.astype(v.dtype), v)

    o = lax.map(one, (qb, jnp.arange(nb)))
    return o.transpose(1, 0, 2, 3, 4).reshape(B, S, H * Dv)


def mla_group(q_dn, kv_dn, k_pe, pos, q_norm, w_uq, kv_norm, w_ukv):
    B, S, _ = q_dn.shape
    q = (rmsnorm(q_dn, q_norm) @ w_uq).reshape(B, S, MLA_HEADS, MLA_NOPE + MLA_ROPE)
    q = jnp.concatenate([q[..., :MLA_NOPE], rope(q[..., MLA_NOPE:], pos)], axis=-1)
    kv = (rmsnorm(kv_dn, kv_norm) @ w_ukv).reshape(B, S, MLA_HEADS, MLA_NOPE + MLA_V)
    k_rope = jnp.broadcast_to(rope(k_pe[:, :, None, :], pos), (B, S, MLA_HEADS, MLA_ROPE))
    k = jnp.concatenate([kv[..., :MLA_NOPE], k_rope], axis=-1)
    v = kv[..., MLA_NOPE:]
    return causal_block_attention(q, k, v, (MLA_NOPE + MLA_ROPE) ** -0.5)


def compress(kv, pos_emb, w1, w2):
    B, S = kv.shape[0], kv.shape[1]
    nc = (S - CMP_LEN) // CMP_STRIDE + 1
    idx = jnp.arange(nc)[:, None] * CMP_STRIDE + jnp.arange(CMP_LEN)[None, :]
    win = kv[:, idx] + pos_emb[None, None, :, None, :]
    flat = win.transpose(0, 1, 3, 2, 4).reshape(B, nc, NSA_KV_HEADS, CMP_LEN * NSA_HEAD_DIM)
    return jax.nn.gelu(flat @ w1) @ w2


def nsa_group(q_in, kc_in, vc_in, ks_in, vs_in, kw_in, vw_in, g_in, gate_b, pos, cmp_pos, cmp_w1, cmp_w2):
    B, S, _ = q_in.shape
    G, P, D = NSA_KV_HEADS, NSA_GROUP, NSA_HEAD_DIM
    q = rope(q_in.reshape(B, S, NSA_HEADS, D), pos)
    kc = compress(rope(kc_in.reshape(B, S, G, D), pos), cmp_pos[0], cmp_w1[0], cmp_w2[0])
    vc = compress(vc_in.reshape(B, S, G, D), cmp_pos[1], cmp_w1[1], cmp_w2[1])
    nc = kc.shape[1]
    ns = S // SEL_LEN
    n_top = min(SEL_TOPK, ns)
    ks_blk = rope(ks_in.reshape(B, S, G, D), pos).reshape(B, ns, SEL_LEN, G, D).transpose(0, 3, 1, 2, 4)
    vs_blk = vs_in.reshape(B, ns, SEL_LEN, G, D).transpose(0, 3, 1, 2, 4)
    pad = ((0, 0), (WINDOW, 0), (0, 0), (0, 0))
    kw_p = jnp.pad(rope(kw_in.reshape(B, S, G, D), pos), pad)
    vw_p = jnp.pad(vw_in.reshape(B, S, G, D), pad)
    gates = jax.nn.sigmoid(g_in + gate_b)

    cmp_start = jnp.arange(nc) * CMP_STRIDE
    cmp_end = cmp_start + CMP_LEN - 1
    sel_start = jnp.arange(ns) * SEL_LEN
    overlap = ((cmp_start[:, None] < sel_start[None, :] + SEL_LEN)
               & (cmp_end[:, None] >= sel_start[None, :])).astype(jnp.float32)
    scale = D ** -0.5
    nb = S // Q_BLOCK
    qb = q.reshape(B, nb, Q_BLOCK, G, P, D).transpose(1, 0, 2, 3, 4, 5)
    gb = gates.reshape(B, nb, Q_BLOCK, G, P, 3).transpose(1, 0, 2, 3, 4, 5)
    b_ix = jnp.arange(B)[:, None, None, None]
    g_ix = jnp.arange(G)[None, :, None, None]

    def one(args):
        qi, gt, bidx = args
        dt = qi.dtype
        qpos = bidx * Q_BLOCK + jnp.arange(Q_BLOCK)
        s_c = jnp.einsum('bqgpd,bngd->bgpqn', qi, kc).astype(jnp.float32) * scale
        p_c = masked_softmax(s_c, cmp_end[None, :] <= qpos[:, None])
        o_c = jnp.einsum('bgpqn,bngd->bqgpd', p_c.astype(dt), vc)
        imp = jnp.einsum('bgpqn,ns->bgqs', p_c, overlap)
        valid = sel_start[None, :] <= qpos[:, None]
        forced = (sel_start[None, :] == 0) | (sel_start[None, :] == ((qpos // SEL_LEN) * SEL_LEN)[:, None])
        score = jnp.where(valid, imp, -1.0)
        score = jnp.where(forced, score + FORCE_BONUS, score)
        top_val, top_idx = lax.top_k(score, n_top)
        k_sel = ks_blk[b_ix, g_ix, top_idx]
        v_sel = vs_blk[b_ix, g_ix, top_idx]
        kpos = top_idx[..., None] * SEL_LEN + jnp.arange(SEL_LEN)
        m_s = (kpos <= qpos[None, None, :, None, None]) & (top_val >= 0.0)[..., None]
        s_s = jnp.einsum('bqgpd,bgqkld->bgpqkl', qi, k_sel).astype(jnp.float32) * scale
        p_s = masked_softmax(s_s.reshape(B, G, P, Q_BLOCK, n_top * SEL_LEN),
                             m_s.reshape(B, G, 1, Q_BLOCK, n_top * SEL_LEN))
        o_s = jnp.einsum('bgpqkl,bgqkld->bqgpd',
                         p_s.reshape(B, G, P, Q_BLOCK, n_top, SEL_LEN).astype(dt), v_sel)
        k_win = lax.dynamic_slice_in_dim(kw_p, bidx * Q_BLOCK, Q_BLOCK + WINDOW, axis=1)
        v_win = lax.dynamic_slice_in_dim(vw_p, bidx * Q_BLOCK, Q_BLOCK + WINDOW, axis=1)
        wpos = bidx * Q_BLOCK - WINDOW + jnp.arange(Q_BLOCK + WINDOW)
        dist = qpos[:, None] - wpos[None, :]
        m_w = (wpos[None, :] >= 0) & (dist >= 0) & (dist < WINDOW)
        s_w = jnp.einsum('bqgpd,bkgd->bgpqk', qi, k_win).astype(jnp.float32) * scale
        p_w = masked_softmax(s_w, m_w)
        o_w = jnp.einsum('bgpqk,bkgd->bqgpd', p_w.astype(dt), v_win)
        out = gt[..., 0:1] * o_c + gt[..., 1:2] * o_s + gt[..., 2:3] * o_w
        return out.reshape(B, Q_BLOCK, NSA_HEADS * D)

    o = lax.map(one, (qb, gb, jnp.arange(nb)))
    return o.transpose(1, 0, 2, 3).reshape(B, S, NSA_HEADS * D)


def mixer(h, pos, w_in, conv_w, conv_b, wa, ba, wx, bx, lam, q_norm, w_uq, kv_norm, w_ukv,
          cmp_pos, cmp_w1, cmp_w2, gate_b, w_out):
    u = h @ w_in
    (x_lru, y_lru, q_dn, kv_dn, k_pe, q_n, kc, vc, ks, vs, kw, vw, g_n) = jnp.split(
        u, np.cumsum(IN_SPLITS)[:-1].tolist(), axis=-1)
    a_out = recurrent_group(x_lru, y_lru, conv_w, conv_b, wa, ba, wx, bx, lam)
    b_out = mla_group(q_dn, kv_dn, k_pe, pos, q_norm, w_uq, kv_norm, w_ukv)
    c_out = nsa_group(q_n, kc, vc, ks, vs, kw, vw, g_n, gate_b, pos, cmp_pos, cmp_w1, cmp_w2)
    return jnp.concatenate([a_out, b_out, c_out], axis=-1) @ w_out


def swiglu(x, wg, wu, wd):
    return (jax.nn.silu(x @ wg) * (x @ wu)) @ wd


def moe_ffn(x, router_w, wg, wu, wd):
    logits = (x @ router_w).astype(jnp.float32)
    top_val, top_idx = lax.top_k(logits, TOP_K)
    w = jax.nn.softmax(top_val, axis=-1)
    gate = jnp.sum(jax.nn.one_hot(top_idx, N_EXPERTS, dtype=jnp.float32) * w[..., None], axis=-2).astype(x.dtype)
    y = jnp.zeros_like(x)
    for e in range(N_EXPERTS):
        y = y + gate[..., e:e + 1] * swiglu(x, wg[e], wu[e], wd[e])
    return y


def setup_inputs(seed: int = 0) -> dict:
    key = jax.random.key(seed)
    keys = iter(jax.random.split(key, 32))
    f32 = jnp.float32

    def nrm(shape, fan):
        return jax.random.normal(next(keys), shape, f32) * fan ** -0.5

    def small(shape, s=0.01):
        return jax.random.normal(next(keys), shape, f32) * s

    def gain(shape):
        return 1.0 + small(shape)

    x = jax.random.normal(next(keys), (BATCH, SEQ, D_MODEL), f32)
    norm_mix = gain((DEPTH, D_MODEL))
    w_in = nrm((DEPTH, D_MODEL, IN_COLS), D_MODEL)
    lru_conv_w = nrm((DEPTH, CONV_WIDTH, LRU_WIDTH), CONV_WIDTH)
    lru_conv_b = small((DEPTH, LRU_WIDTH))
    lru_wa = nrm((DEPTH, LRU_BLOCKS, LRU_BLOCK_W, LRU_BLOCK_W), LRU_BLOCK_W)
    lru_ba = small((DEPTH, LRU_WIDTH))
    lru_wx = nrm((DEPTH, LRU_BLOCKS, LRU_BLOCK_W, LRU_BLOCK_W), LRU_BLOCK_W)
    lru_bx = small((DEPTH, LRU_WIDTH))
    a_c = jax.random.uniform(next(keys), (DEPTH, LRU_WIDTH), f32, minval=0.9, maxval=0.999)
    a0 = a_c ** (1.0 / LRU_C)
    lru_lambda = jnp.log(a0) - jnp.log1p(-a0)
    mla_q_norm = gain((DEPTH, MLA_Q_RANK))
    mla_w_uq = nrm((DEPTH, MLA_Q_RANK, MLA_HEADS * (MLA_NOPE + MLA_ROPE)), MLA_Q_RANK)
    mla_kv_norm = gain((DEPTH, MLA_KV_RANK))
    mla_w_ukv = nrm((DEPTH, MLA_KV_RANK, MLA_HEADS * (MLA_NOPE + MLA_V)), MLA_KV_RANK)
    nsa_cmp_pos = small((DEPTH, 2, CMP_LEN, NSA_HEAD_DIM), 0.02)
    nsa_cmp_w1 = nrm((DEPTH, 2, CMP_LEN * NSA_HEAD_DIM, CMP_HIDDEN), CMP_LEN * NSA_HEAD_DIM)
    nsa_cmp_w2 = nrm((DEPTH, 2, CMP_HIDDEN, NSA_HEAD_DIM), CMP_HIDDEN)
    nsa_gate_b = small((DEPTH, 3 * NSA_HEADS))
    w_out = nrm((DEPTH, D_MIX, D_MODEL), D_MIX)
    norm_ffn = gain((DEPTH, D_MODEL))
    ffn_w_gate = nrm((N_DENSE, D_MODEL, D_FF), D_MODEL)
    ffn_w_up = nrm((N_DENSE, D_MODEL, D_FF), D_MODEL)
    ffn_w_down = nrm((N_DENSE, D_FF, D_MODEL), D_FF)
    router_w = nrm((N_MOE, D_MODEL, N_EXPERTS), D_MODEL)
    moe_w_gate = nrm((N_MOE, N_EXPERTS, D_MODEL, D_FF_EXPERT), D_MODEL)
    moe_w_up = nrm((N_MOE, N_EXPERTS, D_MODEL, D_FF_EXPERT), D_MODEL)
    moe_w_down = nrm((N_MOE, N_EXPERTS, D_FF_EXPERT, D_MODEL), D_FF_EXPERT)
    norm_final = gain((D_MODEL,))
    return {'x': x, 'norm_mix': norm_mix, 'w_in': w_in, 'lru_conv_w': lru_conv_w, 'lru_conv_b': lru_conv_b,
            'lru_wa': lru_wa, 'lru_ba': lru_ba, 'lru_wx': lru_wx, 'lru_bx': lru_bx, 'lru_lambda': lru_lambda,
            'mla_q_norm': mla_q_norm, 'mla_w_uq': mla_w_uq, 'mla_kv_norm': mla_kv_norm, 'mla_w_ukv': mla_w_ukv,
            'nsa_cmp_pos': nsa_cmp_pos, 'nsa_cmp_w1': nsa_cmp_w1, 'nsa_cmp_w2': nsa_cmp_w2, 'nsa_gate_b': nsa_gate_b,
            'w_out': w_out, 'norm_ffn': norm_ffn, 'ffn_w_gate': ffn_w_gate, 'ffn_w_up': ffn_w_up,
            'ffn_w_down': ffn_w_down, 'router_w': router_w, 'moe_w_gate': moe_w_gate, 'moe_w_up': moe_w_up,
            'moe_w_down': moe_w_down, 'norm_final': norm_final}


def reference(x, norm_mix, w_in, lru_conv_w, lru_conv_b, lru_wa, lru_ba, lru_wx, lru_bx, lru_lambda,
              mla_q_norm, mla_w_uq, mla_kv_norm, mla_w_ukv, nsa_cmp_pos, nsa_cmp_w1, nsa_cmp_w2, nsa_gate_b,
              w_out, norm_ffn, ffn_w_gate, ffn_w_up, ffn_w_down, router_w, moe_w_gate, moe_w_up,
              moe_w_down, norm_final):
    pos = jnp.arange(x.shape[1])
    for l in range(DEPTH):
        h = rmsnorm(x, norm_mix[l])
        x = x + mixer(h, pos, w_in[l], lru_conv_w[l], lru_conv_b[l], lru_wa[l], lru_ba[l], lru_wx[l],
                      lru_bx[l], lru_lambda[l], mla_q_norm[l], mla_w_uq[l], mla_kv_norm[l], mla_w_ukv[l],
                      nsa_cmp_pos[l], nsa_cmp_w1[l], nsa_cmp_w2[l], nsa_gate_b[l], w_out[l])
        h = rmsnorm(x, norm_ffn[l])
        if l % 2 == 0:
            j = l // 2
            x = x + swiglu(h, ffn_w_gate[j], ffn_w_up[j], ffn_w_down[j])
        else:
            j = l // 2
            x = x + moe_ffn(h, router_w[j], moe_w_gate[j], moe_w_up[j], moe_w_down[j])
    return rmsnorm(x, norm_final)
```

```python
import functools
import math

import numpy as np
import jax
import jax.numpy as jnp
from jax import lax
from jax.experimental import pallas as pl
from jax.experimental.pallas import tpu as pltpu

F32 = jnp.float32
BF16 = jnp.bfloat16

D_MODEL = 1024
LRU_WIDTH = 256
LRU_BLOCKS = 4
LRU_BLOCK_W = LRU_WIDTH // LRU_BLOCKS
CONV_WIDTH = 4
LRU_C = 8.0
MLA_HEADS = 6
MLA_NOPE = 64
MLA_ROPE = 32
MLA_V = 64
MLA_Q_RANK = 192
MLA_KV_RANK = 128
NSA_HEADS = 6
NSA_KV_HEADS = 2
NSA_GROUP = NSA_HEADS // NSA_KV_HEADS
NSA_HEAD_DIM = 64
CMP_LEN = 32
CMP_STRIDE = 16
CMP_HIDDEN = 128
SEL_LEN = 64
SEL_TOPK = 16
WINDOW = 512
FORCE_BONUS = 1.0e3
D_FF = 2816
N_EXPERTS = 8
D_FF_EXPERT = 1408
ROPE_THETA = 10000.0
EPS = 1e-6
NEG_INF = -1.0e30

LANES = 128
VMEM_LIMIT = 56 * 1024 * 1024

C_LRU = 0
C_MLA = 512
C_KPE_ROT = 1024
C_QN = 1152
C_QN_ROT = 1920
C_KC = 2688
C_KC_ROT = 2816
C_VC = 2944
C_KS = 3072
C_KS_ROT = 3200
C_VS = 3328
C_KW = 3456
C_KW_ROT = 3584
C_VW = 3712
C_GATE = 3840
IN_COLS_PAD = 3968


def _cparams(sem):
    return pltpu.CompilerParams(dimension_semantics=sem, vmem_limit_bytes=VMEM_LIMIT)


def _gelu_tanh(x):
    return 0.5 * x * (1.0 + jnp.tanh(math.sqrt(2.0 / math.pi) * (x + 0.044715 * (x * x * x))))


def _sigmoid(x):
    return 1.0 / (1.0 + jnp.exp(-x))


def _dot(a, b):
    return jnp.dot(a, b, preferred_element_type=F32)


def _dot_nt(a, b):
    return lax.dot_general(a, b, (((1,), (1,)), ((), ())), preferred_element_type=F32)


def _rot_cols(w, half):
    return jnp.concatenate([-w[:, half:], w[:, :half]], axis=1)


def _pad_cols(w, n):
    return jnp.pad(w, ((0, 0), (0, n - w.shape[1])))


def _prep_w_in(w_in):
    splits = np.cumsum([256, 256, 192, 128, 32, 384, 128, 128, 128, 128, 128, 128, 18])[:-1].tolist()
    (x_l, y_l, q_dn, kv_dn, k_pe, q_n, kc, vc, ks, vs, kw, vw, g_n) = jnp.split(w_in, splits, axis=1)
    z = lambda n: jnp.zeros((w_in.shape[0], n), w_in.dtype)

    def kpe_slot(w):
        return jnp.concatenate([z(64), w, z(32)], axis=1)

    def qn_slots(fn):
        out = []
        for h in range(NSA_HEADS):
            g = h // NSA_GROUP
            wh = fn(q_n[:, h * 64:(h + 1) * 64])
            out.append(jnp.concatenate([z(64 * g), wh, z(64 * (1 - g))], axis=1))
        return jnp.concatenate(out, axis=1)

    def kv_rot(w):
        return jnp.concatenate([_rot_cols(w[:, :64], 32), _rot_cols(w[:, 64:], 32)], axis=1)

    cols = [x_l, y_l, _pad_cols(q_dn, 256), kv_dn, kpe_slot(k_pe), kpe_slot(_rot_cols(k_pe, 16)),
            qn_slots(lambda w: w), qn_slots(lambda w: _rot_cols(w, 32)),
            kc, kv_rot(kc), vc, ks, kv_rot(ks), vs, kw, kv_rot(kw), vw, _pad_cols(g_n, 128)]
    w = jnp.concatenate(cols, axis=1)
    assert w.shape[1] == IN_COLS_PAD
    return w.astype(BF16)


def _rope_tables(seq):
    pos = jnp.arange(seq, dtype=F32)[:, None]
    inv64 = ROPE_THETA ** (-jnp.arange(32, dtype=F32) * 2.0 / 64)
    inv32 = ROPE_THETA ** (-jnp.arange(16, dtype=F32) * 2.0 / 32)
    a64 = pos * inv64[None, :]
    a32 = pos * inv32[None, :]
    c64, s64 = jnp.cos(a64), jnp.sin(a64)
    c32, s32 = jnp.cos(a32), jnp.sin(a32)
    cos_n = jnp.concatenate([c64, c64, c64, c64], axis=1)
    sin_n = jnp.concatenate([s64, s64, s64, s64], axis=1)
    one = jnp.ones((seq, 64), F32)
    cos_m = jnp.concatenate([one, c32, c32, one[:, :32]], axis=1)
    sin_m = jnp.concatenate([0 * one, s32, s32, 0 * one[:, :32]], axis=1)
    return cos_n, sin_n, cos_m, sin_m


def _in_proj_kernel(x_ref, g_ref, w_ref, cn_ref, sn_ref, cm_ref, sm_ref, gb_ref,
                    lru_ref, mla_ref, qn_ref, kc_ref, vc_ref, kvsw_ref, gate_ref):
    x = x_ref[...]
    ms = jnp.mean(x * x, axis=-1, keepdims=True)
    h = ((x * lax.rsqrt(ms + EPS)) * g_ref[...]).astype(BF16)

    def proj(c0, n):
        return _dot(h, w_ref[:, c0:c0 + n])

    cn, sn = cn_ref[...], sn_ref[...]
    lru_ref[...] = proj(C_LRU, 512)
    mla_ref[:, 0:384] = proj(C_MLA, 384)
    mla_ref[:, 384:512] = proj(C_MLA + 384, 128) * cm_ref[...] + proj(C_KPE_ROT, 128) * sm_ref[...]
    scale = NSA_HEAD_DIM ** -0.5
    for hd in range(NSA_HEADS):
        a = proj(C_QN + hd * LANES, LANES)
        b = proj(C_QN_ROT + hd * LANES, LANES)
        qn_ref[:, hd * LANES:(hd + 1) * LANES] = ((a * cn + b * sn) * scale).astype(BF16)
    kc_ref[...] = proj(C_KC, 128) * cn + proj(C_KC_ROT, 128) * sn
    vc_ref[...] = proj(C_VC, 128)
    kvsw_ref[:, 0:128] = (proj(C_KS, 128) * cn + proj(C_KS_ROT, 128) * sn).astype(BF16)
    kvsw_ref[:, 128:256] = proj(C_VS, 128).astype(BF16)
    kvsw_ref[:, 256:384] = (proj(C_KW, 128) * cn + proj(C_KW_ROT, 128) * sn).astype(BF16)
    kvsw_ref[:, 384:512] = proj(C_VW, 128).astype(BF16)
    gate_ref[...] = _sigmoid(proj(C_GATE, 128) + gb_ref[...])


def _in_proj(x2, g, w_cat, tables, gate_b, seq, tm=512):
    T = x2.shape[0]
    nt = T // tm
    npos = seq // tm
    cn, sn, cm, sm = tables
    row = lambda i: (i, 0)
    fixed = lambda i: (0, 0)
    posmap = lambda i: (i % npos, 0)
    tab_spec = pl.BlockSpec((tm, LANES), posmap)
    out_shapes = (
        jax.ShapeDtypeStruct((T, 512), F32),
        jax.ShapeDtypeStruct((T, 512), F32),
        jax.ShapeDtypeStruct((T, 768), BF16),
        jax.ShapeDtypeStruct((T, 128), F32),
        jax.ShapeDtypeStruct((T, 128), F32),
        jax.ShapeDtypeStruct((T, 512), BF16),
        jax.ShapeDtypeStruct((T, 128), F32),
    )
    return pl.pallas_call(
        _in_proj_kernel,
        grid=(nt,),
        in_specs=[pl.BlockSpec((tm, D_MODEL), row), pl.BlockSpec((1, D_MODEL), fixed),
                  pl.BlockSpec((D_MODEL, IN_COLS_PAD), fixed),
                  tab_spec, tab_spec, tab_spec, tab_spec, pl.BlockSpec((1, LANES), fixed)],
        out_specs=[pl.BlockSpec((tm, s.shape[1]), row) for s in out_shapes],
        out_shape=out_shapes,
        compiler_params=_cparams(("parallel",)),
        name="in_proj",
    )(x2, g.reshape(1, -1), w_cat, cn, sn, cm, sm, _pad_cols(gate_b.reshape(1, -1), LANES))


def _lru_kernel(xy_ref, cw_ref, cb_ref, wa_ref, ba_ref, wx_ref, bx_ref, lam_ref, out_ref,
                xe_sc, h_sc, *, tt):
    t = pl.program_id(1)

    @pl.when(t == 0)
    def _():
        xe_sc[0:8, :] = jnp.zeros((8, LRU_WIDTH), F32)
        h_sc[...] = jnp.zeros_like(h_sc)

    x = xy_ref[0, :, 0:LRU_WIDTH]
    y = xy_ref[0, :, LRU_WIDTH:2 * LRU_WIDTH]
    xe_sc[8:8 + tt, :] = x
    xc = cb_ref[...]
    for k in range(CONV_WIDTH):
        off = 8 - (CONV_WIDTH - 1) + k
        xc = xc + xe_sc[off:off + tt, :] * cw_ref[k:k + 1, :]
    xe_sc[0:8, :] = x[tt - 8:tt, :]

    xb = xc.astype(BF16)
    r = _sigmoid(_dot(xb, wa_ref[...]) + ba_ref[...])
    i = _sigmoid(_dot(xb, wx_ref[...]) + bx_ref[...])
    log_a = (-LRU_C * r) * jax.nn.softplus(-lam_ref[...])
    a = jnp.exp(log_a)
    b = jnp.sqrt(jnp.tanh(-log_a) * (a * a + 1.0)) * (i * xc)

    row = lax.broadcasted_iota(jnp.int32, (tt, LRU_WIDTH), 0)
    k = 1
    while k < tt:
        keep = row >= k
        a_sh = jnp.where(keep, pltpu.roll(a, k, 0), 1.0)
        b_sh = jnp.where(keep, pltpu.roll(b, k, 0), 0.0)
        b = a * b_sh + b
        a = a * a_sh
        k *= 2
    h = b + a * h_sc[0:1, :]
    h_sc[0:1, :] = h[tt - 1:tt, :]
    out_ref[0] = h * _gelu_tanh(y)


def _block_diag_dense(w):
    n, c, d = w.shape
    out = jnp.zeros((n * c, n * d), w.dtype)
    for j in range(n):
        out = out.at[j * c:(j + 1) * c, j * d:(j + 1) * d].set(w[j])
    return out


def _lru(lru_xy, conv_w, conv_b, wa, ba, wx, bx, lam, batch, seq, tt=512):
    xy = lru_xy.reshape(batch, seq, 512)
    fixed = lambda b, t: (0, 0)
    vec = pl.BlockSpec((1, LRU_WIDTH), fixed)
    mat = pl.BlockSpec((LRU_WIDTH, LRU_WIDTH), fixed)
    out = pl.pallas_call(
        functools.partial(_lru_kernel, tt=tt),
        grid=(batch, seq // tt),
        in_specs=[pl.BlockSpec((1, tt, 512), lambda b, t: (b, t, 0)),
                  pl.BlockSpec((CONV_WIDTH, LRU_WIDTH), fixed), vec, mat, vec, mat, vec, vec],
        out_specs=pl.BlockSpec((1, tt, LRU_WIDTH), lambda b, t: (b, t, 0)),
        out_shape=jax.ShapeDtypeStruct((batch, seq, LRU_WIDTH), F32),
        scratch_shapes=[pltpu.VMEM((tt + 8, LRU_WIDTH), F32), pltpu.VMEM((8, LRU_WIDTH), F32)],
        compiler_params=_cparams(("parallel", "arbitrary")),
        name="rg_lru",
    )(xy, conv_w, conv_b.reshape(1, -1), _block_diag_dense(wa).astype(BF16), ba.reshape(1, -1),
      _block_diag_dense(wx).astype(BF16), bx.reshape(1, -1), lam.reshape(1, -1))
    return out.reshape(batch * seq, LRU_WIDTH)


def _mla_prep_kernel(dn_ref, qn_ref, wqa_ref, wqb_ref, kvn_ref, wk_ref, wv_ref, cm_ref, sm_ref,
                     q_ref, k_ref, v_ref):
    q_dn = dn_ref[:, 0:256]
    ms = jnp.sum(q_dn * q_dn, axis=-1, keepdims=True) * (1.0 / MLA_Q_RANK)
    ql = ((q_dn * lax.rsqrt(ms + EPS)) * qn_ref[...]).astype(BF16)
    kv_dn = dn_ref[:, 256:384]
    ms = jnp.mean(kv_dn * kv_dn, axis=-1, keepdims=True)
    kvl = ((kv_dn * lax.rsqrt(ms + EPS)) * kvn_ref[...]).astype(BF16)
    kpe = dn_ref[:, 384:512]
    cm, sm = cm_ref[...], sm_ref[...]
    for hd in range(MLA_HEADS):
        sl = slice(hd * LANES, (hd + 1) * LANES)
        q_ref[:, sl] = (_dot(ql, wqa_ref[:, sl]) * cm + _dot(ql, wqb_ref[:, sl]) * sm).astype(BF16)
        k_ref[:, sl] = (_dot(kvl, wk_ref[:, sl]) + kpe).astype(BF16)
    v_ref[...] = _dot(kvl, wv_ref[...]).astype(BF16)


def _prep_mla_weights(w_uq, w_ukv):
    zq = jnp.zeros((MLA_Q_RANK, 32), w_uq.dtype)
    qa, qb, kk, vv = [], [], [], []
    for h in range(MLA_HEADS):
        wq = w_uq[:, h * 96:(h + 1) * 96]
        nope, ropew = wq[:, :64], wq[:, 64:]
        qa.append(jnp.concatenate([nope, ropew, zq], axis=1))
        qb.append(jnp.concatenate([0 * nope, _rot_cols(ropew, 16), zq], axis=1))
        wkv = w_ukv[:, h * 128:(h + 1) * 128]
        kk.append(_pad_cols(wkv[:, :64], LANES))
        vv.append(wkv[:, 64:])
    pad_rows = lambda w: jnp.pad(w, ((0, 256 - MLA_Q_RANK), (0, 0)))
    return (pad_rows(jnp.concatenate(qa, axis=1)).astype(BF16), pad_rows(jnp.concatenate(qb, axis=1)).astype(BF16),
            jnp.concatenate(kk, axis=1).astype(BF16), jnp.concatenate(vv, axis=1).astype(BF16))


def _mla_prep(mla_dn, q_norm, w_uq, kv_norm, w_ukv, tables, seq, tm=512):
    T = mla_dn.shape[0]
    npos = seq // tm
    _, _, cm, sm = tables
    wqa, wqb, wk, wv = _prep_mla_weights(w_uq, w_ukv)
    row = lambda i: (i, 0)
    fixed = lambda i: (0, 0)
    tab = pl.BlockSpec((tm, LANES), lambda i: (i % npos, 0))
    return pl.pallas_call(
        _mla_prep_kernel,
        grid=(T // tm,),
        in_specs=[pl.BlockSpec((tm, 512), row), pl.BlockSpec((1, 256), fixed),
                  pl.BlockSpec((256, 768), fixed), pl.BlockSpec((256, 768), fixed),
                  pl.BlockSpec((1, 128), fixed), pl.BlockSpec((128, 768), fixed),
                  pl.BlockSpec((128, 384), fixed), tab, tab],
        out_specs=[pl.BlockSpec((tm, 768), row), pl.BlockSpec((tm, 768), row), pl.BlockSpec((tm, 384), row)],
        out_shape=(jax.ShapeDtypeStruct((T, 768), BF16), jax.ShapeDtypeStruct((T, 768), BF16),
                   jax.ShapeDtypeStruct((T, 384), BF16)),
        compiler_params=_cparams(("parallel",)),
        name="mla_prep",
    )(mla_dn, _pad_cols(q_norm.reshape(1, -1), 256), wqa, wqb, kv_norm.reshape(1, -1), wk, wv, cm, sm)


def _flash_kernel(qt_ref, kt_ref, first_ref, last_ref, *refs, mode, n_heads, q_slots, k_slots,
                  scale, tq, tk, pair_out):
    if mode == "select":
        q_ref, k_ref, v_ref, sb_ref, oh_ref, o_ref, m_sc, l_sc, acc_sc = refs
    else:
        q_ref, k_ref, v_ref, o_ref, m_sc, l_sc, acc_sc = refs
    step = pl.program_id(2)
    qt = qt_ref[step]
    kt = kt_ref[step]

    @pl.when(first_ref[step] == 1)
    def _():
        m_sc[...] = jnp.full_like(m_sc, NEG_INF)
        l_sc[...] = jnp.zeros_like(l_sc)
        acc_sc[...] = jnp.zeros_like(acc_sc)

    qpos = qt * tq + lax.broadcasted_iota(jnp.int32, (tq, tk), 0)
    kpos = kt * tk + lax.broadcasted_iota(jnp.int32, (tq, tk), 1)
    dist = qpos - kpos
    if mode == "window":
        valid = (dist >= 0) & (dist < WINDOW)
    else:
        valid = dist >= 0
    v = v_ref[0]
    for j in range(n_heads):
        q = q_ref[0, :, q_slots[j] * LANES:(q_slots[j] + 1) * LANES]
        k = k_ref[0, :, k_slots[j] * LANES:(k_slots[j] + 1) * LANES]
        if mode == "select":
            q = jnp.concatenate([q, sb_ref[0, 0]], axis=1)
            k = jnp.concatenate([k, oh_ref[...]], axis=1)
        s = _dot_nt(q, k)
        if scale != 1.0:
            s = s * scale
        s = jnp.where(valid, s, NEG_INF)
        m_prev = m_sc[j]
        m_new = jnp.maximum(m_prev, jnp.max(s, axis=1, keepdims=True))
        alpha = jnp.exp(m_prev - m_new)
        p = jnp.exp(s - m_new)
        l_sc[j] = alpha * l_sc[j] + jnp.sum(p, axis=1, keepdims=True)
        acc_sc[j] = alpha * acc_sc[j] + _dot(p.astype(BF16), v)
        m_sc[j] = m_new

    @pl.when(last_ref[step] == 1)
    def _():
        outs = [acc_sc[j] / l_sc[j] for j in range(n_heads)]
        if pair_out:
            lane = lax.broadcasted_iota(jnp.int32, (tq, LANES), 1)
            o_ref[0] = jnp.where(lane < 64, outs[0], outs[1]).astype(o_ref.dtype)
        else:
            for j in range(n_heads):
                o_ref[0, :, j * LANES:(j + 1) * LANES] = outs[j].astype(o_ref.dtype)


def _pair_tables(nq, mode, tq, tk):
    qts, kts, first, last = [], [], [], []
    for qt in range(nq):
        hi = (qt * tq + tq - 1) // tk
        lo = 0 if mode != "window" else max(0, (qt * tq - (WINDOW - 1)) // tk)
        for kt in range(lo, hi + 1):
            qts.append(qt)
            kts.append(kt)
            first.append(1 if kt == lo else 0)
            last.append(1 if kt == hi else 0)
    arr = lambda v: jnp.asarray(np.array(v, dtype=np.int32))
    return arr(qts), arr(kts), arr(first), arr(last)


def _flash(q, k, v, *, mode, batch, seq, n_groups, n_heads, q_slots, k_slots, q_width, k_width,
           k_col, v_col, out_width, scale, pair_out, out_dtype, selbias=None, onehot=None, tq=512, tk=512):
    nq = seq // tq
    qt_tab, kt_tab, first_tab, last_tab = _pair_tables(nq, mode, tq, tk)
    n_steps = int(qt_tab.shape[0])
    kblk = k_width // LANES
    in_specs = [
        pl.BlockSpec((1, tq, q_width), lambda b, g, s, qt, kt, f, l: (b, qt[s], g)),
        pl.BlockSpec((1, tk, k_width), lambda b, g, s, qt, kt, f, l: (b, kt[s], k_col(g))),
        pl.BlockSpec((1, tk, LANES), lambda b, g, s, qt, kt, f, l: (b, kt[s], v_col(g))),
    ]
    args = [q, k, v]
    if mode == "select":
        in_specs.append(pl.BlockSpec((1, 1, tq, LANES), lambda b, g, s, qt, kt, f, l: (b, g, qt[s], 0)))
        in_specs.append(pl.BlockSpec((tk, LANES), lambda b, g, s, qt, kt, f, l: (kt[s], 0)))
        args += [selbias, onehot]
    del kblk
    kern = functools.partial(_flash_kernel, mode=mode, n_heads=n_heads, q_slots=q_slots, k_slots=k_slots,
                             scale=scale, tq=tq, tk=tk, pair_out=pair_out)
    return pl.pallas_call(
        kern,
        grid_spec=pltpu.PrefetchScalarGridSpec(
            num_scalar_prefetch=4,
            grid=(batch, n_groups, n_steps),
            in_specs=in_specs,
            out_specs=pl.BlockSpec((1, tq, out_width), lambda b, g, s, qt, kt, f, l: (b, qt[s], g)),
            scratch_shapes=[pltpu.VMEM((n_heads, tq, 1), F32), pltpu.VMEM((n_heads, tq, 1), F32),
                            pltpu.VMEM((n_heads, tq, LANES), F32)],
        ),
        out_shape=jax.ShapeDtypeStruct((batch, seq, n_groups * out_width), out_dtype),
        compiler_params=_cparams(("parallel", "parallel", "arbitrary")),
        name="flash_" + mode,
    )(qt_tab, kt_tab, first_tab, last_tab, *args)


def _compress_kernel(kc_ref, vc_ref, pos_ref, w1a_ref, w1b_ref, w2_ref, ko_ref, vo_ref, *, nchunk):
    row = lax.broadcasted_iota(jnp.int32, (nchunk, LANES), 0)
    for br, (src, dst) in enumerate(((kc_ref, ko_ref), (vc_ref, vo_ref))):
        r = src[0]
        pa = _dot((r + pos_ref[br, 0:1, :]).astype(BF16), w1a_ref[br])
        pb = _dot((r + pos_ref[br, 1:2, :]).astype(BF16), w1b_ref[br])
        hid = _gelu_tanh(pa + pltpu.roll(pb, nchunk - 1, 0))
        out = _dot(hid.astype(BF16), w2_ref[br])
        dst[0] = jnp.where(row < nchunk - 1, out, 0.0).astype(dst.dtype)


def _prep_compress_weights(cmp_pos, cmp_w1, cmp_w2):
    half = CMP_LEN // 2
    pos, w1a, w1b, w2 = [], [], [], []
    for br in range(2):
        p = cmp_pos[br]
        tile = lambda ph: jnp.concatenate([ph, ph], axis=1).reshape(1, half * LANES)
        pos.append(jnp.concatenate([tile(p[:half]), tile(p[half:])], axis=0))
        w = cmp_w1[br].reshape(CMP_LEN, NSA_HEAD_DIM, CMP_HIDDEN)
        z = jnp.zeros_like(w[:half])

        def big(wh):
            g0 = jnp.concatenate([wh, z], axis=1)
            g1 = jnp.concatenate([z, wh], axis=1)
            return jnp.concatenate([g0.reshape(half * LANES, CMP_HIDDEN), g1.reshape(half * LANES, CMP_HIDDEN)], axis=1)

        w1a.append(big(w[:half]))
        w1b.append(big(w[half:]))
        w2.append(_block_diag_dense(jnp.stack([cmp_w2[br], cmp_w2[br]])))
    return (jnp.stack(pos), jnp.stack(w1a).astype(BF16), jnp.stack(w1b).astype(BF16), jnp.stack(w2).astype(BF16))


def _compress(kc, vc, cmp_pos, cmp_w1, cmp_w2, batch, seq):
    nchunk = seq // CMP_STRIDE
    width = CMP_STRIDE * LANES
    pos, w1a, w1b, w2 = _prep_compress_weights(cmp_pos, cmp_w1, cmp_w2)
    blk = pl.BlockSpec((1, nchunk, width), lambda b: (b, 0, 0))
    full3 = lambda shape: pl.BlockSpec(shape, lambda b: (0, 0, 0))
    oblk = pl.BlockSpec((1, nchunk, LANES), lambda b: (b, 0, 0))
    return pl.pallas_call(
        functools.partial(_compress_kernel, nchunk=nchunk),
        grid=(batch,),
        in_specs=[blk, blk, full3(pos.shape), full3(w1a.shape), full3(w1b.shape), full3(w2.shape)],
        out_specs=[oblk, oblk],
        out_shape=(jax.ShapeDtypeStruct((batch, nchunk, LANES), BF16),) * 2,
        compiler_params=_cparams(("parallel",)),
        name="nsa_compress",
    )(kc.reshape(batch, nchunk, width), vc.reshape(batch, nchunk, width), pos, w1a, w1b, w2)


def _select_kernel(q_ref, kc_ref, vc_ref, ov_ref, oc_ref, sb_ref, *, tq, nchunk):
    nsel = LANES
    qt = pl.program_id(2)
    kc = kc_ref[0]
    vc = vc_ref[0]
    qpos = qt * tq + lax.broadcasted_iota(jnp.int32, (tq, nchunk), 0)
    cend = lax.broadcasted_iota(jnp.int32, (tq, nchunk), 1) * CMP_STRIDE + (CMP_LEN - 1)
    cvalid = cend <= qpos
    psum = jnp.zeros((tq, nchunk), F32)
    for j in range(NSA_GROUP):
        q = q_ref[0, :, j * LANES:(j + 1) * LANES]
        s = jnp.where(cvalid, _dot_nt(q, kc), NEG_INF)
        m = jnp.max(s, axis=1, keepdims=True)
        e = jnp.where(cvalid, jnp.exp(s - m), 0.0)
        p = e / jnp.maximum(jnp.sum(e, axis=1, keepdims=True), 1e-30)
        oc_ref[0, :, j * LANES:(j + 1) * LANES] = _dot(p.astype(BF16), vc)
        psum = psum + p
    ov = ov_ref[...]
    hi = psum.astype(BF16)
    r1 = psum - hi.astype(F32)
    mid = r1.astype(BF16)
    lo = (r1 - mid.astype(F32)).astype(BF16)
    imp = _dot(hi, ov) + _dot(mid, ov) + _dot(lo, ov)
    score = imp.T
    blk = lax.broadcasted_iota(jnp.int32, (nsel, tq), 0)
    qp = qt * tq + lax.broadcasted_iota(jnp.int32, (nsel, tq), 1)
    valid = blk * SEL_LEN <= qp
    forced = (blk == 0) | (blk == qp // SEL_LEN)
    score = jnp.where(valid, score, -1.0)
    score = jnp.where(forced, score + FORCE_BONUS, score)
    sel = jnp.zeros((nsel, tq), F32)
    for _ in range(min(SEL_TOPK, nsel)):
        m = jnp.max(score, axis=0, keepdims=True)
        idx = jnp.min(jnp.where(score == m, blk, nsel), axis=0, keepdims=True)
        hit = blk == idx
        sel = jnp.where(hit & (m >= 0.0), 1.0, sel)
        score = jnp.where(hit, -jnp.inf, score)
    bias = jnp.where(sel > 0.0, 0.0, NEG_INF)
    sb_ref[0, 0] = bias.T.astype(sb_ref.dtype)


def _overlap_matrix(nchunk, nsel):
    n = np.arange(nchunk)[:, None]
    s = np.arange(nsel)[None, :]
    cs = n * CMP_STRIDE
    ov = (cs < s * SEL_LEN + SEL_LEN) & (cs + CMP_LEN - 1 >= s * SEL_LEN) & (n < nchunk - 1)
    return jnp.asarray(ov.astype(np.float32)).astype(BF16)


def _select(qn, kcmp, vcmp, batch, seq, tq=256):
    nchunk = seq // CMP_STRIDE
    nsel = LANES
    assert seq // SEL_LEN <= nsel
    gw = NSA_GROUP * LANES
    ov = _overlap_matrix(nchunk, nsel)
    cblk = pl.BlockSpec((1, nchunk, LANES), lambda b, g, t: (b, 0, 0))
    return pl.pallas_call(
        functools.partial(_select_kernel, tq=tq, nchunk=nchunk),
        grid=(batch, NSA_KV_HEADS, seq // tq),
        in_specs=[pl.BlockSpec((1, tq, gw), lambda b, g, t: (b, t, g)), cblk, cblk,
                  pl.BlockSpec((nchunk, nsel), lambda b, g, t: (0, 0))],
        out_specs=[pl.BlockSpec((1, tq, gw), lambda b, g, t: (b, t, g)),
                   pl.BlockSpec((1, 1, tq, nsel), lambda b, g, t: (b, g, t, 0))],
        out_shape=(jax.ShapeDtypeStruct((batch, seq, NSA_KV_HEADS * gw), F32),
                   jax.ShapeDtypeStruct((batch, NSA_KV_HEADS, seq, nsel), BF16)),
        compiler_params=_cparams(("parallel", "parallel", "parallel")),
        name="nsa_select",
    )(qn, kcmp, vcmp, ov)


def _out_proj_kernel(x_ref, a_ref, b_ref, oc_ref, os_ref, ow_ref, gt_ref, wa_ref, wb_ref, wc_ref, o_ref):
    acc = x_ref[...] + _dot(a_ref[...].astype(BF16), wa_ref[...])
    acc = acc + _dot(b_ref[...], wb_ref[...])
    gt = gt_ref[...]
    for hd in range(NSA_HEADS):
        sl = slice(hd * LANES, (hd + 1) * LANES)
        c = (gt[:, 3 * hd:3 * hd + 1] * oc_ref[:, sl] + gt[:, 3 * hd + 1:3 * hd + 2] * os_ref[:, sl]
             + gt[:, 3 * hd + 2:3 * hd + 3] * ow_ref[:, sl])
        acc = acc + _dot(c.astype(BF16), wc_ref[sl, :])
    o_ref[...] = acc


def _out_proj(x2, a_out, b_out, o_c, o_s, o_w, gates, w_out, tm=512):
    T = x2.shape[0]
    wa = w_out[0:256].astype(BF16)
    wb = w_out[256:640].astype(BF16)
    wc_rows = []
    zero = jnp.zeros((64, D_MODEL), w_out.dtype)
    for h in range(NSA_HEADS):
        wh = w_out[640 + h * 64:640 + (h + 1) * 64]
        wc_rows += [wh, zero] if h // NSA_GROUP == 0 else [zero, wh]
    wc = jnp.concatenate(wc_rows, axis=0).astype(BF16)
    row = lambda i: (i, 0)
    fixed = lambda i: (0, 0)
    return pl.pallas_call(
        _out_proj_kernel,
        grid=(T // tm,),
        in_specs=[pl.BlockSpec((tm, D_MODEL), row), pl.BlockSpec((tm, 256), row), pl.BlockSpec((tm, 384), row),
                  pl.BlockSpec((tm, 768), row), pl.BlockSpec((tm, 768), row), pl.BlockSpec((tm, 768), row),
                  pl.BlockSpec((tm, 128), row), pl.BlockSpec((256, D_MODEL), fixed),
                  pl.BlockSpec((384, D_MODEL), fixed), pl.BlockSpec((768, D_MODEL), fixed)],
        out_specs=pl.BlockSpec((tm, D_MODEL), row),
        out_shape=jax.ShapeDtypeStruct((T, D_MODEL), F32),
        compiler_params=_cparams(("parallel",)),
        name="out_proj",
    )(x2, a_out, b_out, o_c, o_s, o_w, gates, wa, wb, wc)


def _final_norm(y, gf_ref):
    ms = jnp.mean(y * y, axis=-1, keepdims=True)
    return (y * lax.rsqrt(ms + EPS)) * gf_ref[...]


def _ffn_kernel(x_ref, g_ref, wg_ref, wu_ref, wd_ref, gf_ref, o_ref, h_sc, acc_sc, *, final_norm):
    f = pl.program_id(1)

    @pl.when(f == 0)
    def _():
        x = x_ref[...]
        ms = jnp.mean(x * x, axis=-1, keepdims=True)
        h_sc[...] = ((x * lax.rsqrt(ms + EPS)) * g_ref[...]).astype(BF16)
        acc_sc[...] = x

    h = h_sc[...]
    gte = _dot(h, wg_ref[...])
    up = _dot(h, wu_ref[...])
    act = (gte * _sigmoid(gte)) * up
    acc_sc[...] += _dot(act.astype(BF16), wd_ref[...])

    @pl.when(f == pl.num_programs(1) - 1)
    def _():
        y = acc_sc[...]
        o_ref[...] = _final_norm(y, gf_ref) if final_norm else y


def _ffn(x2, g, wg, wu, wd, g_final, final_norm, tm=512, tf=1408):
    T = x2.shape[0]
    nf = D_FF // tf
    return pl.pallas_call(
        functools.partial(_ffn_kernel, final_norm=final_norm),
        grid=(T // tm, nf),
        in_specs=[pl.BlockSpec((tm, D_MODEL), lambda i, f: (i, 0)), pl.BlockSpec((1, D_MODEL), lambda i, f: (0, 0)),
                  pl.BlockSpec((D_MODEL, tf), lambda i, f: (0, f)), pl.BlockSpec((D_MODEL, tf), lambda i, f: (0, f)),
                  pl.BlockSpec((tf, D_MODEL), lambda i, f: (f, 0)), pl.BlockSpec((1, D_MODEL), lambda i, f: (0, 0))],
        out_specs=pl.BlockSpec((tm, D_MODEL), lambda i, f: (i, 0)),
        out_shape=jax.ShapeDtypeStruct((T, D_MODEL), F32),
        scratch_shapes=[pltpu.VMEM((tm, D_MODEL), BF16), pltpu.VMEM((tm, D_MODEL), F32)],
        compiler_params=_cparams(("parallel", "arbitrary")),
        name="ffn_dense",
    )(x2, g.reshape(1, -1), wg.astype(BF16), wu.astype(BF16), wd.astype(BF16), g_final.reshape(1, -1))


def _moe_kernel(x_ref, g_ref, rw_ref, wg_ref, wu_ref, wd_ref, gf_ref, o_ref, h_sc, gate_sc, acc_sc, *,
                final_norm, tm):
    e = pl.program_id(1)
    lane = lax.broadcasted_iota(jnp.int32, (tm, LANES), 1)

    @pl.when(e == 0)
    def _():
        x = x_ref[...]
        ms = jnp.mean(x * x, axis=-1, keepdims=True)
        hf = (x * lax.rsqrt(ms + EPS)) * g_ref[...]
        h_sc[...] = hf.astype(BF16)
        acc_sc[...] = x
        h_hi = hf.astype(BF16)
        h_lo = (hf - h_hi.astype(F32)).astype(BF16)
        logits = (_dot(h_hi, rw_ref[0]) + _dot(h_lo, rw_ref[0]) + _dot(h_hi, rw_ref[1]))
        logits = jnp.where(lane < N_EXPERTS, logits, -jnp.inf)
        m1 = jnp.max(logits, axis=1, keepdims=True)
        i1 = jnp.min(jnp.where(logits == m1, lane, LANES), axis=1, keepdims=True)
        rest = jnp.where(lane == i1, -jnp.inf, logits)
        m2 = jnp.max(rest, axis=1, keepdims=True)
        i2 = jnp.min(jnp.where(rest == m2, lane, LANES), axis=1, keepdims=True)
        e2 = jnp.exp(m2 - m1)
        den = 1.0 + e2
        gate_sc[...] = jnp.where(lane == i1, 1.0 / den, 0.0) + jnp.where(lane == i2, e2 / den, 0.0)

    h = h_sc[...]
    gte = _dot(h, wg_ref[0])
    up = _dot(h, wu_ref[0])
    act = (gte * _sigmoid(gte)) * up
    y = _dot(act.astype(BF16), wd_ref[0])
    ge = jnp.sum(jnp.where(lane == e, gate_sc[...], 0.0), axis=1, keepdims=True)
    acc_sc[...] += ge * y

    @pl.when(e == pl.num_programs(1) - 1)
    def _():
        out = acc_sc[...]
        o_ref[...] = _final_norm(out, gf_ref) if final_norm else out


def _moe(x2, g, router_w, wg, wu, wd, g_final, final_norm, tm=512):
    T = x2.shape[0]
    rw = _pad_cols(router_w, LANES)
    rw_hi = rw.astype(BF16)
    rw_lo = (rw - rw_hi.astype(F32)).astype(BF16)
    rw2 = jnp.stack([rw_hi, rw_lo])
    return pl.pallas_call(
        functools.partial(_moe_kernel, final_norm=final_norm, tm=tm),
        grid=(T // tm, N_EXPERTS),
        in_specs=[pl.BlockSpec((tm, D_MODEL), lambda i, e: (i, 0)), pl.BlockSpec((1, D_MODEL), lambda i, e: (0, 0)),
                  pl.BlockSpec((2, D_MODEL, LANES), lambda i, e: (0, 0, 0)),
                  pl.BlockSpec((1, D_MODEL, D_FF_EXPERT), lambda i, e: (e, 0, 0)),
                  pl.BlockSpec((1, D_MODEL, D_FF_EXPERT), lambda i, e: (e, 0, 0)),
                  pl.BlockSpec((1, D_FF_EXPERT, D_MODEL), lambda i, e: (e, 0, 0)),
                  pl.BlockSpec((1, D_MODEL), lambda i, e: (0, 0))],
        out_specs=pl.BlockSpec((tm, D_MODEL), lambda i, e: (i, 0)),
        out_shape=jax.ShapeDtypeStruct((T, D_MODEL), F32),
        scratch_shapes=[pltpu.VMEM((tm, D_MODEL), BF16), pltpu.VMEM((tm, LANES), F32),
                        pltpu.VMEM((tm, D_MODEL), F32)],
        compiler_params=_cparams(("parallel", "arbitrary")),
        name="moe_dense",
    )(x2, g.reshape(1, -1), rw2, wg.astype(BF16), wu.astype(BF16), wd.astype(BF16), g_final.reshape(1, -1))


def _mixer(x2, batch, seq, tables, onehot, norm_g, w_in, conv_w, conv_b, wa, ba, wx, bx, lam,
           q_norm, w_uq, kv_norm, w_ukv, cmp_pos, cmp_w1, cmp_w2, gate_b, w_out):
    lru_xy, mla_dn, qn, kc, vc, kvsw, gates = _in_proj(x2, norm_g, _prep_w_in(w_in), tables, gate_b, seq)
    a_out = _lru(lru_xy, conv_w, conv_b, wa, ba, wx, bx, lam, batch, seq)

    q_m, k_m, v_m = _mla_prep(mla_dn, q_norm, w_uq, kv_norm, w_ukv, tables, seq)
    b3 = lambda t: t.reshape(batch, seq, t.shape[-1])
    b_out = _flash(b3(q_m), b3(k_m), b3(v_m), mode="causal", batch=batch, seq=seq, n_groups=MLA_HEADS // 2,
                   n_heads=2, q_slots=(0, 1), k_slots=(0, 1), q_width=256, k_width=256,
                   k_col=lambda g: g, v_col=lambda g: g, out_width=LANES,
                   scale=(MLA_NOPE + MLA_ROPE) ** -0.5, pair_out=True, out_dtype=BF16)

    kcmp, vcmp = _compress(kc, vc, cmp_pos, cmp_w1, cmp_w2, batch, seq)
    qn3 = b3(qn)
    o_c, selbias = _select(qn3, kcmp, vcmp, batch, seq)
    kvsw3 = b3(kvsw)
    nsa_kw = dict(batch=batch, seq=seq, n_groups=NSA_KV_HEADS, n_heads=NSA_GROUP, q_slots=(0, 1, 2),
                  k_slots=(0, 0, 0), q_width=NSA_GROUP * LANES, k_width=LANES, out_width=NSA_GROUP * LANES,
                  scale=1.0, pair_out=False, out_dtype=F32)
    o_s = _flash(qn3, kvsw3, kvsw3, mode="select", k_col=lambda g: 0, v_col=lambda g: 1,
                 selbias=selbias, onehot=onehot, **nsa_kw)
    o_w = _flash(qn3, kvsw3, kvsw3, mode="window", k_col=lambda g: 2, v_col=lambda g: 3, **nsa_kw)
    T = batch * seq
    return _out_proj(x2, a_out, b_out.reshape(T, 384), o_c.reshape(T, 768), o_s.reshape(T, 768),
                     o_w.reshape(T, 768), gates, w_out)


def kernel(x, norm_mix, w_in, lru_conv_w, lru_conv_b, lru_wa, lru_ba, lru_wx, lru_bx, lru_lambda, mla_q_norm, mla_w_uq, mla_kv_norm, mla_w_ukv, nsa_cmp_pos, nsa_cmp_w1, nsa_cmp_w2, nsa_gate_b, w_out, norm_ffn, ffn_w_gate, ffn_w_up, ffn_w_down, router_w, moe_w_gate, moe_w_up, moe_w_down, norm_final):
    batch, seq, _ = x.shape
    depth = norm_mix.shape[0]
    tables = _rope_tables(seq)
    blk_id = np.arange(seq)[:, None] // SEL_LEN
    onehot = jnp.asarray((blk_id == np.arange(LANES)[None, :]).astype(np.float32)).astype(BF16)
    x2 = x.reshape(batch * seq, D_MODEL)
    for l in range(depth):
        x2 = _mixer(x2, batch, seq, tables, onehot, norm_mix[l], w_in[l], lru_conv_w[l], lru_conv_b[l],
                    lru_wa[l], lru_ba[l], lru_wx[l], lru_bx[l], lru_lambda[l], mla_q_norm[l], mla_w_uq[l],
                    mla_kv_norm[l], mla_w_ukv[l], nsa_cmp_pos[l], nsa_cmp_w1[l], nsa_cmp_w2[l],
                    nsa_gate_b[l], w_out[l])
        last = l == depth - 1
        j = l // 2
        if l % 2 == 0:
            x2 = _ffn(x2, norm_ffn[l], ffn_w_gate[j], ffn_w_up[j], ffn_w_down[j], norm_final, last)
        else:
            x2 = _moe(x2, norm_ffn[l], router_w[j], moe_w_gate[j], moe_w_up[j], moe_w_down[j], norm_final, last)
    if depth == 0:
        raise ValueError("depth must be positive")
    return x2.reshape(batch, seq, D_MODEL)
```

```python
import functools
import math

import numpy as np
import jax
import jax.numpy as jnp
from jax import lax
from jax.experimental import pallas as pl
from jax.experimental.pallas import tpu as pltpu

F32 = jnp.float32
BF16 = jnp.bfloat16

D_MODEL = 1024
LRU_WIDTH = 256
LRU_BLOCKS = 4
LRU_BLOCK_W = LRU_WIDTH // LRU_BLOCKS
CONV_WIDTH = 4
LRU_C = 8.0
MLA_HEADS = 6
MLA_NOPE = 64
MLA_ROPE = 32
MLA_V = 64
MLA_Q_RANK = 192
MLA_KV_RANK = 128
NSA_HEADS = 6
NSA_KV_HEADS = 2
NSA_GROUP = NSA_HEADS // NSA_KV_HEADS
NSA_HEAD_DIM = 64
CMP_LEN = 32
CMP_STRIDE = 16
CMP_HIDDEN = 128
SEL_LEN = 64
SEL_TOPK = 16
WINDOW = 512
FORCE_BONUS = 1.0e3
D_FF = 2816
N_EXPERTS = 8
D_FF_EXPERT = 1408
ROPE_THETA = 10000.0
EPS = 1e-6
NEG_INF = -1.0e30

LANES = 128
VMEM_LIMIT = 56 * 1024 * 1024

C_LRU = 0
C_MLA = 512
C_KPE_ROT = 1024
C_QN = 1152
C_QN_ROT = 1920
C_KC = 2688
C_KC_ROT = 2816
C_VC = 2944
C_KS = 3072
C_KS_ROT = 3200
C_KW = 3328
C_KW_ROT = 3456
C_GATE = 3584
IN_COLS_PAD = 3712
LOG2E = 1.4426950408889634


def _cparams(sem):
    return pltpu.CompilerParams(dimension_semantics=sem, vmem_limit_bytes=VMEM_LIMIT)


def _gelu_tanh(x):
    return 0.5 * x * (1.0 + jnp.tanh(math.sqrt(2.0 / math.pi) * (x + 0.044715 * (x * x * x))))


def _sigmoid(x):
    return 1.0 / (1.0 + jnp.exp(-x))


def _dot(a, b):
    return jnp.dot(a, b, preferred_element_type=F32)


def _dot_nt(a, b):
    return lax.dot_general(a, b, (((1,), (1,)), ((), ())), preferred_element_type=F32)


def _rot_cols(w, half):
    return jnp.concatenate([-w[:, half:], w[:, :half]], axis=1)


def _pad_cols(w, n):
    return jnp.pad(w, ((0, 0), (0, n - w.shape[1])))


def _prep_w_in(w_in):
    splits = np.cumsum([256, 256, 192, 128, 32, 384, 128, 128, 128, 128, 128, 128, 18])[:-1].tolist()
    (x_l, y_l, q_dn, kv_dn, k_pe, q_n, kc, vc, ks, vs, kw, vw, g_n) = jnp.split(w_in, splits, axis=1)
    z = lambda n: jnp.zeros((w_in.shape[0], n), w_in.dtype)

    def kpe_slot(w):
        return jnp.concatenate([z(64), w, z(32)], axis=1)

    def qn_slots(fn):
        out = []
        for h in range(NSA_HEADS):
            g = h // NSA_GROUP
            wh = fn(q_n[:, h * 64:(h + 1) * 64])
            out.append(jnp.concatenate([z(64 * g), wh, z(64 * (1 - g))], axis=1))
        return jnp.concatenate(out, axis=1)

    def kv_rot(w):
        return jnp.concatenate([_rot_cols(w[:, :64], 32), _rot_cols(w[:, 64:], 32)], axis=1)

    cols = [x_l, y_l, _pad_cols(q_dn, 256), kv_dn, kpe_slot(k_pe), kpe_slot(_rot_cols(k_pe, 16)),
            qn_slots(lambda w: w), qn_slots(lambda w: _rot_cols(w, 32)),
            kc, kv_rot(kc), vc, ks, kv_rot(ks), kw, kv_rot(kw), _pad_cols(g_n, 128)]
    w = jnp.concatenate(cols, axis=1)
    assert w.shape[1] == IN_COLS_PAD
    w_vt = jnp.concatenate([vs, vw], axis=1).T
    return w.astype(BF16), w_vt.astype(BF16)


def _rope_tables(seq):
    pos = jnp.arange(seq, dtype=F32)[:, None]
    inv64 = ROPE_THETA ** (-jnp.arange(32, dtype=F32) * 2.0 / 64)
    inv32 = ROPE_THETA ** (-jnp.arange(16, dtype=F32) * 2.0 / 32)
    a64 = pos * inv64[None, :]
    a32 = pos * inv32[None, :]
    c64, s64 = jnp.cos(a64), jnp.sin(a64)
    c32, s32 = jnp.cos(a32), jnp.sin(a32)
    cos_n = jnp.concatenate([c64, c64, c64, c64], axis=1)
    sin_n = jnp.concatenate([s64, s64, s64, s64], axis=1)
    one = jnp.ones((seq, 64), F32)
    cos_m = jnp.concatenate([one, c32, c32, one[:, :32]], axis=1)
    sin_m = jnp.concatenate([0 * one, s32, s32, 0 * one[:, :32]], axis=1)
    return cos_n, sin_n, cos_m, sin_m


def _in_proj_kernel(x_ref, g_ref, w_ref, wvt_ref, cn_ref, sn_ref, cm_ref, sm_ref, gb_ref,
                    lru_ref, mla_ref, qn_ref, kc_ref, vc_ref, ksw_ref, gate_ref, vt_ref):
    x = x_ref[...]
    ms = jnp.mean(x * x, axis=-1, keepdims=True)
    h = ((x * lax.rsqrt(ms + EPS)) * g_ref[...]).astype(BF16)

    def proj(c0, n):
        return _dot(h, w_ref[:, c0:c0 + n])

    cn, sn = cn_ref[...], sn_ref[...]
    lru_ref[...] = proj(C_LRU, 512)
    mla_ref[:, 0:384] = proj(C_MLA, 384)
    mla_ref[:, 384:512] = proj(C_MLA + 384, 128) * cm_ref[...] + proj(C_KPE_ROT, 128) * sm_ref[...]
    scale = NSA_HEAD_DIM ** -0.5 * LOG2E
    for hd in range(NSA_HEADS):
        a = proj(C_QN + hd * LANES, LANES)
        b = proj(C_QN_ROT + hd * LANES, LANES)
        qn_ref[:, hd * LANES:(hd + 1) * LANES] = ((a * cn + b * sn) * scale).astype(BF16)
    kc_ref[...] = proj(C_KC, 128) * cn + proj(C_KC_ROT, 128) * sn
    vc_ref[...] = proj(C_VC, 128)
    ksw_ref[:, 0:128] = (proj(C_KS, 128) * cn + proj(C_KS_ROT, 128) * sn).astype(BF16)
    ksw_ref[:, 128:256] = (proj(C_KW, 128) * cn + proj(C_KW_ROT, 128) * sn).astype(BF16)
    gate_ref[...] = _sigmoid(proj(C_GATE, 128) + gb_ref[...])
    vt_ref[0] = _dot_nt(wvt_ref[...], h).astype(BF16)


def _in_proj(x2, g, w_cat, w_vt, tables, gate_b, batch, seq, tm=512):
    T = x2.shape[0]
    nt = T // tm
    npos = seq // tm
    cn, sn, cm, sm = tables
    row = lambda i: (i, 0)
    fixed = lambda i: (0, 0)
    posmap = lambda i: (i % npos, 0)
    tab_spec = pl.BlockSpec((tm, LANES), posmap)
    out_shapes = (
        jax.ShapeDtypeStruct((T, 512), F32),
        jax.ShapeDtypeStruct((T, 512), F32),
        jax.ShapeDtypeStruct((T, 768), BF16),
        jax.ShapeDtypeStruct((T, 128), F32),
        jax.ShapeDtypeStruct((T, 128), F32),
        jax.ShapeDtypeStruct((T, 256), BF16),
        jax.ShapeDtypeStruct((T, 128), F32),
    )
    vt_shape = jax.ShapeDtypeStruct((batch, 256, seq), BF16)
    return pl.pallas_call(
        _in_proj_kernel,
        grid=(nt,),
        in_specs=[pl.BlockSpec((tm, D_MODEL), row), pl.BlockSpec((1, D_MODEL), fixed),
                  pl.BlockSpec((D_MODEL, IN_COLS_PAD), fixed), pl.BlockSpec((256, D_MODEL), fixed),
                  tab_spec, tab_spec, tab_spec, tab_spec, pl.BlockSpec((1, LANES), fixed)],
        out_specs=[pl.BlockSpec((tm, s.shape[1]), row) for s in out_shapes]
        + [pl.BlockSpec((1, 256, tm), lambda i: (i // npos, 0, i % npos))],
        out_shape=out_shapes + (vt_shape,),
        compiler_params=_cparams(("parallel",)),
        name="in_proj",
    )(x2, g.reshape(1, -1), w_cat, w_vt, cn, sn, cm, sm, _pad_cols(gate_b.reshape(1, -1), LANES))


def _lru_kernel(xy_ref, cw_ref, cb_ref, wa_ref, ba_ref, wx_ref, bx_ref, lam_ref, out_ref,
                xe_sc, h_sc, *, tt):
    t = pl.program_id(1)

    @pl.when(t == 0)
    def _():
        xe_sc[0:8, :] = jnp.zeros((8, LRU_WIDTH), F32)
        h_sc[...] = jnp.zeros_like(h_sc)

    x = xy_ref[0, :, 0:LRU_WIDTH]
    y = xy_ref[0, :, LRU_WIDTH:2 * LRU_WIDTH]
    xe_sc[8:8 + tt, :] = x
    xc = cb_ref[...]
    for k in range(CONV_WIDTH):
        off = 8 - (CONV_WIDTH - 1) + k
        xc = xc + xe_sc[off:off + tt, :] * cw_ref[k:k + 1, :]
    xe_sc[0:8, :] = x[tt - 8:tt, :]

    xb = xc.astype(BF16)
    r = _sigmoid(_dot(xb, wa_ref[...]) + ba_ref[...])
    i = _sigmoid(_dot(xb, wx_ref[...]) + bx_ref[...])
    log_a = (-LRU_C * r) * jax.nn.softplus(-lam_ref[...])
    a = jnp.exp(log_a)
    b = jnp.sqrt(jnp.tanh(-log_a) * (a * a + 1.0)) * (i * xc)

    row = lax.broadcasted_iota(jnp.int32, (tt, LRU_WIDTH), 0)
    k = 1
    while k < tt:
        keep = row >= k
        a_sh = jnp.where(keep, pltpu.roll(a, k, 0), 1.0)
        b_sh = jnp.where(keep, pltpu.roll(b, k, 0), 0.0)
        b = a * b_sh + b
        a = a * a_sh
        k *= 2
    h = b + a * h_sc[0:1, :]
    h_sc[0:1, :] = h[tt - 1:tt, :]
    out_ref[0] = h * _gelu_tanh(y)


def _block_diag_dense(w):
    n, c, d = w.shape
    out = jnp.zeros((n * c, n * d), w.dtype)
    for j in range(n):
        out = out.at[j * c:(j + 1) * c, j * d:(j + 1) * d].set(w[j])
    return out


def _lru(lru_xy, conv_w, conv_b, wa, ba, wx, bx, lam, batch, seq, tt=512):
    xy = lru_xy.reshape(batch, seq, 512)
    fixed = lambda b, t: (0, 0)
    vec = pl.BlockSpec((1, LRU_WIDTH), fixed)
    mat = pl.BlockSpec((LRU_WIDTH, LRU_WIDTH), fixed)
    out = pl.pallas_call(
        functools.partial(_lru_kernel, tt=tt),
        grid=(batch, seq // tt),
        in_specs=[pl.BlockSpec((1, tt, 512), lambda b, t: (b, t, 0)),
                  pl.BlockSpec((CONV_WIDTH, LRU_WIDTH), fixed), vec, mat, vec, mat, vec, vec],
        out_specs=pl.BlockSpec((1, tt, LRU_WIDTH), lambda b, t: (b, t, 0)),
        out_shape=jax.ShapeDtypeStruct((batch, seq, LRU_WIDTH), F32),
        scratch_shapes=[pltpu.VMEM((tt + 8, LRU_WIDTH), F32), pltpu.VMEM((8, LRU_WIDTH), F32)],
        compiler_params=_cparams(("parallel", "arbitrary")),
        name="rg_lru",
    )(xy, conv_w, conv_b.reshape(1, -1), _block_diag_dense(wa).astype(BF16), ba.reshape(1, -1),
      _block_diag_dense(wx).astype(BF16), bx.reshape(1, -1), lam.reshape(1, -1))
    return out.reshape(batch * seq, LRU_WIDTH)


def _mla_prep_kernel(dn_ref, qn_ref, wqa_ref, wqb_ref, kvn_ref, wk_ref, wvt_ref, cm_ref, sm_ref,
                     q_ref, k_ref, vt_ref):
    q_dn = dn_ref[:, 0:256]
    ms = jnp.sum(q_dn * q_dn, axis=-1, keepdims=True) * (1.0 / MLA_Q_RANK)
    ql = ((q_dn * lax.rsqrt(ms + EPS)) * qn_ref[...]).astype(BF16)
    kv_dn = dn_ref[:, 256:384]
    ms = jnp.mean(kv_dn * kv_dn, axis=-1, keepdims=True)
    kvl = ((kv_dn * lax.rsqrt(ms + EPS)) * kvn_ref[...]).astype(BF16)
    kpe = dn_ref[:, 384:512]
    cm, sm = cm_ref[...], sm_ref[...]
    scale = (MLA_NOPE + MLA_ROPE) ** -0.5 * LOG2E
    for hd in range(MLA_HEADS):
        sl = slice(hd * LANES, (hd + 1) * LANES)
        q_ref[:, sl] = ((_dot(ql, wqa_ref[:, sl]) * cm + _dot(ql, wqb_ref[:, sl]) * sm) * scale).astype(BF16)
        k_ref[:, sl] = (_dot(kvl, wk_ref[:, sl]) + kpe).astype(BF16)
    vt_ref[0] = _dot_nt(wvt_ref[...], kvl).astype(BF16)


def _prep_mla_weights(w_uq, w_ukv):
    zq = jnp.zeros((MLA_Q_RANK, 32), w_uq.dtype)
    qa, qb, kk, vv = [], [], [], []
    for h in range(MLA_HEADS):
        wq = w_uq[:, h * 96:(h + 1) * 96]
        nope, ropew = wq[:, :64], wq[:, 64:]
        qa.append(jnp.concatenate([nope, ropew, zq], axis=1))
        qb.append(jnp.concatenate([0 * nope, _rot_cols(ropew, 16), zq], axis=1))
        wkv = w_ukv[:, h * 128:(h + 1) * 128]
        kk.append(_pad_cols(wkv[:, :64], LANES))
        vv.append(wkv[:, 64:])
    pad_rows = lambda w: jnp.pad(w, ((0, 256 - MLA_Q_RANK), (0, 0)))
    return (pad_rows(jnp.concatenate(qa, axis=1)).astype(BF16), pad_rows(jnp.concatenate(qb, axis=1)).astype(BF16),
            jnp.concatenate(kk, axis=1).astype(BF16), jnp.concatenate(vv, axis=1).T.astype(BF16))


def _mla_prep(mla_dn, q_norm, w_uq, kv_norm, w_ukv, tables, batch, seq, tm=512):
    T = mla_dn.shape[0]
    npos = seq // tm
    _, _, cm, sm = tables
    wqa, wqb, wk, wvt = _prep_mla_weights(w_uq, w_ukv)
    row = lambda i: (i, 0)
    fixed = lambda i: (0, 0)
    tab = pl.BlockSpec((tm, LANES), lambda i: (i % npos, 0))
    return pl.pallas_call(
        _mla_prep_kernel,
        grid=(T // tm,),
        in_specs=[pl.BlockSpec((tm, 512), row), pl.BlockSpec((1, 256), fixed),
                  pl.BlockSpec((256, 768), fixed), pl.BlockSpec((256, 768), fixed),
                  pl.BlockSpec((1, 128), fixed), pl.BlockSpec((128, 768), fixed),
                  pl.BlockSpec((384, 128), fixed), tab, tab],
        out_specs=[pl.BlockSpec((tm, 768), row), pl.BlockSpec((tm, 768), row),
                   pl.BlockSpec((1, 384, tm), lambda i: (i // npos, 0, i % npos))],
        out_shape=(jax.ShapeDtypeStruct((T, 768), BF16), jax.ShapeDtypeStruct((T, 768), BF16),
                   jax.ShapeDtypeStruct((batch, 384, seq), BF16)),
        compiler_params=_cparams(("parallel",)),
        name="mla_prep",
    )(mla_dn, _pad_cols(q_norm.reshape(1, -1), 256), wqa, wqb, kv_norm.reshape(1, -1), wk, wvt, cm, sm)


HEAD_V = 64


def _flash_kernel(qt_ref, kt_ref, first_ref, last_ref, *refs, mode, n_heads, q_slots, k_slots, v_rows,
                  tq, tk, pair_out):
    if mode == "select":
        q_ref, k_ref, vt_ref, sb_ref, oh_ref, o_ref, m_sc, l_sc, acc_sc = refs
    else:
        q_ref, k_ref, vt_ref, o_ref, m_sc, l_sc, acc_sc = refs
    step = pl.program_id(2)
    qt = qt_ref[step]
    kt = kt_ref[step]

    @pl.when(first_ref[step] == 1)
    def _():
        m_sc[...] = jnp.full_like(m_sc, NEG_INF)
        l_sc[...] = jnp.zeros_like(l_sc)
        acc_sc[...] = jnp.zeros_like(acc_sc)

    def update(masked):
        if masked:
            kpos = kt * tk + lax.broadcasted_iota(jnp.int32, (tk, tq), 0)
            qpos = qt * tq + lax.broadcasted_iota(jnp.int32, (tk, tq), 1)
            dist = qpos - kpos
            valid = (dist >= 0) & (dist < WINDOW) if mode == "window" else dist >= 0
        for j in range(n_heads):
            q = q_ref[0, :, q_slots[j] * LANES:(q_slots[j] + 1) * LANES]
            k = k_ref[0, :, k_slots[j] * LANES:(k_slots[j] + 1) * LANES]
            if mode == "select":
                q = jnp.concatenate([q, sb_ref[0, 0]], axis=1)
                k = jnp.concatenate([k, oh_ref[...]], axis=1)
            s = _dot_nt(k, q)
            if masked:
                s = jnp.where(valid, s, NEG_INF)
            m_prev = m_sc[j]
            m_new = jnp.maximum(m_prev, jnp.max(s, axis=0, keepdims=True))
            alpha = jnp.exp2(m_prev - m_new)
            p = jnp.exp2(s - m_new)
            l_sc[j] = alpha * l_sc[j] + jnp.sum(p, axis=0, keepdims=True)
            vt = vt_ref[0, v_rows[j] * HEAD_V:(v_rows[j] + 1) * HEAD_V, :]
            acc_sc[j] = alpha * acc_sc[j] + _dot(vt, p.astype(BF16))
            m_sc[j] = m_new

    if mode == "window":
        update(True)
    else:
        pl.when(kt == qt)(lambda: update(True))
        pl.when(kt != qt)(lambda: update(False))

    @pl.when(last_ref[step] == 1)
    def _():
        outs = [acc_sc[j] / l_sc[j] for j in range(n_heads)]
        if pair_out:
            o_ref[0] = jnp.concatenate(outs, axis=0).T.astype(o_ref.dtype)
        else:
            zero = jnp.zeros((LANES - HEAD_V, tq), F32)
            for j in range(n_heads):
                o_ref[0, :, j * LANES:(j + 1) * LANES] = jnp.concatenate([outs[j], zero], axis=0).T.astype(o_ref.dtype)


def _pair_tables(nq, mode, tq, tk):
    qts, kts, first, last = [], [], [], []
    for qt in range(nq):
        hi = (qt * tq + tq - 1) // tk
        lo = 0 if mode != "window" else max(0, (qt * tq - (WINDOW - 1)) // tk)
        for kt in range(lo, hi + 1):
            qts.append(qt)
            kts.append(kt)
            first.append(1 if kt == lo else 0)
            last.append(1 if kt == hi else 0)
    arr = lambda v: jnp.asarray(np.array(v, dtype=np.int32))
    return arr(qts), arr(kts), arr(first), arr(last)


def _flash(q, k, vt, *, mode, batch, seq, n_groups, n_heads, q_slots, k_slots, v_rows, q_width, k_width,
           vt_rows, k_col, vt_row, out_width, pair_out, out_dtype, selbias=None, onehot=None, tq=512, tk=512):
    assert tq == tk
    nq = seq // tq
    qt_tab, kt_tab, first_tab, last_tab = _pair_tables(nq, mode, tq, tk)
    n_steps = int(qt_tab.shape[0])
    in_specs = [
        pl.BlockSpec((1, tq, q_width), lambda b, g, s, qt, kt, f, l: (b, qt[s], g)),
        pl.BlockSpec((1, tk, k_width), lambda b, g, s, qt, kt, f, l: (b, kt[s], k_col(g))),
        pl.BlockSpec((1, vt_rows, tk), lambda b, g, s, qt, kt, f, l: (b, vt_row(g), kt[s])),
    ]
    args = [q, k, vt]
    if mode == "select":
        in_specs.append(pl.BlockSpec((1, 1, tq, LANES), lambda b, g, s, qt, kt, f, l: (b, g, qt[s], 0)))
        in_specs.append(pl.BlockSpec((tk, LANES), lambda b, g, s, qt, kt, f, l: (kt[s], 0)))
        args += [selbias, onehot]
    kern = functools.partial(_flash_kernel, mode=mode, n_heads=n_heads, q_slots=q_slots, k_slots=k_slots,
                             v_rows=v_rows, tq=tq, tk=tk, pair_out=pair_out)
    return pl.pallas_call(
        kern,
        grid_spec=pltpu.PrefetchScalarGridSpec(
            num_scalar_prefetch=4,
            grid=(batch, n_groups, n_steps),
            in_specs=in_specs,
            out_specs=pl.BlockSpec((1, tq, out_width), lambda b, g, s, qt, kt, f, l: (b, qt[s], g)),
            scratch_shapes=[pltpu.VMEM((n_heads, 1, tq), F32), pltpu.VMEM((n_heads, 1, tq), F32),
                            pltpu.VMEM((n_heads, HEAD_V, tq), F32)],
        ),
        out_shape=jax.ShapeDtypeStruct((batch, seq, n_groups * out_width), out_dtype),
        compiler_params=_cparams(("parallel", "parallel", "arbitrary")),
        name="flash_" + mode,
    )(qt_tab, kt_tab, first_tab, last_tab, *args)


def _compress_kernel(kc_ref, vc_ref, pos_ref, w1a_ref, w1b_ref, w2_ref, ko_ref, vo_ref, *, nchunk):
    row = lax.broadcasted_iota(jnp.int32, (nchunk, 2 * LANES), 0)
    for br, (src, dst) in enumerate(((kc_ref, ko_ref), (vc_ref, vo_ref))):
        r = src[0]
        pa = _dot((r + pos_ref[br, 0:1, :]).astype(BF16), w1a_ref[br])
        pb = _dot((r + pos_ref[br, 1:2, :]).astype(BF16), w1b_ref[br])
        hid = _gelu_tanh(pa + pltpu.roll(pb, nchunk - 1, 0))
        out = _dot(hid.astype(BF16), w2_ref[br])
        dst[0] = jnp.where(row < nchunk - 1, out, 0.0).astype(dst.dtype)


def _prep_compress_weights(cmp_pos, cmp_w1, cmp_w2):
    half = CMP_LEN // 2
    pos, w1a, w1b, w2 = [], [], [], []
    for br in range(2):
        p = cmp_pos[br]
        tile = lambda ph: jnp.concatenate([ph, ph], axis=1).reshape(1, half * LANES)
        pos.append(jnp.concatenate([tile(p[:half]), tile(p[half:])], axis=0))
        w = cmp_w1[br].reshape(CMP_LEN, NSA_HEAD_DIM, CMP_HIDDEN)
        z = jnp.zeros_like(w[:half])

        def big(wh):
            g0 = jnp.concatenate([wh, z], axis=1)
            g1 = jnp.concatenate([z, wh], axis=1)
            return jnp.concatenate([g0.reshape(half * LANES, CMP_HIDDEN), g1.reshape(half * LANES, CMP_HIDDEN)], axis=1)

        w1a.append(big(w[:half]))
        w1b.append(big(w[half:]))
        zc = jnp.zeros((CMP_HIDDEN, NSA_HEAD_DIM), cmp_w2.dtype)
        top = [cmp_w2[br], zc, zc, zc]
        bot = [zc, cmp_w2[br], zc, zc] if br == 0 else [zc, zc, cmp_w2[br], zc]
        w2.append(jnp.concatenate([jnp.concatenate(top, axis=1), jnp.concatenate(bot, axis=1)], axis=0))
    return (jnp.stack(pos), jnp.stack(w1a).astype(BF16), jnp.stack(w1b).astype(BF16), jnp.stack(w2).astype(BF16))


def _compress(kc, vc, cmp_pos, cmp_w1, cmp_w2, batch, seq):
    nchunk = seq // CMP_STRIDE
    width = CMP_STRIDE * LANES
    pos, w1a, w1b, w2 = _prep_compress_weights(cmp_pos, cmp_w1, cmp_w2)
    blk = pl.BlockSpec((1, nchunk, width), lambda b: (b, 0, 0))
    full3 = lambda shape: pl.BlockSpec(shape, lambda b: (0, 0, 0))
    oblk = pl.BlockSpec((1, nchunk, 2 * LANES), lambda b: (b, 0, 0))
    return pl.pallas_call(
        functools.partial(_compress_kernel, nchunk=nchunk),
        grid=(batch,),
        in_specs=[blk, blk, full3(pos.shape), full3(w1a.shape), full3(w1b.shape), full3(w2.shape)],
        out_specs=[oblk, oblk],
        out_shape=(jax.ShapeDtypeStruct((batch, nchunk, 2 * LANES), BF16),) * 2,
        compiler_params=_cparams(("parallel",)),
        name="nsa_compress",
    )(kc.reshape(batch, nchunk, width), vc.reshape(batch, nchunk, width), pos, w1a, w1b, w2)


def _select_kernel(q_ref, kc_ref, vc_ref, ov_ref, oc_ref, sb_ref, *, tq, nchunk):
    nsel = LANES
    qt = pl.program_id(2)
    kc = kc_ref[0]
    vc = vc_ref[0]
    qpos = qt * tq + lax.broadcasted_iota(jnp.int32, (tq, nchunk), 0)
    cend = lax.broadcasted_iota(jnp.int32, (tq, nchunk), 1) * CMP_STRIDE + (CMP_LEN - 1)
    cvalid = cend <= qpos
    psum = jnp.zeros((tq, nchunk), F32)
    for j in range(NSA_GROUP):
        q = q_ref[0, :, j * LANES:(j + 1) * LANES]
        s = jnp.where(cvalid, _dot_nt(q, kc), NEG_INF)
        m = jnp.max(s, axis=1, keepdims=True)
        e = jnp.where(cvalid, jnp.exp2(s - m), 0.0)
        p = e / jnp.maximum(jnp.sum(e, axis=1, keepdims=True), 1e-30)
        oc_ref[0, :, j * LANES:(j + 1) * LANES] = _dot(p.astype(BF16), vc)
        psum = psum + p
    ov = ov_ref[...]
    hi = psum.astype(BF16)
    r1 = psum - hi.astype(F32)
    mid = r1.astype(BF16)
    lo = (r1 - mid.astype(F32)).astype(BF16)
    imp = _dot(hi, ov) + _dot(mid, ov) + _dot(lo, ov)
    score = imp.T
    blk = lax.broadcasted_iota(jnp.int32, (nsel, tq), 0)
    qp = qt * tq + lax.broadcasted_iota(jnp.int32, (nsel, tq), 1)
    valid = blk * SEL_LEN <= qp
    forced = (blk == 0) | (blk == qp // SEL_LEN)
    score = jnp.where(valid, score, -1.0)
    score = jnp.where(forced, score + FORCE_BONUS, score)
    sel = jnp.zeros((nsel, tq), F32)
    for _ in range(min(SEL_TOPK, nsel)):
        m = jnp.max(score, axis=0, keepdims=True)
        idx = jnp.min(jnp.where(score == m, blk, nsel), axis=0, keepdims=True)
        hit = blk == idx
        sel = jnp.where(hit & (m >= 0.0), 1.0, sel)
        score = jnp.where(hit, -jnp.inf, score)
    bias = jnp.where(sel > 0.0, 0.0, NEG_INF)
    sb_ref[0, 0] = bias.T.astype(sb_ref.dtype)


def _overlap_matrix(nchunk, nsel):
    n = np.arange(nchunk)[:, None]
    s = np.arange(nsel)[None, :]
    cs = n * CMP_STRIDE
    ov = (cs < s * SEL_LEN + SEL_LEN) & (cs + CMP_LEN - 1 >= s * SEL_LEN) & (n < nchunk - 1)
    return jnp.asarray(ov.astype(np.float32)).astype(BF16)


def _select(qn, kcmp, vcmp, batch, seq, tq=256):
    nchunk = seq // CMP_STRIDE
    nsel = LANES
    assert seq // SEL_LEN <= nsel
    gw = NSA_GROUP * LANES
    ov = _overlap_matrix(nchunk, nsel)
    kblk = pl.BlockSpec((1, nchunk, LANES), lambda b, g, t: (b, 0, 0))
    vblk = pl.BlockSpec((1, nchunk, LANES), lambda b, g, t: (b, 0, g))
    return pl.pallas_call(
        functools.partial(_select_kernel, tq=tq, nchunk=nchunk),
        grid=(batch, NSA_KV_HEADS, seq // tq),
        in_specs=[pl.BlockSpec((1, tq, gw), lambda b, g, t: (b, t, g)), kblk, vblk,
                  pl.BlockSpec((nchunk, nsel), lambda b, g, t: (0, 0))],
        out_specs=[pl.BlockSpec((1, tq, gw), lambda b, g, t: (b, t, g)),
                   pl.BlockSpec((1, 1, tq, nsel), lambda b, g, t: (b, g, t, 0))],
        out_shape=(jax.ShapeDtypeStruct((batch, seq, NSA_KV_HEADS * gw), F32),
                   jax.ShapeDtypeStruct((batch, NSA_KV_HEADS, seq, nsel), BF16)),
        compiler_params=_cparams(("parallel", "parallel", "parallel")),
        name="nsa_select",
    )(qn, kcmp, vcmp, ov)


def _out_proj_kernel(x_ref, a_ref, b_ref, oc_ref, os_ref, ow_ref, gt_ref, wa_ref, wb_ref, wc_ref, o_ref):
    acc = x_ref[...] + _dot(a_ref[...].astype(BF16), wa_ref[...])
    acc = acc + _dot(b_ref[...], wb_ref[...])
    gt = gt_ref[...]
    for hd in range(NSA_HEADS):
        sl = slice(hd * LANES, (hd + 1) * LANES)
        c = (gt[:, 3 * hd:3 * hd + 1] * oc_ref[:, sl] + gt[:, 3 * hd + 1:3 * hd + 2] * os_ref[:, sl]
             + gt[:, 3 * hd + 2:3 * hd + 3] * ow_ref[:, sl])
        acc = acc + _dot(c.astype(BF16), wc_ref[sl, :])
    o_ref[...] = acc


def _out_proj(x2, a_out, b_out, o_c, o_s, o_w, gates, w_out, tm=512):
    T = x2.shape[0]
    wa = w_out[0:256].astype(BF16)
    wb = w_out[256:640].astype(BF16)
    wc_rows = []
    zero = jnp.zeros((64, D_MODEL), w_out.dtype)
    for h in range(NSA_HEADS):
        wh = w_out[640 + h * 64:640 + (h + 1) * 64]
        wc_rows += [wh, zero]
    wc = jnp.concatenate(wc_rows, axis=0).astype(BF16)
    row = lambda i: (i, 0)
    fixed = lambda i: (0, 0)
    return pl.pallas_call(
        _out_proj_kernel,
        grid=(T // tm,),
        in_specs=[pl.BlockSpec((tm, D_MODEL), row), pl.BlockSpec((tm, 256), row), pl.BlockSpec((tm, 384), row),
                  pl.BlockSpec((tm, 768), row), pl.BlockSpec((tm, 768), row), pl.BlockSpec((tm, 768), row),
                  pl.BlockSpec((tm, 128), row), pl.BlockSpec((256, D_MODEL), fixed),
                  pl.BlockSpec((384, D_MODEL), fixed), pl.BlockSpec((768, D_MODEL), fixed)],
        out_specs=pl.BlockSpec((tm, D_MODEL), row),
        out_shape=jax.ShapeDtypeStruct((T, D_MODEL), F32),
        compiler_params=_cparams(("parallel",)),
        name="out_proj",
    )(x2, a_out, b_out, o_c, o_s, o_w, gates, wa, wb, wc)


def _final_norm(y, gf_ref):
    ms = jnp.mean(y * y, axis=-1, keepdims=True)
    return (y * lax.rsqrt(ms + EPS)) * gf_ref[...]


def _ffn_kernel(x_ref, g_ref, wg_ref, wu_ref, wd_ref, gf_ref, o_ref, h_sc, acc_sc, *, final_norm):
    f = pl.program_id(1)

    @pl.when(f == 0)
    def _():
        x = x_ref[...]
        ms = jnp.mean(x * x, axis=-1, keepdims=True)
        h_sc[...] = ((x * lax.rsqrt(ms + EPS)) * g_ref[...]).astype(BF16)
        acc_sc[...] = x

    h = h_sc[...]
    gte = _dot(h, wg_ref[...])
    up = _dot(h, wu_ref[...])
    act = (gte * _sigmoid(gte)) * up
    acc_sc[...] += _dot(act.astype(BF16), wd_ref[...])

    @pl.when(f == pl.num_programs(1) - 1)
    def _():
        y = acc_sc[...]
        o_ref[...] = _final_norm(y, gf_ref) if final_norm else y


def _ffn(x2, g, wg, wu, wd, g_final, final_norm, tm=512, tf=1408):
    T = x2.shape[0]
    nf = D_FF // tf
    return pl.pallas_call(
        functools.partial(_ffn_kernel, final_norm=final_norm),
        grid=(T // tm, nf),
        in_specs=[pl.BlockSpec((tm, D_MODEL), lambda i, f: (i, 0)), pl.BlockSpec((1, D_MODEL), lambda i, f: (0, 0)),
                  pl.BlockSpec((D_MODEL, tf), lambda i, f: (0, f)), pl.BlockSpec((D_MODEL, tf), lambda i, f: (0, f)),
                  pl.BlockSpec((tf, D_MODEL), lambda i, f: (f, 0)), pl.BlockSpec((1, D_MODEL), lambda i, f: (0, 0))],
        out_specs=pl.BlockSpec((tm, D_MODEL), lambda i, f: (i, 0)),
        out_shape=jax.ShapeDtypeStruct((T, D_MODEL), F32),
        scratch_shapes=[pltpu.VMEM((tm, D_MODEL), BF16), pltpu.VMEM((tm, D_MODEL), F32)],
        compiler_params=_cparams(("parallel", "arbitrary")),
        name="ffn_dense",
    )(x2, g.reshape(1, -1), wg.astype(BF16), wu.astype(BF16), wd.astype(BF16), g_final.reshape(1, -1))


def _moe_kernel(x_ref, g_ref, rw_ref, wg_ref, wu_ref, wd_ref, gf_ref, o_ref, h_sc, gate_sc, acc_sc, *,
                final_norm, tm):
    e = pl.program_id(1)
    lane = lax.broadcasted_iota(jnp.int32, (tm, LANES), 1)

    @pl.when(e == 0)
    def _():
        x = x_ref[...]
        ms = jnp.mean(x * x, axis=-1, keepdims=True)
        hf = (x * lax.rsqrt(ms + EPS)) * g_ref[...]
        h_sc[...] = hf.astype(BF16)
        acc_sc[...] = x
        h_hi = hf.astype(BF16)
        h_lo = (hf - h_hi.astype(F32)).astype(BF16)
        logits = (_dot(h_hi, rw_ref[0]) + _dot(h_lo, rw_ref[0]) + _dot(h_hi, rw_ref[1]))
        logits = jnp.where(lane < N_EXPERTS, logits, -jnp.inf)
        m1 = jnp.max(logits, axis=1, keepdims=True)
        i1 = jnp.min(jnp.where(logits == m1, lane, LANES), axis=1, keepdims=True)
        rest = jnp.where(lane == i1, -jnp.inf, logits)
        m2 = jnp.max(rest, axis=1, keepdims=True)
        i2 = jnp.min(jnp.where(rest == m2, lane, LANES), axis=1, keepdims=True)
        e2 = jnp.exp(m2 - m1)
        den = 1.0 + e2
        gate_sc[...] = jnp.where(lane == i1, 1.0 / den, 0.0) + jnp.where(lane == i2, e2 / den, 0.0)

    h = h_sc[...]
    gte = _dot(h, wg_ref[0])
    up = _dot(h, wu_ref[0])
    act = (gte * _sigmoid(gte)) * up
    y = _dot(act.astype(BF16), wd_ref[0])
    ge = jnp.sum(jnp.where(lane == e, gate_sc[...], 0.0), axis=1, keepdims=True)
    acc_sc[...] += ge * y

    @pl.when(e == pl.num_programs(1) - 1)
    def _():
        out = acc_sc[...]
        o_ref[...] = _final_norm(out, gf_ref) if final_norm else out


def _moe(x2, g, router_w, wg, wu, wd, g_final, final_norm, tm=512):
    T = x2.shape[0]
    rw = _pad_cols(router_w, LANES)
    rw_hi = rw.astype(BF16)
    rw_lo = (rw - rw_hi.astype(F32)).astype(BF16)
    rw2 = jnp.stack([rw_hi, rw_lo])
    return pl.pallas_call(
        functools.partial(_moe_kernel, final_norm=final_norm, tm=tm),
        grid=(T // tm, N_EXPERTS),
        in_specs=[pl.BlockSpec((tm, D_MODEL), lambda i, e: (i, 0)), pl.BlockSpec((1, D_MODEL), lambda i, e: (0, 0)),
                  pl.BlockSpec((2, D_MODEL, LANES), lambda i, e: (0, 0, 0)),
                  pl.BlockSpec((1, D_MODEL, D_FF_EXPERT), lambda i, e: (e, 0, 0)),
                  pl.BlockSpec((1, D_MODEL, D_FF_EXPERT), lambda i, e: (e, 0, 0)),
                  pl.BlockSpec((1, D_FF_EXPERT, D_MODEL), lambda i, e: (e, 0, 0)),
                  pl.BlockSpec((1, D_MODEL), lambda i, e: (0, 0))],
        out_specs=pl.BlockSpec((tm, D_MODEL), lambda i, e: (i, 0)),
        out_shape=jax.ShapeDtypeStruct((T, D_MODEL), F32),
        scratch_shapes=[pltpu.VMEM((tm, D_MODEL), BF16), pltpu.VMEM((tm, LANES), F32),
                        pltpu.VMEM((tm, D_MODEL), F32)],
        compiler_params=_cparams(("parallel", "arbitrary")),
        name="moe_dense",
    )(x2, g.reshape(1, -1), rw2, wg.astype(BF16), wu.astype(BF16), wd.astype(BF16), g_final.reshape(1, -1))


def _mixer(x2, batch, seq, tables, onehot, norm_g, w_in, conv_w, conv_b, wa, ba, wx, bx, lam,
           q_norm, w_uq, kv_norm, w_ukv, cmp_pos, cmp_w1, cmp_w2, gate_b, w_out):
    w_cat, w_vt = _prep_w_in(w_in)
    lru_xy, mla_dn, qn, kc, vc, ksw, gates, vt_sw = _in_proj(x2, norm_g, w_cat, w_vt, tables, gate_b, batch, seq)
    a_out = _lru(lru_xy, conv_w, conv_b, wa, ba, wx, bx, lam, batch, seq)

    q_m, k_m, vt_m = _mla_prep(mla_dn, q_norm, w_uq, kv_norm, w_ukv, tables, batch, seq)
    b3 = lambda t: t.reshape(batch, seq, t.shape[-1])
    b_out = _flash(b3(q_m), b3(k_m), vt_m, mode="causal", batch=batch, seq=seq, n_groups=MLA_HEADS // 2,
                   n_heads=2, q_slots=(0, 1), k_slots=(0, 1), v_rows=(0, 1), q_width=256, k_width=256,
                   vt_rows=2 * HEAD_V, k_col=lambda g: g, vt_row=lambda g: g, out_width=LANES,
                   pair_out=True, out_dtype=BF16)

    kcmp, vcmp = _compress(kc, vc, cmp_pos, cmp_w1, cmp_w2, batch, seq)
    qn3 = b3(qn)
    o_c, selbias = _select(qn3, kcmp, vcmp, batch, seq)
    ksw3 = b3(ksw)
    nsa_kw = dict(batch=batch, seq=seq, n_groups=NSA_KV_HEADS, n_heads=NSA_GROUP, q_slots=(0, 1, 2),
                  k_slots=(0, 0, 0), v_rows=(0, 0, 0), q_width=NSA_GROUP * LANES, k_width=LANES,
                  vt_rows=HEAD_V, out_width=NSA_GROUP * LANES, pair_out=False, out_dtype=F32)
    o_s = _flash(qn3, ksw3, vt_sw, mode="select", k_col=lambda g: 0, vt_row=lambda g: g,
                 selbias=selbias, onehot=onehot, **nsa_kw)
    o_w = _flash(qn3, ksw3, vt_sw, mode="window", k_col=lambda g: 1, vt_row=lambda g: NSA_KV_HEADS + g, **nsa_kw)
    T = batch * seq
    return _out_proj(x2, a_out, b_out.reshape(T, 384), o_c.reshape(T, 768), o_s.reshape(T, 768),
                     o_w.reshape(T, 768), gates, w_out)


def kernel(x, norm_mix, w_in, lru_conv_w, lru_conv_b, lru_wa, lru_ba, lru_wx, lru_bx, lru_lambda, mla_q_norm, mla_w_uq, mla_kv_norm, mla_w_ukv, nsa_cmp_pos, nsa_cmp_w1, nsa_cmp_w2, nsa_gate_b, w_out, norm_ffn, ffn_w_gate, ffn_w_up, ffn_w_down, router_w, moe_w_gate, moe_w_up, moe_w_down, norm_final):
    batch, seq, _ = x.shape
    depth = norm_mix.shape[0]
    tables = _rope_tables(seq)
    blk_id = np.arange(seq)[:, None] // SEL_LEN
    onehot = jnp.asarray((blk_id == np.arange(LANES)[None, :]).astype(np.float32)).astype(BF16)
    x2 = x.reshape(batch * seq, D_MODEL)
    for l in range(depth):
        x2 = _mixer(x2, batch, seq, tables, onehot, norm_mix[l], w_in[l], lru_conv_w[l], lru_conv_b[l],
                    lru_wa[l], lru_ba[l], lru_wx[l], lru_bx[l], lru_lambda[l], mla_q_norm[l], mla_w_uq[l],
                    mla_kv_norm[l], mla_w_ukv[l], nsa_cmp_pos[l], nsa_cmp_w1[l], nsa_cmp_w2[l],
                    nsa_gate_b[l], w_out[l])
        last = l == depth - 1
        j = l // 2
        if l % 2 == 0:
            x2 = _ffn(x2, norm_ffn[l], ffn_w_gate[j], ffn_w_up[j], ffn_w_down[j], norm_final, last)
        else:
            x2 = _moe(x2, norm_ffn[l], router_w[j], moe_w_gate[j], moe_w_up[j], moe_w_down[j], norm_final, last)
    if depth == 0:
        raise ValueError("depth must be positive")
    return x2.reshape(batch, seq, D_MODEL)
```

```python
import functools
import math

import numpy as np
import jax
import jax.numpy as jnp
from jax import lax
from jax.experimental import pallas as pl
from jax.experimental.pallas import tpu as pltpu

F32 = jnp.float32
BF16 = jnp.bfloat16

D_MODEL = 1024
LRU_WIDTH = 256
LRU_BLOCKS = 4
LRU_BLOCK_W = LRU_WIDTH // LRU_BLOCKS
CONV_WIDTH = 4
LRU_C = 8.0
MLA_HEADS = 6
MLA_NOPE = 64
MLA_ROPE = 32
MLA_V = 64
MLA_Q_RANK = 192
MLA_KV_RANK = 128
NSA_HEADS = 6
NSA_KV_HEADS = 2
NSA_GROUP = NSA_HEADS // NSA_KV_HEADS
NSA_HEAD_DIM = 64
CMP_LEN = 32
CMP_STRIDE = 16
CMP_HIDDEN = 128
SEL_LEN = 64
SEL_TOPK = 16
WINDOW = 512
FORCE_BONUS = 1.0e3
D_FF = 2816
N_EXPERTS = 8
D_FF_EXPERT = 1408
ROPE_THETA = 10000.0
EPS = 1e-6
NEG_INF = -1.0e30

LANES = 128
VMEM_LIMIT = 56 * 1024 * 1024

C_LRU = 0
C_MLA = 512
C_KPE_ROT = 1024
C_QN = 1152
C_QN_ROT = 1920
C_KC = 2688
C_KC_ROT = 2816
C_VC = 2944
C_KS = 3072
C_KS_ROT = 3200
C_KW = 3328
C_KW_ROT = 3456
IN_COLS_PAD = 3584
N_ATT_HEADS = 6
HEAD_V = 64
LONG_KEY_TILE = 1024
SUM_ROWS = 16
GATE_ROWS = 32
LOG2E = 1.4426950408889634


def _cparams(sem, flags=None):
    return pltpu.CompilerParams(dimension_semantics=sem, vmem_limit_bytes=VMEM_LIMIT, flags=flags)


def _gelu_tanh(x):
    return 0.5 * x * (1.0 + jnp.tanh(math.sqrt(2.0 / math.pi) * (x + 0.044715 * (x * x * x))))


def _sigmoid(x):
    return 1.0 / (1.0 + jnp.exp(-x))


def _dot(a, b):
    return jnp.dot(a, b, preferred_element_type=F32)


def _dot_nt(a, b):
    return lax.dot_general(a, b, (((1,), (1,)), ((), ())), preferred_element_type=F32)


def _rot_cols(w, half):
    return jnp.concatenate([-w[:, half:], w[:, :half]], axis=1)


def _pad_cols(w, n):
    return jnp.pad(w, ((0, 0), (0, n - w.shape[1])))


def _prep_w_in(w_in):
    splits = np.cumsum([256, 256, 192, 128, 32, 384, 128, 128, 128, 128, 128, 128, 18])[:-1].tolist()
    (x_l, y_l, q_dn, kv_dn, k_pe, q_n, kc, vc, ks, vs, kw, vw, g_n) = jnp.split(w_in, splits, axis=1)
    z = lambda n: jnp.zeros((w_in.shape[0], n), w_in.dtype)

    def kpe_slot(w):
        return jnp.concatenate([z(64), w, z(32)], axis=1)

    def qn_slots(fn):
        out = []
        for h in range(NSA_HEADS):
            g = h // NSA_GROUP
            wh = fn(q_n[:, h * 64:(h + 1) * 64])
            out.append(jnp.concatenate([z(64 * g), wh, z(64 * (1 - g))], axis=1))
        return jnp.concatenate(out, axis=1)

    def kv_rot(w):
        return jnp.concatenate([_rot_cols(w[:, :64], 32), _rot_cols(w[:, 64:], 32)], axis=1)

    cols = [x_l, y_l, _pad_cols(q_dn, 256), kv_dn, kpe_slot(k_pe), kpe_slot(_rot_cols(k_pe, 16)),
            qn_slots(lambda w: w), qn_slots(lambda w: _rot_cols(w, 32)),
            kc, kv_rot(kc), vc, ks, kv_rot(ks), kw, kv_rot(kw)]
    w = jnp.concatenate(cols, axis=1)
    assert w.shape[1] == IN_COLS_PAD
    w_t = jnp.concatenate([vs, vw, _pad_cols(g_n, GATE_ROWS)], axis=1).T
    return w.astype(BF16), w_t.astype(BF16)


def _rope_tables(seq):
    pos = jnp.arange(seq, dtype=F32)[:, None]
    inv64 = ROPE_THETA ** (-jnp.arange(32, dtype=F32) * 2.0 / 64)
    inv32 = ROPE_THETA ** (-jnp.arange(16, dtype=F32) * 2.0 / 32)
    a64 = pos * inv64[None, :]
    a32 = pos * inv32[None, :]
    c64, s64 = jnp.cos(a64), jnp.sin(a64)
    c32, s32 = jnp.cos(a32), jnp.sin(a32)
    cos_n = jnp.concatenate([c64, c64, c64, c64], axis=1)
    sin_n = jnp.concatenate([s64, s64, s64, s64], axis=1)
    one = jnp.ones((seq, 64), F32)
    cos_m = jnp.concatenate([one, c32, c32, one[:, :32]], axis=1)
    sin_m = jnp.concatenate([0 * one, s32, s32, 0 * one[:, :32]], axis=1)
    return cos_n, sin_n, cos_m, sin_m


def _in_proj_kernel(x_ref, g_ref, w_ref, wt_ref, cn_ref, sn_ref, cm_ref, sm_ref, gb_ref,
                    lru_ref, mla_ref, qn_ref, kc_ref, vc_ref, ksw_ref, vt_ref, gt_ref):
    x = x_ref[...]
    ms = jnp.mean(x * x, axis=-1, keepdims=True)
    h = ((x * lax.rsqrt(ms + EPS)) * g_ref[...]).astype(BF16)

    def proj(c0, n):
        return _dot(h, w_ref[:, c0:c0 + n])

    cn, sn = cn_ref[...], sn_ref[...]
    lru_ref[...] = proj(C_LRU, 512)
    mla_ref[:, 0:384] = proj(C_MLA, 384)
    mla_ref[:, 384:512] = proj(C_MLA + 384, 128) * cm_ref[...] + proj(C_KPE_ROT, 128) * sm_ref[...]
    scale = NSA_HEAD_DIM ** -0.5 * LOG2E
    for hd in range(NSA_HEADS):
        a = proj(C_QN + hd * LANES, LANES)
        b = proj(C_QN_ROT + hd * LANES, LANES)
        qn_ref[:, hd * LANES:(hd + 1) * LANES] = ((a * cn + b * sn) * scale).astype(BF16)
    kc_ref[...] = proj(C_KC, 128) * cn + proj(C_KC_ROT, 128) * sn
    vc_ref[...] = proj(C_VC, 128)
    ksw_ref[:, 0:128] = (proj(C_KS, 128) * cn + proj(C_KS_ROT, 128) * sn).astype(BF16)
    ksw_ref[:, 128:256] = (proj(C_KW, 128) * cn + proj(C_KW_ROT, 128) * sn).astype(BF16)
    vt_ref[0] = _dot_nt(wt_ref[0:256, :], h).astype(BF16)
    gt_ref[0] = _sigmoid(_dot_nt(wt_ref[256:256 + GATE_ROWS, :], h) + gb_ref[...])


def _in_proj(x2, g, w_cat, w_t, tables, gate_b, batch, seq, tm=512):
    T = x2.shape[0]
    nt = T // tm
    npos = seq // tm
    cn, sn, cm, sm = tables
    row = lambda i: (i, 0)
    fixed = lambda i: (0, 0)
    posmap = lambda i: (i % npos, 0)
    tab_spec = pl.BlockSpec((tm, LANES), posmap)
    out_shapes = (
        jax.ShapeDtypeStruct((T, 512), F32),
        jax.ShapeDtypeStruct((T, 512), F32),
        jax.ShapeDtypeStruct((T, 768), BF16),
        jax.ShapeDtypeStruct((T, 128), F32),
        jax.ShapeDtypeStruct((T, 128), F32),
        jax.ShapeDtypeStruct((T, 256), BF16),
    )
    t_shapes = (jax.ShapeDtypeStruct((batch, 256, seq), BF16),
                jax.ShapeDtypeStruct((batch, GATE_ROWS, seq), F32))
    tmap = lambda i: (i // npos, 0, i % npos)
    gate_b_col = jnp.pad(gate_b, (0, GATE_ROWS - gate_b.shape[0])).reshape(GATE_ROWS, 1)
    return pl.pallas_call(
        _in_proj_kernel,
        grid=(nt,),
        in_specs=[pl.BlockSpec((tm, D_MODEL), row), pl.BlockSpec((1, D_MODEL), fixed),
                  pl.BlockSpec((D_MODEL, IN_COLS_PAD), fixed), pl.BlockSpec((256 + GATE_ROWS, D_MODEL), fixed),
                  tab_spec, tab_spec, tab_spec, tab_spec, pl.BlockSpec((GATE_ROWS, 1), fixed)],
        out_specs=[pl.BlockSpec((tm, s.shape[1]), row) for s in out_shapes]
        + [pl.BlockSpec((1, s.shape[1], tm), tmap) for s in t_shapes],
        out_shape=out_shapes + t_shapes,
        compiler_params=_cparams(("parallel",)),
        name="in_proj",
    )(x2, g.reshape(1, -1), w_cat, w_t, cn, sn, cm, sm, gate_b_col)


def _lru_kernel(xy_ref, cw_ref, cb_ref, wa_ref, ba_ref, wx_ref, bx_ref, lam_ref, out_ref,
                xe_sc, h_sc, *, tt):
    t = pl.program_id(1)

    @pl.when(t == 0)
    def _():
        xe_sc[0:8, :] = jnp.zeros((8, LRU_WIDTH), F32)
        h_sc[...] = jnp.zeros_like(h_sc)

    x = xy_ref[0, :, 0:LRU_WIDTH]
    y = xy_ref[0, :, LRU_WIDTH:2 * LRU_WIDTH]
    xe_sc[8:8 + tt, :] = x
    xc = cb_ref[...]
    for k in range(CONV_WIDTH):
        off = 8 - (CONV_WIDTH - 1) + k
        xc = xc + xe_sc[off:off + tt, :] * cw_ref[k:k + 1, :]
    xe_sc[0:8, :] = x[tt - 8:tt, :]

    xb = xc.astype(BF16)
    r = _sigmoid(_dot(xb, wa_ref[...]) + ba_ref[...])
    i = _sigmoid(_dot(xb, wx_ref[...]) + bx_ref[...])
    log_a = (-LRU_C * r) * jax.nn.softplus(-lam_ref[...])
    a = jnp.exp(log_a)
    b = jnp.sqrt(jnp.tanh(-log_a) * (a * a + 1.0)) * (i * xc)

    row = lax.broadcasted_iota(jnp.int32, (tt, LRU_WIDTH), 0)
    k = 1
    while k < tt:
        keep = row >= k
        a_sh = jnp.where(keep, pltpu.roll(a, k, 0), 1.0)
        b_sh = jnp.where(keep, pltpu.roll(b, k, 0), 0.0)
        b = a * b_sh + b
        a = a * a_sh
        k *= 2
    h = b + a * h_sc[0:1, :]
    h_sc[0:1, :] = h[tt - 1:tt, :]
    out_ref[0] = h * _gelu_tanh(y)


def _block_diag_dense(w):
    n, c, d = w.shape
    out = jnp.zeros((n * c, n * d), w.dtype)
    for j in range(n):
        out = out.at[j * c:(j + 1) * c, j * d:(j + 1) * d].set(w[j])
    return out


def _lru(lru_xy, conv_w, conv_b, wa, ba, wx, bx, lam, batch, seq, tt=512):
    xy = lru_xy.reshape(batch, seq, 512)
    fixed = lambda b, t: (0, 0)
    vec = pl.BlockSpec((1, LRU_WIDTH), fixed)
    mat = pl.BlockSpec((LRU_WIDTH, LRU_WIDTH), fixed)
    out = pl.pallas_call(
        functools.partial(_lru_kernel, tt=tt),
        grid=(batch, seq // tt),
        in_specs=[pl.BlockSpec((1, tt, 512), lambda b, t: (b, t, 0)),
                  pl.BlockSpec((CONV_WIDTH, LRU_WIDTH), fixed), vec, mat, vec, mat, vec, vec],
        out_specs=pl.BlockSpec((1, tt, LRU_WIDTH), lambda b, t: (b, t, 0)),
        out_shape=jax.ShapeDtypeStruct((batch, seq, LRU_WIDTH), F32),
        scratch_shapes=[pltpu.VMEM((tt + 8, LRU_WIDTH), F32), pltpu.VMEM((8, LRU_WIDTH), F32)],
        compiler_params=_cparams(("parallel", "arbitrary")),
        name="rg_lru",
    )(xy, conv_w, conv_b.reshape(1, -1), _block_diag_dense(wa).astype(BF16), ba.reshape(1, -1),
      _block_diag_dense(wx).astype(BF16), bx.reshape(1, -1), lam.reshape(1, -1))
    return out.reshape(batch * seq, LRU_WIDTH)


def _mla_prep_kernel(dn_ref, qn_ref, wqa_ref, wqb_ref, kvn_ref, wk_ref, wvt_ref, cm_ref, sm_ref,
                     q_ref, k_ref, vt_ref):
    q_dn = dn_ref[:, 0:256]
    ms = jnp.sum(q_dn * q_dn, axis=-1, keepdims=True) * (1.0 / MLA_Q_RANK)
    ql = ((q_dn * lax.rsqrt(ms + EPS)) * qn_ref[...]).astype(BF16)
    kv_dn = dn_ref[:, 256:384]
    ms = jnp.mean(kv_dn * kv_dn, axis=-1, keepdims=True)
    kvl = ((kv_dn * lax.rsqrt(ms + EPS)) * kvn_ref[...]).astype(BF16)
    kpe = dn_ref[:, 384:512]
    cm, sm = cm_ref[...], sm_ref[...]
    scale = (MLA_NOPE + MLA_ROPE) ** -0.5 * LOG2E
    for hd in range(MLA_HEADS):
        sl = slice(hd * LANES, (hd + 1) * LANES)
        q_ref[:, sl] = ((_dot(ql, wqa_ref[:, sl]) * cm + _dot(ql, wqb_ref[:, sl]) * sm) * scale).astype(BF16)
        k_ref[:, sl] = (_dot(kvl, wk_ref[:, sl]) + kpe).astype(BF16)
    vt_ref[0] = _dot_nt(wvt_ref[...], kvl).astype(BF16)


def _prep_mla_weights(w_uq, w_ukv):
    zq = jnp.zeros((MLA_Q_RANK, 32), w_uq.dtype)
    qa, qb, kk, vv = [], [], [], []
    for h in range(MLA_HEADS):
        wq = w_uq[:, h * 96:(h + 1) * 96]
        nope, ropew = wq[:, :64], wq[:, 64:]
        qa.append(jnp.concatenate([nope, ropew, zq], axis=1))
        qb.append(jnp.concatenate([0 * nope, _rot_cols(ropew, 16), zq], axis=1))
        wkv = w_ukv[:, h * 128:(h + 1) * 128]
        kk.append(_pad_cols(wkv[:, :64], LANES))
        vv.append(wkv[:, 64:])
    pad_rows = lambda w: jnp.pad(w, ((0, 256 - MLA_Q_RANK), (0, 0)))
    return (pad_rows(jnp.concatenate(qa, axis=1)).astype(BF16), pad_rows(jnp.concatenate(qb, axis=1)).astype(BF16),
            jnp.concatenate(kk, axis=1).astype(BF16), jnp.concatenate(vv, axis=1).T.astype(BF16))


def _mla_prep(mla_dn, q_norm, w_uq, kv_norm, w_ukv, tables, batch, seq, tm=512):
    T = mla_dn.shape[0]
    npos = seq // tm
    _, _, cm, sm = tables
    wqa, wqb, wk, wvt = _prep_mla_weights(w_uq, w_ukv)
    row = lambda i: (i, 0)
    fixed = lambda i: (0, 0)
    tab = pl.BlockSpec((tm, LANES), lambda i: (i % npos, 0))
    return pl.pallas_call(
        _mla_prep_kernel,
        grid=(T // tm,),
        in_specs=[pl.BlockSpec((tm, 512), row), pl.BlockSpec((1, 256), fixed),
                  pl.BlockSpec((256, 768), fixed), pl.BlockSpec((256, 768), fixed),
                  pl.BlockSpec((1, 128), fixed), pl.BlockSpec((128, 768), fixed),
                  pl.BlockSpec((384, 128), fixed), tab, tab],
        out_specs=[pl.BlockSpec((tm, 768), row), pl.BlockSpec((tm, 768), row),
                   pl.BlockSpec((1, 384, tm), lambda i: (i // npos, 0, i % npos))],
        out_shape=(jax.ShapeDtypeStruct((T, 768), BF16), jax.ShapeDtypeStruct((T, 768), BF16),
                   jax.ShapeDtypeStruct((batch, 384, seq), BF16)),
        compiler_params=_cparams(("parallel",)),
        name="mla_prep",
    )(mla_dn, _pad_cols(q_norm.reshape(1, -1), 256), wqa, wqb, kv_norm.reshape(1, -1), wk, wvt, cm, sm)


def _flash_kernel(qt_ref, kt_ref, first_ref, last_ref, *refs, mode, k_per_head, v_per_head, gate_rows, tq, tk):
    refs = list(refs)
    q_ref, k_ref, vt_ref = refs[:3]
    o_ref, m_sc, acc_sc = refs[-7:-4]
    s_sc, mx_sc = refs[-4:-2], refs[-2:]
    extra = refs[3:-7]
    sb_ref, oh_ref = (extra[0], extra[1]) if mode == "select" else (None, None)
    gt_ref = extra[-1] if gate_rows is not None else None
    step = pl.program_id(1)
    qt = qt_ref[step]
    kt = kt_ref[step]

    @pl.when(first_ref[step] == 1)
    def _():
        m_sc[...] = jnp.full_like(m_sc, NEG_INF)
        acc_sc[...] = jnp.zeros_like(acc_sc)

    ones = jnp.ones((SUM_ROWS, tk), BF16)

    def update(masked):
        if masked:
            kpos = kt * tk + lax.broadcasted_iota(jnp.int32, (tk, tq), 0)
            qpos = qt * tq + lax.broadcasted_iota(jnp.int32, (tk, tq), 1)
            dist = qpos - kpos
            valid = (dist >= 0) & (dist < WINDOW) if mode == "window" else dist >= 0

        def scores(j, slot):
            q = q_ref[0, :, pl.ds(pl.multiple_of(j * LANES, LANES), LANES)]
            if k_per_head:
                k = k_ref[0, :, pl.ds(pl.multiple_of(j * LANES, LANES), LANES)]
            else:
                k = k_ref[0]
            if mode == "select":
                q = jnp.concatenate([q, sb_ref[0, j // NSA_GROUP]], axis=1)
                k = jnp.concatenate([k, oh_ref[...]], axis=1)
            s = _dot_nt(k, q)
            if masked:
                s = jnp.where(valid, s, NEG_INF)
            s_sc[slot][...] = s
            mx_sc[slot][...] = jnp.max(s, axis=0, keepdims=True)

        def accumulate(j, slot):
            m_prev = m_sc[j]
            m_new = jnp.maximum(m_prev, mx_sc[slot][...])
            alpha = jnp.exp2(m_prev - m_new)
            p = jnp.exp2((s_sc[slot][...] - m_new).astype(BF16))
            row = (j if v_per_head else j // NSA_GROUP) * HEAD_V
            vt = vt_ref[0, pl.ds(pl.multiple_of(row, HEAD_V), HEAD_V), :]
            acc_sc[j] = alpha * acc_sc[j] + _dot(jnp.concatenate([vt, ones], axis=0), p)
            m_sc[j] = m_new

        scores(0, 0)
        for j in range(N_ATT_HEADS):
            if j + 1 < N_ATT_HEADS:
                scores(j + 1, (j + 1) % 2)
            accumulate(j, j % 2)

    if mode == "window":
        update(True)
    else:
        crosses = kt * tk + (tk - 1) > qt * tq
        pl.when(crosses)(lambda: update(True))
        pl.when(jnp.logical_not(crosses))(lambda: update(False))

    @pl.when(last_ref[step] == 1)
    def _():
        outs = []
        for j in range(N_ATT_HEADS):
            o = acc_sc[j, 0:HEAD_V, :] / acc_sc[j, HEAD_V:HEAD_V + 1, :]
            if gate_rows is not None:
                o = o * gt_ref[0, gate_rows[j]:gate_rows[j] + 1, :]
            outs.append(o)
        o_ref[0] = jnp.concatenate(outs, axis=0).T.astype(o_ref.dtype)


def _pair_tables(nq, mode, tq, tk):
    qts, kts, first, last = [], [], [], []
    for qt in range(nq):
        hi = (qt * tq + tq - 1) // tk
        lo = 0 if mode != "window" else max(0, (qt * tq - (WINDOW - 1)) // tk)
        for kt in range(lo, hi + 1):
            qts.append(qt)
            kts.append(kt)
            first.append(1 if kt == lo else 0)
            last.append(1 if kt == hi else 0)
    arr = lambda v: jnp.asarray(np.array(v, dtype=np.int32))
    return arr(qts), arr(kts), arr(first), arr(last)


def _flash(q, k, vt, *, mode, batch, seq, per_head_kv, k_width, k_col, vt_rows, vt_row, out_dtype,
           selbias=None, onehot=None, gate_rows=None, gates_t=None, tq=512, tk=512):
    tk = min(tk, seq)
    nq = seq // tq
    qt_tab, kt_tab, first_tab, last_tab = _pair_tables(nq, mode, tq, tk)
    n_steps = int(qt_tab.shape[0])
    qw = N_ATT_HEADS * LANES
    ow = N_ATT_HEADS * HEAD_V
    in_specs = [
        pl.BlockSpec((1, tq, qw), lambda b, s, qt, kt, f, l: (b, qt[s], 0)),
        pl.BlockSpec((1, tk, k_width), lambda b, s, qt, kt, f, l: (b, kt[s], k_col)),
        pl.BlockSpec((1, vt_rows, tk), lambda b, s, qt, kt, f, l: (b, vt_row, kt[s])),
    ]
    args = [q, k, vt]
    if mode == "select":
        in_specs.append(pl.BlockSpec((1, NSA_KV_HEADS, tq, LANES), lambda b, s, qt, kt, f, l: (b, 0, qt[s], 0)))
        in_specs.append(pl.BlockSpec((tk, LANES), lambda b, s, qt, kt, f, l: (kt[s], 0)))
        args += [selbias, onehot]
    if gate_rows is not None:
        in_specs.append(pl.BlockSpec((1, GATE_ROWS, tq), lambda b, s, qt, kt, f, l: (b, 0, qt[s])))
        args.append(gates_t)
    kern = functools.partial(_flash_kernel, mode=mode, k_per_head=per_head_kv, v_per_head=per_head_kv,
                             gate_rows=gate_rows, tq=tq, tk=tk)
    return pl.pallas_call(
        kern,
        grid_spec=pltpu.PrefetchScalarGridSpec(
            num_scalar_prefetch=4,
            grid=(batch, n_steps),
            in_specs=in_specs,
            out_specs=pl.BlockSpec((1, tq, ow), lambda b, s, qt, kt, f, l: (b, qt[s], 0)),
            scratch_shapes=[pltpu.VMEM((N_ATT_HEADS, 1, tq), F32),
                            pltpu.VMEM((N_ATT_HEADS, HEAD_V + SUM_ROWS, tq), F32),
                            pltpu.VMEM((tk, tq), F32), pltpu.VMEM((tk, tq), F32),
                            pltpu.VMEM((1, tq), F32), pltpu.VMEM((1, tq), F32)],
        ),
        out_shape=jax.ShapeDtypeStruct((batch, seq, ow), out_dtype),
        compiler_params=_cparams(("parallel", "arbitrary")),
        name="flash_" + mode,
    )(qt_tab, kt_tab, first_tab, last_tab, *args)


def _compress_kernel(kc_ref, vc_ref, pos_ref, w1a_ref, w1b_ref, w2_ref, ko_ref, vo_ref, *, nchunk):
    row = lax.broadcasted_iota(jnp.int32, (nchunk, LANES), 0)
    for br, src in enumerate((kc_ref, vc_ref)):
        r = src[0]
        pa = _dot((r + pos_ref[br, 0:1, :]).astype(BF16), w1a_ref[br])
        pb = _dot((r + pos_ref[br, 1:2, :]).astype(BF16), w1b_ref[br])
        hid = _gelu_tanh(pa + pltpu.roll(pb, nchunk - 1, 0))
        out = jnp.where(row < nchunk - 1, _dot(hid.astype(BF16), w2_ref[br]), 0.0)
        if br == 0:
            ko_ref[0] = out.astype(ko_ref.dtype)
        else:
            vo_ref[0] = out.T.astype(vo_ref.dtype)


def _prep_compress_weights(cmp_pos, cmp_w1, cmp_w2):
    half = CMP_LEN // 2
    pos, w1a, w1b, w2 = [], [], [], []
    for br in range(2):
        p = cmp_pos[br]
        tile = lambda ph: jnp.concatenate([ph, ph], axis=1).reshape(1, half * LANES)
        pos.append(jnp.concatenate([tile(p[:half]), tile(p[half:])], axis=0))
        w = cmp_w1[br].reshape(CMP_LEN, NSA_HEAD_DIM, CMP_HIDDEN)
        z = jnp.zeros_like(w[:half])

        def big(wh):
            g0 = jnp.concatenate([wh, z], axis=1)
            g1 = jnp.concatenate([z, wh], axis=1)
            return jnp.concatenate([g0.reshape(half * LANES, CMP_HIDDEN), g1.reshape(half * LANES, CMP_HIDDEN)], axis=1)

        w1a.append(big(w[:half]))
        w1b.append(big(w[half:]))
        w2.append(_block_diag_dense(jnp.stack([cmp_w2[br], cmp_w2[br]])))
    return (jnp.stack(pos), jnp.stack(w1a).astype(BF16), jnp.stack(w1b).astype(BF16), jnp.stack(w2).astype(BF16))


def _compress(kc, vc, cmp_pos, cmp_w1, cmp_w2, batch, seq):
    nchunk = seq // CMP_STRIDE
    width = CMP_STRIDE * LANES
    pos, w1a, w1b, w2 = _prep_compress_weights(cmp_pos, cmp_w1, cmp_w2)
    blk = pl.BlockSpec((1, nchunk, width), lambda b: (b, 0, 0))
    full3 = lambda shape: pl.BlockSpec(shape, lambda b: (0, 0, 0))
    return pl.pallas_call(
        functools.partial(_compress_kernel, nchunk=nchunk),
        grid=(batch,),
        in_specs=[blk, blk, full3(pos.shape), full3(w1a.shape), full3(w1b.shape), full3(w2.shape)],
        out_specs=[pl.BlockSpec((1, nchunk, LANES), lambda b: (b, 0, 0)),
                   pl.BlockSpec((1, LANES, nchunk), lambda b: (b, 0, 0))],
        out_shape=(jax.ShapeDtypeStruct((batch, nchunk, LANES), BF16),
                   jax.ShapeDtypeStruct((batch, LANES, nchunk), BF16)),
        compiler_params=_cparams(("parallel",)),
        name="nsa_compress",
    )(kc.reshape(batch, nchunk, width), vc.reshape(batch, nchunk, width), pos, w1a, w1b, w2)


def _select_kernel(q_ref, kc_ref, vct_ref, ovt_ref, gt_ref, oc_ref, sb_ref, *, tq, nchunk):
    nsel = LANES
    qt = pl.program_id(1)
    kc = kc_ref[0]
    qpos = qt * tq + lax.broadcasted_iota(jnp.int32, (nchunk, tq), 1)
    cend = lax.broadcasted_iota(jnp.int32, (nchunk, tq), 0) * CMP_STRIDE + (CMP_LEN - 1)
    cvalid = cend <= qpos
    ovt = ovt_ref[...]
    blk = lax.broadcasted_iota(jnp.int32, (nsel, tq), 0)
    qp = qt * tq + lax.broadcasted_iota(jnp.int32, (nsel, tq), 1)
    valid = blk * SEL_LEN <= qp
    forced = (blk == 0) | (blk == qp // SEL_LEN)
    outs = []
    for g in range(NSA_KV_HEADS):
        vct = vct_ref[0, g * HEAD_V:(g + 1) * HEAD_V, :]
        psum = jnp.zeros((nchunk, tq), F32)
        for pj in range(NSA_GROUP):
            hd = g * NSA_GROUP + pj
            q = q_ref[0, :, hd * LANES:(hd + 1) * LANES]
            s = jnp.where(cvalid, _dot_nt(kc, q), NEG_INF)
            m = jnp.max(s, axis=0, keepdims=True)
            e = jnp.where(cvalid, jnp.exp2(s - m), 0.0)
            p = e * (1.0 / jnp.maximum(jnp.sum(e, axis=0, keepdims=True), 1e-30))
            outs.append(_dot(vct, p.astype(BF16)) * gt_ref[0, 3 * hd:3 * hd + 1, :])
            psum = psum + p
        hi = psum.astype(BF16)
        r1 = psum - hi.astype(F32)
        mid = r1.astype(BF16)
        lo = (r1 - mid.astype(F32)).astype(BF16)
        score = _dot(ovt, hi) + _dot(ovt, mid) + _dot(ovt, lo)
        score = jnp.where(valid, score, -1.0)
        score = jnp.where(forced, score + FORCE_BONUS, score)
        sel = jnp.zeros((nsel, tq), F32)
        for _ in range(min(SEL_TOPK, nsel)):
            m = jnp.max(score, axis=0, keepdims=True)
            idx = jnp.min(jnp.where(score == m, blk, nsel), axis=0, keepdims=True)
            hit = blk == idx
            sel = jnp.where(hit & (m >= 0.0), 1.0, sel)
            score = jnp.where(hit, -jnp.inf, score)
        bias = jnp.where(sel > 0.0, 0.0, NEG_INF)
        sb_ref[0, g] = bias.T.astype(sb_ref.dtype)
    oc_ref[0] = jnp.concatenate(outs, axis=0).T


def _overlap_matrix_t(nchunk, nsel):
    n = np.arange(nchunk)[None, :]
    s = np.arange(nsel)[:, None]
    cs = n * CMP_STRIDE
    ov = (cs < s * SEL_LEN + SEL_LEN) & (cs + CMP_LEN - 1 >= s * SEL_LEN) & (n < nchunk - 1)
    return jnp.asarray(ov.astype(np.float32)).astype(BF16)


def _select(qn, kcmp, vcmp_t, gates_t, batch, seq, tq=256):
    nchunk = seq // CMP_STRIDE
    nsel = LANES
    assert seq // SEL_LEN <= nsel
    qw = N_ATT_HEADS * LANES
    ow = N_ATT_HEADS * HEAD_V
    ovt = _overlap_matrix_t(nchunk, nsel)
    return pl.pallas_call(
        functools.partial(_select_kernel, tq=tq, nchunk=nchunk),
        grid=(batch, seq // tq),
        in_specs=[pl.BlockSpec((1, tq, qw), lambda b, t: (b, t, 0)),
                  pl.BlockSpec((1, nchunk, LANES), lambda b, t: (b, 0, 0)),
                  pl.BlockSpec((1, NSA_KV_HEADS * HEAD_V, nchunk), lambda b, t: (b, 0, 0)),
                  pl.BlockSpec((nsel, nchunk), lambda b, t: (0, 0)),
                  pl.BlockSpec((1, GATE_ROWS, tq), lambda b, t: (b, 0, t))],
        out_specs=[pl.BlockSpec((1, tq, ow), lambda b, t: (b, t, 0)),
                   pl.BlockSpec((1, NSA_KV_HEADS, tq, nsel), lambda b, t: (b, 0, t, 0))],
        out_shape=(jax.ShapeDtypeStruct((batch, seq, ow), F32),
                   jax.ShapeDtypeStruct((batch, NSA_KV_HEADS, seq, nsel), BF16)),
        compiler_params=_cparams(("parallel", "parallel")),
        name="nsa_select",
    )(qn, kcmp, vcmp_t, ovt, gates_t)


def _out_proj_kernel(x_ref, a_ref, b_ref, oc_ref, os_ref, ow_ref, wa_ref, wb_ref, wc_ref, o_ref):
    acc = x_ref[...] + _dot(a_ref[...].astype(BF16), wa_ref[...])
    acc = acc + _dot(b_ref[...], wb_ref[...])
    c = (oc_ref[...] + os_ref[...]) + ow_ref[...]
    o_ref[...] = acc + _dot(c.astype(BF16), wc_ref[...])


def _out_proj(x2, a_out, b_out, o_c, o_s, o_w, w_out, tm=512):
    T = x2.shape[0]
    wa = w_out[0:256].astype(BF16)
    wb = w_out[256:640].astype(BF16)
    wc = w_out[640:1024].astype(BF16)
    row = lambda i: (i, 0)
    fixed = lambda i: (0, 0)
    act = lambda n: pl.BlockSpec((tm, n), row)
    return pl.pallas_call(
        _out_proj_kernel,
        grid=(T // tm,),
        in_specs=[act(D_MODEL), act(256), act(384), act(384), act(384), act(384),
                  pl.BlockSpec((256, D_MODEL), fixed), pl.BlockSpec((384, D_MODEL), fixed),
                  pl.BlockSpec((384, D_MODEL), fixed)],
        out_specs=pl.BlockSpec((tm, D_MODEL), row),
        out_shape=jax.ShapeDtypeStruct((T, D_MODEL), F32),
        compiler_params=_cparams(("parallel",)),
        name="out_proj",
    )(x2, a_out, b_out, o_c, o_s, o_w, wa, wb, wc)


def _final_norm(y, gf_ref):
    ms = jnp.mean(y * y, axis=-1, keepdims=True)
    return (y * lax.rsqrt(ms + EPS)) * gf_ref[...]


def _ffn_kernel(x_ref, g_ref, wg_ref, wu_ref, wd_ref, gf_ref, o_ref, h_sc, acc_sc, *, final_norm):
    f = pl.program_id(1)

    @pl.when(f == 0)
    def _():
        x = x_ref[...]
        ms = jnp.mean(x * x, axis=-1, keepdims=True)
        h_sc[...] = ((x * lax.rsqrt(ms + EPS)) * g_ref[...]).astype(BF16)
        acc_sc[...] = x

    h = h_sc[...]
    gte = _dot(h, wg_ref[...])
    up = _dot(h, wu_ref[...])
    act = (gte * _sigmoid(gte)) * up
    acc_sc[...] += _dot(act.astype(BF16), wd_ref[...])

    @pl.when(f == pl.num_programs(1) - 1)
    def _():
        y = acc_sc[...]
        o_ref[...] = _final_norm(y, gf_ref) if final_norm else y


def _ffn(x2, g, wg, wu, wd, g_final, final_norm, tm=512, tf=1408):
    T = x2.shape[0]
    nf = D_FF // tf
    return pl.pallas_call(
        functools.partial(_ffn_kernel, final_norm=final_norm),
        grid=(T // tm, nf),
        in_specs=[pl.BlockSpec((tm, D_MODEL), lambda i, f: (i, 0)), pl.BlockSpec((1, D_MODEL), lambda i, f: (0, 0)),
                  pl.BlockSpec((D_MODEL, tf), lambda i, f: (0, f)), pl.BlockSpec((D_MODEL, tf), lambda i, f: (0, f)),
                  pl.BlockSpec((tf, D_MODEL), lambda i, f: (f, 0)), pl.BlockSpec((1, D_MODEL), lambda i, f: (0, 0))],
        out_specs=pl.BlockSpec((tm, D_MODEL), lambda i, f: (i, 0)),
        out_shape=jax.ShapeDtypeStruct((T, D_MODEL), F32),
        scratch_shapes=[pltpu.VMEM((tm, D_MODEL), BF16), pltpu.VMEM((tm, D_MODEL), F32)],
        compiler_params=_cparams(("parallel", "arbitrary")),
        name="ffn_dense",
    )(x2, g.reshape(1, -1), wg.astype(BF16), wu.astype(BF16), wd.astype(BF16), g_final.reshape(1, -1))


def _moe_kernel(x_ref, g_ref, rw_ref, wg_ref, wu_ref, wd_ref, gf_ref, o_ref, h_sc, gate_sc, acc_sc, *,
                final_norm, tm):
    e = pl.program_id(1)
    lane = lax.broadcasted_iota(jnp.int32, (tm, LANES), 1)

    @pl.when(e == 0)
    def _():
        x = x_ref[...]
        ms = jnp.mean(x * x, axis=-1, keepdims=True)
        hf = (x * lax.rsqrt(ms + EPS)) * g_ref[...]
        h_sc[...] = hf.astype(BF16)
        acc_sc[...] = x
        h_hi = hf.astype(BF16)
        h_lo = (hf - h_hi.astype(F32)).astype(BF16)
        logits = (_dot(h_hi, rw_ref[0]) + _dot(h_lo, rw_ref[0]) + _dot(h_hi, rw_ref[1]))
        logits = jnp.where(lane < N_EXPERTS, logits, -jnp.inf)
        m1 = jnp.max(logits, axis=1, keepdims=True)
        i1 = jnp.min(jnp.where(logits == m1, lane, LANES), axis=1, keepdims=True)
        rest = jnp.where(lane == i1, -jnp.inf, logits)
        m2 = jnp.max(rest, axis=1, keepdims=True)
        i2 = jnp.min(jnp.where(rest == m2, lane, LANES), axis=1, keepdims=True)
        e2 = jnp.exp(m2 - m1)
        den = 1.0 + e2
        gate_sc[...] = jnp.where(lane == i1, 1.0 / den, 0.0) + jnp.where(lane == i2, e2 / den, 0.0)

    h = h_sc[...]
    gte = _dot(h, wg_ref[0])
    up = _dot(h, wu_ref[0])
    act = (gte * _sigmoid(gte)) * up
    y = _dot(act.astype(BF16), wd_ref[0])
    ge = jnp.sum(jnp.where(lane == e, gate_sc[...], 0.0), axis=1, keepdims=True)
    acc_sc[...] += ge * y

    @pl.when(e == pl.num_programs(1) - 1)
    def _():
        out = acc_sc[...]
        o_ref[...] = _final_norm(out, gf_ref) if final_norm else out


def _moe(x2, g, router_w, wg, wu, wd, g_final, final_norm, tm=512):
    T = x2.shape[0]
    rw = _pad_cols(router_w, LANES)
    rw_hi = rw.astype(BF16)
    rw_lo = (rw - rw_hi.astype(F32)).astype(BF16)
    rw2 = jnp.stack([rw_hi, rw_lo])
    return pl.pallas_call(
        functools.partial(_moe_kernel, final_norm=final_norm, tm=tm),
        grid=(T // tm, N_EXPERTS),
        in_specs=[pl.BlockSpec((tm, D_MODEL), lambda i, e: (i, 0)), pl.BlockSpec((1, D_MODEL), lambda i, e: (0, 0)),
                  pl.BlockSpec((2, D_MODEL, LANES), lambda i, e: (0, 0, 0)),
                  pl.BlockSpec((1, D_MODEL, D_FF_EXPERT), lambda i, e: (e, 0, 0)),
                  pl.BlockSpec((1, D_MODEL, D_FF_EXPERT), lambda i, e: (e, 0, 0)),
                  pl.BlockSpec((1, D_FF_EXPERT, D_MODEL), lambda i, e: (e, 0, 0)),
                  pl.BlockSpec((1, D_MODEL), lambda i, e: (0, 0))],
        out_specs=pl.BlockSpec((tm, D_MODEL), lambda i, e: (i, 0)),
        out_shape=jax.ShapeDtypeStruct((T, D_MODEL), F32),
        scratch_shapes=[pltpu.VMEM((tm, D_MODEL), BF16), pltpu.VMEM((tm, LANES), F32),
                        pltpu.VMEM((tm, D_MODEL), F32)],
        compiler_params=_cparams(("parallel", "arbitrary")),
        name="moe_dense",
    )(x2, g.reshape(1, -1), rw2, wg.astype(BF16), wu.astype(BF16), wd.astype(BF16), g_final.reshape(1, -1))


def _mixer(x2, batch, seq, tables, onehot, norm_g, w_in, conv_w, conv_b, wa, ba, wx, bx, lam,
           q_norm, w_uq, kv_norm, w_ukv, cmp_pos, cmp_w1, cmp_w2, gate_b, w_out):
    w_cat, w_t = _prep_w_in(w_in)
    lru_xy, mla_dn, qn, kc, vc, ksw, vt_sw, gates_t = _in_proj(x2, norm_g, w_cat, w_t, tables, gate_b, batch, seq)
    a_out = _lru(lru_xy, conv_w, conv_b, wa, ba, wx, bx, lam, batch, seq)

    q_m, k_m, vt_m = _mla_prep(mla_dn, q_norm, w_uq, kv_norm, w_ukv, tables, batch, seq)
    b3 = lambda t: t.reshape(batch, seq, t.shape[-1])
    heads = tuple(range(N_ATT_HEADS))
    b_out = _flash(b3(q_m), b3(k_m), vt_m, mode="causal", batch=batch, seq=seq, per_head_kv=True,
                   k_width=N_ATT_HEADS * LANES, k_col=0, vt_rows=N_ATT_HEADS * HEAD_V, vt_row=0, out_dtype=BF16,
                   tk=LONG_KEY_TILE)

    kcmp, vcmp_t = _compress(kc, vc, cmp_pos, cmp_w1, cmp_w2, batch, seq)
    qn3 = b3(qn)
    o_c, selbias = _select(qn3, kcmp, vcmp_t, gates_t, batch, seq)
    ksw3 = b3(ksw)
    nsa_kw = dict(batch=batch, seq=seq, per_head_kv=False, k_width=LANES,
                  vt_rows=NSA_KV_HEADS * HEAD_V, out_dtype=F32, gates_t=gates_t)
    o_s = _flash(qn3, ksw3, vt_sw, mode="select", k_col=0, vt_row=0, selbias=selbias,
                 onehot=onehot, gate_rows=tuple(3 * h + 1 for h in heads), tk=LONG_KEY_TILE, **nsa_kw)
    o_w = _flash(qn3, ksw3, vt_sw, mode="window", k_col=1, vt_row=1,
                 gate_rows=tuple(3 * h + 2 for h in heads), **nsa_kw)
    T = batch * seq
    flat = lambda t: t.reshape(T, t.shape[-1])
    return _out_proj(x2, a_out, flat(b_out), flat(o_c), flat(o_s), flat(o_w), w_out)


def kernel(x, norm_mix, w_in, lru_conv_w, lru_conv_b, lru_wa, lru_ba, lru_wx, lru_bx, lru_lambda, mla_q_norm, mla_w_uq, mla_kv_norm, mla_w_ukv, nsa_cmp_pos, nsa_cmp_w1, nsa_cmp_w2, nsa_gate_b, w_out, norm_ffn, ffn_w_gate, ffn_w_up, ffn_w_down, router_w, moe_w_gate, moe_w_up, moe_w_down, norm_final):
    batch, seq, _ = x.shape
    depth = norm_mix.shape[0]
    tables = _rope_tables(seq)
    blk_id = np.arange(seq)[:, None] // SEL_LEN
    onehot = jnp.asarray((blk_id == np.arange(LANES)[None, :]).astype(np.float32)).astype(BF16)
    x2 = x.reshape(batch * seq, D_MODEL)
    for l in range(depth):
        x2 = _mixer(x2, batch, seq, tables, onehot, norm_mix[l], w_in[l], lru_conv_w[l], lru_conv_b[l],
                    lru_wa[l], lru_ba[l], lru_wx[l], lru_bx[l], lru_lambda[l], mla_q_norm[l], mla_w_uq[l],
                    mla_kv_norm[l], mla_w_ukv[l], nsa_cmp_pos[l], nsa_cmp_w1[l], nsa_cmp_w2[l],
                    nsa_gate_b[l], w_out[l])
        last = l == depth - 1
        j = l // 2
        if l % 2 == 0:
            x2 = _ffn(x2, norm_ffn[l], ffn_w_gate[j], ffn_w_up[j], ffn_w_down[j], norm_final, last)
        else:
            x2 = _moe(x2, norm_ffn[l], router_w[j], moe_w_gate[j], moe_w_up[j], moe_w_down[j], norm_final, last)
    if depth == 0:
        raise ValueError("depth must be positive")
    return x2.reshape(batch, seq, D_MODEL)
```

```python
import functools
import math

import numpy as np
import jax
import jax.numpy as jnp
from jax import lax
from jax.experimental import pallas as pl
from jax.experimental.pallas import tpu as pltpu

F32 = jnp.float32
BF16 = jnp.bfloat16

D_MODEL = 1024
LRU_WIDTH = 256
LRU_BLOCKS = 4
LRU_BLOCK_W = LRU_WIDTH // LRU_BLOCKS
CONV_WIDTH = 4
LRU_C = 8.0
MLA_HEADS = 6
MLA_NOPE = 64
MLA_ROPE = 32
MLA_V = 64
MLA_Q_RANK = 192
MLA_KV_RANK = 128
NSA_HEADS = 6
NSA_KV_HEADS = 2
NSA_GROUP = NSA_HEADS // NSA_KV_HEADS
NSA_HEAD_DIM = 64
CMP_LEN = 32
CMP_STRIDE = 16
CMP_HIDDEN = 128
SEL_LEN = 64
SEL_TOPK = 16
WINDOW = 512
FORCE_BONUS = 1.0e3
D_FF = 2816
N_EXPERTS = 8
D_FF_EXPERT = 1408
ROPE_THETA = 10000.0
EPS = 1e-6
NEG_INF = -1.0e30

LANES = 128
VMEM_LIMIT = 56 * 1024 * 1024

C_LRU = 0
C_MLA = 512
C_KPE_ROT = 1024
C_QN = 1152
C_QN_ROT = 1536
C_KC = 1920
C_KC_ROT = 2048
C_VC = 2176
C_KS = 2304
C_KS_ROT = 2432
C_KW = 2560
C_KW_ROT = 2688
IN_COLS_PAD = 2816
N_ATT_HEADS = 6
HEAD_V = 64
LONG_KEY_TILE = 1024
SUM_ROWS = 16
GATE_ROWS = 32
LOG2E = 1.4426950408889634


def _cparams(sem, flags=None):
    return pltpu.CompilerParams(dimension_semantics=sem, vmem_limit_bytes=VMEM_LIMIT, flags=flags)


def _gelu_tanh(x):
    return 0.5 * x * (1.0 + jnp.tanh(math.sqrt(2.0 / math.pi) * (x + 0.044715 * (x * x * x))))


def _sigmoid(x):
    return 1.0 / (1.0 + jnp.exp(-x))


def _dot(a, b):
    return jnp.dot(a, b, preferred_element_type=F32)


def _dot_nt(a, b):
    return lax.dot_general(a, b, (((1,), (1,)), ((), ())), preferred_element_type=F32)


def _rot_cols(w, half):
    return jnp.concatenate([-w[:, half:], w[:, :half]], axis=1)


def _pad_cols(w, n):
    return jnp.pad(w, ((0, 0), (0, n - w.shape[1])))


def _prep_w_in(w_in):
    splits = np.cumsum([256, 256, 192, 128, 32, 384, 128, 128, 128, 128, 128, 128, 18])[:-1].tolist()
    (x_l, y_l, q_dn, kv_dn, k_pe, q_n, kc, vc, ks, vs, kw, vw, g_n) = jnp.split(w_in, splits, axis=1)
    z = lambda n: jnp.zeros((w_in.shape[0], n), w_in.dtype)

    def kpe_slot(w):
        return jnp.concatenate([z(64), w, z(32)], axis=1)

    def qn_slots(fn):
        out = []
        for p in range(NSA_GROUP):
            for h in (p, p + NSA_GROUP):
                out.append(fn(q_n[:, h * 64:(h + 1) * 64]))
        return jnp.concatenate(out, axis=1)

    def kv_rot(w):
        return jnp.concatenate([_rot_cols(w[:, :64], 32), _rot_cols(w[:, 64:], 32)], axis=1)

    cols = [x_l, y_l, _pad_cols(q_dn, 256), kv_dn, kpe_slot(k_pe), kpe_slot(_rot_cols(k_pe, 16)),
            qn_slots(lambda w: w), qn_slots(lambda w: _rot_cols(w, 32)),
            kc, kv_rot(kc), vc, ks, kv_rot(ks), kw, kv_rot(kw)]
    w = jnp.concatenate(cols, axis=1)
    assert w.shape[1] == IN_COLS_PAD
    w_t = jnp.concatenate([vs, vw, _pad_cols(g_n, GATE_ROWS)], axis=1).T
    return w.astype(BF16), w_t.astype(BF16)


def _rope_tables(seq):
    pos = jnp.arange(seq, dtype=F32)[:, None]
    inv64 = ROPE_THETA ** (-jnp.arange(32, dtype=F32) * 2.0 / 64)
    inv32 = ROPE_THETA ** (-jnp.arange(16, dtype=F32) * 2.0 / 32)
    a64 = pos * inv64[None, :]
    a32 = pos * inv32[None, :]
    c64, s64 = jnp.cos(a64), jnp.sin(a64)
    c32, s32 = jnp.cos(a32), jnp.sin(a32)
    cos_n = jnp.concatenate([c64, c64, c64, c64], axis=1)
    sin_n = jnp.concatenate([s64, s64, s64, s64], axis=1)
    one = jnp.ones((seq, 64), F32)
    cos_m = jnp.concatenate([one, c32, c32, one[:, :32]], axis=1)
    sin_m = jnp.concatenate([0 * one, s32, s32, 0 * one[:, :32]], axis=1)
    return cos_n, sin_n, cos_m, sin_m


def _in_proj_kernel(x_ref, g_ref, w_ref, wt_ref, cn_ref, sn_ref, cm_ref, sm_ref, gb_ref,
                    lru_ref, mla_ref, qn_ref, kc_ref, vc_ref, ksw_ref, vt_ref, gt_ref):
    x = x_ref[...]
    ms = jnp.mean(x * x, axis=-1, keepdims=True)
    h = ((x * lax.rsqrt(ms + EPS)) * g_ref[...]).astype(BF16)

    def proj(c0, n):
        return _dot(h, w_ref[:, c0:c0 + n])

    cn, sn = cn_ref[...], sn_ref[...]
    lru_ref[...] = proj(C_LRU, 512)
    mla_ref[:, 0:384] = proj(C_MLA, 384)
    mla_ref[:, 384:512] = proj(C_MLA + 384, 128) * cm_ref[...] + proj(C_KPE_ROT, 128) * sm_ref[...]
    scale = NSA_HEAD_DIM ** -0.5 * LOG2E
    low_half = lax.broadcasted_iota(jnp.int32, cn.shape, 1) < NSA_HEAD_DIM
    for p in range(NSA_GROUP):
        a = proj(C_QN + p * LANES, LANES)
        b = proj(C_QN_ROT + p * LANES, LANES)
        r = (a * cn + b * sn) * scale
        hi_hd = p + NSA_GROUP
        qn_ref[:, p * LANES:(p + 1) * LANES] = jnp.where(low_half, r, 0.0).astype(BF16)
        qn_ref[:, hi_hd * LANES:(hi_hd + 1) * LANES] = jnp.where(low_half, 0.0, r).astype(BF16)
    kc_ref[...] = proj(C_KC, 128) * cn + proj(C_KC_ROT, 128) * sn
    vc_ref[...] = proj(C_VC, 128)
    ksw_ref[:, 0:128] = (proj(C_KS, 128) * cn + proj(C_KS_ROT, 128) * sn).astype(BF16)
    ksw_ref[:, 128:256] = (proj(C_KW, 128) * cn + proj(C_KW_ROT, 128) * sn).astype(BF16)
    vt_ref[0] = _dot_nt(wt_ref[0:256, :], h).astype(BF16)
    gt_ref[0] = _sigmoid(_dot_nt(wt_ref[256:256 + GATE_ROWS, :], h) + gb_ref[...])


def _in_proj(x2, g, w_cat, w_t, tables, gate_b, batch, seq, tm=512):
    T = x2.shape[0]
    nt = T // tm
    npos = seq // tm
    cn, sn, cm, sm = tables
    row = lambda i: (i, 0)
    fixed = lambda i: (0, 0)
    posmap = lambda i: (i % npos, 0)
    tab_spec = pl.BlockSpec((tm, LANES), posmap)
    out_shapes = (
        jax.ShapeDtypeStruct((T, 512), F32),
        jax.ShapeDtypeStruct((T, 512), F32),
        jax.ShapeDtypeStruct((T, 768), BF16),
        jax.ShapeDtypeStruct((T, 128), F32),
        jax.ShapeDtypeStruct((T, 128), F32),
        jax.ShapeDtypeStruct((T, 256), BF16),
    )
    t_shapes = (jax.ShapeDtypeStruct((batch, 256, seq), BF16),
                jax.ShapeDtypeStruct((batch, GATE_ROWS, seq), F32))
    tmap = lambda i: (i // npos, 0, i % npos)
    gate_b_col = jnp.pad(gate_b, (0, GATE_ROWS - gate_b.shape[0])).reshape(GATE_ROWS, 1)
    return pl.pallas_call(
        _in_proj_kernel,
        grid=(nt,),
        in_specs=[pl.BlockSpec((tm, D_MODEL), row), pl.BlockSpec((1, D_MODEL), fixed),
                  pl.BlockSpec((D_MODEL, IN_COLS_PAD), fixed), pl.BlockSpec((256 + GATE_ROWS, D_MODEL), fixed),
                  tab_spec, tab_spec, tab_spec, tab_spec, pl.BlockSpec((GATE_ROWS, 1), fixed)],
        out_specs=[pl.BlockSpec((tm, s.shape[1]), row) for s in out_shapes]
        + [pl.BlockSpec((1, s.shape[1], tm), tmap) for s in t_shapes],
        out_shape=out_shapes + t_shapes,
        compiler_params=_cparams(("parallel",)),
        name="in_proj",
    )(x2, g.reshape(1, -1), w_cat, w_t, cn, sn, cm, sm, gate_b_col)


def _lru_kernel(xy_ref, cw_ref, cb_ref, wa_ref, ba_ref, wx_ref, bx_ref, lam_ref, out_ref,
                xe_sc, h_sc, *, tt):
    t = pl.program_id(1)

    @pl.when(t == 0)
    def _():
        xe_sc[0:8, :] = jnp.zeros((8, LRU_WIDTH), F32)
        h_sc[...] = jnp.zeros_like(h_sc)

    x = xy_ref[0, :, 0:LRU_WIDTH]
    y = xy_ref[0, :, LRU_WIDTH:2 * LRU_WIDTH]
    xe_sc[8:8 + tt, :] = x
    xc = cb_ref[...]
    for k in range(CONV_WIDTH):
        off = 8 - (CONV_WIDTH - 1) + k
        xc = xc + xe_sc[off:off + tt, :] * cw_ref[k:k + 1, :]
    xe_sc[0:8, :] = x[tt - 8:tt, :]

    xb = xc.astype(BF16)
    r = _sigmoid(_dot(xb, wa_ref[...]) + ba_ref[...])
    i = _sigmoid(_dot(xb, wx_ref[...]) + bx_ref[...])
    log_a = (-LRU_C * r) * jax.nn.softplus(-lam_ref[...])
    a = jnp.exp(log_a)
    b = jnp.sqrt(jnp.tanh(-log_a) * (a * a + 1.0)) * (i * xc)

    row = lax.broadcasted_iota(jnp.int32, (tt, LRU_WIDTH), 0)
    k = 1
    while k < tt:
        keep = row >= k
        a_sh = jnp.where(keep, pltpu.roll(a, k, 0), 1.0)
        b_sh = jnp.where(keep, pltpu.roll(b, k, 0), 0.0)
        b = a * b_sh + b
        a = a * a_sh
        k *= 2
    h = b + a * h_sc[0:1, :]
    h_sc[0:1, :] = h[tt - 1:tt, :]
    out_ref[0] = h * _gelu_tanh(y)


def _block_diag_dense(w):
    n, c, d = w.shape
    out = jnp.zeros((n * c, n * d), w.dtype)
    for j in range(n):
        out = out.at[j * c:(j + 1) * c, j * d:(j + 1) * d].set(w[j])
    return out


def _lru(lru_xy, conv_w, conv_b, wa, ba, wx, bx, lam, batch, seq, tt=512):
    xy = lru_xy.reshape(batch, seq, 512)
    fixed = lambda b, t: (0, 0)
    vec = pl.BlockSpec((1, LRU_WIDTH), fixed)
    mat = pl.BlockSpec((LRU_WIDTH, LRU_WIDTH), fixed)
    out = pl.pallas_call(
        functools.partial(_lru_kernel, tt=tt),
        grid=(batch, seq // tt),
        in_specs=[pl.BlockSpec((1, tt, 512), lambda b, t: (b, t, 0)),
                  pl.BlockSpec((CONV_WIDTH, LRU_WIDTH), fixed), vec, mat, vec, mat, vec, vec],
        out_specs=pl.BlockSpec((1, tt, LRU_WIDTH), lambda b, t: (b, t, 0)),
        out_shape=jax.ShapeDtypeStruct((batch, seq, LRU_WIDTH), F32),
        scratch_shapes=[pltpu.VMEM((tt + 8, LRU_WIDTH), F32), pltpu.VMEM((8, LRU_WIDTH), F32)],
        compiler_params=_cparams(("parallel", "arbitrary")),
        name="rg_lru",
    )(xy, conv_w, conv_b.reshape(1, -1), _block_diag_dense(wa).astype(BF16), ba.reshape(1, -1),
      _block_diag_dense(wx).astype(BF16), bx.reshape(1, -1), lam.reshape(1, -1))
    return out.reshape(batch * seq, LRU_WIDTH)


def _mla_prep_kernel(dn_ref, qn_ref, wqa_ref, wqb_ref, kvn_ref, wk_ref, wvt_ref, cm_ref, sm_ref,
                     q_ref, k_ref, vt_ref):
    q_dn = dn_ref[:, 0:256]
    ms = jnp.sum(q_dn * q_dn, axis=-1, keepdims=True) * (1.0 / MLA_Q_RANK)
    ql = ((q_dn * lax.rsqrt(ms + EPS)) * qn_ref[...]).astype(BF16)
    kv_dn = dn_ref[:, 256:384]
    ms = jnp.mean(kv_dn * kv_dn, axis=-1, keepdims=True)
    kvl = ((kv_dn * lax.rsqrt(ms + EPS)) * kvn_ref[...]).astype(BF16)
    kpe = dn_ref[:, 384:512]
    cm, sm = cm_ref[...], sm_ref[...]
    scale = (MLA_NOPE + MLA_ROPE) ** -0.5 * LOG2E
    for hd in range(MLA_HEADS):
        sl = slice(hd * LANES, (hd + 1) * LANES)
        q_ref[:, sl] = ((_dot(ql, wqa_ref[:, sl]) * cm + _dot(ql, wqb_ref[:, sl]) * sm) * scale).astype(BF16)
        k_ref[:, sl] = (_dot(kvl, wk_ref[:, sl]) + kpe).astype(BF16)
    vt_ref[0] = _dot_nt(wvt_ref[...], kvl).astype(BF16)


def _prep_mla_weights(w_uq, w_ukv):
    zq = jnp.zeros((MLA_Q_RANK, 32), w_uq.dtype)
    qa, qb, kk, vv = [], [], [], []
    for h in range(MLA_HEADS):
        wq = w_uq[:, h * 96:(h + 1) * 96]
        nope, ropew = wq[:, :64], wq[:, 64:]
        qa.append(jnp.concatenate([nope, ropew, zq], axis=1))
        qb.append(jnp.concatenate([0 * nope, _rot_cols(ropew, 16), zq], axis=1))
        wkv = w_ukv[:, h * 128:(h + 1) * 128]
        kk.append(_pad_cols(wkv[:, :64], LANES))
        vv.append(wkv[:, 64:])
    pad_rows = lambda w: jnp.pad(w, ((0, 256 - MLA_Q_RANK), (0, 0)))
    return (pad_rows(jnp.concatenate(qa, axis=1)).astype(BF16), pad_rows(jnp.concatenate(qb, axis=1)).astype(BF16),
            jnp.concatenate(kk, axis=1).astype(BF16), jnp.concatenate(vv, axis=1).T.astype(BF16))


def _mla_prep(mla_dn, q_norm, w_uq, kv_norm, w_ukv, tables, batch, seq, tm=512):
    T = mla_dn.shape[0]
    npos = seq // tm
    _, _, cm, sm = tables
    wqa, wqb, wk, wvt = _prep_mla_weights(w_uq, w_ukv)
    row = lambda i: (i, 0)
    fixed = lambda i: (0, 0)
    tab = pl.BlockSpec((tm, LANES), lambda i: (i % npos, 0))
    return pl.pallas_call(
        _mla_prep_kernel,
        grid=(T // tm,),
        in_specs=[pl.BlockSpec((tm, 512), row), pl.BlockSpec((1, 256), fixed),
                  pl.BlockSpec((256, 768), fixed), pl.BlockSpec((256, 768), fixed),
                  pl.BlockSpec((1, 128), fixed), pl.BlockSpec((128, 768), fixed),
                  pl.BlockSpec((384, 128), fixed), tab, tab],
        out_specs=[pl.BlockSpec((tm, 768), row), pl.BlockSpec((tm, 768), row),
                   pl.BlockSpec((1, 384, tm), lambda i: (i // npos, 0, i % npos))],
        out_shape=(jax.ShapeDtypeStruct((T, 768), BF16), jax.ShapeDtypeStruct((T, 768), BF16),
                   jax.ShapeDtypeStruct((batch, 384, seq), BF16)),
        compiler_params=_cparams(("parallel",)),
        name="mla_prep",
    )(mla_dn, _pad_cols(q_norm.reshape(1, -1), 256), wqa, wqb, kv_norm.reshape(1, -1), wk, wvt, cm, sm)


def _flash_kernel(qt_ref, kt_ref, first_ref, last_ref, *refs, mode, k_per_head, v_per_head, gate_rows, tq, tk):
    refs = list(refs)
    q_ref, k_ref, vt_ref = refs[:3]
    o_ref, m_sc, acc_sc = refs[-7:-4]
    s_sc, mx_sc = refs[-4:-2], refs[-2:]
    extra = refs[3:-7]
    sb_ref, oh_ref = (extra[0], extra[1]) if mode == "select" else (None, None)
    gt_ref = extra[-1] if gate_rows is not None else None
    step = pl.program_id(1)
    qt = qt_ref[step]
    kt = kt_ref[step]

    @pl.when(first_ref[step] == 1)
    def _():
        m_sc[...] = jnp.full_like(m_sc, NEG_INF)
        acc_sc[...] = jnp.zeros_like(acc_sc)

    def update(masked, nk):
        ones = jnp.ones((SUM_ROWS, nk), BF16)
        if masked:
            kpos = kt * tk + lax.broadcasted_iota(jnp.int32, (nk, tq), 0)
            qpos = qt * tq + lax.broadcasted_iota(jnp.int32, (nk, tq), 1)
            dist = qpos - kpos
            valid = (dist >= 0) & (dist < WINDOW) if mode == "window" else dist >= 0

        def scores(j, slot):
            q = q_ref[0, :, j * LANES:(j + 1) * LANES]
            k = k_ref[0, 0:nk, j * LANES:(j + 1) * LANES] if k_per_head else k_ref[0, 0:nk, :]
            if mode == "select":
                q = jnp.concatenate([q, sb_ref[0, j // NSA_GROUP]], axis=1)
                k = jnp.concatenate([k, oh_ref[0:nk, :]], axis=1)
            s = _dot_nt(k, q)
            if masked:
                s = jnp.where(valid, s, NEG_INF)
            s_sc[slot][0:nk, :] = s
            mx_sc[slot][...] = jnp.max(s, axis=0, keepdims=True)

        def accumulate(j, slot):
            m_prev = m_sc[j]
            m_new = jnp.maximum(m_prev, mx_sc[slot][...])
            alpha = jnp.exp2(m_prev - m_new)
            p = jnp.exp2((s_sc[slot][0:nk, :] - m_new).astype(BF16))
            row = (j if v_per_head else j // NSA_GROUP) * HEAD_V
            vt = vt_ref[0, row:row + HEAD_V, 0:nk]
            acc_sc[j] = alpha * acc_sc[j] + _dot(jnp.concatenate([vt, ones], axis=0), p)
            m_sc[j] = m_new

        scores(0, 0)
        for j in range(N_ATT_HEADS):
            if j + 1 < N_ATT_HEADS:
                scores(j + 1, (j + 1) % 2)
            accumulate(j, j % 2)

    if mode == "window":
        update(True, tk)
    else:
        first_q, last_q = qt * tq, qt * tq + (tq - 1)
        crosses = kt * tk + (tk - 1) > first_q
        pl.when(jnp.logical_not(crosses))(lambda: update(False, tk))
        if tk > tq:
            half_only = kt * tk + tk // 2 > last_q
            pl.when(crosses & half_only)(lambda: update(True, tk // 2))
            pl.when(crosses & jnp.logical_not(half_only))(lambda: update(True, tk))
        else:
            pl.when(crosses)(lambda: update(True, tk))

    @pl.when(last_ref[step] == 1)
    def _():
        outs = []
        for j in range(N_ATT_HEADS):
            o = acc_sc[j, 0:HEAD_V, :] / acc_sc[j, HEAD_V:HEAD_V + 1, :]
            if gate_rows is not None:
                o = o * gt_ref[0, gate_rows[j]:gate_rows[j] + 1, :]
            outs.append(o)
        o_ref[0] = jnp.concatenate(outs, axis=0).T.astype(o_ref.dtype)


def _pair_tables(nq, mode, tq, tk):
    qts, kts, first, last = [], [], [], []
    for qt in range(nq):
        hi = (qt * tq + tq - 1) // tk
        lo = 0 if mode != "window" else max(0, (qt * tq - (WINDOW - 1)) // tk)
        for kt in range(lo, hi + 1):
            qts.append(qt)
            kts.append(kt)
            first.append(1 if kt == lo else 0)
            last.append(1 if kt == hi else 0)
    arr = lambda v: jnp.asarray(np.array(v, dtype=np.int32))
    return arr(qts), arr(kts), arr(first), arr(last)


def _flash(q, k, vt, *, mode, batch, seq, per_head_kv, k_width, k_col, vt_rows, vt_row, out_dtype,
           selbias=None, onehot=None, gate_rows=None, gates_t=None, tq=512, tk=512):
    tk = min(tk, seq)
    nq = seq // tq
    qt_tab, kt_tab, first_tab, last_tab = _pair_tables(nq, mode, tq, tk)
    n_steps = int(qt_tab.shape[0])
    qw = N_ATT_HEADS * LANES
    ow = N_ATT_HEADS * HEAD_V
    in_specs = [
        pl.BlockSpec((1, tq, qw), lambda b, s, qt, kt, f, l: (b, qt[s], 0)),
        pl.BlockSpec((1, tk, k_width), lambda b, s, qt, kt, f, l: (b, kt[s], k_col)),
        pl.BlockSpec((1, vt_rows, tk), lambda b, s, qt, kt, f, l: (b, vt_row, kt[s])),
    ]
    args = [q, k, vt]
    if mode == "select":
        in_specs.append(pl.BlockSpec((1, NSA_KV_HEADS, tq, LANES), lambda b, s, qt, kt, f, l: (b, 0, qt[s], 0)))
        in_specs.append(pl.BlockSpec((tk, LANES), lambda b, s, qt, kt, f, l: (kt[s], 0)))
        args += [selbias, onehot]
    if gate_rows is not None:
        in_specs.append(pl.BlockSpec((1, GATE_ROWS, tq), lambda b, s, qt, kt, f, l: (b, 0, qt[s])))
        args.append(gates_t)
    kern = functools.partial(_flash_kernel, mode=mode, k_per_head=per_head_kv, v_per_head=per_head_kv,
                             gate_rows=gate_rows, tq=tq, tk=tk)
    return pl.pallas_call(
        kern,
        grid_spec=pltpu.PrefetchScalarGridSpec(
            num_scalar_prefetch=4,
            grid=(batch, n_steps),
            in_specs=in_specs,
            out_specs=pl.BlockSpec((1, tq, ow), lambda b, s, qt, kt, f, l: (b, qt[s], 0)),
            scratch_shapes=[pltpu.VMEM((N_ATT_HEADS, 1, tq), F32),
                            pltpu.VMEM((N_ATT_HEADS, HEAD_V + SUM_ROWS, tq), F32),
                            pltpu.VMEM((tk, tq), F32), pltpu.VMEM((tk, tq), F32),
                            pltpu.VMEM((1, tq), F32), pltpu.VMEM((1, tq), F32)],
        ),
        out_shape=jax.ShapeDtypeStruct((batch, seq, ow), out_dtype),
        compiler_params=_cparams(("parallel", "arbitrary")),
        name="flash_" + mode,
    )(qt_tab, kt_tab, first_tab, last_tab, *args)


def _compress_kernel(kc_ref, vc_ref, pos_ref, w1a_ref, w1b_ref, w2_ref, ko_ref, vo_ref, *, nchunk):
    row = lax.broadcasted_iota(jnp.int32, (nchunk, LANES), 0)
    for br, src in enumerate((kc_ref, vc_ref)):
        r = src[0]
        pa = _dot((r + pos_ref[br, 0:1, :]).astype(BF16), w1a_ref[br])
        pb = _dot((r + pos_ref[br, 1:2, :]).astype(BF16), w1b_ref[br])
        hid = _gelu_tanh(pa + pltpu.roll(pb, nchunk - 1, 0))
        out = jnp.where(row < nchunk - 1, _dot(hid.astype(BF16), w2_ref[br]), 0.0)
        if br == 0:
            ko_ref[0] = out.astype(ko_ref.dtype)
        else:
            vo_ref[0] = out.T.astype(vo_ref.dtype)


def _prep_compress_weights(cmp_pos, cmp_w1, cmp_w2):
    half = CMP_LEN // 2
    pos, w1a, w1b, w2 = [], [], [], []
    for br in range(2):
        p = cmp_pos[br]
        tile = lambda ph: jnp.concatenate([ph, ph], axis=1).reshape(1, half * LANES)
        pos.append(jnp.concatenate([tile(p[:half]), tile(p[half:])], axis=0))
        w = cmp_w1[br].reshape(CMP_LEN, NSA_HEAD_DIM, CMP_HIDDEN)
        z = jnp.zeros_like(w[:half])

        def big(wh):
            g0 = jnp.concatenate([wh, z], axis=1)
            g1 = jnp.concatenate([z, wh], axis=1)
            return jnp.concatenate([g0.reshape(half * LANES, CMP_HIDDEN), g1.reshape(half * LANES, CMP_HIDDEN)], axis=1)

        w1a.append(big(w[:half]))
        w1b.append(big(w[half:]))
        w2.append(_block_diag_dense(jnp.stack([cmp_w2[br], cmp_w2[br]])))
    return (jnp.stack(pos), jnp.stack(w1a).astype(BF16), jnp.stack(w1b).astype(BF16), jnp.stack(w2).astype(BF16))


def _compress(kc, vc, cmp_pos, cmp_w1, cmp_w2, batch, seq):
    nchunk = seq // CMP_STRIDE
    width = CMP_STRIDE * LANES
    pos, w1a, w1b, w2 = _prep_compress_weights(cmp_pos, cmp_w1, cmp_w2)
    blk = pl.BlockSpec((1, nchunk, width), lambda b: (b, 0, 0))
    full3 = lambda shape: pl.BlockSpec(shape, lambda b: (0, 0, 0))
    return pl.pallas_call(
        functools.partial(_compress_kernel, nchunk=nchunk),
        grid=(batch,),
        in_specs=[blk, blk, full3(pos.shape), full3(w1a.shape), full3(w1b.shape), full3(w2.shape)],
        out_specs=[pl.BlockSpec((1, nchunk, LANES), lambda b: (b, 0, 0)),
                   pl.BlockSpec((1, LANES, nchunk), lambda b: (b, 0, 0))],
        out_shape=(jax.ShapeDtypeStruct((batch, nchunk, LANES), BF16),
                   jax.ShapeDtypeStruct((batch, LANES, nchunk), BF16)),
        compiler_params=_cparams(("parallel",)),
        name="nsa_compress",
    )(kc.reshape(batch, nchunk, width), vc.reshape(batch, nchunk, width), pos, w1a, w1b, w2)


def _select_kernel(q_ref, kc_ref, vct_ref, ovt_ref, gt_ref, oc_ref, sb_ref, ot_sc, imp_sc, *, tq, nchunk, variants):
    nsel = LANES
    qt = pl.program_id(1)
    col_ok = (qt * tq + lax.broadcasted_iota(jnp.int32, (1, tq), 1) >= CMP_LEN - 1).astype(F32)

    def attend(nc):
        kc = kc_ref[0, 0:nc, :]
        qpos = qt * tq + lax.broadcasted_iota(jnp.int32, (nc, tq), 1)
        cend = lax.broadcasted_iota(jnp.int32, (nc, tq), 0) * CMP_STRIDE + (CMP_LEN - 1)
        cvalid = cend <= qpos
        ovt = ovt_ref[:, 0:nc]
        for g in range(NSA_KV_HEADS):
            vct = vct_ref[0, g * HEAD_V:(g + 1) * HEAD_V, 0:nc]
            psum = jnp.zeros((nc, tq), F32)
            for pj in range(NSA_GROUP):
                hd = g * NSA_GROUP + pj
                q = q_ref[0, :, hd * LANES:(hd + 1) * LANES]
                s = jnp.where(cvalid, _dot_nt(kc, q), NEG_INF)
                m = jnp.max(s, axis=0, keepdims=True)
                e = jnp.exp2(s - m)
                p = e * (col_ok / jnp.maximum(jnp.sum(e, axis=0, keepdims=True), 1e-30))
                ot_sc[hd * HEAD_V:(hd + 1) * HEAD_V, :] = _dot(vct, p.astype(BF16)) * gt_ref[0, 3 * hd:3 * hd + 1, :]
                psum = psum + p
            hi = psum.astype(BF16)
            r1 = psum - hi.astype(F32)
            mid = r1.astype(BF16)
            lo = (r1 - mid.astype(F32)).astype(BF16)
            imp_sc[g] = _dot(ovt, hi) + _dot(ovt, mid) + _dot(ovt, lo)

    needed = (qt * tq + tq - CMP_STRIDE) // CMP_STRIDE
    lo_bound = 0
    for nc in variants:
        pl.when((needed > lo_bound) & (needed <= nc) if nc != variants[-1] else needed > lo_bound)(
            functools.partial(attend, nc))
        lo_bound = nc

    blk = lax.broadcasted_iota(jnp.int32, (nsel, tq), 0)
    blkf = blk.astype(F32)
    qp = qt * tq + lax.broadcasted_iota(jnp.int32, (nsel, tq), 1)
    valid = blk * SEL_LEN <= qp
    forced = (blk == 0) | (blk == qp // SEL_LEN)
    for g in range(NSA_KV_HEADS):
        score = jnp.where(valid, imp_sc[g], -1.0)
        score0 = jnp.where(forced, score + FORCE_BONUS, score)
        score = score0
        for _ in range(min(SEL_TOPK, nsel)):
            m = jnp.max(score, axis=0, keepdims=True)
            idx = jnp.min(jnp.where(score == m, blkf, float(nsel)), axis=0, keepdims=True)
            score = jnp.where(blkf == idx, -jnp.inf, score)
        bias = jnp.where(score0 >= 0.0, jnp.where(score == -jnp.inf, 0.0, NEG_INF), NEG_INF)
        sb_ref[0, g] = bias.T.astype(sb_ref.dtype)
    oc_ref[0] = ot_sc[...].T


def _overlap_matrix_t(nchunk, nsel):
    n = np.arange(nchunk)[None, :]
    s = np.arange(nsel)[:, None]
    cs = n * CMP_STRIDE
    ov = (cs < s * SEL_LEN + SEL_LEN) & (cs + CMP_LEN - 1 >= s * SEL_LEN) & (n < nchunk - 1)
    return jnp.asarray(ov.astype(np.float32)).astype(BF16)


def _select(qn, kcmp, vcmp_t, gates_t, batch, seq, tq=256):
    nchunk = seq // CMP_STRIDE
    nsel = LANES
    assert seq // SEL_LEN <= nsel
    qw = N_ATT_HEADS * LANES
    ow = N_ATT_HEADS * HEAD_V
    ovt = _overlap_matrix_t(nchunk, nsel)
    quarter = nchunk // 4
    variants = tuple(quarter * i for i in range(1, 5)) if quarter % LANES == 0 else (nchunk,)
    return pl.pallas_call(
        functools.partial(_select_kernel, tq=tq, nchunk=nchunk, variants=variants),
        grid=(batch, seq // tq),
        in_specs=[pl.BlockSpec((1, tq, qw), lambda b, t: (b, t, 0)),
                  pl.BlockSpec((1, nchunk, LANES), lambda b, t: (b, 0, 0)),
                  pl.BlockSpec((1, NSA_KV_HEADS * HEAD_V, nchunk), lambda b, t: (b, 0, 0)),
                  pl.BlockSpec((nsel, nchunk), lambda b, t: (0, 0)),
                  pl.BlockSpec((1, GATE_ROWS, tq), lambda b, t: (b, 0, t))],
        out_specs=[pl.BlockSpec((1, tq, ow), lambda b, t: (b, t, 0)),
                   pl.BlockSpec((1, NSA_KV_HEADS, tq, nsel), lambda b, t: (b, 0, t, 0))],
        out_shape=(jax.ShapeDtypeStruct((batch, seq, ow), F32),
                   jax.ShapeDtypeStruct((batch, NSA_KV_HEADS, seq, nsel), BF16)),
        scratch_shapes=[pltpu.VMEM((ow, tq), F32), pltpu.VMEM((NSA_KV_HEADS, nsel, tq), F32)],
        compiler_params=_cparams(("parallel", "parallel")),
        name="nsa_select",
    )(qn, kcmp, vcmp_t, ovt, gates_t)


def _out_proj_kernel(x_ref, a_ref, b_ref, oc_ref, os_ref, ow_ref, wa_ref, wb_ref, wc_ref, o_ref):
    acc = x_ref[...] + _dot(a_ref[...].astype(BF16), wa_ref[...])
    acc = acc + _dot(b_ref[...], wb_ref[...])
    c = (oc_ref[...] + os_ref[...]) + ow_ref[...]
    o_ref[...] = acc + _dot(c.astype(BF16), wc_ref[...])


def _out_proj(x2, a_out, b_out, o_c, o_s, o_w, w_out, tm=512):
    T = x2.shape[0]
    wa = w_out[0:256].astype(BF16)
    wb = w_out[256:640].astype(BF16)
    wc = w_out[640:1024].astype(BF16)
    row = lambda i: (i, 0)
    fixed = lambda i: (0, 0)
    act = lambda n: pl.BlockSpec((tm, n), row)
    return pl.pallas_call(
        _out_proj_kernel,
        grid=(T // tm,),
        in_specs=[act(D_MODEL), act(256), act(384), act(384), act(384), act(384),
                  pl.BlockSpec((256, D_MODEL), fixed), pl.BlockSpec((384, D_MODEL), fixed),
                  pl.BlockSpec((384, D_MODEL), fixed)],
        out_specs=pl.BlockSpec((tm, D_MODEL), row),
        out_shape=jax.ShapeDtypeStruct((T, D_MODEL), F32),
        compiler_params=_cparams(("parallel",)),
        name="out_proj",
    )(x2, a_out, b_out, o_c, o_s, o_w, wa, wb, wc)


def _final_norm(y, gf_ref):
    ms = jnp.mean(y * y, axis=-1, keepdims=True)
    return (y * lax.rsqrt(ms + EPS)) * gf_ref[...]


def _ffn_kernel(x_ref, g_ref, wg_ref, wu_ref, wd_ref, gf_ref, o_ref, h_sc, acc_sc, *, final_norm):
    f = pl.program_id(1)

    @pl.when(f == 0)
    def _():
        x = x_ref[...]
        ms = jnp.mean(x * x, axis=-1, keepdims=True)
        h_sc[...] = ((x * lax.rsqrt(ms + EPS)) * g_ref[...]).astype(BF16)
        acc_sc[...] = x

    h = h_sc[...]
    gte = _dot(h, wg_ref[...])
    up = _dot(h, wu_ref[...])
    act = (gte * _sigmoid(gte)) * up
    acc_sc[...] += _dot(act.astype(BF16), wd_ref[...])

    @pl.when(f == pl.num_programs(1) - 1)
    def _():
        y = acc_sc[...]
        o_ref[...] = _final_norm(y, gf_ref) if final_norm else y


def _ffn(x2, g, wg, wu, wd, g_final, final_norm, tm=512, tf=1408):
    T = x2.shape[0]
    nf = D_FF // tf
    return pl.pallas_call(
        functools.partial(_ffn_kernel, final_norm=final_norm),
        grid=(T // tm, nf),
        in_specs=[pl.BlockSpec((tm, D_MODEL), lambda i, f: (i, 0)), pl.BlockSpec((1, D_MODEL), lambda i, f: (0, 0)),
                  pl.BlockSpec((D_MODEL, tf), lambda i, f: (0, f)), pl.BlockSpec((D_MODEL, tf), lambda i, f: (0, f)),
                  pl.BlockSpec((tf, D_MODEL), lambda i, f: (f, 0)), pl.BlockSpec((1, D_MODEL), lambda i, f: (0, 0))],
        out_specs=pl.BlockSpec((tm, D_MODEL), lambda i, f: (i, 0)),
        out_shape=jax.ShapeDtypeStruct((T, D_MODEL), F32),
        scratch_shapes=[pltpu.VMEM((tm, D_MODEL), BF16), pltpu.VMEM((tm, D_MODEL), F32)],
        compiler_params=_cparams(("parallel", "arbitrary")),
        name="ffn_dense",
    )(x2, g.reshape(1, -1), wg.astype(BF16), wu.astype(BF16), wd.astype(BF16), g_final.reshape(1, -1))


def _moe_kernel(x_ref, g_ref, rw_ref, wg_ref, wu_ref, wd_ref, gf_ref, o_ref, h_sc, gate_sc, acc_sc, *,
                final_norm, tm):
    e = pl.program_id(1)
    lane = lax.broadcasted_iota(jnp.int32, (tm, LANES), 1)

    @pl.when(e == 0)
    def _():
        x = x_ref[...]
        ms = jnp.mean(x * x, axis=-1, keepdims=True)
        hf = (x * lax.rsqrt(ms + EPS)) * g_ref[...]
        h_sc[...] = hf.astype(BF16)
        acc_sc[...] = x
        h_hi = hf.astype(BF16)
        h_lo = (hf - h_hi.astype(F32)).astype(BF16)
        logits = (_dot(h_hi, rw_ref[0]) + _dot(h_lo, rw_ref[0]) + _dot(h_hi, rw_ref[1]))
        logits = jnp.where(lane < N_EXPERTS, logits, -jnp.inf)
        m1 = jnp.max(logits, axis=1, keepdims=True)
        i1 = jnp.min(jnp.where(logits == m1, lane, LANES), axis=1, keepdims=True)
        rest = jnp.where(lane == i1, -jnp.inf, logits)
        m2 = jnp.max(rest, axis=1, keepdims=True)
        i2 = jnp.min(jnp.where(rest == m2, lane, LANES), axis=1, keepdims=True)
        e2 = jnp.exp(m2 - m1)
        den = 1.0 + e2
        gate_sc[...] = jnp.where(lane == i1, 1.0 / den, 0.0) + jnp.where(lane == i2, e2 / den, 0.0)

    h = h_sc[...]
    gte = _dot(h, wg_ref[0])
    up = _dot(h, wu_ref[0])
    act = (gte * _sigmoid(gte)) * up
    y = _dot(act.astype(BF16), wd_ref[0])
    ge = jnp.sum(jnp.where(lane == e, gate_sc[...], 0.0), axis=1, keepdims=True)
    acc_sc[...] += ge * y

    @pl.when(e == pl.num_programs(1) - 1)
    def _():
        out = acc_sc[...]
        o_ref[...] = _final_norm(out, gf_ref) if final_norm else out


def _moe(x2, g, router_w, wg, wu, wd, g_final, final_norm, tm=512):
    T = x2.shape[0]
    rw = _pad_cols(router_w, LANES)
    rw_hi = rw.astype(BF16)
    rw_lo = (rw - rw_hi.astype(F32)).astype(BF16)
    rw2 = jnp.stack([rw_hi, rw_lo])
    return pl.pallas_call(
        functools.partial(_moe_kernel, final_norm=final_norm, tm=tm),
        grid=(T // tm, N_EXPERTS),
        in_specs=[pl.BlockSpec((tm, D_MODEL), lambda i, e: (i, 0)), pl.BlockSpec((1, D_MODEL), lambda i, e: (0, 0)),
                  pl.BlockSpec((2, D_MODEL, LANES), lambda i, e: (0, 0, 0)),
                  pl.BlockSpec((1, D_MODEL, D_FF_EXPERT), lambda i, e: (e, 0, 0)),
                  pl.BlockSpec((1, D_MODEL, D_FF_EXPERT), lambda i, e: (e, 0, 0)),
                  pl.BlockSpec((1, D_FF_EXPERT, D_MODEL), lambda i, e: (e, 0, 0)),
                  pl.BlockSpec((1, D_MODEL), lambda i, e: (0, 0))],
        out_specs=pl.BlockSpec((tm, D_MODEL), lambda i, e: (i, 0)),
        out_shape=jax.ShapeDtypeStruct((T, D_MODEL), F32),
        scratch_shapes=[pltpu.VMEM((tm, D_MODEL), BF16), pltpu.VMEM((tm, LANES), F32),
                        pltpu.VMEM((tm, D_MODEL), F32)],
        compiler_params=_cparams(("parallel", "arbitrary")),
        name="moe_dense",
    )(x2, g.reshape(1, -1), rw2, wg.astype(BF16), wu.astype(BF16), wd.astype(BF16), g_final.reshape(1, -1))


def _mixer(x2, batch, seq, tables, onehot, norm_g, w_in, conv_w, conv_b, wa, ba, wx, bx, lam,
           q_norm, w_uq, kv_norm, w_ukv, cmp_pos, cmp_w1, cmp_w2, gate_b, w_out):
    w_cat, w_t = _prep_w_in(w_in)
    lru_xy, mla_dn, qn, kc, vc, ksw, vt_sw, gates_t = _in_proj(x2, norm_g, w_cat, w_t, tables, gate_b, batch, seq)
    a_out = _lru(lru_xy, conv_w, conv_b, wa, ba, wx, bx, lam, batch, seq)

    q_m, k_m, vt_m = _mla_prep(mla_dn, q_norm, w_uq, kv_norm, w_ukv, tables, batch, seq)
    b3 = lambda t: t.reshape(batch, seq, t.shape[-1])
    heads = tuple(range(N_ATT_HEADS))
    b_out = _flash(b3(q_m), b3(k_m), vt_m, mode="causal", batch=batch, seq=seq, per_head_kv=True,
                   k_width=N_ATT_HEADS * LANES, k_col=0, vt_rows=N_ATT_HEADS * HEAD_V, vt_row=0, out_dtype=BF16,
                   tk=LONG_KEY_TILE)

    kcmp, vcmp_t = _compress(kc, vc, cmp_pos, cmp_w1, cmp_w2, batch, seq)
    qn3 = b3(qn)
    o_c, selbias = _select(qn3, kcmp, vcmp_t, gates_t, batch, seq)
    ksw3 = b3(ksw)
    nsa_kw = dict(batch=batch, seq=seq, per_head_kv=False, k_width=LANES,
                  vt_rows=NSA_KV_HEADS * HEAD_V, out_dtype=F32, gates_t=gates_t)
    o_s = _flash(qn3, ksw3, vt_sw, mode="select", k_col=0, vt_row=0, selbias=selbias,
                 onehot=onehot, gate_rows=tuple(3 * h + 1 for h in heads), tk=LONG_KEY_TILE, **nsa_kw)
    o_w = _flash(qn3, ksw3, vt_sw, mode="window", k_col=1, vt_row=1,
                 gate_rows=tuple(3 * h + 2 for h in heads), **nsa_kw)
    T = batch * seq
    flat = lambda t: t.reshape(T, t.shape[-1])
    return _out_proj(x2, a_out, flat(b_out), flat(o_c), flat(o_s), flat(o_w), w_out)


def kernel(x, norm_mix, w_in, lru_conv_w, lru_conv_b, lru_wa, lru_ba, lru_wx, lru_bx, lru_lambda, mla_q_norm, mla_w_uq, mla_kv_norm, mla_w_ukv, nsa_cmp_pos, nsa_cmp_w1, nsa_cmp_w2, nsa_gate_b, w_out, norm_ffn, ffn_w_gate, ffn_w_up, ffn_w_down, router_w, moe_w_gate, moe_w_up, moe_w_down, norm_final):
    batch, seq, _ = x.shape
    depth = norm_mix.shape[0]
    tables = _rope_tables(seq)
    blk_id = np.arange(seq)[:, None] // SEL_LEN
    onehot = jnp.asarray((blk_id == np.arange(LANES)[None, :]).astype(np.float32)).astype(BF16)
    x2 = x.reshape(batch * seq, D_MODEL)
    for l in range(depth):
        x2 = _mixer(x2, batch, seq, tables, onehot, norm_mix[l], w_in[l], lru_conv_w[l], lru_conv_b[l],
                    lru_wa[l], lru_ba[l], lru_wx[l], lru_bx[l], lru_lambda[l], mla_q_norm[l], mla_w_uq[l],
                    mla_kv_norm[l], mla_w_ukv[l], nsa_cmp_pos[l], nsa_cmp_w1[l], nsa_cmp_w2[l],
                    nsa_gate_b[l], w_out[l])
        last = l == depth - 1
        j = l // 2
        if l % 2 == 0:
            x2 = _ffn(x2, norm_ffn[l], ffn_w_gate[j], ffn_w_up[j], ffn_w_down[j], norm_final, last)
        else:
            x2 = _moe(x2, norm_ffn[l], router_w[j], moe_w_gate[j], moe_w_up[j], moe_w_down[j], norm_final, last)
    if depth == 0:
        raise ValueError("depth must be positive")
    return x2.reshape(batch, seq, D_MODEL)
```

```python
import functools
import math

import numpy as np
import jax
import jax.numpy as jnp
from jax import lax
from jax.experimental import pallas as pl
from jax.experimental.pallas import tpu as pltpu

F32 = jnp.float32
BF16 = jnp.bfloat16

D_MODEL = 1024
LRU_WIDTH = 256
LRU_BLOCKS = 4
LRU_BLOCK_W = LRU_WIDTH // LRU_BLOCKS
CONV_WIDTH = 4
LRU_C = 8.0
MLA_HEADS = 6
MLA_NOPE = 64
MLA_ROPE = 32
MLA_V = 64
MLA_Q_RANK = 192
MLA_KV_RANK = 128
NSA_HEADS = 6
NSA_KV_HEADS = 2
NSA_GROUP = NSA_HEADS // NSA_KV_HEADS
NSA_HEAD_DIM = 64
CMP_LEN = 32
CMP_STRIDE = 16
CMP_HIDDEN = 128
SEL_LEN = 64
SEL_TOPK = 16
WINDOW = 512
FORCE_BONUS = 1.0e3
D_FF = 2816
N_EXPERTS = 8
D_FF_EXPERT = 1408
ROPE_THETA = 10000.0
EPS = 1e-6
NEG_INF = -1.0e30

LANES = 128
VMEM_LIMIT = 56 * 1024 * 1024

C_LRU = 0
C_MLA = 512
C_KPE_ROT = 1024
C_QN = 1152
C_QN_ROT = 1536
C_KC = 1920
C_KC_ROT = 2048
C_VC = 2176
C_KS = 2304
C_KS_ROT = 2432
C_KW = 2560
C_KW_ROT = 2688
IN_COLS_PAD = 2816
N_ATT_HEADS = 6
HEAD_V = 64
LONG_KEY_TILE = 1024
SUM_ROWS = 16
GATE_ROWS = 32
LOG2E = 1.4426950408889634


def _cparams(sem, flags=None):
    return pltpu.CompilerParams(dimension_semantics=sem, vmem_limit_bytes=VMEM_LIMIT, flags=flags)


def _gelu_tanh(x):
    return 0.5 * x * (1.0 + jnp.tanh(math.sqrt(2.0 / math.pi) * (x + 0.044715 * (x * x * x))))


def _sigmoid(x):
    return 1.0 / (1.0 + jnp.exp(-x))


def _dot(a, b):
    return jnp.dot(a, b, preferred_element_type=F32)


def _dot_nt(a, b):
    return lax.dot_general(a, b, (((1,), (1,)), ((), ())), preferred_element_type=F32)


def _rot_cols(w, half):
    return jnp.concatenate([-w[:, half:], w[:, :half]], axis=1)


def _pad_cols(w, n):
    return jnp.pad(w, ((0, 0), (0, n - w.shape[1])))


def _prep_w_in(w_in):
    splits = np.cumsum([256, 256, 192, 128, 32, 384, 128, 128, 128, 128, 128, 128, 18])[:-1].tolist()
    (x_l, y_l, q_dn, kv_dn, k_pe, q_n, kc, vc, ks, vs, kw, vw, g_n) = jnp.split(w_in, splits, axis=1)
    z = lambda n: jnp.zeros((w_in.shape[0], n), w_in.dtype)

    def kpe_slot(w):
        return jnp.concatenate([z(64), w, z(32)], axis=1)

    def qn_slots(fn):
        out = []
        for p in range(NSA_GROUP):
            for h in (p, p + NSA_GROUP):
                out.append(fn(q_n[:, h * 64:(h + 1) * 64]))
        return jnp.concatenate(out, axis=1)

    def kv_rot(w):
        return jnp.concatenate([_rot_cols(w[:, :64], 32), _rot_cols(w[:, 64:], 32)], axis=1)

    cols = [x_l, y_l, _pad_cols(q_dn, 256), kv_dn, kpe_slot(k_pe), kpe_slot(_rot_cols(k_pe, 16)),
            qn_slots(lambda w: w), qn_slots(lambda w: _rot_cols(w, 32)),
            kc, kv_rot(kc), vc, ks, kv_rot(ks), kw, kv_rot(kw)]
    w = jnp.concatenate(cols, axis=1)
    assert w.shape[1] == IN_COLS_PAD
    w_t = jnp.concatenate([vs, vw, _pad_cols(g_n, GATE_ROWS)], axis=1).T
    return w.astype(BF16), w_t.astype(BF16)


def _rope_tables(seq):
    pos = jnp.arange(seq, dtype=F32)[:, None]
    inv64 = ROPE_THETA ** (-jnp.arange(32, dtype=F32) * 2.0 / 64)
    inv32 = ROPE_THETA ** (-jnp.arange(16, dtype=F32) * 2.0 / 32)
    a64 = pos * inv64[None, :]
    a32 = pos * inv32[None, :]
    c64, s64 = jnp.cos(a64), jnp.sin(a64)
    c32, s32 = jnp.cos(a32), jnp.sin(a32)
    cos_n = jnp.concatenate([c64, c64, c64, c64], axis=1)
    sin_n = jnp.concatenate([s64, s64, s64, s64], axis=1)
    one = jnp.ones((seq, 64), F32)
    cos_m = jnp.concatenate([one, c32, c32, one[:, :32]], axis=1)
    sin_m = jnp.concatenate([0 * one, s32, s32, 0 * one[:, :32]], axis=1)
    return cos_n, sin_n, cos_m, sin_m


def _in_proj_kernel(x_ref, g_ref, w_ref, wt_ref, cn_ref, sn_ref, cm_ref, sm_ref, gb_ref,
                    lru_ref, mla_ref, qn_ref, kc_ref, vc_ref, ksw_ref, vt_ref, gt_ref):
    x = x_ref[...]
    ms = jnp.mean(x * x, axis=-1, keepdims=True)
    h = ((x * lax.rsqrt(ms + EPS)) * g_ref[...]).astype(BF16)

    def proj(c0, n):
        return _dot(h, w_ref[:, c0:c0 + n])

    cn, sn = cn_ref[...], sn_ref[...]
    lru_ref[...] = proj(C_LRU, 512)
    mla_ref[:, 0:384] = proj(C_MLA, 384)
    mla_ref[:, 384:512] = proj(C_MLA + 384, 128) * cm_ref[...] + proj(C_KPE_ROT, 128) * sm_ref[...]
    scale = NSA_HEAD_DIM ** -0.5 * LOG2E
    low_half = lax.broadcasted_iota(jnp.int32, cn.shape, 1) < NSA_HEAD_DIM
    for p in range(NSA_GROUP):
        a = proj(C_QN + p * LANES, LANES)
        b = proj(C_QN_ROT + p * LANES, LANES)
        r = (a * cn + b * sn) * scale
        hi_hd = p + NSA_GROUP
        qn_ref[:, p * LANES:(p + 1) * LANES] = jnp.where(low_half, r, 0.0).astype(BF16)
        qn_ref[:, hi_hd * LANES:(hi_hd + 1) * LANES] = jnp.where(low_half, 0.0, r).astype(BF16)
    kc_ref[...] = proj(C_KC, 128) * cn + proj(C_KC_ROT, 128) * sn
    vc_ref[...] = proj(C_VC, 128)
    ksw_ref[:, 0:128] = (proj(C_KS, 128) * cn + proj(C_KS_ROT, 128) * sn).astype(BF16)
    ksw_ref[:, 128:256] = (proj(C_KW, 128) * cn + proj(C_KW_ROT, 128) * sn).astype(BF16)
    vt_ref[0] = _dot_nt(wt_ref[0:256, :], h).astype(BF16)
    gt_ref[0] = _sigmoid(_dot_nt(wt_ref[256:256 + GATE_ROWS, :], h) + gb_ref[...])


def _in_proj(x2, g, w_cat, w_t, tables, gate_b, batch, seq, tm=512):
    T = x2.shape[0]
    nt = T // tm
    npos = seq // tm
    cn, sn, cm, sm = tables
    row = lambda i: (i, 0)
    fixed = lambda i: (0, 0)
    posmap = lambda i: (i % npos, 0)
    tab_spec = pl.BlockSpec((tm, LANES), posmap)
    out_shapes = (
        jax.ShapeDtypeStruct((T, 512), F32),
        jax.ShapeDtypeStruct((T, 512), F32),
        jax.ShapeDtypeStruct((T, 768), BF16),
        jax.ShapeDtypeStruct((T, 128), F32),
        jax.ShapeDtypeStruct((T, 128), F32),
        jax.ShapeDtypeStruct((T, 256), BF16),
    )
    t_shapes = (jax.ShapeDtypeStruct((batch, 256, seq), BF16),
                jax.ShapeDtypeStruct((batch, GATE_ROWS, seq), F32))
    tmap = lambda i: (i // npos, 0, i % npos)
    gate_b_col = jnp.pad(gate_b, (0, GATE_ROWS - gate_b.shape[0])).reshape(GATE_ROWS, 1)
    return pl.pallas_call(
        _in_proj_kernel,
        grid=(nt,),
        in_specs=[pl.BlockSpec((tm, D_MODEL), row), pl.BlockSpec((1, D_MODEL), fixed),
                  pl.BlockSpec((D_MODEL, IN_COLS_PAD), fixed), pl.BlockSpec((256 + GATE_ROWS, D_MODEL), fixed),
                  tab_spec, tab_spec, tab_spec, tab_spec, pl.BlockSpec((GATE_ROWS, 1), fixed)],
        out_specs=[pl.BlockSpec((tm, s.shape[1]), row) for s in out_shapes]
        + [pl.BlockSpec((1, s.shape[1], tm), tmap) for s in t_shapes],
        out_shape=out_shapes + t_shapes,
        compiler_params=_cparams(("parallel",)),
        name="in_proj",
    )(x2, g.reshape(1, -1), w_cat, w_t, cn, sn, cm, sm, gate_b_col)


def _lru_kernel(xy_ref, cw_ref, cb_ref, wa_ref, ba_ref, wx_ref, bx_ref, lam_ref, out_ref,
                xe_sc, h_sc, *, tt):
    t = pl.program_id(1)

    @pl.when(t == 0)
    def _():
        xe_sc[0:8, :] = jnp.zeros((8, LRU_WIDTH), F32)
        h_sc[...] = jnp.zeros_like(h_sc)

    x = xy_ref[0, :, 0:LRU_WIDTH]
    y = xy_ref[0, :, LRU_WIDTH:2 * LRU_WIDTH]
    xe_sc[8:8 + tt, :] = x
    xc = cb_ref[...]
    for k in range(CONV_WIDTH):
        off = 8 - (CONV_WIDTH - 1) + k
        xc = xc + xe_sc[off:off + tt, :] * cw_ref[k:k + 1, :]
    xe_sc[0:8, :] = x[tt - 8:tt, :]

    xb = xc.astype(BF16)
    r = _sigmoid(_dot(xb, wa_ref[...]) + ba_ref[...])
    i = _sigmoid(_dot(xb, wx_ref[...]) + bx_ref[...])
    log_a = (-LRU_C * r) * jax.nn.softplus(-lam_ref[...])
    a = jnp.exp(log_a)
    b = jnp.sqrt(jnp.tanh(-log_a) * (a * a + 1.0)) * (i * xc)

    row = lax.broadcasted_iota(jnp.int32, (tt, LRU_WIDTH), 0)
    k = 1
    while k < tt:
        keep = row >= k
        a_sh = jnp.where(keep, pltpu.roll(a, k, 0), 1.0)
        b_sh = jnp.where(keep, pltpu.roll(b, k, 0), 0.0)
        b = a * b_sh + b
        a = a * a_sh
        k *= 2
    h = b + a * h_sc[0:1, :]
    h_sc[0:1, :] = h[tt - 1:tt, :]
    out_ref[0] = h * _gelu_tanh(y)


def _block_diag_dense(w):
    n, c, d = w.shape
    out = jnp.zeros((n * c, n * d), w.dtype)
    for j in range(n):
        out = out.at[j * c:(j + 1) * c, j * d:(j + 1) * d].set(w[j])
    return out


def _lru(lru_xy, conv_w, conv_b, wa, ba, wx, bx, lam, batch, seq, tt=512):
    xy = lru_xy.reshape(batch, seq, 512)
    fixed = lambda b, t: (0, 0)
    vec = pl.BlockSpec((1, LRU_WIDTH), fixed)
    mat = pl.BlockSpec((LRU_WIDTH, LRU_WIDTH), fixed)
    out = pl.pallas_call(
        functools.partial(_lru_kernel, tt=tt),
        grid=(batch, seq // tt),
        in_specs=[pl.BlockSpec((1, tt, 512), lambda b, t: (b, t, 0)),
                  pl.BlockSpec((CONV_WIDTH, LRU_WIDTH), fixed), vec, mat, vec, mat, vec, vec],
        out_specs=pl.BlockSpec((1, tt, LRU_WIDTH), lambda b, t: (b, t, 0)),
        out_shape=jax.ShapeDtypeStruct((batch, seq, LRU_WIDTH), F32),
        scratch_shapes=[pltpu.VMEM((tt + 8, LRU_WIDTH), F32), pltpu.VMEM((8, LRU_WIDTH), F32)],
        compiler_params=_cparams(("parallel", "arbitrary")),
        name="rg_lru",
    )(xy, conv_w, conv_b.reshape(1, -1), _block_diag_dense(wa).astype(BF16), ba.reshape(1, -1),
      _block_diag_dense(wx).astype(BF16), bx.reshape(1, -1), lam.reshape(1, -1))
    return out.reshape(batch * seq, LRU_WIDTH)


def _mla_prep_kernel(dn_ref, qn_ref, wqa_ref, wqb_ref, kvn_ref, wk_ref, wvt_ref, cm_ref, sm_ref,
                     q_ref, k_ref, vt_ref):
    q_dn = dn_ref[:, 0:256]
    ms = jnp.sum(q_dn * q_dn, axis=-1, keepdims=True) * (1.0 / MLA_Q_RANK)
    ql = ((q_dn * lax.rsqrt(ms + EPS)) * qn_ref[...]).astype(BF16)
    kv_dn = dn_ref[:, 256:384]
    ms = jnp.mean(kv_dn * kv_dn, axis=-1, keepdims=True)
    kvl = ((kv_dn * lax.rsqrt(ms + EPS)) * kvn_ref[...]).astype(BF16)
    kpe = dn_ref[:, 384:512]
    cm, sm = cm_ref[...], sm_ref[...]
    scale = (MLA_NOPE + MLA_ROPE) ** -0.5 * LOG2E
    for hd in range(MLA_HEADS):
        sl = slice(hd * LANES, (hd + 1) * LANES)
        q_ref[:, sl] = ((_dot(ql, wqa_ref[:, sl]) * cm + _dot(ql, wqb_ref[:, sl]) * sm) * scale).astype(BF16)
        k_ref[:, sl] = (_dot(kvl, wk_ref[:, sl]) + kpe).astype(BF16)
    vt_ref[0] = _dot_nt(wvt_ref[...], kvl).astype(BF16)


def _prep_mla_weights(w_uq, w_ukv):
    zq = jnp.zeros((MLA_Q_RANK, 32), w_uq.dtype)
    qa, qb, kk, vv = [], [], [], []
    for h in range(MLA_HEADS):
        wq = w_uq[:, h * 96:(h + 1) * 96]
        nope, ropew = wq[:, :64], wq[:, 64:]
        qa.append(jnp.concatenate([nope, ropew, zq], axis=1))
        qb.append(jnp.concatenate([0 * nope, _rot_cols(ropew, 16), zq], axis=1))
        wkv = w_ukv[:, h * 128:(h + 1) * 128]
        kk.append(_pad_cols(wkv[:, :64], LANES))
        vv.append(wkv[:, 64:])
    pad_rows = lambda w: jnp.pad(w, ((0, 256 - MLA_Q_RANK), (0, 0)))
    return (pad_rows(jnp.concatenate(qa, axis=1)).astype(BF16), pad_rows(jnp.concatenate(qb, axis=1)).astype(BF16),
            jnp.concatenate(kk, axis=1).astype(BF16), jnp.concatenate(vv, axis=1).T.astype(BF16))


def _mla_prep(mla_dn, q_norm, w_uq, kv_norm, w_ukv, tables, batch, seq, tm=512):
    T = mla_dn.shape[0]
    npos = seq // tm
    _, _, cm, sm = tables
    wqa, wqb, wk, wvt = _prep_mla_weights(w_uq, w_ukv)
    row = lambda i: (i, 0)
    fixed = lambda i: (0, 0)
    tab = pl.BlockSpec((tm, LANES), lambda i: (i % npos, 0))
    return pl.pallas_call(
        _mla_prep_kernel,
        grid=(T // tm,),
        in_specs=[pl.BlockSpec((tm, 512), row), pl.BlockSpec((1, 256), fixed),
                  pl.BlockSpec((256, 768), fixed), pl.BlockSpec((256, 768), fixed),
                  pl.BlockSpec((1, 128), fixed), pl.BlockSpec((128, 768), fixed),
                  pl.BlockSpec((384, 128), fixed), tab, tab],
        out_specs=[pl.BlockSpec((tm, 768), row), pl.BlockSpec((tm, 768), row),
                   pl.BlockSpec((1, 384, tm), lambda i: (i // npos, 0, i % npos))],
        out_shape=(jax.ShapeDtypeStruct((T, 768), BF16), jax.ShapeDtypeStruct((T, 768), BF16),
                   jax.ShapeDtypeStruct((batch, 384, seq), BF16)),
        compiler_params=_cparams(("parallel",)),
        name="mla_prep",
    )(mla_dn, _pad_cols(q_norm.reshape(1, -1), 256), wqa, wqb, kv_norm.reshape(1, -1), wk, wvt, cm, sm)


def _flash_kernel(qt_ref, kt_ref, first_ref, last_ref, *refs, mode, k_per_head, v_per_head, gate_rows, tq, tk):
    refs = list(refs)
    q_ref, k_ref, vt_ref = refs[:3]
    o_ref, m_sc, acc_sc = refs[-7:-4]
    s_sc, mx_sc = refs[-4:-2], refs[-2:]
    extra = refs[3:-7]
    sb_ref, oh_ref = (extra[0], extra[1]) if mode == "select" else (None, None)
    gt_ref = extra[-1] if gate_rows is not None else None
    step = pl.program_id(1)
    qt = qt_ref[step]
    kt = kt_ref[step]

    @pl.when(first_ref[step] == 1)
    def _():
        m_sc[...] = jnp.full_like(m_sc, NEG_INF)
        acc_sc[...] = jnp.zeros_like(acc_sc)

    def update(masked, nk):
        ones = jnp.ones((SUM_ROWS, nk), BF16)
        if masked:
            kpos = kt * tk + lax.broadcasted_iota(jnp.int32, (nk, tq), 0)
            qpos = qt * tq + lax.broadcasted_iota(jnp.int32, (nk, tq), 1)
            dist = qpos - kpos
            valid = (dist >= 0) & (dist < WINDOW) if mode == "window" else dist >= 0

        def scores(j, slot):
            q = q_ref[0, :, j * LANES:(j + 1) * LANES]
            k = k_ref[0, 0:nk, j * LANES:(j + 1) * LANES] if k_per_head else k_ref[0, 0:nk, :]
            if mode == "select":
                q = jnp.concatenate([q, sb_ref[0, j // NSA_GROUP]], axis=1)
                k = jnp.concatenate([k, oh_ref[0:nk, :]], axis=1)
            s = _dot_nt(k, q)
            if masked:
                s = jnp.where(valid, s, NEG_INF)
            s_sc[slot][0:nk, :] = s
            mx_sc[slot][...] = jnp.max(s, axis=0, keepdims=True)

        def accumulate(j, slot):
            m_prev = m_sc[j]
            m_new = jnp.maximum(m_prev, mx_sc[slot][...])
            alpha = jnp.exp2(m_prev - m_new)
            p = jnp.exp2((s_sc[slot][0:nk, :] - m_new).astype(BF16))
            row = (j if v_per_head else j // NSA_GROUP) * HEAD_V
            vt = vt_ref[0, row:row + HEAD_V, 0:nk]
            acc_sc[j] = alpha * acc_sc[j] + _dot(jnp.concatenate([vt, ones], axis=0), p)
            m_sc[j] = m_new

        scores(0, 0)
        for j in range(N_ATT_HEADS):
            if j + 1 < N_ATT_HEADS:
                scores(j + 1, (j + 1) % 2)
            accumulate(j, j % 2)

    if mode == "window":
        update(True, tk)
    else:
        first_q, last_q = qt * tq, qt * tq + (tq - 1)
        crosses = kt * tk + (tk - 1) > first_q
        pl.when(jnp.logical_not(crosses))(lambda: update(False, tk))
        if tk > tq:
            half_only = kt * tk + tk // 2 > last_q
            pl.when(crosses & half_only)(lambda: update(True, tk // 2))
            pl.when(crosses & jnp.logical_not(half_only))(lambda: update(True, tk))
        else:
            pl.when(crosses)(lambda: update(True, tk))

    @pl.when(last_ref[step] == 1)
    def _():
        outs = []
        for j in range(N_ATT_HEADS):
            o = acc_sc[j, 0:HEAD_V, :] / acc_sc[j, HEAD_V:HEAD_V + 1, :]
            if gate_rows is not None:
                o = o * gt_ref[0, gate_rows[j]:gate_rows[j] + 1, :]
            outs.append(o)
        o_ref[0] = jnp.concatenate(outs, axis=0).T.astype(o_ref.dtype)


def _pair_tables(nq, mode, tq, tk):
    qts, kts, first, last = [], [], [], []
    for qt in range(nq):
        hi = (qt * tq + tq - 1) // tk
        lo = 0 if mode != "window" else max(0, (qt * tq - (WINDOW - 1)) // tk)
        for kt in range(lo, hi + 1):
            qts.append(qt)
            kts.append(kt)
            first.append(1 if kt == lo else 0)
            last.append(1 if kt == hi else 0)
    arr = lambda v: jnp.asarray(np.array(v, dtype=np.int32))
    return arr(qts), arr(kts), arr(first), arr(last)


def _flash(q, k, vt, *, mode, batch, seq, per_head_kv, k_width, k_col, vt_rows, vt_row, out_dtype,
           selbias=None, onehot=None, gate_rows=None, gates_t=None, tq=512, tk=512):
    tk = min(tk, seq)
    nq = seq // tq
    qt_tab, kt_tab, first_tab, last_tab = _pair_tables(nq, mode, tq, tk)
    n_steps = int(qt_tab.shape[0])
    qw = N_ATT_HEADS * LANES
    ow = N_ATT_HEADS * HEAD_V
    in_specs = [
        pl.BlockSpec((1, tq, qw), lambda b, s, qt, kt, f, l: (b, qt[s], 0)),
        pl.BlockSpec((1, tk, k_width), lambda b, s, qt, kt, f, l: (b, kt[s], k_col)),
        pl.BlockSpec((1, vt_rows, tk), lambda b, s, qt, kt, f, l: (b, vt_row, kt[s])),
    ]
    args = [q, k, vt]
    if mode == "select":
        in_specs.append(pl.BlockSpec((1, NSA_KV_HEADS, tq, LANES), lambda b, s, qt, kt, f, l: (b, 0, qt[s], 0)))
        in_specs.append(pl.BlockSpec((tk, LANES), lambda b, s, qt, kt, f, l: (kt[s], 0)))
        args += [selbias, onehot]
    if gate_rows is not None:
        in_specs.append(pl.BlockSpec((1, GATE_ROWS, tq), lambda b, s, qt, kt, f, l: (b, 0, qt[s])))
        args.append(gates_t)
    kern = functools.partial(_flash_kernel, mode=mode, k_per_head=per_head_kv, v_per_head=per_head_kv,
                             gate_rows=gate_rows, tq=tq, tk=tk)
    return pl.pallas_call(
        kern,
        grid_spec=pltpu.PrefetchScalarGridSpec(
            num_scalar_prefetch=4,
            grid=(batch, n_steps),
            in_specs=in_specs,
            out_specs=pl.BlockSpec((1, tq, ow), lambda b, s, qt, kt, f, l: (b, qt[s], 0)),
            scratch_shapes=[pltpu.VMEM((N_ATT_HEADS, 1, tq), F32),
                            pltpu.VMEM((N_ATT_HEADS, HEAD_V + SUM_ROWS, tq), F32),
                            pltpu.VMEM((tk, tq), F32), pltpu.VMEM((tk, tq), F32),
                            pltpu.VMEM((1, tq), F32), pltpu.VMEM((1, tq), F32)],
        ),
        out_shape=jax.ShapeDtypeStruct((batch, seq, ow), out_dtype),
        compiler_params=_cparams(("parallel", "arbitrary")),
        name="flash_" + mode,
    )(qt_tab, kt_tab, first_tab, last_tab, *args)


def _compress_kernel(kc_ref, vc_ref, pos_ref, w1a_ref, w1b_ref, w2_ref, ko_ref, vo_ref, *, nchunk):
    row = lax.broadcasted_iota(jnp.int32, (nchunk, LANES), 0)
    for br, src in enumerate((kc_ref, vc_ref)):
        r = src[0]
        pa = _dot((r + pos_ref[br, 0:1, :]).astype(BF16), w1a_ref[br])
        pb = _dot((r + pos_ref[br, 1:2, :]).astype(BF16), w1b_ref[br])
        hid = _gelu_tanh(pa + pltpu.roll(pb, nchunk - 1, 0))
        out = jnp.where(row < nchunk - 1, _dot(hid.astype(BF16), w2_ref[br]), 0.0)
        if br == 0:
            ko_ref[0] = out.astype(ko_ref.dtype)
        else:
            vo_ref[0] = out.T.astype(vo_ref.dtype)


def _prep_compress_weights(cmp_pos, cmp_w1, cmp_w2):
    half = CMP_LEN // 2
    pos, w1a, w1b, w2 = [], [], [], []
    for br in range(2):
        p = cmp_pos[br]
        tile = lambda ph: jnp.concatenate([ph, ph], axis=1).reshape(1, half * LANES)
        pos.append(jnp.concatenate([tile(p[:half]), tile(p[half:])], axis=0))
        w = cmp_w1[br].reshape(CMP_LEN, NSA_HEAD_DIM, CMP_HIDDEN)
        z = jnp.zeros_like(w[:half])

        def big(wh):
            g0 = jnp.concatenate([wh, z], axis=1)
            g1 = jnp.concatenate([z, wh], axis=1)
            return jnp.concatenate([g0.reshape(half * LANES, CMP_HIDDEN), g1.reshape(half * LANES, CMP_HIDDEN)], axis=1)

        w1a.append(big(w[:half]))
        w1b.append(big(w[half:]))
        w2.append(_block_diag_dense(jnp.stack([cmp_w2[br], cmp_w2[br]])))
    return (jnp.stack(pos), jnp.stack(w1a).astype(BF16), jnp.stack(w1b).astype(BF16), jnp.stack(w2).astype(BF16))


def _compress(kc, vc, cmp_pos, cmp_w1, cmp_w2, batch, seq):
    nchunk = seq // CMP_STRIDE
    width = CMP_STRIDE * LANES
    pos, w1a, w1b, w2 = _prep_compress_weights(cmp_pos, cmp_w1, cmp_w2)
    blk = pl.BlockSpec((1, nchunk, width), lambda b: (b, 0, 0))
    full3 = lambda shape: pl.BlockSpec(shape, lambda b: (0, 0, 0))
    return pl.pallas_call(
        functools.partial(_compress_kernel, nchunk=nchunk),
        grid=(batch,),
        in_specs=[blk, blk, full3(pos.shape), full3(w1a.shape), full3(w1b.shape), full3(w2.shape)],
        out_specs=[pl.BlockSpec((1, nchunk, LANES), lambda b: (b, 0, 0)),
                   pl.BlockSpec((1, LANES, nchunk), lambda b: (b, 0, 0))],
        out_shape=(jax.ShapeDtypeStruct((batch, nchunk, LANES), BF16),
                   jax.ShapeDtypeStruct((batch, LANES, nchunk), BF16)),
        compiler_params=_cparams(("parallel",)),
        name="nsa_compress",
    )(kc.reshape(batch, nchunk, width), vc.reshape(batch, nchunk, width), pos, w1a, w1b, w2)


def _select_kernel(q_ref, kc_ref, vct_ref, ovt_ref, gt_ref, oc_ref, sb_ref, ot_sc, imp_sc, *, tq, nchunk, variants):
    nsel = LANES
    qt = pl.program_id(1)
    col_ok = (qt * tq + lax.broadcasted_iota(jnp.int32, (1, tq), 1) >= CMP_LEN - 1).astype(F32)

    def attend(nc):
        kc = kc_ref[0, 0:nc, :]
        qpos = qt * tq + lax.broadcasted_iota(jnp.int32, (nc, tq), 1)
        cend = lax.broadcasted_iota(jnp.int32, (nc, tq), 0) * CMP_STRIDE + (CMP_LEN - 1)
        cvalid = cend <= qpos
        ovt = ovt_ref[:, 0:nc]
        for g in range(NSA_KV_HEADS):
            vct = vct_ref[0, g * HEAD_V:(g + 1) * HEAD_V, 0:nc]
            psum = jnp.zeros((nc, tq), F32)
            for pj in range(NSA_GROUP):
                hd = g * NSA_GROUP + pj
                q = q_ref[0, :, hd * LANES:(hd + 1) * LANES]
                s = jnp.where(cvalid, _dot_nt(kc, q), NEG_INF)
                m = jnp.max(s, axis=0, keepdims=True)
                e = jnp.exp2(s - m)
                p = e * (col_ok / jnp.maximum(jnp.sum(e, axis=0, keepdims=True), 1e-30))
                ot_sc[hd * HEAD_V:(hd + 1) * HEAD_V, :] = _dot(vct, p.astype(BF16)) * gt_ref[0, 3 * hd:3 * hd + 1, :]
                psum = psum + p
            hi = psum.astype(BF16)
            r1 = psum - hi.astype(F32)
            mid = r1.astype(BF16)
            lo = (r1 - mid.astype(F32)).astype(BF16)
            imp_sc[g] = _dot(ovt, hi) + _dot(ovt, mid) + _dot(ovt, lo)

    needed = (qt * tq + tq - CMP_STRIDE) // CMP_STRIDE
    lo_bound = 0
    for nc in variants:
        pl.when((needed > lo_bound) & (needed <= nc) if nc != variants[-1] else needed > lo_bound)(
            functools.partial(attend, nc))
        lo_bound = nc

    blk = lax.broadcasted_iota(jnp.int32, (nsel, tq), 0)
    blkf = blk.astype(F32)
    qp = qt * tq + lax.broadcasted_iota(jnp.int32, (nsel, tq), 1)
    valid = blk * SEL_LEN <= qp
    forced = (blk == 0) | (blk == qp // SEL_LEN)
    for g in range(NSA_KV_HEADS):
        score = jnp.where(valid, imp_sc[g], -1.0)
        score0 = jnp.where(forced, score + FORCE_BONUS, score)
        score = score0
        for _ in range(min(SEL_TOPK, nsel)):
            m = jnp.max(score, axis=0, keepdims=True)
            idx = jnp.min(jnp.where(score == m, blkf, float(nsel)), axis=0, keepdims=True)
            score = jnp.where(blkf == idx, -jnp.inf, score)
        bias = jnp.where(score0 >= 0.0, jnp.where(score == -jnp.inf, 0.0, NEG_INF), NEG_INF)
        sb_ref[0, g] = bias.T.astype(sb_ref.dtype)
    oc_ref[0] = ot_sc[...].T


def _overlap_matrix_t(nchunk, nsel):
    n = np.arange(nchunk)[None, :]
    s = np.arange(nsel)[:, None]
    cs = n * CMP_STRIDE
    ov = (cs < s * SEL_LEN + SEL_LEN) & (cs + CMP_LEN - 1 >= s * SEL_LEN) & (n < nchunk - 1)
    return jnp.asarray(ov.astype(np.float32)).astype(BF16)


def _select(qn, kcmp, vcmp_t, gates_t, batch, seq, tq=256):
    nchunk = seq // CMP_STRIDE
    nsel = LANES
    assert seq // SEL_LEN <= nsel
    qw = N_ATT_HEADS * LANES
    ow = N_ATT_HEADS * HEAD_V
    ovt = _overlap_matrix_t(nchunk, nsel)
    quarter = nchunk // 4
    variants = tuple(quarter * i for i in range(1, 5)) if quarter % LANES == 0 else (nchunk,)
    return pl.pallas_call(
        functools.partial(_select_kernel, tq=tq, nchunk=nchunk, variants=variants),
        grid=(batch, seq // tq),
        in_specs=[pl.BlockSpec((1, tq, qw), lambda b, t: (b, t, 0)),
                  pl.BlockSpec((1, nchunk, LANES), lambda b, t: (b, 0, 0)),
                  pl.BlockSpec((1, NSA_KV_HEADS * HEAD_V, nchunk), lambda b, t: (b, 0, 0)),
                  pl.BlockSpec((nsel, nchunk), lambda b, t: (0, 0)),
                  pl.BlockSpec((1, GATE_ROWS, tq), lambda b, t: (b, 0, t))],
        out_specs=[pl.BlockSpec((1, tq, ow), lambda b, t: (b, t, 0)),
                   pl.BlockSpec((1, NSA_KV_HEADS, tq, nsel), lambda b, t: (b, 0, t, 0))],
        out_shape=(jax.ShapeDtypeStruct((batch, seq, ow), F32),
                   jax.ShapeDtypeStruct((batch, NSA_KV_HEADS, seq, nsel), BF16)),
        scratch_shapes=[pltpu.VMEM((ow, tq), F32), pltpu.VMEM((NSA_KV_HEADS, nsel, tq), F32)],
        compiler_params=_cparams(("parallel", "parallel")),
        name="nsa_select",
    )(qn, kcmp, vcmp_t, ovt, gates_t)


def _out_proj_kernel(x_ref, a_ref, b_ref, oc_ref, os_ref, ow_ref, wa_ref, wb_ref, wc_ref, o_ref):
    acc = x_ref[...] + _dot(a_ref[...].astype(BF16), wa_ref[...])
    acc = acc + _dot(b_ref[...], wb_ref[...])
    c = (oc_ref[...] + os_ref[...]) + ow_ref[...]
    o_ref[...] = acc + _dot(c.astype(BF16), wc_ref[...])


def _out_proj(x2, a_out, b_out, o_c, o_s, o_w, w_out, tm=512):
    T = x2.shape[0]
    wa = w_out[0:256].astype(BF16)
    wb = w_out[256:640].astype(BF16)
    wc = w_out[640:1024].astype(BF16)
    row = lambda i: (i, 0)
    fixed = lambda i: (0, 0)
    act = lambda n: pl.BlockSpec((tm, n), row)
    return pl.pallas_call(
        _out_proj_kernel,
        grid=(T // tm,),
        in_specs=[act(D_MODEL), act(256), act(384), act(384), act(384), act(384),
                  pl.BlockSpec((256, D_MODEL), fixed), pl.BlockSpec((384, D_MODEL), fixed),
                  pl.BlockSpec((384, D_MODEL), fixed)],
        out_specs=pl.BlockSpec((tm, D_MODEL), row),
        out_shape=jax.ShapeDtypeStruct((T, D_MODEL), F32),
        compiler_params=_cparams(("parallel",)),
        name="out_proj",
    )(x2, a_out, b_out, o_c, o_s, o_w, wa, wb, wc)


def _final_norm(y, gf_ref):
    ms = jnp.mean(y * y, axis=-1, keepdims=True)
    return (y * lax.rsqrt(ms + EPS)) * gf_ref[...]


def _ffn_kernel(x_ref, g_ref, wg_ref, wu_ref, wd_ref, gf_ref, o_ref, h_sc, acc_sc, *, final_norm):
    f = pl.program_id(1)

    @pl.when(f == 0)
    def _():
        x = x_ref[...]
        ms = jnp.mean(x * x, axis=-1, keepdims=True)
        h_sc[...] = ((x * lax.rsqrt(ms + EPS)) * g_ref[...]).astype(BF16)
        acc_sc[...] = x

    h = h_sc[...]
    gte = _dot(h, wg_ref[...])
    up = _dot(h, wu_ref[...])
    act = (gte * _sigmoid(gte)) * up
    acc_sc[...] += _dot(act.astype(BF16), wd_ref[...])

    @pl.when(f == pl.num_programs(1) - 1)
    def _():
        y = acc_sc[...]
        o_ref[...] = _final_norm(y, gf_ref) if final_norm else y


def _ffn(x2, g, wg, wu, wd, g_final, final_norm, tm=512, tf=1408):
    T = x2.shape[0]
    nf = D_FF // tf
    return pl.pallas_call(
        functools.partial(_ffn_kernel, final_norm=final_norm),
        grid=(T // tm, nf),
        in_specs=[pl.BlockSpec((tm, D_MODEL), lambda i, f: (i, 0)), pl.BlockSpec((1, D_MODEL), lambda i, f: (0, 0)),
                  pl.BlockSpec((D_MODEL, tf), lambda i, f: (0, f)), pl.BlockSpec((D_MODEL, tf), lambda i, f: (0, f)),
                  pl.BlockSpec((tf, D_MODEL), lambda i, f: (f, 0)), pl.BlockSpec((1, D_MODEL), lambda i, f: (0, 0))],
        out_specs=pl.BlockSpec((tm, D_MODEL), lambda i, f: (i, 0)),
        out_shape=jax.ShapeDtypeStruct((T, D_MODEL), F32),
        scratch_shapes=[pltpu.VMEM((tm, D_MODEL), BF16), pltpu.VMEM((tm, D_MODEL), F32)],
        compiler_params=_cparams(("parallel", "arbitrary")),
        name="ffn_dense",
    )(x2, g.reshape(1, -1), wg.astype(BF16), wu.astype(BF16), wd.astype(BF16), g_final.reshape(1, -1))


MOE_CHUNK = 256


def _moe_kernel(x_ref, g_ref, rw_ref, tri_ref, wg_ref, wu_ref, wd_ref, gf_ref, o_ref,
                h_sc, gate_sc, key_sc, keyt_sc, acc_sc, *, final_norm, tm):
    e = pl.program_id(1)
    lane = lax.broadcasted_iota(jnp.int32, (tm, LANES), 1)

    @pl.when(e == 0)
    def _():
        x = x_ref[...]
        ms = jnp.mean(x * x, axis=-1, keepdims=True)
        hf = (x * lax.rsqrt(ms + EPS)) * g_ref[...]
        h_sc[...] = hf.astype(BF16)
        acc_sc[...] = x
        h_hi = hf.astype(BF16)
        h_lo = (hf - h_hi.astype(F32)).astype(BF16)
        logits = (_dot(h_hi, rw_ref[0]) + _dot(h_lo, rw_ref[0]) + _dot(h_hi, rw_ref[1]))
        logits = jnp.where(lane < N_EXPERTS, logits, -jnp.inf)
        m1 = jnp.max(logits, axis=1, keepdims=True)
        i1 = jnp.min(jnp.where(logits == m1, lane, LANES), axis=1, keepdims=True)
        rest = jnp.where(lane == i1, -jnp.inf, logits)
        m2 = jnp.max(rest, axis=1, keepdims=True)
        i2 = jnp.min(jnp.where(rest == m2, lane, LANES), axis=1, keepdims=True)
        e2 = jnp.exp(m2 - m1)
        den = 1.0 + e2
        gate_sc[...] = jnp.where(lane == i1, 1.0 / den, 0.0) + jnp.where(lane == i2, e2 / den, 0.0)
        routed = jnp.where((lane == i1) | (lane == i2), 1.0, 0.0)
        key = jnp.where(routed > 0.0, _dot(tri_ref[...], routed.astype(BF16)), -1.0)
        key_sc[...] = key
        keyt_sc[...] = key.T

    sel = lane == e
    kcol = jnp.max(jnp.where(sel, key_sc[...], -1.0), axis=1, keepdims=True)
    wcol = jnp.sum(jnp.where(sel, gate_sc[...], 0.0), axis=1, keepdims=True)
    krow = keyt_sc[pl.ds(e, 1), :]
    count = jnp.sum(jnp.where(kcol >= 0.0, 1.0, 0.0)).astype(jnp.int32)
    n_chunks = (count + (MOE_CHUNK - 1)) // MOE_CHUNK
    row_id = lax.broadcasted_iota(jnp.int32, (MOE_CHUNK, tm), 0).astype(F32)
    col_id = lax.broadcasted_iota(jnp.int32, (tm, MOE_CHUNK), 1).astype(F32)

    def chunk(ci, carry):
        base = (ci * MOE_CHUNK).astype(F32)
        gather = jnp.where(krow - base == row_id, 1.0, 0.0).astype(BF16)
        xg = _dot(gather, h_sc[...]).astype(BF16)
        gte = _dot(xg, wg_ref[0])
        up = _dot(xg, wu_ref[0])
        act = (gte * _sigmoid(gte)) * up
        y = _dot(act.astype(BF16), wd_ref[0])
        scatter = jnp.where(kcol - base == col_id, 1.0, 0.0).astype(BF16)
        acc_sc[...] += wcol * _dot(scatter, y.astype(BF16))
        return carry

    lax.fori_loop(0, n_chunks, chunk, 0)

    @pl.when(e == pl.num_programs(1) - 1)
    def _():
        out = acc_sc[...]
        o_ref[...] = _final_norm(out, gf_ref) if final_norm else out


def _moe(x2, g, router_w, wg, wu, wd, g_final, final_norm, tm=512):
    T = x2.shape[0]
    rw = _pad_cols(router_w, LANES)
    rw_hi = rw.astype(BF16)
    rw_lo = (rw - rw_hi.astype(F32)).astype(BF16)
    rw2 = jnp.stack([rw_hi, rw_lo])
    tri = jnp.asarray(np.tril(np.ones((tm, tm), np.float32), -1)).astype(BF16)
    return pl.pallas_call(
        functools.partial(_moe_kernel, final_norm=final_norm, tm=tm),
        grid=(T // tm, N_EXPERTS),
        in_specs=[pl.BlockSpec((tm, D_MODEL), lambda i, e: (i, 0)), pl.BlockSpec((1, D_MODEL), lambda i, e: (0, 0)),
                  pl.BlockSpec((2, D_MODEL, LANES), lambda i, e: (0, 0, 0)),
                  pl.BlockSpec((tm, tm), lambda i, e: (0, 0)),
                  pl.BlockSpec((1, D_MODEL, D_FF_EXPERT), lambda i, e: (e, 0, 0)),
                  pl.BlockSpec((1, D_MODEL, D_FF_EXPERT), lambda i, e: (e, 0, 0)),
                  pl.BlockSpec((1, D_FF_EXPERT, D_MODEL), lambda i, e: (e, 0, 0)),
                  pl.BlockSpec((1, D_MODEL), lambda i, e: (0, 0))],
        out_specs=pl.BlockSpec((tm, D_MODEL), lambda i, e: (i, 0)),
        out_shape=jax.ShapeDtypeStruct((T, D_MODEL), F32),
        scratch_shapes=[pltpu.VMEM((tm, D_MODEL), BF16), pltpu.VMEM((tm, LANES), F32),
                        pltpu.VMEM((tm, LANES), F32), pltpu.VMEM((LANES, tm), F32),
                        pltpu.VMEM((tm, D_MODEL), F32)],
        compiler_params=_cparams(("parallel", "arbitrary")),
        name="moe_routed",
    )(x2, g.reshape(1, -1), rw2, tri, wg.astype(BF16), wu.astype(BF16), wd.astype(BF16), g_final.reshape(1, -1))


def _mixer(x2, batch, seq, tables, onehot, norm_g, w_in, conv_w, conv_b, wa, ba, wx, bx, lam,
           q_norm, w_uq, kv_norm, w_ukv, cmp_pos, cmp_w1, cmp_w2, gate_b, w_out):
    w_cat, w_t = _prep_w_in(w_in)
    lru_xy, mla_dn, qn, kc, vc, ksw, vt_sw, gates_t = _in_proj(x2, norm_g, w_cat, w_t, tables, gate_b, batch, seq)
    a_out = _lru(lru_xy, conv_w, conv_b, wa, ba, wx, bx, lam, batch, seq)

    q_m, k_m, vt_m = _mla_prep(mla_dn, q_norm, w_uq, kv_norm, w_ukv, tables, batch, seq)
    b3 = lambda t: t.reshape(batch, seq, t.shape[-1])
    heads = tuple(range(N_ATT_HEADS))
    b_out = _flash(b3(q_m), b3(k_m), vt_m, mode="causal", batch=batch, seq=seq, per_head_kv=True,
                   k_width=N_ATT_HEADS * LANES, k_col=0, vt_rows=N_ATT_HEADS * HEAD_V, vt_row=0, out_dtype=BF16,
                   tk=LONG_KEY_TILE)

    kcmp, vcmp_t = _compress(kc, vc, cmp_pos, cmp_w1, cmp_w2, batch, seq)
    qn3 = b3(qn)
    o_c, selbias = _select(qn3, kcmp, vcmp_t, gates_t, batch, seq)
    ksw3 = b3(ksw)
    nsa_kw = dict(batch=batch, seq=seq, per_head_kv=False, k_width=LANES,
                  vt_rows=NSA_KV_HEADS * HEAD_V, out_dtype=F32, gates_t=gates_t)
    o_s = _flash(qn3, ksw3, vt_sw, mode="select", k_col=0, vt_row=0, selbias=selbias,
                 onehot=onehot, gate_rows=tuple(3 * h + 1 for h in heads), tk=LONG_KEY_TILE, **nsa_kw)
    o_w = _flash(qn3, ksw3, vt_sw, mode="window", k_col=1, vt_row=1,
                 gate_rows=tuple(3 * h + 2 for h in heads), **nsa_kw)
    T = batch * seq
    flat = lambda t: t.reshape(T, t.shape[-1])
    return _out_proj(x2, a_out, flat(b_out), flat(o_c), flat(o_s), flat(o_w), w_out)


def kernel(x, norm_mix, w_in, lru_conv_w, lru_conv_b, lru_wa, lru_ba, lru_wx, lru_bx, lru_lambda, mla_q_norm, mla_w_uq, mla_kv_norm, mla_w_ukv, nsa_cmp_pos, nsa_cmp_w1, nsa_cmp_w2, nsa_gate_b, w_out, norm_ffn, ffn_w_gate, ffn_w_up, ffn_w_down, router_w, moe_w_gate, moe_w_up, moe_w_down, norm_final):
    batch, seq, _ = x.shape
    depth = norm_mix.shape[0]
    tables = _rope_tables(seq)
    blk_id = np.arange(seq)[:, None] // SEL_LEN
    onehot = jnp.asarray((blk_id == np.arange(LANES)[None, :]).astype(np.float32)).astype(BF16)
    x2 = x.reshape(batch * seq, D_MODEL)
    for l in range(depth):
        x2 = _mixer(x2, batch, seq, tables, onehot, norm_mix[l], w_in[l], lru_conv_w[l], lru_conv_b[l],
                    lru_wa[l], lru_ba[l], lru_wx[l], lru_bx[l], lru_lambda[l], mla_q_norm[l], mla_w_uq[l],
                    mla_kv_norm[l], mla_w_ukv[l], nsa_cmp_pos[l], nsa_cmp_w1[l], nsa_cmp_w2[l],
                    nsa_gate_b[l], w_out[l])
        last = l == depth - 1
        j = l // 2
        if l % 2 == 0:
            x2 = _ffn(x2, norm_ffn[l], ffn_w_gate[j], ffn_w_up[j], ffn_w_down[j], norm_final, last)
        else:
            x2 = _moe(x2, norm_ffn[l], router_w[j], moe_w_gate[j], moe_w_up[j], moe_w_down[j], norm_final, last)
    if depth == 0:
        raise ValueError("depth must be positive")
    return x2.reshape(batch, seq, D_MODEL)
```

```python
import functools
import math

import numpy as np
import jax
import jax.numpy as jnp
from jax import lax
from jax.experimental import pallas as pl
from jax.experimental.pallas import tpu as pltpu

F32 = jnp.float32
BF16 = jnp.bfloat16

D_MODEL = 1024
LRU_WIDTH = 256
LRU_BLOCKS = 4
LRU_BLOCK_W = LRU_WIDTH // LRU_BLOCKS
CONV_WIDTH = 4
LRU_C = 8.0
MLA_HEADS = 6
MLA_NOPE = 64
MLA_ROPE = 32
MLA_V = 64
MLA_Q_RANK = 192
MLA_KV_RANK = 128
NSA_HEADS = 6
NSA_KV_HEADS = 2
NSA_GROUP = NSA_HEADS // NSA_KV_HEADS
NSA_HEAD_DIM = 64
CMP_LEN = 32
CMP_STRIDE = 16
CMP_HIDDEN = 128
SEL_LEN = 64
SEL_TOPK = 16
WINDOW = 512
FORCE_BONUS = 1.0e3
D_FF = 2816
N_EXPERTS = 8
D_FF_EXPERT = 1408
ROPE_THETA = 10000.0
EPS = 1e-6
NEG_INF = -1.0e30

LANES = 128
VMEM_LIMIT = 56 * 1024 * 1024

C_LRU = 0
C_MLA = 512
C_KPE_ROT = 1024
C_QN = 1152
C_QN_ROT = 1536
C_KC = 1920
C_KC_ROT = 2048
C_VC = 2176
C_KS = 2304
C_KS_ROT = 2432
C_KW = 2560
C_KW_ROT = 2688
IN_COLS_PAD = 2816
N_ATT_HEADS = 6
HEAD_V = 64
LONG_KEY_TILE = 1024
SUM_ROWS = 16
GATE_ROWS = 32
LOG2E = 1.4426950408889634


def _cparams(sem, flags=None):
    return pltpu.CompilerParams(dimension_semantics=sem, vmem_limit_bytes=VMEM_LIMIT, flags=flags)


def _gelu_tanh(x):
    return 0.5 * x * (1.0 + jnp.tanh(math.sqrt(2.0 / math.pi) * (x + 0.044715 * (x * x * x))))


def _sigmoid(x):
    return 1.0 / (1.0 + jnp.exp(-x))


def _dot(a, b):
    return jnp.dot(a, b, preferred_element_type=F32)


def _dot_nt(a, b):
    return lax.dot_general(a, b, (((1,), (1,)), ((), ())), preferred_element_type=F32)


def _rot_cols(w, half):
    return jnp.concatenate([-w[:, half:], w[:, :half]], axis=1)


def _pad_cols(w, n):
    return jnp.pad(w, ((0, 0), (0, n - w.shape[1])))


def _prep_w_in(w_in):
    splits = np.cumsum([256, 256, 192, 128, 32, 384, 128, 128, 128, 128, 128, 128, 18])[:-1].tolist()
    (x_l, y_l, q_dn, kv_dn, k_pe, q_n, kc, vc, ks, vs, kw, vw, g_n) = jnp.split(w_in, splits, axis=1)
    z = lambda n: jnp.zeros((w_in.shape[0], n), w_in.dtype)

    def kpe_slot(w):
        return jnp.concatenate([z(64), w, z(32)], axis=1)

    def qn_slots(fn):
        out = []
        for p in range(NSA_GROUP):
            for h in (p, p + NSA_GROUP):
                out.append(fn(q_n[:, h * 64:(h + 1) * 64]))
        return jnp.concatenate(out, axis=1)

    def kv_rot(w):
        return jnp.concatenate([_rot_cols(w[:, :64], 32), _rot_cols(w[:, 64:], 32)], axis=1)

    cols = [x_l, y_l, _pad_cols(q_dn, 256), kv_dn, kpe_slot(k_pe), kpe_slot(_rot_cols(k_pe, 16)),
            qn_slots(lambda w: w), qn_slots(lambda w: _rot_cols(w, 32)),
            kc, kv_rot(kc), vc, ks, kv_rot(ks), kw, kv_rot(kw)]
    w = jnp.concatenate(cols, axis=1)
    assert w.shape[1] == IN_COLS_PAD
    w_t = jnp.concatenate([vs, vw, _pad_cols(g_n, GATE_ROWS)], axis=1).T
    return w.astype(BF16), w_t.astype(BF16)


def _rope_tables(seq):
    pos = jnp.arange(seq, dtype=F32)[:, None]
    inv64 = ROPE_THETA ** (-jnp.arange(32, dtype=F32) * 2.0 / 64)
    inv32 = ROPE_THETA ** (-jnp.arange(16, dtype=F32) * 2.0 / 32)
    a64 = pos * inv64[None, :]
    a32 = pos * inv32[None, :]
    c64, s64 = jnp.cos(a64), jnp.sin(a64)
    c32, s32 = jnp.cos(a32), jnp.sin(a32)
    cos_n = jnp.concatenate([c64, c64, c64, c64], axis=1)
    sin_n = jnp.concatenate([s64, s64, s64, s64], axis=1)
    one = jnp.ones((seq, 64), F32)
    cos_m = jnp.concatenate([one, c32, c32, one[:, :32]], axis=1)
    sin_m = jnp.concatenate([0 * one, s32, s32, 0 * one[:, :32]], axis=1)
    return cos_n, sin_n, cos_m, sin_m


def _in_proj_kernel(x_ref, g_ref, w_ref, wt_ref, cn_ref, sn_ref, cm_ref, sm_ref, gb_ref,
                    lru_ref, mla_ref, qn_ref, kc_ref, vc_ref, ksw_ref, vt_ref, gt_ref):
    x = x_ref[...]
    ms = jnp.mean(x * x, axis=-1, keepdims=True)
    h = ((x * lax.rsqrt(ms + EPS)) * g_ref[...]).astype(BF16)

    def proj(c0, n):
        return _dot(h, w_ref[:, c0:c0 + n])

    cn, sn = cn_ref[...], sn_ref[...]
    lru_ref[...] = proj(C_LRU, 512)
    mla_ref[:, 0:384] = proj(C_MLA, 384)
    mla_ref[:, 384:512] = proj(C_MLA + 384, 128) * cm_ref[...] + proj(C_KPE_ROT, 128) * sm_ref[...]
    scale = NSA_HEAD_DIM ** -0.5 * LOG2E
    low_half = lax.broadcasted_iota(jnp.int32, cn.shape, 1) < NSA_HEAD_DIM
    for p in range(NSA_GROUP):
        a = proj(C_QN + p * LANES, LANES)
        b = proj(C_QN_ROT + p * LANES, LANES)
        r = (a * cn + b * sn) * scale
        hi_hd = p + NSA_GROUP
        qn_ref[:, p * LANES:(p + 1) * LANES] = jnp.where(low_half, r, 0.0).astype(BF16)
        qn_ref[:, hi_hd * LANES:(hi_hd + 1) * LANES] = jnp.where(low_half, 0.0, r).astype(BF16)
    kc_ref[...] = proj(C_KC, 128) * cn + proj(C_KC_ROT, 128) * sn
    vc_ref[...] = proj(C_VC, 128)
    ksw_ref[:, 0:128] = (proj(C_KS, 128) * cn + proj(C_KS_ROT, 128) * sn).astype(BF16)
    ksw_ref[:, 128:256] = (proj(C_KW, 128) * cn + proj(C_KW_ROT, 128) * sn).astype(BF16)
    vt_ref[0] = _dot_nt(wt_ref[0:256, :], h).astype(BF16)
    gt_ref[0] = _sigmoid(_dot_nt(wt_ref[256:256 + GATE_ROWS, :], h) + gb_ref[...])


def _in_proj(x2, g, w_cat, w_t, tables, gate_b, batch, seq, tm=512):
    T = x2.shape[0]
    nt = T // tm
    npos = seq // tm
    cn, sn, cm, sm = tables
    row = lambda i: (i, 0)
    fixed = lambda i: (0, 0)
    posmap = lambda i: (i % npos, 0)
    tab_spec = pl.BlockSpec((tm, LANES), posmap)
    out_shapes = (
        jax.ShapeDtypeStruct((T, 512), F32),
        jax.ShapeDtypeStruct((T, 512), F32),
        jax.ShapeDtypeStruct((T, 768), BF16),
        jax.ShapeDtypeStruct((T, 128), F32),
        jax.ShapeDtypeStruct((T, 128), F32),
        jax.ShapeDtypeStruct((T, 256), BF16),
    )
    t_shapes = (jax.ShapeDtypeStruct((batch, 256, seq), BF16),
                jax.ShapeDtypeStruct((batch, GATE_ROWS, seq), F32))
    tmap = lambda i: (i // npos, 0, i % npos)
    gate_b_col = jnp.pad(gate_b, (0, GATE_ROWS - gate_b.shape[0])).reshape(GATE_ROWS, 1)
    return pl.pallas_call(
        _in_proj_kernel,
        grid=(nt,),
        in_specs=[pl.BlockSpec((tm, D_MODEL), row), pl.BlockSpec((1, D_MODEL), fixed),
                  pl.BlockSpec((D_MODEL, IN_COLS_PAD), fixed), pl.BlockSpec((256 + GATE_ROWS, D_MODEL), fixed),
                  tab_spec, tab_spec, tab_spec, tab_spec, pl.BlockSpec((GATE_ROWS, 1), fixed)],
        out_specs=[pl.BlockSpec((tm, s.shape[1]), row) for s in out_shapes]
        + [pl.BlockSpec((1, s.shape[1], tm), tmap) for s in t_shapes],
        out_shape=out_shapes + t_shapes,
        compiler_params=_cparams(("parallel",)),
        name="in_proj",
    )(x2, g.reshape(1, -1), w_cat, w_t, cn, sn, cm, sm, gate_b_col)


def _lru_kernel(xy_ref, cw_ref, cb_ref, wa_ref, ba_ref, wx_ref, bx_ref, lam_ref, out_ref,
                xe_sc, h_sc, *, tt):
    t = pl.program_id(1)

    @pl.when(t == 0)
    def _():
        xe_sc[0:8, :] = jnp.zeros((8, LRU_WIDTH), F32)
        h_sc[...] = jnp.zeros_like(h_sc)

    x = xy_ref[0, :, 0:LRU_WIDTH]
    y = xy_ref[0, :, LRU_WIDTH:2 * LRU_WIDTH]
    xe_sc[8:8 + tt, :] = x
    xc = cb_ref[...]
    for k in range(CONV_WIDTH):
        off = 8 - (CONV_WIDTH - 1) + k
        xc = xc + xe_sc[off:off + tt, :] * cw_ref[k:k + 1, :]
    xe_sc[0:8, :] = x[tt - 8:tt, :]

    xb = xc.astype(BF16)
    r = _sigmoid(_dot(xb, wa_ref[...]) + ba_ref[...])
    i = _sigmoid(_dot(xb, wx_ref[...]) + bx_ref[...])
    log_a = (-LRU_C * r) * jax.nn.softplus(-lam_ref[...])
    a = jnp.exp(log_a)
    b = jnp.sqrt(jnp.tanh(-log_a) * (a * a + 1.0)) * (i * xc)

    row = lax.broadcasted_iota(jnp.int32, (tt, LRU_WIDTH), 0)
    k = 1
    while k < tt:
        keep = row >= k
        a_sh = jnp.where(keep, pltpu.roll(a, k, 0), 1.0)
        b_sh = jnp.where(keep, pltpu.roll(b, k, 0), 0.0)
        b = a * b_sh + b
        a = a * a_sh
        k *= 2
    h = b + a * h_sc[0:1, :]
    h_sc[0:1, :] = h[tt - 1:tt, :]
    out_ref[0] = h * _gelu_tanh(y)


def _block_diag_dense(w):
    n, c, d = w.shape
    out = jnp.zeros((n * c, n * d), w.dtype)
    for j in range(n):
        out = out.at[j * c:(j + 1) * c, j * d:(j + 1) * d].set(w[j])
    return out


def _lru(lru_xy, conv_w, conv_b, wa, ba, wx, bx, lam, batch, seq, tt=512):
    xy = lru_xy.reshape(batch, seq, 512)
    fixed = lambda b, t: (0, 0)
    vec = pl.BlockSpec((1, LRU_WIDTH), fixed)
    mat = pl.BlockSpec((LRU_WIDTH, LRU_WIDTH), fixed)
    out = pl.pallas_call(
        functools.partial(_lru_kernel, tt=tt),
        grid=(batch, seq // tt),
        in_specs=[pl.BlockSpec((1, tt, 512), lambda b, t: (b, t, 0)),
                  pl.BlockSpec((CONV_WIDTH, LRU_WIDTH), fixed), vec, mat, vec, mat, vec, vec],
        out_specs=pl.BlockSpec((1, tt, LRU_WIDTH), lambda b, t: (b, t, 0)),
        out_shape=jax.ShapeDtypeStruct((batch, seq, LRU_WIDTH), F32),
        scratch_shapes=[pltpu.VMEM((tt + 8, LRU_WIDTH), F32), pltpu.VMEM((8, LRU_WIDTH), F32)],
        compiler_params=_cparams(("parallel", "arbitrary")),
        name="rg_lru",
    )(xy, conv_w, conv_b.reshape(1, -1), _block_diag_dense(wa).astype(BF16), ba.reshape(1, -1),
      _block_diag_dense(wx).astype(BF16), bx.reshape(1, -1), lam.reshape(1, -1))
    return out.reshape(batch * seq, LRU_WIDTH)


def _mla_prep_kernel(dn_ref, qn_ref, wqa_ref, wqb_ref, kvn_ref, wk_ref, wvt_ref, cm_ref, sm_ref,
                     q_ref, k_ref, vt_ref):
    q_dn = dn_ref[:, 0:256]
    ms = jnp.sum(q_dn * q_dn, axis=-1, keepdims=True) * (1.0 / MLA_Q_RANK)
    ql = ((q_dn * lax.rsqrt(ms + EPS)) * qn_ref[...]).astype(BF16)
    kv_dn = dn_ref[:, 256:384]
    ms = jnp.mean(kv_dn * kv_dn, axis=-1, keepdims=True)
    kvl = ((kv_dn * lax.rsqrt(ms + EPS)) * kvn_ref[...]).astype(BF16)
    kpe = dn_ref[:, 384:512]
    cm, sm = cm_ref[...], sm_ref[...]
    scale = (MLA_NOPE + MLA_ROPE) ** -0.5 * LOG2E
    for hd in range(MLA_HEADS):
        sl = slice(hd * LANES, (hd + 1) * LANES)
        q_ref[:, sl] = ((_dot(ql, wqa_ref[:, sl]) * cm + _dot(ql, wqb_ref[:, sl]) * sm) * scale).astype(BF16)
        k_ref[:, sl] = (_dot(kvl, wk_ref[:, sl]) + kpe).astype(BF16)
    vt_ref[0] = _dot_nt(wvt_ref[...], kvl).astype(BF16)


def _prep_mla_weights(w_uq, w_ukv):
    zq = jnp.zeros((MLA_Q_RANK, 32), w_uq.dtype)
    qa, qb, kk, vv = [], [], [], []
    for h in range(MLA_HEADS):
        wq = w_uq[:, h * 96:(h + 1) * 96]
        nope, ropew = wq[:, :64], wq[:, 64:]
        qa.append(jnp.concatenate([nope, ropew, zq], axis=1))
        qb.append(jnp.concatenate([0 * nope, _rot_cols(ropew, 16), zq], axis=1))
        wkv = w_ukv[:, h * 128:(h + 1) * 128]
        kk.append(_pad_cols(wkv[:, :64], LANES))
        vv.append(wkv[:, 64:])
    pad_rows = lambda w: jnp.pad(w, ((0, 256 - MLA_Q_RANK), (0, 0)))
    return (pad_rows(jnp.concatenate(qa, axis=1)).astype(BF16), pad_rows(jnp.concatenate(qb, axis=1)).astype(BF16),
            jnp.concatenate(kk, axis=1).astype(BF16), jnp.concatenate(vv, axis=1).T.astype(BF16))


def _mla_prep(mla_dn, q_norm, w_uq, kv_norm, w_ukv, tables, batch, seq, tm=512):
    T = mla_dn.shape[0]
    npos = seq // tm
    _, _, cm, sm = tables
    wqa, wqb, wk, wvt = _prep_mla_weights(w_uq, w_ukv)
    row = lambda i: (i, 0)
    fixed = lambda i: (0, 0)
    tab = pl.BlockSpec((tm, LANES), lambda i: (i % npos, 0))
    return pl.pallas_call(
        _mla_prep_kernel,
        grid=(T // tm,),
        in_specs=[pl.BlockSpec((tm, 512), row), pl.BlockSpec((1, 256), fixed),
                  pl.BlockSpec((256, 768), fixed), pl.BlockSpec((256, 768), fixed),
                  pl.BlockSpec((1, 128), fixed), pl.BlockSpec((128, 768), fixed),
                  pl.BlockSpec((384, 128), fixed), tab, tab],
        out_specs=[pl.BlockSpec((tm, 768), row), pl.BlockSpec((tm, 768), row),
                   pl.BlockSpec((1, 384, tm), lambda i: (i // npos, 0, i % npos))],
        out_shape=(jax.ShapeDtypeStruct((T, 768), BF16), jax.ShapeDtypeStruct((T, 768), BF16),
                   jax.ShapeDtypeStruct((batch, 384, seq), BF16)),
        compiler_params=_cparams(("parallel",)),
        name="mla_prep",
    )(mla_dn, _pad_cols(q_norm.reshape(1, -1), 256), wqa, wqb, kv_norm.reshape(1, -1), wk, wvt, cm, sm)


def _flash_kernel(qt_ref, kt_ref, first_ref, last_ref, *refs, mode, k_per_head, v_per_head, gate_rows, tq, tk):
    refs = list(refs)
    q_ref, k_ref, vt_ref = refs[:3]
    o_ref, m_sc, acc_sc = refs[-7:-4]
    s_sc, mx_sc = refs[-4:-2], refs[-2:]
    extra = refs[3:-7]
    sb_ref, oh_ref = (extra[0], extra[1]) if mode == "select" else (None, None)
    gt_ref = extra[-1] if gate_rows is not None else None
    step = pl.program_id(1)
    qt = qt_ref[step]
    kt = kt_ref[step]

    @pl.when(first_ref[step] == 1)
    def _():
        m_sc[...] = jnp.full_like(m_sc, NEG_INF)
        acc_sc[...] = jnp.zeros_like(acc_sc)

    def update(masked, nk):
        ones = jnp.ones((SUM_ROWS, nk), BF16)
        if masked:
            kpos = kt * tk + lax.broadcasted_iota(jnp.int32, (nk, tq), 0)
            qpos = qt * tq + lax.broadcasted_iota(jnp.int32, (nk, tq), 1)
            dist = qpos - kpos
            valid = (dist >= 0) & (dist < WINDOW) if mode == "window" else dist >= 0

        def scores(j, slot):
            q = q_ref[0, :, j * LANES:(j + 1) * LANES]
            k = k_ref[0, 0:nk, j * LANES:(j + 1) * LANES] if k_per_head else k_ref[0, 0:nk, :]
            if mode == "select":
                q = jnp.concatenate([q, sb_ref[0, j // NSA_GROUP]], axis=1)
                k = jnp.concatenate([k, oh_ref[0:nk, :]], axis=1)
            s = _dot_nt(k, q)
            if masked:
                s = jnp.where(valid, s, NEG_INF)
            s_sc[slot][0:nk, :] = s
            mx_sc[slot][...] = jnp.max(s, axis=0, keepdims=True)

        def accumulate(j, slot):
            m_prev = m_sc[j]
            m_new = jnp.maximum(m_prev, mx_sc[slot][...])
            alpha = jnp.exp2(m_prev - m_new)
            p = jnp.exp2((s_sc[slot][0:nk, :] - m_new).astype(BF16))
            row = (j if v_per_head else j // NSA_GROUP) * HEAD_V
            vt = vt_ref[0, row:row + HEAD_V, 0:nk]
            acc_sc[j] = alpha * acc_sc[j] + _dot(jnp.concatenate([vt, ones], axis=0), p)
            m_sc[j] = m_new

        scores(0, 0)
        for j in range(N_ATT_HEADS):
            if j + 1 < N_ATT_HEADS:
                scores(j + 1, (j + 1) % 2)
            accumulate(j, j % 2)

    if mode == "window":
        update(True, tk)
    else:
        first_q, last_q = qt * tq, qt * tq + (tq - 1)
        crosses = kt * tk + (tk - 1) > first_q
        pl.when(jnp.logical_not(crosses))(lambda: update(False, tk))
        if tk > tq:
            half_only = kt * tk + tk // 2 > last_q
            pl.when(crosses & half_only)(lambda: update(True, tk // 2))
            pl.when(crosses & jnp.logical_not(half_only))(lambda: update(True, tk))
        else:
            pl.when(crosses)(lambda: update(True, tk))

    @pl.when(last_ref[step] == 1)
    def _():
        outs = []
        for j in range(N_ATT_HEADS):
            o = acc_sc[j, 0:HEAD_V, :] / acc_sc[j, HEAD_V:HEAD_V + 1, :]
            if gate_rows is not None:
                o = o * gt_ref[0, gate_rows[j]:gate_rows[j] + 1, :]
            outs.append(o)
        o_ref[0] = jnp.concatenate(outs, axis=0).T.astype(o_ref.dtype)


def _pair_tables(nq, mode, tq, tk):
    qts, kts, first, last = [], [], [], []
    for qt in range(nq):
        hi = (qt * tq + tq - 1) // tk
        lo = 0 if mode != "window" else max(0, (qt * tq - (WINDOW - 1)) // tk)
        for kt in range(lo, hi + 1):
            qts.append(qt)
            kts.append(kt)
            first.append(1 if kt == lo else 0)
            last.append(1 if kt == hi else 0)
    arr = lambda v: jnp.asarray(np.array(v, dtype=np.int32))
    return arr(qts), arr(kts), arr(first), arr(last)


def _flash(q, k, vt, *, mode, batch, seq, per_head_kv, k_width, k_col, vt_rows, vt_row, out_dtype,
           selbias=None, onehot=None, gate_rows=None, gates_t=None, tq=512, tk=512):
    tk = min(tk, seq)
    nq = seq // tq
    qt_tab, kt_tab, first_tab, last_tab = _pair_tables(nq, mode, tq, tk)
    n_steps = int(qt_tab.shape[0])
    qw = N_ATT_HEADS * LANES
    ow = N_ATT_HEADS * HEAD_V
    in_specs = [
        pl.BlockSpec((1, tq, qw), lambda b, s, qt, kt, f, l: (b, qt[s], 0)),
        pl.BlockSpec((1, tk, k_width), lambda b, s, qt, kt, f, l: (b, kt[s], k_col)),
        pl.BlockSpec((1, vt_rows, tk), lambda b, s, qt, kt, f, l: (b, vt_row, kt[s])),
    ]
    args = [q, k, vt]
    if mode == "select":
        in_specs.append(pl.BlockSpec((1, NSA_KV_HEADS, tq, LANES), lambda b, s, qt, kt, f, l: (b, 0, qt[s], 0)))
        in_specs.append(pl.BlockSpec((tk, LANES), lambda b, s, qt, kt, f, l: (kt[s], 0)))
        args += [selbias, onehot]
    if gate_rows is not None:
        in_specs.append(pl.BlockSpec((1, GATE_ROWS, tq), lambda b, s, qt, kt, f, l: (b, 0, qt[s])))
        args.append(gates_t)
    kern = functools.partial(_flash_kernel, mode=mode, k_per_head=per_head_kv, v_per_head=per_head_kv,
                             gate_rows=gate_rows, tq=tq, tk=tk)
    return pl.pallas_call(
        kern,
        grid_spec=pltpu.PrefetchScalarGridSpec(
            num_scalar_prefetch=4,
            grid=(batch, n_steps),
            in_specs=in_specs,
            out_specs=pl.BlockSpec((1, tq, ow), lambda b, s, qt, kt, f, l: (b, qt[s], 0)),
            scratch_shapes=[pltpu.VMEM((N_ATT_HEADS, 1, tq), F32),
                            pltpu.VMEM((N_ATT_HEADS, HEAD_V + SUM_ROWS, tq), F32),
                            pltpu.VMEM((tk, tq), F32), pltpu.VMEM((tk, tq), F32),
                            pltpu.VMEM((1, tq), F32), pltpu.VMEM((1, tq), F32)],
        ),
        out_shape=jax.ShapeDtypeStruct((batch, seq, ow), out_dtype),
        compiler_params=_cparams(("parallel", "arbitrary")),
        name="flash_" + mode,
    )(qt_tab, kt_tab, first_tab, last_tab, *args)


def _compress_kernel(kc_ref, vc_ref, pos_ref, w1a_ref, w1b_ref, w2_ref, ko_ref, vo_ref, *, nchunk):
    row = lax.broadcasted_iota(jnp.int32, (nchunk, LANES), 0)
    for br, src in enumerate((kc_ref, vc_ref)):
        r = src[0]
        pa = _dot((r + pos_ref[br, 0:1, :]).astype(BF16), w1a_ref[br])
        pb = _dot((r + pos_ref[br, 1:2, :]).astype(BF16), w1b_ref[br])
        hid = _gelu_tanh(pa + pltpu.roll(pb, nchunk - 1, 0))
        out = jnp.where(row < nchunk - 1, _dot(hid.astype(BF16), w2_ref[br]), 0.0)
        if br == 0:
            ko_ref[0] = out.astype(ko_ref.dtype)
        else:
            vo_ref[0] = out.T.astype(vo_ref.dtype)


def _prep_compress_weights(cmp_pos, cmp_w1, cmp_w2):
    half = CMP_LEN // 2
    pos, w1a, w1b, w2 = [], [], [], []
    for br in range(2):
        p = cmp_pos[br]
        tile = lambda ph: jnp.concatenate([ph, ph], axis=1).reshape(1, half * LANES)
        pos.append(jnp.concatenate([tile(p[:half]), tile(p[half:])], axis=0))
        w = cmp_w1[br].reshape(CMP_LEN, NSA_HEAD_DIM, CMP_HIDDEN)
        z = jnp.zeros_like(w[:half])

        def big(wh):
            g0 = jnp.concatenate([wh, z], axis=1)
            g1 = jnp.concatenate([z, wh], axis=1)
            return jnp.concatenate([g0.reshape(half * LANES, CMP_HIDDEN), g1.reshape(half * LANES, CMP_HIDDEN)], axis=1)

        w1a.append(big(w[:half]))
        w1b.append(big(w[half:]))
        w2.append(_block_diag_dense(jnp.stack([cmp_w2[br], cmp_w2[br]])))
    return (jnp.stack(pos), jnp.stack(w1a).astype(BF16), jnp.stack(w1b).astype(BF16), jnp.stack(w2).astype(BF16))


def _compress(kc, vc, cmp_pos, cmp_w1, cmp_w2, batch, seq):
    nchunk = seq // CMP_STRIDE
    width = CMP_STRIDE * LANES
    pos, w1a, w1b, w2 = _prep_compress_weights(cmp_pos, cmp_w1, cmp_w2)
    blk = pl.BlockSpec((1, nchunk, width), lambda b: (b, 0, 0))
    full3 = lambda shape: pl.BlockSpec(shape, lambda b: (0, 0, 0))
    return pl.pallas_call(
        functools.partial(_compress_kernel, nchunk=nchunk),
        grid=(batch,),
        in_specs=[blk, blk, full3(pos.shape), full3(w1a.shape), full3(w1b.shape), full3(w2.shape)],
        out_specs=[pl.BlockSpec((1, nchunk, LANES), lambda b: (b, 0, 0)),
                   pl.BlockSpec((1, LANES, nchunk), lambda b: (b, 0, 0))],
        out_shape=(jax.ShapeDtypeStruct((batch, nchunk, LANES), BF16),
                   jax.ShapeDtypeStruct((batch, LANES, nchunk), BF16)),
        compiler_params=_cparams(("parallel",)),
        name="nsa_compress",
    )(kc.reshape(batch, nchunk, width), vc.reshape(batch, nchunk, width), pos, w1a, w1b, w2)


def _select_kernel(q_ref, kc_ref, vct_ref, ovt_ref, gt_ref, oc_ref, sb_ref, ot_sc, imp_sc, *, tq, nchunk, variants):
    nsel = LANES
    qt = pl.program_id(1)
    col_ok = (qt * tq + lax.broadcasted_iota(jnp.int32, (1, tq), 1) >= CMP_LEN - 1).astype(F32)

    def attend(nc):
        kc = kc_ref[0, 0:nc, :]
        qpos = qt * tq + lax.broadcasted_iota(jnp.int32, (nc, tq), 1)
        cend = lax.broadcasted_iota(jnp.int32, (nc, tq), 0) * CMP_STRIDE + (CMP_LEN - 1)
        cvalid = cend <= qpos
        ovt = ovt_ref[:, 0:nc]
        for g in range(NSA_KV_HEADS):
            vct = vct_ref[0, g * HEAD_V:(g + 1) * HEAD_V, 0:nc]
            psum = jnp.zeros((nc, tq), F32)
            for pj in range(NSA_GROUP):
                hd = g * NSA_GROUP + pj
                q = q_ref[0, :, hd * LANES:(hd + 1) * LANES]
                s = jnp.where(cvalid, _dot_nt(kc, q), NEG_INF)
                m = jnp.max(s, axis=0, keepdims=True)
                e = jnp.exp2(s - m)
                p = e * (col_ok / jnp.maximum(jnp.sum(e, axis=0, keepdims=True), 1e-30))
                ot_sc[hd * HEAD_V:(hd + 1) * HEAD_V, :] = _dot(vct, p.astype(BF16)) * gt_ref[0, 3 * hd:3 * hd + 1, :]
                psum = psum + p
            hi = psum.astype(BF16)
            r1 = psum - hi.astype(F32)
            mid = r1.astype(BF16)
            lo = (r1 - mid.astype(F32)).astype(BF16)
            imp_sc[g] = _dot(ovt, hi) + _dot(ovt, mid) + _dot(ovt, lo)

    needed = (qt * tq + tq - CMP_STRIDE) // CMP_STRIDE
    lo_bound = 0
    for nc in variants:
        pl.when((needed > lo_bound) & (needed <= nc) if nc != variants[-1] else needed > lo_bound)(
            functools.partial(attend, nc))
        lo_bound = nc

    blk = lax.broadcasted_iota(jnp.int32, (nsel, tq), 0)
    blkf = blk.astype(F32)
    qp = qt * tq + lax.broadcasted_iota(jnp.int32, (nsel, tq), 1)
    valid = blk * SEL_LEN <= qp
    forced = (blk == 0) | (blk == qp // SEL_LEN)
    for g in range(NSA_KV_HEADS):
        score = jnp.where(valid, imp_sc[g], -1.0)
        score0 = jnp.where(forced, score + FORCE_BONUS, score)
        score = score0
        for _ in range(min(SEL_TOPK, nsel)):
            m = jnp.max(score, axis=0, keepdims=True)
            idx = jnp.min(jnp.where(score == m, blkf, float(nsel)), axis=0, keepdims=True)
            score = jnp.where(blkf == idx, -jnp.inf, score)
        bias = jnp.where(score0 >= 0.0, jnp.where(score == -jnp.inf, 0.0, NEG_INF), NEG_INF)
        sb_ref[0, g] = bias.T.astype(sb_ref.dtype)
    oc_ref[0] = ot_sc[...].T


def _overlap_matrix_t(nchunk, nsel):
    n = np.arange(nchunk)[None, :]
    s = np.arange(nsel)[:, None]
    cs = n * CMP_STRIDE
    ov = (cs < s * SEL_LEN + SEL_LEN) & (cs + CMP_LEN - 1 >= s * SEL_LEN) & (n < nchunk - 1)
    return jnp.asarray(ov.astype(np.float32)).astype(BF16)


def _select(qn, kcmp, vcmp_t, gates_t, batch, seq, tq=256):
    nchunk = seq // CMP_STRIDE
    nsel = LANES
    assert seq // SEL_LEN <= nsel
    qw = N_ATT_HEADS * LANES
    ow = N_ATT_HEADS * HEAD_V
    ovt = _overlap_matrix_t(nchunk, nsel)
    quarter = nchunk // 4
    variants = tuple(quarter * i for i in range(1, 5)) if quarter % LANES == 0 else (nchunk,)
    return pl.pallas_call(
        functools.partial(_select_kernel, tq=tq, nchunk=nchunk, variants=variants),
        grid=(batch, seq // tq),
        in_specs=[pl.BlockSpec((1, tq, qw), lambda b, t: (b, t, 0)),
                  pl.BlockSpec((1, nchunk, LANES), lambda b, t: (b, 0, 0)),
                  pl.BlockSpec((1, NSA_KV_HEADS * HEAD_V, nchunk), lambda b, t: (b, 0, 0)),
                  pl.BlockSpec((nsel, nchunk), lambda b, t: (0, 0)),
                  pl.BlockSpec((1, GATE_ROWS, tq), lambda b, t: (b, 0, t))],
        out_specs=[pl.BlockSpec((1, tq, ow), lambda b, t: (b, t, 0)),
                   pl.BlockSpec((1, NSA_KV_HEADS, tq, nsel), lambda b, t: (b, 0, t, 0))],
        out_shape=(jax.ShapeDtypeStruct((batch, seq, ow), F32),
                   jax.ShapeDtypeStruct((batch, NSA_KV_HEADS, seq, nsel), BF16)),
        scratch_shapes=[pltpu.VMEM((ow, tq), F32), pltpu.VMEM((NSA_KV_HEADS, nsel, tq), F32)],
        compiler_params=_cparams(("parallel", "parallel")),
        name="nsa_select",
    )(qn, kcmp, vcmp_t, ovt, gates_t)


def _out_proj_kernel(x_ref, a_ref, b_ref, oc_ref, os_ref, ow_ref, wa_ref, wb_ref, wc_ref, o_ref):
    acc = x_ref[...] + _dot(a_ref[...].astype(BF16), wa_ref[...])
    acc = acc + _dot(b_ref[...], wb_ref[...])
    c = (oc_ref[...] + os_ref[...]) + ow_ref[...]
    o_ref[...] = acc + _dot(c.astype(BF16), wc_ref[...])


def _out_proj(x2, a_out, b_out, o_c, o_s, o_w, w_out, tm=512):
    T = x2.shape[0]
    wa = w_out[0:256].astype(BF16)
    wb = w_out[256:640].astype(BF16)
    wc = w_out[640:1024].astype(BF16)
    row = lambda i: (i, 0)
    fixed = lambda i: (0, 0)
    act = lambda n: pl.BlockSpec((tm, n), row)
    return pl.pallas_call(
        _out_proj_kernel,
        grid=(T // tm,),
        in_specs=[act(D_MODEL), act(256), act(384), act(384), act(384), act(384),
                  pl.BlockSpec((256, D_MODEL), fixed), pl.BlockSpec((384, D_MODEL), fixed),
                  pl.BlockSpec((384, D_MODEL), fixed)],
        out_specs=pl.BlockSpec((tm, D_MODEL), row),
        out_shape=jax.ShapeDtypeStruct((T, D_MODEL), F32),
        compiler_params=_cparams(("parallel",)),
        name="out_proj",
    )(x2, a_out, b_out, o_c, o_s, o_w, wa, wb, wc)


def _final_norm(y, gf_ref):
    ms = jnp.mean(y * y, axis=-1, keepdims=True)
    return (y * lax.rsqrt(ms + EPS)) * gf_ref[...]


def _ffn_kernel(x_ref, g_ref, wg_ref, wu_ref, wd_ref, gf_ref, o_ref, h_sc, acc_sc, *, final_norm):
    f = pl.program_id(1)

    @pl.when(f == 0)
    def _():
        x = x_ref[...]
        ms = jnp.mean(x * x, axis=-1, keepdims=True)
        h_sc[...] = ((x * lax.rsqrt(ms + EPS)) * g_ref[...]).astype(BF16)
        acc_sc[...] = x

    h = h_sc[...]
    gte = _dot(h, wg_ref[...])
    up = _dot(h, wu_ref[...])
    act = (gte * _sigmoid(gte)) * up
    acc_sc[...] += _dot(act.astype(BF16), wd_ref[...])

    @pl.when(f == pl.num_programs(1) - 1)
    def _():
        y = acc_sc[...]
        o_ref[...] = _final_norm(y, gf_ref) if final_norm else y


def _ffn(x2, g, wg, wu, wd, g_final, final_norm, tm=512, tf=1408):
    T = x2.shape[0]
    nf = D_FF // tf
    return pl.pallas_call(
        functools.partial(_ffn_kernel, final_norm=final_norm),
        grid=(T // tm, nf),
        in_specs=[pl.BlockSpec((tm, D_MODEL), lambda i, f: (i, 0)), pl.BlockSpec((1, D_MODEL), lambda i, f: (0, 0)),
                  pl.BlockSpec((D_MODEL, tf), lambda i, f: (0, f)), pl.BlockSpec((D_MODEL, tf), lambda i, f: (0, f)),
                  pl.BlockSpec((tf, D_MODEL), lambda i, f: (f, 0)), pl.BlockSpec((1, D_MODEL), lambda i, f: (0, 0))],
        out_specs=pl.BlockSpec((tm, D_MODEL), lambda i, f: (i, 0)),
        out_shape=jax.ShapeDtypeStruct((T, D_MODEL), F32),
        scratch_shapes=[pltpu.VMEM((tm, D_MODEL), BF16), pltpu.VMEM((tm, D_MODEL), F32)],
        compiler_params=_cparams(("parallel", "arbitrary")),
        name="ffn_dense",
    )(x2, g.reshape(1, -1), wg.astype(BF16), wu.astype(BF16), wd.astype(BF16), g_final.reshape(1, -1))


MOE_TILE = 1024
MOE_CHUNK = 384


def _moe_kernel(x_ref, g_ref, rw_ref, tri_ref, wg_ref, wu_ref, wd_ref, gf_ref, o_ref,
                h_sc, gate_sc, key_sc, keyt_sc, acc_sc, *, final_norm, tm):
    e = pl.program_id(1)
    lane = lax.broadcasted_iota(jnp.int32, (tm, LANES), 1)

    @pl.when(e == 0)
    def _():
        x = x_ref[...]
        ms = jnp.mean(x * x, axis=-1, keepdims=True)
        hf = (x * lax.rsqrt(ms + EPS)) * g_ref[...]
        h_sc[...] = hf.astype(BF16)
        acc_sc[...] = x
        h_hi = hf.astype(BF16)
        h_lo = (hf - h_hi.astype(F32)).astype(BF16)
        logits = (_dot(h_hi, rw_ref[0]) + _dot(h_lo, rw_ref[0]) + _dot(h_hi, rw_ref[1]))
        logits = jnp.where(lane < N_EXPERTS, logits, -jnp.inf)
        m1 = jnp.max(logits, axis=1, keepdims=True)
        i1 = jnp.min(jnp.where(logits == m1, lane, LANES), axis=1, keepdims=True)
        rest = jnp.where(lane == i1, -jnp.inf, logits)
        m2 = jnp.max(rest, axis=1, keepdims=True)
        i2 = jnp.min(jnp.where(rest == m2, lane, LANES), axis=1, keepdims=True)
        e2 = jnp.exp(m2 - m1)
        den = 1.0 + e2
        gate_sc[...] = jnp.where(lane == i1, 1.0 / den, 0.0) + jnp.where(lane == i2, e2 / den, 0.0)
        routed = jnp.where((lane == i1) | (lane == i2), 1.0, 0.0)
        key = jnp.where(routed > 0.0, _dot(tri_ref[...], routed.astype(BF16)), -1.0)
        key_sc[...] = key
        keyt_sc[...] = key.T

    sel = lane == e
    kcol = jnp.max(jnp.where(sel, key_sc[...], -1.0), axis=1, keepdims=True)
    wcol = jnp.sum(jnp.where(sel, gate_sc[...], 0.0), axis=1, keepdims=True)
    krow = keyt_sc[pl.ds(e, 1), :]
    count = jnp.sum(jnp.where(kcol >= 0.0, 1.0, 0.0)).astype(jnp.int32)
    n_chunks = (count + (MOE_CHUNK - 1)) // MOE_CHUNK
    row_id = lax.broadcasted_iota(jnp.int32, (MOE_CHUNK, tm), 0).astype(F32)
    col_id = lax.broadcasted_iota(jnp.int32, (tm, MOE_CHUNK), 1).astype(F32)

    def chunk(ci, carry):
        base = (ci * MOE_CHUNK).astype(F32)
        gather = jnp.where(krow - base == row_id, 1.0, 0.0).astype(BF16)
        xg = _dot(gather, h_sc[...]).astype(BF16)
        gte = _dot(xg, wg_ref[0])
        up = _dot(xg, wu_ref[0])
        act = (gte * _sigmoid(gte)) * up
        y = _dot(act.astype(BF16), wd_ref[0])
        scatter = jnp.where(kcol - base == col_id, 1.0, 0.0).astype(BF16)
        acc_sc[...] += wcol * _dot(scatter, y.astype(BF16))
        return carry

    lax.fori_loop(0, n_chunks, chunk, 0)

    @pl.when(e == pl.num_programs(1) - 1)
    def _():
        out = acc_sc[...]
        o_ref[...] = _final_norm(out, gf_ref) if final_norm else out


def _moe(x2, g, router_w, wg, wu, wd, g_final, final_norm, tm=MOE_TILE):
    T = x2.shape[0]
    rw = _pad_cols(router_w, LANES)
    rw_hi = rw.astype(BF16)
    rw_lo = (rw - rw_hi.astype(F32)).astype(BF16)
    rw2 = jnp.stack([rw_hi, rw_lo])
    tri = jnp.asarray(np.tril(np.ones((tm, tm), np.float32), -1)).astype(BF16)
    return pl.pallas_call(
        functools.partial(_moe_kernel, final_norm=final_norm, tm=tm),
        grid=(T // tm, N_EXPERTS),
        in_specs=[pl.BlockSpec((tm, D_MODEL), lambda i, e: (i, 0)), pl.BlockSpec((1, D_MODEL), lambda i, e: (0, 0)),
                  pl.BlockSpec((2, D_MODEL, LANES), lambda i, e: (0, 0, 0)),
                  pl.BlockSpec((tm, tm), lambda i, e: (0, 0)),
                  pl.BlockSpec((1, D_MODEL, D_FF_EXPERT), lambda i, e: (e, 0, 0)),
                  pl.BlockSpec((1, D_MODEL, D_FF_EXPERT), lambda i, e: (e, 0, 0)),
                  pl.BlockSpec((1, D_FF_EXPERT, D_MODEL), lambda i, e: (e, 0, 0)),
                  pl.BlockSpec((1, D_MODEL), lambda i, e: (0, 0))],
        out_specs=pl.BlockSpec((tm, D_MODEL), lambda i, e: (i, 0)),
        out_shape=jax.ShapeDtypeStruct((T, D_MODEL), F32),
        scratch_shapes=[pltpu.VMEM((tm, D_MODEL), BF16), pltpu.VMEM((tm, LANES), F32),
                        pltpu.VMEM((tm, LANES), F32), pltpu.VMEM((LANES, tm), F32),
                        pltpu.VMEM((tm, D_MODEL), F32)],
        compiler_params=_cparams(("parallel", "arbitrary")),
        name="moe_routed",
    )(x2, g.reshape(1, -1), rw2, tri, wg.astype(BF16), wu.astype(BF16), wd.astype(BF16), g_final.reshape(1, -1))


def _mixer(x2, batch, seq, tables, onehot, norm_g, w_in, conv_w, conv_b, wa, ba, wx, bx, lam,
           q_norm, w_uq, kv_norm, w_ukv, cmp_pos, cmp_w1, cmp_w2, gate_b, w_out):
    w_cat, w_t = _prep_w_in(w_in)
    lru_xy, mla_dn, qn, kc, vc, ksw, vt_sw, gates_t = _in_proj(x2, norm_g, w_cat, w_t, tables, gate_b, batch, seq)
    a_out = _lru(lru_xy, conv_w, conv_b, wa, ba, wx, bx, lam, batch, seq)

    q_m, k_m, vt_m = _mla_prep(mla_dn, q_norm, w_uq, kv_norm, w_ukv, tables, batch, seq)
    b3 = lambda t: t.reshape(batch, seq, t.shape[-1])
    heads = tuple(range(N_ATT_HEADS))
    b_out = _flash(b3(q_m), b3(k_m), vt_m, mode="causal", batch=batch, seq=seq, per_head_kv=True,
                   k_width=N_ATT_HEADS * LANES, k_col=0, vt_rows=N_ATT_HEADS * HEAD_V, vt_row=0, out_dtype=BF16,
                   tk=LONG_KEY_TILE)

    kcmp, vcmp_t = _compress(kc, vc, cmp_pos, cmp_w1, cmp_w2, batch, seq)
    qn3 = b3(qn)
    o_c, selbias = _select(qn3, kcmp, vcmp_t, gates_t, batch, seq)
    ksw3 = b3(ksw)
    nsa_kw = dict(batch=batch, seq=seq, per_head_kv=False, k_width=LANES,
                  vt_rows=NSA_KV_HEADS * HEAD_V, out_dtype=F32, gates_t=gates_t)
    o_s = _flash(qn3, ksw3, vt_sw, mode="select", k_col=0, vt_row=0, selbias=selbias,
                 onehot=onehot, gate_rows=tuple(3 * h + 1 for h in heads), tk=LONG_KEY_TILE, **nsa_kw)
    o_w = _flash(qn3, ksw3, vt_sw, mode="window", k_col=1, vt_row=1,
                 gate_rows=tuple(3 * h + 2 for h in heads), **nsa_kw)
    T = batch * seq
    flat = lambda t: t.reshape(T, t.shape[-1])
    return _out_proj(x2, a_out, flat(b_out), flat(o_c), flat(o_s), flat(o_w), w_out)


def kernel(x, norm_mix, w_in, lru_conv_w, lru_conv_b, lru_wa, lru_ba, lru_wx, lru_bx, lru_lambda, mla_q_norm, mla_w_uq, mla_kv_norm, mla_w_ukv, nsa_cmp_pos, nsa_cmp_w1, nsa_cmp_w2, nsa_gate_b, w_out, norm_ffn, ffn_w_gate, ffn_w_up, ffn_w_down, router_w, moe_w_gate, moe_w_up, moe_w_down, norm_final):
    batch, seq, _ = x.shape
    depth = norm_mix.shape[0]
    tables = _rope_tables(seq)
    blk_id = np.arange(seq)[:, None] // SEL_LEN
    onehot = jnp.asarray((blk_id == np.arange(LANES)[None, :]).astype(np.float32)).astype(BF16)
    x2 = x.reshape(batch * seq, D_MODEL)
    for l in range(depth):
        x2 = _mixer(x2, batch, seq, tables, onehot, norm_mix[l], w_in[l], lru_conv_w[l], lru_conv_b[l],
                    lru_wa[l], lru_ba[l], lru_wx[l], lru_bx[l], lru_lambda[l], mla_q_norm[l], mla_w_uq[l],
                    mla_kv_norm[l], mla_w_ukv[l], nsa_cmp_pos[l], nsa_cmp_w1[l], nsa_cmp_w2[l],
                    nsa_gate_b[l], w_out[l])
        last = l == depth - 1
        j = l // 2
        if l % 2 == 0:
            x2 = _ffn(x2, norm_ffn[l], ffn_w_gate[j], ffn_w_up[j], ffn_w_down[j], norm_final, last)
        else:
            x2 = _moe(x2, norm_ffn[l], router_w[j], moe_w_gate[j], moe_w_up[j], moe_w_down[j], norm_final, last)
    if depth == 0:
        raise ValueError("depth must be positive")
    return x2.reshape(batch, seq, D_MODEL)
```

```python
import functools
import math

import numpy as np
import jax
import jax.numpy as jnp
from jax import lax
from jax.experimental import pallas as pl
from jax.experimental.pallas import tpu as pltpu

F32 = jnp.float32
BF16 = jnp.bfloat16

D_MODEL = 1024
LRU_WIDTH = 256
LRU_BLOCKS = 4
LRU_BLOCK_W = LRU_WIDTH // LRU_BLOCKS
CONV_WIDTH = 4
LRU_C = 8.0
MLA_HEADS = 6
MLA_NOPE = 64
MLA_ROPE = 32
MLA_V = 64
MLA_Q_RANK = 192
MLA_KV_RANK = 128
NSA_HEADS = 6
NSA_KV_HEADS = 2
NSA_GROUP = NSA_HEADS // NSA_KV_HEADS
NSA_HEAD_DIM = 64
CMP_LEN = 32
CMP_STRIDE = 16
CMP_HIDDEN = 128
SEL_LEN = 64
SEL_TOPK = 16
WINDOW = 512
FORCE_BONUS = 1.0e3
D_FF = 2816
N_EXPERTS = 8
D_FF_EXPERT = 1408
ROPE_THETA = 10000.0
EPS = 1e-6
NEG_INF = -1.0e30

LANES = 128
VMEM_LIMIT = 56 * 1024 * 1024

C_LRU = 0
C_MLA = 512
C_KPE_ROT = 1024
C_QN = 1152
C_QN_ROT = 1536
C_KC = 1920
C_KC_ROT = 2048
C_VC = 2176
C_KS = 2304
C_KS_ROT = 2432
C_KW = 2560
C_KW_ROT = 2688
IN_COLS_PAD = 2816
N_ATT_HEADS = 6
HEAD_V = 64
LONG_KEY_TILE = 1024
SUM_ROWS = 16
GATE_ROWS = 32
LOG2E = 1.4426950408889634


def _cparams(sem, flags=None):
    return pltpu.CompilerParams(dimension_semantics=sem, vmem_limit_bytes=VMEM_LIMIT, flags=flags)


def _gelu_tanh(x):
    return 0.5 * x * (1.0 + jnp.tanh(math.sqrt(2.0 / math.pi) * (x + 0.044715 * (x * x * x))))


def _sigmoid(x):
    return 1.0 / (1.0 + jnp.exp(-x))


def _dot(a, b):
    return jnp.dot(a, b, preferred_element_type=F32)


def _dot_nt(a, b):
    return lax.dot_general(a, b, (((1,), (1,)), ((), ())), preferred_element_type=F32)


def _rot_cols(w, half):
    return jnp.concatenate([-w[:, half:], w[:, :half]], axis=1)


def _pad_cols(w, n):
    return jnp.pad(w, ((0, 0), (0, n - w.shape[1])))


def _prep_w_in(w_in):
    splits = np.cumsum([256, 256, 192, 128, 32, 384, 128, 128, 128, 128, 128, 128, 18])[:-1].tolist()
    (x_l, y_l, q_dn, kv_dn, k_pe, q_n, kc, vc, ks, vs, kw, vw, g_n) = jnp.split(w_in, splits, axis=1)
    z = lambda n: jnp.zeros((w_in.shape[0], n), w_in.dtype)

    def kpe_slot(w):
        return jnp.concatenate([z(64), w, z(32)], axis=1)

    def qn_slots(fn):
        out = []
        for p in range(NSA_GROUP):
            for h in (p, p + NSA_GROUP):
                out.append(fn(q_n[:, h * 64:(h + 1) * 64]))
        return jnp.concatenate(out, axis=1)

    def kv_rot(w):
        return jnp.concatenate([_rot_cols(w[:, :64], 32), _rot_cols(w[:, 64:], 32)], axis=1)

    cols = [x_l, y_l, _pad_cols(q_dn, 256), kv_dn, kpe_slot(k_pe), kpe_slot(_rot_cols(k_pe, 16)),
            qn_slots(lambda w: w), qn_slots(lambda w: _rot_cols(w, 32)),
            kc, kv_rot(kc), vc, ks, kv_rot(ks), kw, kv_rot(kw)]
    w = jnp.concatenate(cols, axis=1)
    assert w.shape[1] == IN_COLS_PAD
    w_t = jnp.concatenate([vs, vw, _pad_cols(g_n, GATE_ROWS)], axis=1).T
    return w.astype(BF16), w_t.astype(BF16)


def _rope_tables(seq):
    pos = jnp.arange(seq, dtype=F32)[:, None]
    inv64 = ROPE_THETA ** (-jnp.arange(32, dtype=F32) * 2.0 / 64)
    inv32 = ROPE_THETA ** (-jnp.arange(16, dtype=F32) * 2.0 / 32)
    a64 = pos * inv64[None, :]
    a32 = pos * inv32[None, :]
    c64, s64 = jnp.cos(a64), jnp.sin(a64)
    c32, s32 = jnp.cos(a32), jnp.sin(a32)
    cos_n = jnp.concatenate([c64, c64, c64, c64], axis=1)
    sin_n = jnp.concatenate([s64, s64, s64, s64], axis=1)
    one = jnp.ones((seq, 64), F32)
    cos_m = jnp.concatenate([one, c32, c32, one[:, :32]], axis=1)
    sin_m = jnp.concatenate([0 * one, s32, s32, 0 * one[:, :32]], axis=1)
    return cos_n, sin_n, cos_m, sin_m


def _in_proj_kernel(x_ref, g_ref, w_ref, wt_ref, cn_ref, sn_ref, cm_ref, sm_ref, gb_ref,
                    lru_ref, mla_ref, qn_ref, kc_ref, vc_ref, ksw_ref, vt_ref, gt_ref):
    x = x_ref[...]
    ms = jnp.mean(x * x, axis=-1, keepdims=True)
    h = ((x * lax.rsqrt(ms + EPS)) * g_ref[...]).astype(BF16)

    def proj(c0, n):
        return _dot(h, w_ref[:, c0:c0 + n])

    cn, sn = cn_ref[...], sn_ref[...]
    lru_ref[...] = proj(C_LRU, 512)
    mla_ref[:, 0:384] = proj(C_MLA, 384)
    mla_ref[:, 384:512] = proj(C_MLA + 384, 128) * cm_ref[...] + proj(C_KPE_ROT, 128) * sm_ref[...]
    scale = NSA_HEAD_DIM ** -0.5 * LOG2E
    low_half = lax.broadcasted_iota(jnp.int32, cn.shape, 1) < NSA_HEAD_DIM
    for p in range(NSA_GROUP):
        a = proj(C_QN + p * LANES, LANES)
        b = proj(C_QN_ROT + p * LANES, LANES)
        r = (a * cn + b * sn) * scale
        hi_hd = p + NSA_GROUP
        qn_ref[:, p * LANES:(p + 1) * LANES] = jnp.where(low_half, r, 0.0).astype(BF16)
        qn_ref[:, hi_hd * LANES:(hi_hd + 1) * LANES] = jnp.where(low_half, 0.0, r).astype(BF16)
    kc_ref[...] = proj(C_KC, 128) * cn + proj(C_KC_ROT, 128) * sn
    vc_ref[...] = proj(C_VC, 128)
    ksw_ref[:, 0:128] = (proj(C_KS, 128) * cn + proj(C_KS_ROT, 128) * sn).astype(BF16)
    ksw_ref[:, 128:256] = (proj(C_KW, 128) * cn + proj(C_KW_ROT, 128) * sn).astype(BF16)
    vt_ref[0] = _dot_nt(wt_ref[0:256, :], h).astype(BF16)
    gt_ref[0] = _sigmoid(_dot_nt(wt_ref[256:256 + GATE_ROWS, :], h) + gb_ref[...])


def _in_proj(x2, g, w_cat, w_t, tables, gate_b, batch, seq, tm=512):
    T = x2.shape[0]
    nt = T // tm
    npos = seq // tm
    cn, sn, cm, sm = tables
    row = lambda i: (i, 0)
    fixed = lambda i: (0, 0)
    posmap = lambda i: (i % npos, 0)
    tab_spec = pl.BlockSpec((tm, LANES), posmap)
    out_shapes = (
        jax.ShapeDtypeStruct((T, 512), F32),
        jax.ShapeDtypeStruct((T, 512), F32),
        jax.ShapeDtypeStruct((T, 768), BF16),
        jax.ShapeDtypeStruct((T, 128), F32),
        jax.ShapeDtypeStruct((T, 128), F32),
        jax.ShapeDtypeStruct((T, 256), BF16),
    )
    t_shapes = (jax.ShapeDtypeStruct((batch, 256, seq), BF16),
                jax.ShapeDtypeStruct((batch, GATE_ROWS, seq), F32))
    tmap = lambda i: (i // npos, 0, i % npos)
    gate_b_col = jnp.pad(gate_b, (0, GATE_ROWS - gate_b.shape[0])).reshape(GATE_ROWS, 1)
    return pl.pallas_call(
        _in_proj_kernel,
        grid=(nt,),
        in_specs=[pl.BlockSpec((tm, D_MODEL), row), pl.BlockSpec((1, D_MODEL), fixed),
                  pl.BlockSpec((D_MODEL, IN_COLS_PAD), fixed), pl.BlockSpec((256 + GATE_ROWS, D_MODEL), fixed),
                  tab_spec, tab_spec, tab_spec, tab_spec, pl.BlockSpec((GATE_ROWS, 1), fixed)],
        out_specs=[pl.BlockSpec((tm, s.shape[1]), row) for s in out_shapes]
        + [pl.BlockSpec((1, s.shape[1], tm), tmap) for s in t_shapes],
        out_shape=out_shapes + t_shapes,
        compiler_params=_cparams(("parallel",)),
        name="in_proj",
    )(x2, g.reshape(1, -1), w_cat, w_t, cn, sn, cm, sm, gate_b_col)


def _lru_kernel(xy_ref, cw_ref, cb_ref, wa_ref, ba_ref, wx_ref, bx_ref, lam_ref, out_ref,
                xe_sc, h_sc, *, tt):
    t = pl.program_id(1)

    @pl.when(t == 0)
    def _():
        xe_sc[0:8, :] = jnp.zeros((8, LRU_WIDTH), F32)
        h_sc[...] = jnp.zeros_like(h_sc)

    x = xy_ref[0, :, 0:LRU_WIDTH]
    y = xy_ref[0, :, LRU_WIDTH:2 * LRU_WIDTH]
    xe_sc[8:8 + tt, :] = x
    xc = cb_ref[...]
    for k in range(CONV_WIDTH):
        off = 8 - (CONV_WIDTH - 1) + k
        xc = xc + xe_sc[off:off + tt, :] * cw_ref[k:k + 1, :]
    xe_sc[0:8, :] = x[tt - 8:tt, :]

    xb = xc.astype(BF16)
    r = _sigmoid(_dot(xb, wa_ref[...]) + ba_ref[...])
    i = _sigmoid(_dot(xb, wx_ref[...]) + bx_ref[...])
    log_a = (-LRU_C * r) * jax.nn.softplus(-lam_ref[...])
    a = jnp.exp(log_a)
    b = jnp.sqrt(jnp.tanh(-log_a) * (a * a + 1.0)) * (i * xc)

    row = lax.broadcasted_iota(jnp.int32, (tt, LRU_WIDTH), 0)
    k = 1
    while k < tt:
        keep = row >= k
        a_sh = jnp.where(keep, pltpu.roll(a, k, 0), 1.0)
        b_sh = jnp.where(keep, pltpu.roll(b, k, 0), 0.0)
        b = a * b_sh + b
        a = a * a_sh
        k *= 2
    h = b + a * h_sc[0:1, :]
    h_sc[0:1, :] = h[tt - 1:tt, :]
    out_ref[0] = h * _gelu_tanh(y)


def _block_diag_dense(w):
    n, c, d = w.shape
    out = jnp.zeros((n * c, n * d), w.dtype)
    for j in range(n):
        out = out.at[j * c:(j + 1) * c, j * d:(j + 1) * d].set(w[j])
    return out


def _lru(lru_xy, conv_w, conv_b, wa, ba, wx, bx, lam, batch, seq, tt=512):
    xy = lru_xy.reshape(batch, seq, 512)
    fixed = lambda b, t: (0, 0)
    vec = pl.BlockSpec((1, LRU_WIDTH), fixed)
    mat = pl.BlockSpec((LRU_WIDTH, LRU_WIDTH), fixed)
    out = pl.pallas_call(
        functools.partial(_lru_kernel, tt=tt),
        grid=(batch, seq // tt),
        in_specs=[pl.BlockSpec((1, tt, 512), lambda b, t: (b, t, 0)),
                  pl.BlockSpec((CONV_WIDTH, LRU_WIDTH), fixed), vec, mat, vec, mat, vec, vec],
        out_specs=pl.BlockSpec((1, tt, LRU_WIDTH), lambda b, t: (b, t, 0)),
        out_shape=jax.ShapeDtypeStruct((batch, seq, LRU_WIDTH), F32),
        scratch_shapes=[pltpu.VMEM((tt + 8, LRU_WIDTH), F32), pltpu.VMEM((8, LRU_WIDTH), F32)],
        compiler_params=_cparams(("parallel", "arbitrary")),
        name="rg_lru",
    )(xy, conv_w, conv_b.reshape(1, -1), _block_diag_dense(wa).astype(BF16), ba.reshape(1, -1),
      _block_diag_dense(wx).astype(BF16), bx.reshape(1, -1), lam.reshape(1, -1))
    return out.reshape(batch * seq, LRU_WIDTH)


def _mla_prep_kernel(dn_ref, qn_ref, wqa_ref, wqb_ref, kvn_ref, wk_ref, wvt_ref, cm_ref, sm_ref,
                     q_ref, k_ref, vt_ref):
    q_dn = dn_ref[:, 0:256]
    ms = jnp.sum(q_dn * q_dn, axis=-1, keepdims=True) * (1.0 / MLA_Q_RANK)
    ql = ((q_dn * lax.rsqrt(ms + EPS)) * qn_ref[...]).astype(BF16)
    kv_dn = dn_ref[:, 256:384]
    ms = jnp.mean(kv_dn * kv_dn, axis=-1, keepdims=True)
    kvl = ((kv_dn * lax.rsqrt(ms + EPS)) * kvn_ref[...]).astype(BF16)
    kpe = dn_ref[:, 384:512]
    cm, sm = cm_ref[...], sm_ref[...]
    scale = (MLA_NOPE + MLA_ROPE) ** -0.5 * LOG2E
    for hd in range(MLA_HEADS):
        sl = slice(hd * LANES, (hd + 1) * LANES)
        q_ref[:, sl] = ((_dot(ql, wqa_ref[:, sl]) * cm + _dot(ql, wqb_ref[:, sl]) * sm) * scale).astype(BF16)
        k_ref[:, sl] = (_dot(kvl, wk_ref[:, sl]) + kpe).astype(BF16)
    vt_ref[0] = _dot_nt(wvt_ref[...], kvl).astype(BF16)


def _prep_mla_weights(w_uq, w_ukv):
    zq = jnp.zeros((MLA_Q_RANK, 32), w_uq.dtype)
    qa, qb, kk, vv = [], [], [], []
    for h in range(MLA_HEADS):
        wq = w_uq[:, h * 96:(h + 1) * 96]
        nope, ropew = wq[:, :64], wq[:, 64:]
        qa.append(jnp.concatenate([nope, ropew, zq], axis=1))
        qb.append(jnp.concatenate([0 * nope, _rot_cols(ropew, 16), zq], axis=1))
        wkv = w_ukv[:, h * 128:(h + 1) * 128]
        kk.append(_pad_cols(wkv[:, :64], LANES))
        vv.append(wkv[:, 64:])
    pad_rows = lambda w: jnp.pad(w, ((0, 256 - MLA_Q_RANK), (0, 0)))
    return (pad_rows(jnp.concatenate(qa, axis=1)).astype(BF16), pad_rows(jnp.concatenate(qb, axis=1)).astype(BF16),
            jnp.concatenate(kk, axis=1).astype(BF16), jnp.concatenate(vv, axis=1).T.astype(BF16))


def _mla_prep(mla_dn, q_norm, w_uq, kv_norm, w_ukv, tables, batch, seq, tm=512):
    T = mla_dn.shape[0]
    npos = seq // tm
    _, _, cm, sm = tables
    wqa, wqb, wk, wvt = _prep_mla_weights(w_uq, w_ukv)
    row = lambda i: (i, 0)
    fixed = lambda i: (0, 0)
    tab = pl.BlockSpec((tm, LANES), lambda i: (i % npos, 0))
    return pl.pallas_call(
        _mla_prep_kernel,
        grid=(T // tm,),
        in_specs=[pl.BlockSpec((tm, 512), row), pl.BlockSpec((1, 256), fixed),
                  pl.BlockSpec((256, 768), fixed), pl.BlockSpec((256, 768), fixed),
                  pl.BlockSpec((1, 128), fixed), pl.BlockSpec((128, 768), fixed),
                  pl.BlockSpec((384, 128), fixed), tab, tab],
        out_specs=[pl.BlockSpec((tm, 768), row), pl.BlockSpec((tm, 768), row),
                   pl.BlockSpec((1, 384, tm), lambda i: (i // npos, 0, i % npos))],
        out_shape=(jax.ShapeDtypeStruct((T, 768), BF16), jax.ShapeDtypeStruct((T, 768), BF16),
                   jax.ShapeDtypeStruct((batch, 384, seq), BF16)),
        compiler_params=_cparams(("parallel",)),
        name="mla_prep",
    )(mla_dn, _pad_cols(q_norm.reshape(1, -1), 256), wqa, wqb, kv_norm.reshape(1, -1), wk, wvt, cm, sm)


LAG_LIMIT = 8.0


def _flash_kernel(qt_ref, kt_ref, first_ref, last_ref, par_ref, *refs, mode, k_per_head, v_per_head, gate_rows,
                  tq, tk):
    refs = list(refs)
    q_ref, k_ref, vt_ref = refs[:3]
    o_ref, m_st, acc_st = refs[-8:-5]
    s_sc, mx_sc, flag_sm = refs[-5:-3], refs[-3:-1], refs[-1]
    extra = refs[3:-8]
    sb_ref, oh_ref = (extra[0], extra[1]) if mode == "select" else (None, None)
    gt_ref = extra[-1] if gate_rows is not None else None
    step = pl.program_id(1)
    qt = qt_ref[step]
    kt = kt_ref[step]
    rd = par_ref[step]
    wr = 1 - rd

    def visible(nk):
        kpos = kt * tk + lax.broadcasted_iota(jnp.int32, (nk, tq), 0)
        qpos = qt * tq + lax.broadcasted_iota(jnp.int32, (nk, tq), 1)
        dist = qpos - kpos
        return (dist >= 0) & (dist < WINDOW) if mode == "window" else dist >= 0

    def score_tile(j, nk, valid):
        q = q_ref[0, :, j * LANES:(j + 1) * LANES]
        k = k_ref[0, 0:nk, j * LANES:(j + 1) * LANES] if k_per_head else k_ref[0, 0:nk, :]
        if mode == "select":
            q = jnp.concatenate([q, sb_ref[0, j // NSA_GROUP]], axis=1)
            k = jnp.concatenate([k, oh_ref[0:nk, :]], axis=1)
        s = _dot_nt(k, q)
        return s if valid is None else jnp.where(valid, s, NEG_INF)

    def weighted_values(j, nk, p):
        row = (j if v_per_head else j // NSA_GROUP) * HEAD_V
        vt = vt_ref[0, row:row + HEAD_V, 0:nk]
        return _dot(jnp.concatenate([vt, jnp.ones((SUM_ROWS, nk), BF16)], axis=0), p)

    def exact(nk, init):
        valid = visible(nk)

        def scores(j, slot):
            s = score_tile(j, nk, valid)
            s_sc[slot][0:nk, :] = s
            mx_sc[slot][...] = jnp.max(s, axis=0, keepdims=True)

        def accumulate(j, slot):
            if init:
                m_new = mx_sc[slot][...]
            else:
                m_prev = m_st[rd, j]
                m_new = jnp.maximum(m_prev, mx_sc[slot][...])
            p = jnp.exp2((s_sc[slot][0:nk, :] - m_new).astype(BF16))
            upd = weighted_values(j, nk, p)
            acc_st[wr, j] = upd if init else jnp.exp2(m_prev - m_new) * acc_st[rd, j] + upd
            m_st[wr, j] = m_new

        scores(0, 0)
        for j in range(N_ATT_HEADS):
            if j + 1 < N_ATT_HEADS:
                scores(j + 1, (j + 1) % 2)
            accumulate(j, j % 2)

    def lagged(nk, masked):
        valid = visible(nk) if masked else None
        rise = None
        for j in range(N_ATT_HEADS):
            s = score_tile(j, nk, valid)
            m_prev = m_st[rd, j]
            p = jnp.exp2((s - m_prev).astype(BF16))
            tile_max = jnp.max(s, axis=0, keepdims=True)
            m_new = jnp.maximum(m_prev, tile_max)
            acc_st[wr, j] = (acc_st[rd, j] + weighted_values(j, nk, p)) * jnp.exp2(m_prev - m_new)
            m_st[wr, j] = m_new
            d = tile_max - m_prev
            rise = d if rise is None else jnp.maximum(rise, d)
        flag_sm[0] = (jnp.max(rise) > LAG_LIMIT).astype(jnp.int32)

    first = first_ref[step] == 1
    later = jnp.logical_not(first)
    pl.when(first)(lambda: exact(tk, True))
    if mode == "window":
        pl.when(later)(lambda: exact(tk, False))
    else:
        first_q, last_q = qt * tq, qt * tq + (tq - 1)
        crosses = kt * tk + (tk - 1) > first_q
        flag_sm[0] = 0
        pl.when(later & jnp.logical_not(crosses))(lambda: lagged(tk, False))
        if tk > tq:
            half_only = kt * tk + tk // 2 > last_q
            pl.when(later & crosses & half_only)(lambda: lagged(tk // 2, True))
            pl.when(later & crosses & jnp.logical_not(half_only))(lambda: lagged(tk, True))
        else:
            pl.when(later & crosses)(lambda: lagged(tk, True))
        pl.when(flag_sm[0] == 1)(lambda: exact(tk, False))

    @pl.when(last_ref[step] == 1)
    def _():
        outs = []
        for j in range(N_ATT_HEADS):
            o = acc_st[wr, j, 0:HEAD_V, :] / acc_st[wr, j, HEAD_V:HEAD_V + 1, :]
            if gate_rows is not None:
                o = o * gt_ref[0, gate_rows[j]:gate_rows[j] + 1, :]
            outs.append(o)
        o_ref[0] = jnp.concatenate(outs, axis=0).T.astype(o_ref.dtype)


def _pair_tables(nq, mode, tq, tk):
    qts, kts, first, last, par = [], [], [], [], []
    for qt in range(nq):
        hi = (qt * tq + tq - 1) // tk
        lo = 0 if mode != "window" else max(0, (qt * tq - (WINDOW - 1)) // tk)
        for kt in range(lo, hi + 1):
            qts.append(qt)
            kts.append(kt)
            first.append(1 if kt == lo else 0)
            last.append(1 if kt == hi else 0)
            par.append((kt - lo) % 2)
    arr = lambda v: jnp.asarray(np.array(v, dtype=np.int32))
    return arr(qts), arr(kts), arr(first), arr(last), arr(par)


def _flash(q, k, vt, *, mode, batch, seq, per_head_kv, k_width, k_col, vt_rows, vt_row, out_dtype,
           selbias=None, onehot=None, gate_rows=None, gates_t=None, tq=512, tk=512):
    tk = min(tk, seq)
    nq = seq // tq
    tables = _pair_tables(nq, mode, tq, tk)
    n_steps = int(tables[0].shape[0])
    qw = N_ATT_HEADS * LANES
    ow = N_ATT_HEADS * HEAD_V
    in_specs = [
        pl.BlockSpec((1, tq, qw), lambda b, s, qt, kt, f, l, p: (b, qt[s], 0)),
        pl.BlockSpec((1, tk, k_width), lambda b, s, qt, kt, f, l, p: (b, kt[s], k_col)),
        pl.BlockSpec((1, vt_rows, tk), lambda b, s, qt, kt, f, l, p: (b, vt_row, kt[s])),
    ]
    args = [q, k, vt]
    if mode == "select":
        in_specs.append(pl.BlockSpec((1, NSA_KV_HEADS, tq, LANES), lambda b, s, qt, kt, f, l, p: (b, 0, qt[s], 0)))
        in_specs.append(pl.BlockSpec((tk, LANES), lambda b, s, qt, kt, f, l, p: (kt[s], 0)))
        args += [selbias, onehot]
    if gate_rows is not None:
        in_specs.append(pl.BlockSpec((1, GATE_ROWS, tq), lambda b, s, qt, kt, f, l, p: (b, 0, qt[s])))
        args.append(gates_t)
    kern = functools.partial(_flash_kernel, mode=mode, k_per_head=per_head_kv, v_per_head=per_head_kv,
                             gate_rows=gate_rows, tq=tq, tk=tk)
    return pl.pallas_call(
        kern,
        grid_spec=pltpu.PrefetchScalarGridSpec(
            num_scalar_prefetch=5,
            grid=(batch, n_steps),
            in_specs=in_specs,
            out_specs=pl.BlockSpec((1, tq, ow), lambda b, s, qt, kt, f, l, p: (b, qt[s], 0)),
            scratch_shapes=[pltpu.VMEM((2, N_ATT_HEADS, 1, tq), F32),
                            pltpu.VMEM((2, N_ATT_HEADS, HEAD_V + SUM_ROWS, tq), F32),
                            pltpu.VMEM((tk, tq), F32), pltpu.VMEM((tk, tq), F32),
                            pltpu.VMEM((1, tq), F32), pltpu.VMEM((1, tq), F32),
                            pltpu.SMEM((1,), jnp.int32)],
        ),
        out_shape=jax.ShapeDtypeStruct((batch, seq, ow), out_dtype),
        compiler_params=_cparams(("parallel", "arbitrary")),
        name="flash_" + mode,
    )(*tables, *args)


def _compress_kernel(kc_ref, vc_ref, pos_ref, w1a_ref, w1b_ref, w2_ref, ko_ref, vo_ref, *, nchunk):
    row = lax.broadcasted_iota(jnp.int32, (nchunk, LANES), 0)
    for br, src in enumerate((kc_ref, vc_ref)):
        r = src[0]
        pa = _dot((r + pos_ref[br, 0:1, :]).astype(BF16), w1a_ref[br])
        pb = _dot((r + pos_ref[br, 1:2, :]).astype(BF16), w1b_ref[br])
        hid = _gelu_tanh(pa + pltpu.roll(pb, nchunk - 1, 0))
        out = jnp.where(row < nchunk - 1, _dot(hid.astype(BF16), w2_ref[br]), 0.0)
        if br == 0:
            ko_ref[0] = out.astype(ko_ref.dtype)
        else:
            vo_ref[0] = out.T.astype(vo_ref.dtype)


def _prep_compress_weights(cmp_pos, cmp_w1, cmp_w2):
    half = CMP_LEN // 2
    pos, w1a, w1b, w2 = [], [], [], []
    for br in range(2):
        p = cmp_pos[br]
        tile = lambda ph: jnp.concatenate([ph, ph], axis=1).reshape(1, half * LANES)
        pos.append(jnp.concatenate([tile(p[:half]), tile(p[half:])], axis=0))
        w = cmp_w1[br].reshape(CMP_LEN, NSA_HEAD_DIM, CMP_HIDDEN)
        z = jnp.zeros_like(w[:half])

        def big(wh):
            g0 = jnp.concatenate([wh, z], axis=1)
            g1 = jnp.concatenate([z, wh], axis=1)
            return jnp.concatenate([g0.reshape(half * LANES, CMP_HIDDEN), g1.reshape(half * LANES, CMP_HIDDEN)], axis=1)

        w1a.append(big(w[:half]))
        w1b.append(big(w[half:]))
        w2.append(_block_diag_dense(jnp.stack([cmp_w2[br], cmp_w2[br]])))
    return (jnp.stack(pos), jnp.stack(w1a).astype(BF16), jnp.stack(w1b).astype(BF16), jnp.stack(w2).astype(BF16))


def _compress(kc, vc, cmp_pos, cmp_w1, cmp_w2, batch, seq):
    nchunk = seq // CMP_STRIDE
    width = CMP_STRIDE * LANES
    pos, w1a, w1b, w2 = _prep_compress_weights(cmp_pos, cmp_w1, cmp_w2)
    blk = pl.BlockSpec((1, nchunk, width), lambda b: (b, 0, 0))
    full3 = lambda shape: pl.BlockSpec(shape, lambda b: (0, 0, 0))
    return pl.pallas_call(
        functools.partial(_compress_kernel, nchunk=nchunk),
        grid=(batch,),
        in_specs=[blk, blk, full3(pos.shape), full3(w1a.shape), full3(w1b.shape), full3(w2.shape)],
        out_specs=[pl.BlockSpec((1, nchunk, LANES), lambda b: (b, 0, 0)),
                   pl.BlockSpec((1, LANES, nchunk), lambda b: (b, 0, 0))],
        out_shape=(jax.ShapeDtypeStruct((batch, nchunk, LANES), BF16),
                   jax.ShapeDtypeStruct((batch, LANES, nchunk), BF16)),
        compiler_params=_cparams(("parallel",)),
        name="nsa_compress",
    )(kc.reshape(batch, nchunk, width), vc.reshape(batch, nchunk, width), pos, w1a, w1b, w2)


def _select_kernel(q_ref, kc_ref, vct_ref, ovt_ref, gt_ref, oc_ref, sb_ref, ot_sc, imp_sc, *, tq, nchunk, variants):
    nsel = LANES
    qt = pl.program_id(1)
    col_ok = (qt * tq + lax.broadcasted_iota(jnp.int32, (1, tq), 1) >= CMP_LEN - 1).astype(F32)

    def attend(nc):
        kc = kc_ref[0, 0:nc, :]
        qpos = qt * tq + lax.broadcasted_iota(jnp.int32, (nc, tq), 1)
        cend = lax.broadcasted_iota(jnp.int32, (nc, tq), 0) * CMP_STRIDE + (CMP_LEN - 1)
        cvalid = cend <= qpos
        ovt = ovt_ref[:, 0:nc]
        for g in range(NSA_KV_HEADS):
            vct = vct_ref[0, g * HEAD_V:(g + 1) * HEAD_V, 0:nc]
            psum = jnp.zeros((nc, tq), F32)
            for pj in range(NSA_GROUP):
                hd = g * NSA_GROUP + pj
                q = q_ref[0, :, hd * LANES:(hd + 1) * LANES]
                s = jnp.where(cvalid, _dot_nt(kc, q), NEG_INF)
                m = jnp.max(s, axis=0, keepdims=True)
                e = jnp.exp2(s - m)
                p = e * (col_ok / jnp.maximum(jnp.sum(e, axis=0, keepdims=True), 1e-30))
                ot_sc[hd * HEAD_V:(hd + 1) * HEAD_V, :] = _dot(vct, p.astype(BF16)) * gt_ref[0, 3 * hd:3 * hd + 1, :]
                psum = psum + p
            hi = psum.astype(BF16)
            r1 = psum - hi.astype(F32)
            mid = r1.astype(BF16)
            lo = (r1 - mid.astype(F32)).astype(BF16)
            imp_sc[g] = _dot(ovt, hi) + _dot(ovt, mid) + _dot(ovt, lo)

    needed = (qt * tq + tq - CMP_STRIDE) // CMP_STRIDE
    lo_bound = 0
    for nc in variants:
        pl.when((needed > lo_bound) & (needed <= nc) if nc != variants[-1] else needed > lo_bound)(
            functools.partial(attend, nc))
        lo_bound = nc

    blk = lax.broadcasted_iota(jnp.int32, (nsel, tq), 0)
    blkf = blk.astype(F32)
    qp = qt * tq + lax.broadcasted_iota(jnp.int32, (nsel, tq), 1)
    valid = blk * SEL_LEN <= qp
    forced = (blk == 0) | (blk == qp // SEL_LEN)
    for g in range(NSA_KV_HEADS):
        score = jnp.where(valid, imp_sc[g], -1.0)
        score0 = jnp.where(forced, score + FORCE_BONUS, score)
        score = score0
        for _ in range(min(SEL_TOPK, nsel)):
            m = jnp.max(score, axis=0, keepdims=True)
            idx = jnp.min(jnp.where(score == m, blkf, float(nsel)), axis=0, keepdims=True)
            score = jnp.where(blkf == idx, -jnp.inf, score)
        bias = jnp.where(score0 >= 0.0, jnp.where(score == -jnp.inf, 0.0, NEG_INF), NEG_INF)
        sb_ref[0, g] = bias.T.astype(sb_ref.dtype)
    oc_ref[0] = ot_sc[...].T


def _overlap_matrix_t(nchunk, nsel):
    n = np.arange(nchunk)[None, :]
    s = np.arange(nsel)[:, None]
    cs = n * CMP_STRIDE
    ov = (cs < s * SEL_LEN + SEL_LEN) & (cs + CMP_LEN - 1 >= s * SEL_LEN) & (n < nchunk - 1)
    return jnp.asarray(ov.astype(np.float32)).astype(BF16)


def _select(qn, kcmp, vcmp_t, gates_t, batch, seq, tq=256):
    nchunk = seq // CMP_STRIDE
    nsel = LANES
    assert seq // SEL_LEN <= nsel
    qw = N_ATT_HEADS * LANES
    ow = N_ATT_HEADS * HEAD_V
    ovt = _overlap_matrix_t(nchunk, nsel)
    quarter = nchunk // 4
    variants = tuple(quarter * i for i in range(1, 5)) if quarter % LANES == 0 else (nchunk,)
    return pl.pallas_call(
        functools.partial(_select_kernel, tq=tq, nchunk=nchunk, variants=variants),
        grid=(batch, seq // tq),
        in_specs=[pl.BlockSpec((1, tq, qw), lambda b, t: (b, t, 0)),
                  pl.BlockSpec((1, nchunk, LANES), lambda b, t: (b, 0, 0)),
                  pl.BlockSpec((1, NSA_KV_HEADS * HEAD_V, nchunk), lambda b, t: (b, 0, 0)),
                  pl.BlockSpec((nsel, nchunk), lambda b, t: (0, 0)),
                  pl.BlockSpec((1, GATE_ROWS, tq), lambda b, t: (b, 0, t))],
        out_specs=[pl.BlockSpec((1, tq, ow), lambda b, t: (b, t, 0)),
                   pl.BlockSpec((1, NSA_KV_HEADS, tq, nsel), lambda b, t: (b, 0, t, 0))],
        out_shape=(jax.ShapeDtypeStruct((batch, seq, ow), F32),
                   jax.ShapeDtypeStruct((batch, NSA_KV_HEADS, seq, nsel), BF16)),
        scratch_shapes=[pltpu.VMEM((ow, tq), F32), pltpu.VMEM((NSA_KV_HEADS, nsel, tq), F32)],
        compiler_params=_cparams(("parallel", "parallel")),
        name="nsa_select",
    )(qn, kcmp, vcmp_t, ovt, gates_t)


def _out_proj_kernel(x_ref, a_ref, b_ref, oc_ref, os_ref, ow_ref, wa_ref, wb_ref, wc_ref, o_ref):
    acc = x_ref[...] + _dot(a_ref[...].astype(BF16), wa_ref[...])
    acc = acc + _dot(b_ref[...], wb_ref[...])
    c = (oc_ref[...] + os_ref[...]) + ow_ref[...]
    o_ref[...] = acc + _dot(c.astype(BF16), wc_ref[...])


def _out_proj(x2, a_out, b_out, o_c, o_s, o_w, w_out, tm=512):
    T = x2.shape[0]
    wa = w_out[0:256].astype(BF16)
    wb = w_out[256:640].astype(BF16)
    wc = w_out[640:1024].astype(BF16)
    row = lambda i: (i, 0)
    fixed = lambda i: (0, 0)
    act = lambda n: pl.BlockSpec((tm, n), row)
    return pl.pallas_call(
        _out_proj_kernel,
        grid=(T // tm,),
        in_specs=[act(D_MODEL), act(256), act(384), act(384), act(384), act(384),
                  pl.BlockSpec((256, D_MODEL), fixed), pl.BlockSpec((384, D_MODEL), fixed),
                  pl.BlockSpec((384, D_MODEL), fixed)],
        out_specs=pl.BlockSpec((tm, D_MODEL), row),
        out_shape=jax.ShapeDtypeStruct((T, D_MODEL), F32),
        compiler_params=_cparams(("parallel",)),
        name="out_proj",
    )(x2, a_out, b_out, o_c, o_s, o_w, wa, wb, wc)


def _final_norm(y, gf_ref):
    ms = jnp.mean(y * y, axis=-1, keepdims=True)
    return (y * lax.rsqrt(ms + EPS)) * gf_ref[...]


def _ffn_kernel(x_ref, g_ref, wg_ref, wu_ref, wd_ref, gf_ref, o_ref, h_sc, acc_sc, *, final_norm):
    f = pl.program_id(1)

    @pl.when(f == 0)
    def _():
        x = x_ref[...]
        ms = jnp.mean(x * x, axis=-1, keepdims=True)
        h_sc[...] = ((x * lax.rsqrt(ms + EPS)) * g_ref[...]).astype(BF16)
        acc_sc[...] = x

    h = h_sc[...]
    gte = _dot(h, wg_ref[...])
    up = _dot(h, wu_ref[...])
    act = (gte * _sigmoid(gte)) * up
    acc_sc[...] += _dot(act.astype(BF16), wd_ref[...])

    @pl.when(f == pl.num_programs(1) - 1)
    def _():
        y = acc_sc[...]
        o_ref[...] = _final_norm(y, gf_ref) if final_norm else y


def _ffn(x2, g, wg, wu, wd, g_final, final_norm, tm=512, tf=1408):
    T = x2.shape[0]
    nf = D_FF // tf
    return pl.pallas_call(
        functools.partial(_ffn_kernel, final_norm=final_norm),
        grid=(T // tm, nf),
        in_specs=[pl.BlockSpec((tm, D_MODEL), lambda i, f: (i, 0)), pl.BlockSpec((1, D_MODEL), lambda i, f: (0, 0)),
                  pl.BlockSpec((D_MODEL, tf), lambda i, f: (0, f)), pl.BlockSpec((D_MODEL, tf), lambda i, f: (0, f)),
                  pl.BlockSpec((tf, D_MODEL), lambda i, f: (f, 0)), pl.BlockSpec((1, D_MODEL), lambda i, f: (0, 0))],
        out_specs=pl.BlockSpec((tm, D_MODEL), lambda i, f: (i, 0)),
        out_shape=jax.ShapeDtypeStruct((T, D_MODEL), F32),
        scratch_shapes=[pltpu.VMEM((tm, D_MODEL), BF16), pltpu.VMEM((tm, D_MODEL), F32)],
        compiler_params=_cparams(("parallel", "arbitrary")),
        name="ffn_dense",
    )(x2, g.reshape(1, -1), wg.astype(BF16), wu.astype(BF16), wd.astype(BF16), g_final.reshape(1, -1))


MOE_TILE = 1024
MOE_CHUNK = 384


def _moe_kernel(x_ref, g_ref, rw_ref, tri_ref, wg_ref, wu_ref, wd_ref, gf_ref, o_ref,
                h_sc, gate_sc, key_sc, keyt_sc, acc_sc, *, final_norm, tm):
    e = pl.program_id(1)
    lane = lax.broadcasted_iota(jnp.int32, (tm, LANES), 1)

    @pl.when(e == 0)
    def _():
        x = x_ref[...]
        ms = jnp.mean(x * x, axis=-1, keepdims=True)
        hf = (x * lax.rsqrt(ms + EPS)) * g_ref[...]
        h_sc[...] = hf.astype(BF16)
        acc_sc[...] = x
        h_hi = hf.astype(BF16)
        h_lo = (hf - h_hi.astype(F32)).astype(BF16)
        logits = (_dot(h_hi, rw_ref[0]) + _dot(h_lo, rw_ref[0]) + _dot(h_hi, rw_ref[1]))
        logits = jnp.where(lane < N_EXPERTS, logits, -jnp.inf)
        m1 = jnp.max(logits, axis=1, keepdims=True)
        i1 = jnp.min(jnp.where(logits == m1, lane, LANES), axis=1, keepdims=True)
        rest = jnp.where(lane == i1, -jnp.inf, logits)
        m2 = jnp.max(rest, axis=1, keepdims=True)
        i2 = jnp.min(jnp.where(rest == m2, lane, LANES), axis=1, keepdims=True)
        e2 = jnp.exp(m2 - m1)
        den = 1.0 + e2
        gate_sc[...] = jnp.where(lane == i1, 1.0 / den, 0.0) + jnp.where(lane == i2, e2 / den, 0.0)
        routed = jnp.where((lane == i1) | (lane == i2), 1.0, 0.0)
        key = jnp.where(routed > 0.0, _dot(tri_ref[...], routed.astype(BF16)), -1.0)
        key_sc[...] = key
        keyt_sc[...] = key.T

    sel = lane == e
    kcol = jnp.max(jnp.where(sel, key_sc[...], -1.0), axis=1, keepdims=True)
    wcol = jnp.sum(jnp.where(sel, gate_sc[...], 0.0), axis=1, keepdims=True)
    krow = keyt_sc[pl.ds(e, 1), :]
    count = jnp.sum(jnp.where(kcol >= 0.0, 1.0, 0.0)).astype(jnp.int32)
    n_chunks = (count + (MOE_CHUNK - 1)) // MOE_CHUNK
    row_id = lax.broadcasted_iota(jnp.int32, (MOE_CHUNK, tm), 0).astype(F32)
    col_id = lax.broadcasted_iota(jnp.int32, (tm, MOE_CHUNK), 1).astype(F32)

    def chunk(ci, carry):
        base = (ci * MOE_CHUNK).astype(F32)
        gather = jnp.where(krow - base == row_id, 1.0, 0.0).astype(BF16)
        xg = _dot(gather, h_sc[...]).astype(BF16)
        gte = _dot(xg, wg_ref[0])
        up = _dot(xg, wu_ref[0])
        act = (gte * _sigmoid(gte)) * up
        y = _dot(act.astype(BF16), wd_ref[0])
        scatter = jnp.where(kcol - base == col_id, 1.0, 0.0).astype(BF16)
        acc_sc[...] += wcol * _dot(scatter, y.astype(BF16))
        return carry

    lax.fori_loop(0, n_chunks, chunk, 0)

    @pl.when(e == pl.num_programs(1) - 1)
    def _():
        out = acc_sc[...]
        o_ref[...] = _final_norm(out, gf_ref) if final_norm else out


def _moe(x2, g, router_w, wg, wu, wd, g_final, final_norm, tm=MOE_TILE):
    T = x2.shape[0]
    rw = _pad_cols(router_w, LANES)
    rw_hi = rw.astype(BF16)
    rw_lo = (rw - rw_hi.astype(F32)).astype(BF16)
    rw2 = jnp.stack([rw_hi, rw_lo])
    tri = jnp.asarray(np.tril(np.ones((tm, tm), np.float32), -1)).astype(BF16)
    return pl.pallas_call(
        functools.partial(_moe_kernel, final_norm=final_norm, tm=tm),
        grid=(T // tm, N_EXPERTS),
        in_specs=[pl.BlockSpec((tm, D_MODEL), lambda i, e: (i, 0)), pl.BlockSpec((1, D_MODEL), lambda i, e: (0, 0)),
                  pl.BlockSpec((2, D_MODEL, LANES), lambda i, e: (0, 0, 0)),
                  pl.BlockSpec((tm, tm), lambda i, e: (0, 0)),
                  pl.BlockSpec((1, D_MODEL, D_FF_EXPERT), lambda i, e: (e, 0, 0)),
                  pl.BlockSpec((1, D_MODEL, D_FF_EXPERT), lambda i, e: (e, 0, 0)),
                  pl.BlockSpec((1, D_FF_EXPERT, D_MODEL), lambda i, e: (e, 0, 0)),
                  pl.BlockSpec((1, D_MODEL), lambda i, e: (0, 0))],
        out_specs=pl.BlockSpec((tm, D_MODEL), lambda i, e: (i, 0)),
        out_shape=jax.ShapeDtypeStruct((T, D_MODEL), F32),
        scratch_shapes=[pltpu.VMEM((tm, D_MODEL), BF16), pltpu.VMEM((tm, LANES), F32),
                        pltpu.VMEM((tm, LANES), F32), pltpu.VMEM((LANES, tm), F32),
                        pltpu.VMEM((tm, D_MODEL), F32)],
        compiler_params=_cparams(("parallel", "arbitrary")),
        name="moe_routed",
    )(x2, g.reshape(1, -1), rw2, tri, wg.astype(BF16), wu.astype(BF16), wd.astype(BF16), g_final.reshape(1, -1))


def _mixer(x2, batch, seq, tables, onehot, norm_g, w_in, conv_w, conv_b, wa, ba, wx, bx, lam,
           q_norm, w_uq, kv_norm, w_ukv, cmp_pos, cmp_w1, cmp_w2, gate_b, w_out):
    w_cat, w_t = _prep_w_in(w_in)
    lru_xy, mla_dn, qn, kc, vc, ksw, vt_sw, gates_t = _in_proj(x2, norm_g, w_cat, w_t, tables, gate_b, batch, seq)
    a_out = _lru(lru_xy, conv_w, conv_b, wa, ba, wx, bx, lam, batch, seq)

    q_m, k_m, vt_m = _mla_prep(mla_dn, q_norm, w_uq, kv_norm, w_ukv, tables, batch, seq)
    b3 = lambda t: t.reshape(batch, seq, t.shape[-1])
    heads = tuple(range(N_ATT_HEADS))
    b_out = _flash(b3(q_m), b3(k_m), vt_m, mode="causal", batch=batch, seq=seq, per_head_kv=True,
                   k_width=N_ATT_HEADS * LANES, k_col=0, vt_rows=N_ATT_HEADS * HEAD_V, vt_row=0, out_dtype=BF16,
                   tk=LONG_KEY_TILE)

    kcmp, vcmp_t = _compress(kc, vc, cmp_pos, cmp_w1, cmp_w2, batch, seq)
    qn3 = b3(qn)
    o_c, selbias = _select(qn3, kcmp, vcmp_t, gates_t, batch, seq)
    ksw3 = b3(ksw)
    nsa_kw = dict(batch=batch, seq=seq, per_head_kv=False, k_width=LANES,
                  vt_rows=NSA_KV_HEADS * HEAD_V, out_dtype=F32, gates_t=gates_t)
    o_s = _flash(qn3, ksw3, vt_sw, mode="select", k_col=0, vt_row=0, selbias=selbias,
                 onehot=onehot, gate_rows=tuple(3 * h + 1 for h in heads), tk=LONG_KEY_TILE, **nsa_kw)
    o_w = _flash(qn3, ksw3, vt_sw, mode="window", k_col=1, vt_row=1,
                 gate_rows=tuple(3 * h + 2 for h in heads), **nsa_kw)
    T = batch * seq
    flat = lambda t: t.reshape(T, t.shape[-1])
    return _out_proj(x2, a_out, flat(b_out), flat(o_c), flat(o_s), flat(o_w), w_out)


def kernel(x, norm_mix, w_in, lru_conv_w, lru_conv_b, lru_wa, lru_ba, lru_wx, lru_bx, lru_lambda, mla_q_norm, mla_w_uq, mla_kv_norm, mla_w_ukv, nsa_cmp_pos, nsa_cmp_w1, nsa_cmp_w2, nsa_gate_b, w_out, norm_ffn, ffn_w_gate, ffn_w_up, ffn_w_down, router_w, moe_w_gate, moe_w_up, moe_w_down, norm_final):
    batch, seq, _ = x.shape
    depth = norm_mix.shape[0]
    tables = _rope_tables(seq)
    blk_id = np.arange(seq)[:, None] // SEL_LEN
    onehot = jnp.asarray((blk_id == np.arange(LANES)[None, :]).astype(np.float32)).astype(BF16)
    x2 = x.reshape(batch * seq, D_MODEL)
    for l in range(depth):
        x2 = _mixer(x2, batch, seq, tables, onehot, norm_mix[l], w_in[l], lru_conv_w[l], lru_conv_b[l],
                    lru_wa[l], lru_ba[l], lru_wx[l], lru_bx[l], lru_lambda[l], mla_q_norm[l], mla_w_uq[l],
                    mla_kv_norm[l], mla_w_ukv[l], nsa_cmp_pos[l], nsa_cmp_w1[l], nsa_cmp_w2[l],
                    nsa_gate_b[l], w_out[l])
        last = l == depth - 1
        j = l // 2
        if l % 2 == 0:
            x2 = _ffn(x2, norm_ffn[l], ffn_w_gate[j], ffn_w_up[j], ffn_w_down[j], norm_final, last)
        else:
            x2 = _moe(x2, norm_ffn[l], router_w[j], moe_w_gate[j], moe_w_up[j], moe_w_down[j], norm_final, last)
    if depth == 0:
        raise ValueError("depth must be positive")
    return x2.reshape(batch, seq, D_MODEL)
```

```python
import functools
import math

import numpy as np
import jax
import jax.numpy as jnp
from jax import lax
from jax.experimental import pallas as pl
from jax.experimental.pallas import tpu as pltpu

F32 = jnp.float32
BF16 = jnp.bfloat16

D_MODEL = 1024
LRU_WIDTH = 256
LRU_BLOCKS = 4
LRU_BLOCK_W = LRU_WIDTH // LRU_BLOCKS
CONV_WIDTH = 4
LRU_C = 8.0
MLA_HEADS = 6
MLA_NOPE = 64
MLA_ROPE = 32
MLA_V = 64
MLA_Q_RANK = 192
MLA_KV_RANK = 128
NSA_HEADS = 6
NSA_KV_HEADS = 2
NSA_GROUP = NSA_HEADS // NSA_KV_HEADS
NSA_HEAD_DIM = 64
CMP_LEN = 32
CMP_STRIDE = 16
CMP_HIDDEN = 128
SEL_LEN = 64
SEL_TOPK = 16
WINDOW = 512
FORCE_BONUS = 1.0e3
D_FF = 2816
N_EXPERTS = 8
D_FF_EXPERT = 1408
ROPE_THETA = 10000.0
EPS = 1e-6
NEG_INF = -1.0e30

LANES = 128
VMEM_LIMIT = 56 * 1024 * 1024

C_LRU = 0
C_MLA = 512
C_KPE_ROT = 1024
C_QN = 1152
C_QN_ROT = 1536
C_KC = 1920
C_KC_ROT = 2048
C_VC = 2176
C_KS = 2304
C_KS_ROT = 2432
C_KW = 2560
C_KW_ROT = 2688
IN_COLS_PAD = 2816
N_ATT_HEADS = 6
HEAD_V = 64
LONG_KEY_TILE = 1024
SUM_ROWS = 16
GATE_ROWS = 32
LOG2E = 1.4426950408889634


def _cparams(sem, flags=None):
    return pltpu.CompilerParams(dimension_semantics=sem, vmem_limit_bytes=VMEM_LIMIT, flags=flags)


def _gelu_tanh(x):
    return 0.5 * x * (1.0 + jnp.tanh(math.sqrt(2.0 / math.pi) * (x + 0.044715 * (x * x * x))))


def _sigmoid(x):
    return 1.0 / (1.0 + jnp.exp(-x))


def _dot(a, b):
    return jnp.dot(a, b, preferred_element_type=F32)


def _dot_nt(a, b):
    return lax.dot_general(a, b, (((1,), (1,)), ((), ())), preferred_element_type=F32)


def _rot_cols(w, half):
    return jnp.concatenate([-w[:, half:], w[:, :half]], axis=1)


def _pad_cols(w, n):
    return jnp.pad(w, ((0, 0), (0, n - w.shape[1])))


def _prep_w_in(w_in):
    splits = np.cumsum([256, 256, 192, 128, 32, 384, 128, 128, 128, 128, 128, 128, 18])[:-1].tolist()
    (x_l, y_l, q_dn, kv_dn, k_pe, q_n, kc, vc, ks, vs, kw, vw, g_n) = jnp.split(w_in, splits, axis=1)
    z = lambda n: jnp.zeros((w_in.shape[0], n), w_in.dtype)

    def kpe_slot(w):
        return jnp.concatenate([z(64), w, z(32)], axis=1)

    def qn_slots(fn):
        out = []
        for p in range(NSA_GROUP):
            for h in (p, p + NSA_GROUP):
                out.append(fn(q_n[:, h * 64:(h + 1) * 64]))
        return jnp.concatenate(out, axis=1)

    def kv_rot(w):
        return jnp.concatenate([_rot_cols(w[:, :64], 32), _rot_cols(w[:, 64:], 32)], axis=1)

    cols = [x_l, y_l, _pad_cols(q_dn, 256), kv_dn, kpe_slot(k_pe), kpe_slot(_rot_cols(k_pe, 16)),
            qn_slots(lambda w: w), qn_slots(lambda w: _rot_cols(w, 32)),
            kc, kv_rot(kc), vc, ks, kv_rot(ks), kw, kv_rot(kw)]
    w = jnp.concatenate(cols, axis=1)
    assert w.shape[1] == IN_COLS_PAD
    w_t = jnp.concatenate([vs, vw, _pad_cols(g_n, GATE_ROWS)], axis=1).T
    return w.astype(BF16), w_t.astype(BF16)


def _rope_tables(seq):
    pos = jnp.arange(seq, dtype=F32)[:, None]
    inv64 = ROPE_THETA ** (-jnp.arange(32, dtype=F32) * 2.0 / 64)
    inv32 = ROPE_THETA ** (-jnp.arange(16, dtype=F32) * 2.0 / 32)
    a64 = pos * inv64[None, :]
    a32 = pos * inv32[None, :]
    c64, s64 = jnp.cos(a64), jnp.sin(a64)
    c32, s32 = jnp.cos(a32), jnp.sin(a32)
    cos_n = jnp.concatenate([c64, c64, c64, c64], axis=1)
    sin_n = jnp.concatenate([s64, s64, s64, s64], axis=1)
    one = jnp.ones((seq, 64), F32)
    cos_m = jnp.concatenate([one, c32, c32, one[:, :32]], axis=1)
    sin_m = jnp.concatenate([0 * one, s32, s32, 0 * one[:, :32]], axis=1)
    return cos_n, sin_n, cos_m, sin_m


def _in_proj_kernel(x_ref, g_ref, w_ref, wt_ref, cn_ref, sn_ref, cm_ref, sm_ref, gb_ref,
                    lru_ref, mla_ref, qn_ref, kc_ref, vc_ref, ksw_ref, vt_ref, gt_ref):
    x = x_ref[...]
    ms = jnp.mean(x * x, axis=-1, keepdims=True)
    h = ((x * lax.rsqrt(ms + EPS)) * g_ref[...]).astype(BF16)

    def proj(c0, n):
        return _dot(h, w_ref[:, c0:c0 + n])

    cn, sn = cn_ref[...], sn_ref[...]
    lru_ref[...] = proj(C_LRU, 512)
    mla_ref[:, 0:384] = proj(C_MLA, 384)
    mla_ref[:, 384:512] = proj(C_MLA + 384, 128) * cm_ref[...] + proj(C_KPE_ROT, 128) * sm_ref[...]
    scale = NSA_HEAD_DIM ** -0.5 * LOG2E
    low_half = lax.broadcasted_iota(jnp.int32, cn.shape, 1) < NSA_HEAD_DIM
    for p in range(NSA_GROUP):
        a = proj(C_QN + p * LANES, LANES)
        b = proj(C_QN_ROT + p * LANES, LANES)
        r = (a * cn + b * sn) * scale
        hi_hd = p + NSA_GROUP
        qn_ref[:, p * LANES:(p + 1) * LANES] = jnp.where(low_half, r, 0.0).astype(BF16)
        qn_ref[:, hi_hd * LANES:(hi_hd + 1) * LANES] = jnp.where(low_half, 0.0, r).astype(BF16)
    kc_ref[...] = proj(C_KC, 128) * cn + proj(C_KC_ROT, 128) * sn
    vc_ref[...] = proj(C_VC, 128)
    ksw_ref[:, 0:128] = (proj(C_KS, 128) * cn + proj(C_KS_ROT, 128) * sn).astype(BF16)
    ksw_ref[:, 128:256] = (proj(C_KW, 128) * cn + proj(C_KW_ROT, 128) * sn).astype(BF16)
    vt_ref[0] = _dot_nt(wt_ref[0:256, :], h).astype(BF16)
    gt_ref[0] = _sigmoid(_dot_nt(wt_ref[256:256 + GATE_ROWS, :], h) + gb_ref[...])


def _in_proj(x2, g, w_cat, w_t, tables, gate_b, batch, seq, tm=512):
    T = x2.shape[0]
    nt = T // tm
    npos = seq // tm
    cn, sn, cm, sm = tables
    row = lambda i: (i, 0)
    fixed = lambda i: (0, 0)
    posmap = lambda i: (i % npos, 0)
    tab_spec = pl.BlockSpec((tm, LANES), posmap)
    out_shapes = (
        jax.ShapeDtypeStruct((T, 512), F32),
        jax.ShapeDtypeStruct((T, 512), F32),
        jax.ShapeDtypeStruct((T, 768), BF16),
        jax.ShapeDtypeStruct((T, 128), F32),
        jax.ShapeDtypeStruct((T, 128), F32),
        jax.ShapeDtypeStruct((T, 256), BF16),
    )
    t_shapes = (jax.ShapeDtypeStruct((batch, 256, seq), BF16),
                jax.ShapeDtypeStruct((batch, GATE_ROWS, seq), F32))
    tmap = lambda i: (i // npos, 0, i % npos)
    gate_b_col = jnp.pad(gate_b, (0, GATE_ROWS - gate_b.shape[0])).reshape(GATE_ROWS, 1)
    return pl.pallas_call(
        _in_proj_kernel,
        grid=(nt,),
        in_specs=[pl.BlockSpec((tm, D_MODEL), row), pl.BlockSpec((1, D_MODEL), fixed),
                  pl.BlockSpec((D_MODEL, IN_COLS_PAD), fixed), pl.BlockSpec((256 + GATE_ROWS, D_MODEL), fixed),
                  tab_spec, tab_spec, tab_spec, tab_spec, pl.BlockSpec((GATE_ROWS, 1), fixed)],
        out_specs=[pl.BlockSpec((tm, s.shape[1]), row) for s in out_shapes]
        + [pl.BlockSpec((1, s.shape[1], tm), tmap) for s in t_shapes],
        out_shape=out_shapes + t_shapes,
        compiler_params=_cparams(("parallel",)),
        name="in_proj",
    )(x2, g.reshape(1, -1), w_cat, w_t, cn, sn, cm, sm, gate_b_col)


def _lru_kernel(xy_ref, cw_ref, cb_ref, wa_ref, ba_ref, wx_ref, bx_ref, lam_ref, out_ref,
                xe_sc, h_sc, *, tt):
    t = pl.program_id(1)

    @pl.when(t == 0)
    def _():
        xe_sc[0:8, :] = jnp.zeros((8, LRU_WIDTH), F32)
        h_sc[...] = jnp.zeros_like(h_sc)

    x = xy_ref[0, :, 0:LRU_WIDTH]
    y = xy_ref[0, :, LRU_WIDTH:2 * LRU_WIDTH]
    xe_sc[8:8 + tt, :] = x
    xc = cb_ref[...]
    for k in range(CONV_WIDTH):
        off = 8 - (CONV_WIDTH - 1) + k
        xc = xc + xe_sc[off:off + tt, :] * cw_ref[k:k + 1, :]
    xe_sc[0:8, :] = x[tt - 8:tt, :]

    xb = xc.astype(BF16)
    r = _sigmoid(_dot(xb, wa_ref[...]) + ba_ref[...])
    i = _sigmoid(_dot(xb, wx_ref[...]) + bx_ref[...])
    log_a = (-LRU_C * r) * jax.nn.softplus(-lam_ref[...])
    a = jnp.exp(log_a)
    b = jnp.sqrt(jnp.tanh(-log_a) * (a * a + 1.0)) * (i * xc)

    row = lax.broadcasted_iota(jnp.int32, (tt, LRU_WIDTH), 0)
    k = 1
    while k < tt:
        keep = row >= k
        a_sh = jnp.where(keep, pltpu.roll(a, k, 0), 1.0)
        b_sh = jnp.where(keep, pltpu.roll(b, k, 0), 0.0)
        b = a * b_sh + b
        a = a * a_sh
        k *= 2
    h = b + a * h_sc[0:1, :]
    h_sc[0:1, :] = h[tt - 1:tt, :]
    out_ref[0] = h * _gelu_tanh(y)


def _block_diag_dense(w):
    n, c, d = w.shape
    out = jnp.zeros((n * c, n * d), w.dtype)
    for j in range(n):
        out = out.at[j * c:(j + 1) * c, j * d:(j + 1) * d].set(w[j])
    return out


def _lru(lru_xy, conv_w, conv_b, wa, ba, wx, bx, lam, batch, seq, tt=512):
    xy = lru_xy.reshape(batch, seq, 512)
    fixed = lambda b, t: (0, 0)
    vec = pl.BlockSpec((1, LRU_WIDTH), fixed)
    mat = pl.BlockSpec((LRU_WIDTH, LRU_WIDTH), fixed)
    out = pl.pallas_call(
        functools.partial(_lru_kernel, tt=tt),
        grid=(batch, seq // tt),
        in_specs=[pl.BlockSpec((1, tt, 512), lambda b, t: (b, t, 0)),
                  pl.BlockSpec((CONV_WIDTH, LRU_WIDTH), fixed), vec, mat, vec, mat, vec, vec],
        out_specs=pl.BlockSpec((1, tt, LRU_WIDTH), lambda b, t: (b, t, 0)),
        out_shape=jax.ShapeDtypeStruct((batch, seq, LRU_WIDTH), F32),
        scratch_shapes=[pltpu.VMEM((tt + 8, LRU_WIDTH), F32), pltpu.VMEM((8, LRU_WIDTH), F32)],
        compiler_params=_cparams(("parallel", "arbitrary")),
        name="rg_lru",
    )(xy, conv_w, conv_b.reshape(1, -1), _block_diag_dense(wa).astype(BF16), ba.reshape(1, -1),
      _block_diag_dense(wx).astype(BF16), bx.reshape(1, -1), lam.reshape(1, -1))
    return out.reshape(batch * seq, LRU_WIDTH)


def _mla_prep_kernel(dn_ref, qn_ref, wqa_ref, wqb_ref, kvn_ref, wk_ref, wvt_ref, cm_ref, sm_ref,
                     q_ref, k_ref, vt_ref):
    q_dn = dn_ref[:, 0:256]
    ms = jnp.sum(q_dn * q_dn, axis=-1, keepdims=True) * (1.0 / MLA_Q_RANK)
    ql = ((q_dn * lax.rsqrt(ms + EPS)) * qn_ref[...]).astype(BF16)
    kv_dn = dn_ref[:, 256:384]
    ms = jnp.mean(kv_dn * kv_dn, axis=-1, keepdims=True)
    kvl = ((kv_dn * lax.rsqrt(ms + EPS)) * kvn_ref[...]).astype(BF16)
    kpe = dn_ref[:, 384:512]
    cm, sm = cm_ref[...], sm_ref[...]
    scale = (MLA_NOPE + MLA_ROPE) ** -0.5 * LOG2E
    for hd in range(MLA_HEADS):
        sl = slice(hd * LANES, (hd + 1) * LANES)
        q_ref[:, sl] = ((_dot(ql, wqa_ref[:, sl]) * cm + _dot(ql, wqb_ref[:, sl]) * sm) * scale).astype(BF16)
        k_ref[:, sl] = (_dot(kvl, wk_ref[:, sl]) + kpe).astype(BF16)
    vt_ref[0] = _dot_nt(wvt_ref[...], kvl).astype(BF16)


def _prep_mla_weights(w_uq, w_ukv):
    zq = jnp.zeros((MLA_Q_RANK, 32), w_uq.dtype)
    qa, qb, kk, vv = [], [], [], []
    for h in range(MLA_HEADS):
        wq = w_uq[:, h * 96:(h + 1) * 96]
        nope, ropew = wq[:, :64], wq[:, 64:]
        qa.append(jnp.concatenate([nope, ropew, zq], axis=1))
        qb.append(jnp.concatenate([0 * nope, _rot_cols(ropew, 16), zq], axis=1))
        wkv = w_ukv[:, h * 128:(h + 1) * 128]
        kk.append(_pad_cols(wkv[:, :64], LANES))
        vv.append(wkv[:, 64:])
    pad_rows = lambda w: jnp.pad(w, ((0, 256 - MLA_Q_RANK), (0, 0)))
    return (pad_rows(jnp.concatenate(qa, axis=1)).astype(BF16), pad_rows(jnp.concatenate(qb, axis=1)).astype(BF16),
            jnp.concatenate(kk, axis=1).astype(BF16), jnp.concatenate(vv, axis=1).T.astype(BF16))


def _mla_prep(mla_dn, q_norm, w_uq, kv_norm, w_ukv, tables, batch, seq, tm=512):
    T = mla_dn.shape[0]
    npos = seq // tm
    _, _, cm, sm = tables
    wqa, wqb, wk, wvt = _prep_mla_weights(w_uq, w_ukv)
    row = lambda i: (i, 0)
    fixed = lambda i: (0, 0)
    tab = pl.BlockSpec((tm, LANES), lambda i: (i % npos, 0))
    return pl.pallas_call(
        _mla_prep_kernel,
        grid=(T // tm,),
        in_specs=[pl.BlockSpec((tm, 512), row), pl.BlockSpec((1, 256), fixed),
                  pl.BlockSpec((256, 768), fixed), pl.BlockSpec((256, 768), fixed),
                  pl.BlockSpec((1, 128), fixed), pl.BlockSpec((128, 768), fixed),
                  pl.BlockSpec((384, 128), fixed), tab, tab],
        out_specs=[pl.BlockSpec((tm, 768), row), pl.BlockSpec((tm, 768), row),
                   pl.BlockSpec((1, 384, tm), lambda i: (i // npos, 0, i % npos))],
        out_shape=(jax.ShapeDtypeStruct((T, 768), BF16), jax.ShapeDtypeStruct((T, 768), BF16),
                   jax.ShapeDtypeStruct((batch, 384, seq), BF16)),
        compiler_params=_cparams(("parallel",)),
        name="mla_prep",
    )(mla_dn, _pad_cols(q_norm.reshape(1, -1), 256), wqa, wqb, kv_norm.reshape(1, -1), wk, wvt, cm, sm)


PROBE_KEYS = 128
LAG_LIMIT = 8.0


def _flash_kernel(qt_ref, kt_ref, first_ref, last_ref, par_ref, *refs, mode, k_per_head, v_per_head, gate_rows,
                  tq, tk):
    refs = list(refs)
    q_ref, k_ref, vt_ref = refs[:3]
    o_ref, m_st, acc_st = refs[-8:-5]
    s_sc, mx_sc, flag_sm = refs[-5:-3], refs[-3:-1], refs[-1]
    extra = refs[3:-8]
    sb_ref, oh_ref = (extra[0], extra[1]) if mode == "select" else (None, None)
    gt_ref = extra[-1] if gate_rows is not None else None
    step = pl.program_id(1)
    qt = qt_ref[step]
    kt = kt_ref[step]
    rd = par_ref[step]
    wr = 1 - rd

    def visible(nk):
        kpos = kt * tk + lax.broadcasted_iota(jnp.int32, (nk, tq), 0)
        qpos = qt * tq + lax.broadcasted_iota(jnp.int32, (nk, tq), 1)
        dist = qpos - kpos
        return (dist >= 0) & (dist < WINDOW) if mode == "window" else dist >= 0

    def score_tile(j, nk, valid):
        q = q_ref[0, :, j * LANES:(j + 1) * LANES]
        k = k_ref[0, 0:nk, j * LANES:(j + 1) * LANES] if k_per_head else k_ref[0, 0:nk, :]
        if mode == "select":
            q = jnp.concatenate([q, sb_ref[0, j // NSA_GROUP]], axis=1)
            k = jnp.concatenate([k, oh_ref[0:nk, :]], axis=1)
        s = _dot_nt(k, q)
        return s if valid is None else jnp.where(valid, s, NEG_INF)

    def weighted_values(j, nk, p):
        row = (j if v_per_head else j // NSA_GROUP) * HEAD_V
        vt = vt_ref[0, row:row + HEAD_V, 0:nk]
        return _dot(jnp.concatenate([vt, jnp.ones((SUM_ROWS, nk), BF16)], axis=0), p)

    def exact(nk, init):
        valid = visible(nk)

        def scores(j, slot):
            s = score_tile(j, nk, valid)
            s_sc[slot][0:nk, :] = s
            mx_sc[slot][...] = jnp.max(s, axis=0, keepdims=True)

        def accumulate(j, slot):
            if init:
                m_new = mx_sc[slot][...]
            else:
                m_prev = m_st[rd, j]
                m_new = jnp.maximum(m_prev, mx_sc[slot][...])
            p = jnp.exp2((s_sc[slot][0:nk, :] - m_new).astype(BF16))
            upd = weighted_values(j, nk, p)
            acc_st[wr, j] = upd if init else jnp.exp2(m_prev - m_new) * acc_st[rd, j] + upd
            m_st[wr, j] = m_new

        scores(0, 0)
        for j in range(N_ATT_HEADS):
            if j + 1 < N_ATT_HEADS:
                scores(j + 1, (j + 1) % 2)
            accumulate(j, j % 2)

    def lagged(nk, masked, init):
        valid = visible(nk) if masked else None
        rise = None
        for j in range(N_ATT_HEADS):
            if init:
                probe = score_tile(j, PROBE_KEYS, None if valid is None else valid[0:PROBE_KEYS])
                m_prev = jnp.max(probe, axis=0, keepdims=True)
            else:
                m_prev = m_st[rd, j]
            s = score_tile(j, nk, valid)
            p = jnp.exp2((s - m_prev).astype(BF16))
            tile_max = jnp.max(s, axis=0, keepdims=True)
            m_new = jnp.maximum(m_prev, tile_max)
            upd = weighted_values(j, nk, p)
            acc_st[wr, j] = (upd if init else acc_st[rd, j] + upd) * jnp.exp2(m_prev - m_new)
            m_st[wr, j] = m_new
            d = tile_max - m_prev
            rise = d if rise is None else jnp.maximum(rise, d)
        flag_sm[0] = (jnp.max(rise) > LAG_LIMIT).astype(jnp.int32)

    first = first_ref[step] == 1
    later = jnp.logical_not(first)
    if mode == "window":
        pl.when(first)(lambda: exact(tk, True))
        pl.when(later)(lambda: exact(tk, False))
    else:
        first_q, last_q = qt * tq, qt * tq + (tq - 1)
        crosses = kt * tk + (tk - 1) > first_q
        clear = jnp.logical_not(crosses)
        flag_sm[0] = 0
        for init, when in ((True, first), (False, later)):
            pl.when(when & clear)(functools.partial(lagged, tk, False, init))
            if tk > tq:
                half_only = kt * tk + tk // 2 > last_q
                pl.when(when & crosses & half_only)(functools.partial(lagged, tk // 2, True, init))
                pl.when(when & crosses & jnp.logical_not(half_only))(functools.partial(lagged, tk, True, init))
            else:
                pl.when(when & crosses)(functools.partial(lagged, tk, True, init))
        redo = flag_sm[0] == 1
        pl.when(redo & first)(lambda: exact(tk, True))
        pl.when(redo & later)(lambda: exact(tk, False))

    @pl.when(last_ref[step] == 1)
    def _():
        outs = []
        for j in range(N_ATT_HEADS):
            o = acc_st[wr, j, 0:HEAD_V, :] / acc_st[wr, j, HEAD_V:HEAD_V + 1, :]
            if gate_rows is not None:
                o = o * gt_ref[0, gate_rows[j]:gate_rows[j] + 1, :]
            outs.append(o)
        o_ref[0] = jnp.concatenate(outs, axis=0).T.astype(o_ref.dtype)


def _pair_tables(nq, mode, tq, tk):
    qts, kts, first, last, par = [], [], [], [], []
    for qt in range(nq):
        hi = (qt * tq + tq - 1) // tk
        lo = 0 if mode != "window" else max(0, (qt * tq - (WINDOW - 1)) // tk)
        for kt in range(lo, hi + 1):
            qts.append(qt)
            kts.append(kt)
            first.append(1 if kt == lo else 0)
            last.append(1 if kt == hi else 0)
            par.append((kt - lo) % 2)
    arr = lambda v: jnp.asarray(np.array(v, dtype=np.int32))
    return arr(qts), arr(kts), arr(first), arr(last), arr(par)


def _flash(q, k, vt, *, mode, batch, seq, per_head_kv, k_width, k_col, vt_rows, vt_row, out_dtype,
           selbias=None, onehot=None, gate_rows=None, gates_t=None, tq=512, tk=512):
    tk = min(tk, seq)
    nq = seq // tq
    tables = _pair_tables(nq, mode, tq, tk)
    n_steps = int(tables[0].shape[0])
    qw = N_ATT_HEADS * LANES
    ow = N_ATT_HEADS * HEAD_V
    in_specs = [
        pl.BlockSpec((1, tq, qw), lambda b, s, qt, kt, f, l, p: (b, qt[s], 0)),
        pl.BlockSpec((1, tk, k_width), lambda b, s, qt, kt, f, l, p: (b, kt[s], k_col)),
        pl.BlockSpec((1, vt_rows, tk), lambda b, s, qt, kt, f, l, p: (b, vt_row, kt[s])),
    ]
    args = [q, k, vt]
    if mode == "select":
        in_specs.append(pl.BlockSpec((1, NSA_KV_HEADS, tq, LANES), lambda b, s, qt, kt, f, l, p: (b, 0, qt[s], 0)))
        in_specs.append(pl.BlockSpec((tk, LANES), lambda b, s, qt, kt, f, l, p: (kt[s], 0)))
        args += [selbias, onehot]
    if gate_rows is not None:
        in_specs.append(pl.BlockSpec((1, GATE_ROWS, tq), lambda b, s, qt, kt, f, l, p: (b, 0, qt[s])))
        args.append(gates_t)
    kern = functools.partial(_flash_kernel, mode=mode, k_per_head=per_head_kv, v_per_head=per_head_kv,
                             gate_rows=gate_rows, tq=tq, tk=tk)
    return pl.pallas_call(
        kern,
        grid_spec=pltpu.PrefetchScalarGridSpec(
            num_scalar_prefetch=5,
            grid=(batch, n_steps),
            in_specs=in_specs,
            out_specs=pl.BlockSpec((1, tq, ow), lambda b, s, qt, kt, f, l, p: (b, qt[s], 0)),
            scratch_shapes=[pltpu.VMEM((2, N_ATT_HEADS, 1, tq), F32),
                            pltpu.VMEM((2, N_ATT_HEADS, HEAD_V + SUM_ROWS, tq), F32),
                            pltpu.VMEM((tk, tq), F32), pltpu.VMEM((tk, tq), F32),
                            pltpu.VMEM((1, tq), F32), pltpu.VMEM((1, tq), F32),
                            pltpu.SMEM((1,), jnp.int32)],
        ),
        out_shape=jax.ShapeDtypeStruct((batch, seq, ow), out_dtype),
        compiler_params=_cparams(("parallel", "arbitrary")),
        name="flash_" + mode,
    )(*tables, *args)


def _compress_kernel(kc_ref, vc_ref, pos_ref, w1a_ref, w1b_ref, w2_ref, ko_ref, vo_ref, *, nchunk):
    row = lax.broadcasted_iota(jnp.int32, (nchunk, LANES), 0)
    for br, src in enumerate((kc_ref, vc_ref)):
        r = src[0]
        pa = _dot((r + pos_ref[br, 0:1, :]).astype(BF16), w1a_ref[br])
        pb = _dot((r + pos_ref[br, 1:2, :]).astype(BF16), w1b_ref[br])
        hid = _gelu_tanh(pa + pltpu.roll(pb, nchunk - 1, 0))
        out = jnp.where(row < nchunk - 1, _dot(hid.astype(BF16), w2_ref[br]), 0.0)
        if br == 0:
            ko_ref[0] = out.astype(ko_ref.dtype)
        else:
            vo_ref[0] = out.T.astype(vo_ref.dtype)


def _prep_compress_weights(cmp_pos, cmp_w1, cmp_w2):
    half = CMP_LEN // 2
    pos, w1a, w1b, w2 = [], [], [], []
    for br in range(2):
        p = cmp_pos[br]
        tile = lambda ph: jnp.concatenate([ph, ph], axis=1).reshape(1, half * LANES)
        pos.append(jnp.concatenate([tile(p[:half]), tile(p[half:])], axis=0))
        w = cmp_w1[br].reshape(CMP_LEN, NSA_HEAD_DIM, CMP_HIDDEN)
        z = jnp.zeros_like(w[:half])

        def big(wh):
            g0 = jnp.concatenate([wh, z], axis=1)
            g1 = jnp.concatenate([z, wh], axis=1)
            return jnp.concatenate([g0.reshape(half * LANES, CMP_HIDDEN), g1.reshape(half * LANES, CMP_HIDDEN)], axis=1)

        w1a.append(big(w[:half]))
        w1b.append(big(w[half:]))
        w2.append(_block_diag_dense(jnp.stack([cmp_w2[br], cmp_w2[br]])))
    return (jnp.stack(pos), jnp.stack(w1a).astype(BF16), jnp.stack(w1b).astype(BF16), jnp.stack(w2).astype(BF16))


def _compress(kc, vc, cmp_pos, cmp_w1, cmp_w2, batch, seq):
    nchunk = seq // CMP_STRIDE
    width = CMP_STRIDE * LANES
    pos, w1a, w1b, w2 = _prep_compress_weights(cmp_pos, cmp_w1, cmp_w2)
    blk = pl.BlockSpec((1, nchunk, width), lambda b: (b, 0, 0))
    full3 = lambda shape: pl.BlockSpec(shape, lambda b: (0, 0, 0))
    return pl.pallas_call(
        functools.partial(_compress_kernel, nchunk=nchunk),
        grid=(batch,),
        in_specs=[blk, blk, full3(pos.shape), full3(w1a.shape), full3(w1b.shape), full3(w2.shape)],
        out_specs=[pl.BlockSpec((1, nchunk, LANES), lambda b: (b, 0, 0)),
                   pl.BlockSpec((1, LANES, nchunk), lambda b: (b, 0, 0))],
        out_shape=(jax.ShapeDtypeStruct((batch, nchunk, LANES), BF16),
                   jax.ShapeDtypeStruct((batch, LANES, nchunk), BF16)),
        compiler_params=_cparams(("parallel",)),
        name="nsa_compress",
    )(kc.reshape(batch, nchunk, width), vc.reshape(batch, nchunk, width), pos, w1a, w1b, w2)


def _select_kernel(q_ref, kc_ref, vct_ref, ovt_ref, gt_ref, oc_ref, sb_ref, ot_sc, imp_sc, *, tq, nchunk, variants):
    nsel = LANES
    qt = pl.program_id(1)
    col_ok = (qt * tq + lax.broadcasted_iota(jnp.int32, (1, tq), 1) >= CMP_LEN - 1).astype(F32)

    def attend(nc):
        kc = kc_ref[0, 0:nc, :]
        qpos = qt * tq + lax.broadcasted_iota(jnp.int32, (nc, tq), 1)
        cend = lax.broadcasted_iota(jnp.int32, (nc, tq), 0) * CMP_STRIDE + (CMP_LEN - 1)
        cvalid = cend <= qpos
        ovt = ovt_ref[:, 0:nc]
        for g in range(NSA_KV_HEADS):
            vct = vct_ref[0, g * HEAD_V:(g + 1) * HEAD_V, 0:nc]
            psum = jnp.zeros((nc, tq), F32)
            for pj in range(NSA_GROUP):
                hd = g * NSA_GROUP + pj
                q = q_ref[0, :, hd * LANES:(hd + 1) * LANES]
                s = jnp.where(cvalid, _dot_nt(kc, q), NEG_INF)
                m = jnp.max(s, axis=0, keepdims=True)
                e = jnp.exp2(s - m)
                p = e * (col_ok / jnp.maximum(jnp.sum(e, axis=0, keepdims=True), 1e-30))
                ot_sc[hd * HEAD_V:(hd + 1) * HEAD_V, :] = _dot(vct, p.astype(BF16)) * gt_ref[0, 3 * hd:3 * hd + 1, :]
                psum = psum + p
            hi = psum.astype(BF16)
            r1 = psum - hi.astype(F32)
            mid = r1.astype(BF16)
            lo = (r1 - mid.astype(F32)).astype(BF16)
            imp_sc[g] = _dot(ovt, hi) + _dot(ovt, mid) + _dot(ovt, lo)

    needed = (qt * tq + tq - CMP_STRIDE) // CMP_STRIDE
    lo_bound = 0
    for nc in variants:
        pl.when((needed > lo_bound) & (needed <= nc) if nc != variants[-1] else needed > lo_bound)(
            functools.partial(attend, nc))
        lo_bound = nc

    blk = lax.broadcasted_iota(jnp.int32, (nsel, tq), 0)
    blkf = blk.astype(F32)
    qp = qt * tq + lax.broadcasted_iota(jnp.int32, (nsel, tq), 1)
    valid = blk * SEL_LEN <= qp
    forced = (blk == 0) | (blk == qp // SEL_LEN)
    for g in range(NSA_KV_HEADS):
        score = jnp.where(valid, imp_sc[g], -1.0)
        score0 = jnp.where(forced, score + FORCE_BONUS, score)
        score = score0
        for _ in range(min(SEL_TOPK, nsel)):
            m = jnp.max(score, axis=0, keepdims=True)
            idx = jnp.min(jnp.where(score == m, blkf, float(nsel)), axis=0, keepdims=True)
            score = jnp.where(blkf == idx, -jnp.inf, score)
        bias = jnp.where(score0 >= 0.0, jnp.where(score == -jnp.inf, 0.0, NEG_INF), NEG_INF)
        sb_ref[0, g] = bias.T.astype(sb_ref.dtype)
    oc_ref[0] = ot_sc[...].T


def _overlap_matrix_t(nchunk, nsel):
    n = np.arange(nchunk)[None, :]
    s = np.arange(nsel)[:, None]
    cs = n * CMP_STRIDE
    ov = (cs < s * SEL_LEN + SEL_LEN) & (cs + CMP_LEN - 1 >= s * SEL_LEN) & (n < nchunk - 1)
    return jnp.asarray(ov.astype(np.float32)).astype(BF16)


def _select(qn, kcmp, vcmp_t, gates_t, batch, seq, tq=256):
    nchunk = seq // CMP_STRIDE
    nsel = LANES
    assert seq // SEL_LEN <= nsel
    qw = N_ATT_HEADS * LANES
    ow = N_ATT_HEADS * HEAD_V
    ovt = _overlap_matrix_t(nchunk, nsel)
    quarter = nchunk // 4
    variants = tuple(quarter * i for i in range(1, 5)) if quarter % LANES == 0 else (nchunk,)
    return pl.pallas_call(
        functools.partial(_select_kernel, tq=tq, nchunk=nchunk, variants=variants),
        grid=(batch, seq // tq),
        in_specs=[pl.BlockSpec((1, tq, qw), lambda b, t: (b, t, 0)),
                  pl.BlockSpec((1, nchunk, LANES), lambda b, t: (b, 0, 0)),
                  pl.BlockSpec((1, NSA_KV_HEADS * HEAD_V, nchunk), lambda b, t: (b, 0, 0)),
                  pl.BlockSpec((nsel, nchunk), lambda b, t: (0, 0)),
                  pl.BlockSpec((1, GATE_ROWS, tq), lambda b, t: (b, 0, t))],
        out_specs=[pl.BlockSpec((1, tq, ow), lambda b, t: (b, t, 0)),
                   pl.BlockSpec((1, NSA_KV_HEADS, tq, nsel), lambda b, t: (b, 0, t, 0))],
        out_shape=(jax.ShapeDtypeStruct((batch, seq, ow), F32),
                   jax.ShapeDtypeStruct((batch, NSA_KV_HEADS, seq, nsel), BF16)),
        scratch_shapes=[pltpu.VMEM((ow, tq), F32), pltpu.VMEM((NSA_KV_HEADS, nsel, tq), F32)],
        compiler_params=_cparams(("parallel", "parallel")),
        name="nsa_select",
    )(qn, kcmp, vcmp_t, ovt, gates_t)


def _out_proj_kernel(x_ref, a_ref, b_ref, oc_ref, os_ref, ow_ref, wa_ref, wb_ref, wc_ref, o_ref):
    acc = x_ref[...] + _dot(a_ref[...].astype(BF16), wa_ref[...])
    acc = acc + _dot(b_ref[...], wb_ref[...])
    c = (oc_ref[...] + os_ref[...]) + ow_ref[...]
    o_ref[...] = acc + _dot(c.astype(BF16), wc_ref[...])


def _out_proj(x2, a_out, b_out, o_c, o_s, o_w, w_out, tm=512):
    T = x2.shape[0]
    wa = w_out[0:256].astype(BF16)
    wb = w_out[256:640].astype(BF16)
    wc = w_out[640:1024].astype(BF16)
    row = lambda i: (i, 0)
    fixed = lambda i: (0, 0)
    act = lambda n: pl.BlockSpec((tm, n), row)
    return pl.pallas_call(
        _out_proj_kernel,
        grid=(T // tm,),
        in_specs=[act(D_MODEL), act(256), act(384), act(384), act(384), act(384),
                  pl.BlockSpec((256, D_MODEL), fixed), pl.BlockSpec((384, D_MODEL), fixed),
                  pl.BlockSpec((384, D_MODEL), fixed)],
        out_specs=pl.BlockSpec((tm, D_MODEL), row),
        out_shape=jax.ShapeDtypeStruct((T, D_MODEL), F32),
        compiler_params=_cparams(("parallel",)),
        name="out_proj",
    )(x2, a_out, b_out, o_c, o_s, o_w, wa, wb, wc)


def _final_norm(y, gf_ref):
    ms = jnp.mean(y * y, axis=-1, keepdims=True)
    return (y * lax.rsqrt(ms + EPS)) * gf_ref[...]


def _ffn_kernel(x_ref, g_ref, wg_ref, wu_ref, wd_ref, gf_ref, o_ref, h_sc, acc_sc, *, final_norm):
    f = pl.program_id(1)

    @pl.when(f == 0)
    def _():
        x = x_ref[...]
        ms = jnp.mean(x * x, axis=-1, keepdims=True)
        h_sc[...] = ((x * lax.rsqrt(ms + EPS)) * g_ref[...]).astype(BF16)
        acc_sc[...] = x

    h = h_sc[...]
    gte = _dot(h, wg_ref[...])
    up = _dot(h, wu_ref[...])
    act = (gte * _sigmoid(gte)) * up
    acc_sc[...] += _dot(act.astype(BF16), wd_ref[...])

    @pl.when(f == pl.num_programs(1) - 1)
    def _():
        y = acc_sc[...]
        o_ref[...] = _final_norm(y, gf_ref) if final_norm else y


def _ffn(x2, g, wg, wu, wd, g_final, final_norm, tm=512, tf=1408):
    T = x2.shape[0]
    nf = D_FF // tf
    return pl.pallas_call(
        functools.partial(_ffn_kernel, final_norm=final_norm),
        grid=(T // tm, nf),
        in_specs=[pl.BlockSpec((tm, D_MODEL), lambda i, f: (i, 0)), pl.BlockSpec((1, D_MODEL), lambda i, f: (0, 0)),
                  pl.BlockSpec((D_MODEL, tf), lambda i, f: (0, f)), pl.BlockSpec((D_MODEL, tf), lambda i, f: (0, f)),
                  pl.BlockSpec((tf, D_MODEL), lambda i, f: (f, 0)), pl.BlockSpec((1, D_MODEL), lambda i, f: (0, 0))],
        out_specs=pl.BlockSpec((tm, D_MODEL), lambda i, f: (i, 0)),
        out_shape=jax.ShapeDtypeStruct((T, D_MODEL), F32),
        scratch_shapes=[pltpu.VMEM((tm, D_MODEL), BF16), pltpu.VMEM((tm, D_MODEL), F32)],
        compiler_params=_cparams(("parallel", "arbitrary")),
        name="ffn_dense",
    )(x2, g.reshape(1, -1), wg.astype(BF16), wu.astype(BF16), wd.astype(BF16), g_final.reshape(1, -1))


MOE_TILE = 1024
MOE_CHUNK = 384


def _moe_kernel(x_ref, g_ref, rw_ref, tri_ref, wg_ref, wu_ref, wd_ref, gf_ref, o_ref,
                h_sc, gate_sc, key_sc, keyt_sc, acc_sc, *, final_norm, tm):
    e = pl.program_id(1)
    lane = lax.broadcasted_iota(jnp.int32, (tm, LANES), 1)

    @pl.when(e == 0)
    def _():
        x = x_ref[...]
        ms = jnp.mean(x * x, axis=-1, keepdims=True)
        hf = (x * lax.rsqrt(ms + EPS)) * g_ref[...]
        h_sc[...] = hf.astype(BF16)
        acc_sc[...] = x
        h_hi = hf.astype(BF16)
        h_lo = (hf - h_hi.astype(F32)).astype(BF16)
        logits = (_dot(h_hi, rw_ref[0]) + _dot(h_lo, rw_ref[0]) + _dot(h_hi, rw_ref[1]))
        logits = jnp.where(lane < N_EXPERTS, logits, -jnp.inf)
        m1 = jnp.max(logits, axis=1, keepdims=True)
        i1 = jnp.min(jnp.where(logits == m1, lane, LANES), axis=1, keepdims=True)
        rest = jnp.where(lane == i1, -jnp.inf, logits)
        m2 = jnp.max(rest, axis=1, keepdims=True)
        i2 = jnp.min(jnp.where(rest == m2, lane, LANES), axis=1, keepdims=True)
        e2 = jnp.exp(m2 - m1)
        den = 1.0 + e2
        gate_sc[...] = jnp.where(lane == i1, 1.0 / den, 0.0) + jnp.where(lane == i2, e2 / den, 0.0)
        routed = jnp.where((lane == i1) | (lane == i2), 1.0, 0.0)
        key = jnp.where(routed > 0.0, _dot(tri_ref[...], routed.astype(BF16)), -1.0)
        key_sc[...] = key
        keyt_sc[...] = key.T

    sel = lane == e
    kcol = jnp.max(jnp.where(sel, key_sc[...], -1.0), axis=1, keepdims=True)
    wcol = jnp.sum(jnp.where(sel, gate_sc[...], 0.0), axis=1, keepdims=True)
    krow = keyt_sc[pl.ds(e, 1), :]
    count = jnp.sum(jnp.where(kcol >= 0.0, 1.0, 0.0)).astype(jnp.int32)
    n_chunks = (count + (MOE_CHUNK - 1)) // MOE_CHUNK
    row_id = lax.broadcasted_iota(jnp.int32, (MOE_CHUNK, tm), 0).astype(F32)
    col_id = lax.broadcasted_iota(jnp.int32, (tm, MOE_CHUNK), 1).astype(F32)

    def chunk(ci, carry):
        base = (ci * MOE_CHUNK).astype(F32)
        gather = jnp.where(krow - base == row_id, 1.0, 0.0).astype(BF16)
        xg = _dot(gather, h_sc[...]).astype(BF16)
        gte = _dot(xg, wg_ref[0])
        up = _dot(xg, wu_ref[0])
        act = (gte * _sigmoid(gte)) * up
        y = _dot(act.astype(BF16), wd_ref[0])
        scatter = jnp.where(kcol - base == col_id, 1.0, 0.0).astype(BF16)
        acc_sc[...] += wcol * _dot(scatter, y.astype(BF16))
        return carry

    lax.fori_loop(0, n_chunks, chunk, 0)

    @pl.when(e == pl.num_programs(1) - 1)
    def _():
        out = acc_sc[...]
        o_ref[...] = _final_norm(out, gf_ref) if final_norm else out


def _moe(x2, g, router_w, wg, wu, wd, g_final, final_norm, tm=MOE_TILE):
    T = x2.shape[0]
    rw = _pad_cols(router_w, LANES)
    rw_hi = rw.astype(BF16)
    rw_lo = (rw - rw_hi.astype(F32)).astype(BF16)
    rw2 = jnp.stack([rw_hi, rw_lo])
    tri = jnp.asarray(np.tril(np.ones((tm, tm), np.float32), -1)).astype(BF16)
    return pl.pallas_call(
        functools.partial(_moe_kernel, final_norm=final_norm, tm=tm),
        grid=(T // tm, N_EXPERTS),
        in_specs=[pl.BlockSpec((tm, D_MODEL), lambda i, e: (i, 0)), pl.BlockSpec((1, D_MODEL), lambda i, e: (0, 0)),
                  pl.BlockSpec((2, D_MODEL, LANES), lambda i, e: (0, 0, 0)),
                  pl.BlockSpec((tm, tm), lambda i, e: (0, 0)),
                  pl.BlockSpec((1, D_MODEL, D_FF_EXPERT), lambda i, e: (e, 0, 0)),
                  pl.BlockSpec((1, D_MODEL, D_FF_EXPERT), lambda i, e: (e, 0, 0)),
                  pl.BlockSpec((1, D_FF_EXPERT, D_MODEL), lambda i, e: (e, 0, 0)),
                  pl.BlockSpec((1, D_MODEL), lambda i, e: (0, 0))],
        out_specs=pl.BlockSpec((tm, D_MODEL), lambda i, e: (i, 0)),
        out_shape=jax.ShapeDtypeStruct((T, D_MODEL), F32),
        scratch_shapes=[pltpu.VMEM((tm, D_MODEL), BF16), pltpu.VMEM((tm, LANES), F32),
                        pltpu.VMEM((tm, LANES), F32), pltpu.VMEM((LANES, tm), F32),
                        pltpu.VMEM((tm, D_MODEL), F32)],
        compiler_params=_cparams(("parallel", "arbitrary")),
        name="moe_routed",
    )(x2, g.reshape(1, -1), rw2, tri, wg.astype(BF16), wu.astype(BF16), wd.astype(BF16), g_final.reshape(1, -1))


def _mixer(x2, batch, seq, tables, onehot, norm_g, w_in, conv_w, conv_b, wa, ba, wx, bx, lam,
           q_norm, w_uq, kv_norm, w_ukv, cmp_pos, cmp_w1, cmp_w2, gate_b, w_out):
    w_cat, w_t = _prep_w_in(w_in)
    lru_xy, mla_dn, qn, kc, vc, ksw, vt_sw, gates_t = _in_proj(x2, norm_g, w_cat, w_t, tables, gate_b, batch, seq)
    a_out = _lru(lru_xy, conv_w, conv_b, wa, ba, wx, bx, lam, batch, seq)

    q_m, k_m, vt_m = _mla_prep(mla_dn, q_norm, w_uq, kv_norm, w_ukv, tables, batch, seq)
    b3 = lambda t: t.reshape(batch, seq, t.shape[-1])
    heads = tuple(range(N_ATT_HEADS))
    b_out = _flash(b3(q_m), b3(k_m), vt_m, mode="causal", batch=batch, seq=seq, per_head_kv=True,
                   k_width=N_ATT_HEADS * LANES, k_col=0, vt_rows=N_ATT_HEADS * HEAD_V, vt_row=0, out_dtype=BF16,
                   tk=LONG_KEY_TILE)

    kcmp, vcmp_t = _compress(kc, vc, cmp_pos, cmp_w1, cmp_w2, batch, seq)
    qn3 = b3(qn)
    o_c, selbias = _select(qn3, kcmp, vcmp_t, gates_t, batch, seq)
    ksw3 = b3(ksw)
    nsa_kw = dict(batch=batch, seq=seq, per_head_kv=False, k_width=LANES,
                  vt_rows=NSA_KV_HEADS * HEAD_V, out_dtype=F32, gates_t=gates_t)
    o_s = _flash(qn3, ksw3, vt_sw, mode="select", k_col=0, vt_row=0, selbias=selbias,
                 onehot=onehot, gate_rows=tuple(3 * h + 1 for h in heads), tk=LONG_KEY_TILE, **nsa_kw)
    o_w = _flash(qn3, ksw3, vt_sw, mode="window", k_col=1, vt_row=1,
                 gate_rows=tuple(3 * h + 2 for h in heads), **nsa_kw)
    T = batch * seq
    flat = lambda t: t.reshape(T, t.shape[-1])
    return _out_proj(x2, a_out, flat(b_out), flat(o_c), flat(o_s), flat(o_w), w_out)


def kernel(x, norm_mix, w_in, lru_conv_w, lru_conv_b, lru_wa, lru_ba, lru_wx, lru_bx, lru_lambda, mla_q_norm, mla_w_uq, mla_kv_norm, mla_w_ukv, nsa_cmp_pos, nsa_cmp_w1, nsa_cmp_w2, nsa_gate_b, w_out, norm_ffn, ffn_w_gate, ffn_w_up, ffn_w_down, router_w, moe_w_gate, moe_w_up, moe_w_down, norm_final):
    batch, seq, _ = x.shape
    depth = norm_mix.shape[0]
    tables = _rope_tables(seq)
    blk_id = np.arange(seq)[:, None] // SEL_LEN
    onehot = jnp.asarray((blk_id == np.arange(LANES)[None, :]).astype(np.float32)).astype(BF16)
    x2 = x.reshape(batch * seq, D_MODEL)
    for l in range(depth):
        x2 = _mixer(x2, batch, seq, tables, onehot, norm_mix[l], w_in[l], lru_conv_w[l], lru_conv_b[l],
                    lru_wa[l], lru_ba[l], lru_wx[l], lru_bx[l], lru_lambda[l], mla_q_norm[l], mla_w_uq[l],
                    mla_kv_norm[l], mla_w_ukv[l], nsa_cmp_pos[l], nsa_cmp_w1[l], nsa_cmp_w2[l],
                    nsa_gate_b[l], w_out[l])
        last = l == depth - 1
        j = l // 2
        if l % 2 == 0:
            x2 = _ffn(x2, norm_ffn[l], ffn_w_gate[j], ffn_w_up[j], ffn_w_down[j], norm_final, last)
        else:
            x2 = _moe(x2, norm_ffn[l], router_w[j], moe_w_gate[j], moe_w_up[j], moe_w_down[j], norm_final, last)
    if depth == 0:
        raise ValueError("depth must be positive")
    return x2.reshape(batch, seq, D_MODEL)
```

```python
import functools
import math

import numpy as np
import jax
import jax.numpy as jnp
from jax import lax
from jax.experimental import pallas as pl
from jax.experimental.pallas import tpu as pltpu

F32 = jnp.float32
BF16 = jnp.bfloat16

D_MODEL = 1024
LRU_WIDTH = 256
LRU_BLOCKS = 4
LRU_BLOCK_W = LRU_WIDTH // LRU_BLOCKS
CONV_WIDTH = 4
LRU_C = 8.0
MLA_HEADS = 6
MLA_NOPE = 64
MLA_ROPE = 32
MLA_V = 64
MLA_Q_RANK = 192
MLA_KV_RANK = 128
NSA_HEADS = 6
NSA_KV_HEADS = 2
NSA_GROUP = NSA_HEADS // NSA_KV_HEADS
NSA_HEAD_DIM = 64
CMP_LEN = 32
CMP_STRIDE = 16
CMP_HIDDEN = 128
SEL_LEN = 64
SEL_TOPK = 16
WINDOW = 512
FORCE_BONUS = 1.0e3
D_FF = 2816
N_EXPERTS = 8
D_FF_EXPERT = 1408
ROPE_THETA = 10000.0
EPS = 1e-6
NEG_INF = -1.0e30

LANES = 128
VMEM_LIMIT = 56 * 1024 * 1024

C_LRU = 0
C_MLA = 512
C_KPE_ROT = 1024
C_QN = 1152
C_QN_ROT = 1536
C_KC = 1920
C_KC_ROT = 2048
C_VC = 2176
C_KS = 2304
C_KS_ROT = 2432
C_KW = 2560
C_KW_ROT = 2688
IN_COLS_PAD = 2816
N_ATT_HEADS = 6
HEAD_V = 64
LONG_KEY_TILE = 1024
SUM_ROWS = 16
GATE_ROWS = 32
LOG2E = 1.4426950408889634


def _cparams(sem, flags=None):
    return pltpu.CompilerParams(dimension_semantics=sem, vmem_limit_bytes=VMEM_LIMIT, flags=flags)


def _gelu_tanh(x):
    return 0.5 * x * (1.0 + jnp.tanh(math.sqrt(2.0 / math.pi) * (x + 0.044715 * (x * x * x))))


def _sigmoid(x):
    return 1.0 / (1.0 + jnp.exp(-x))


def _dot(a, b):
    return jnp.dot(a, b, preferred_element_type=F32)


def _dot_nt(a, b):
    return lax.dot_general(a, b, (((1,), (1,)), ((), ())), preferred_element_type=F32)


def _rot_cols(w, half):
    return jnp.concatenate([-w[:, half:], w[:, :half]], axis=1)


def _pad_cols(w, n):
    return jnp.pad(w, ((0, 0), (0, n - w.shape[1])))


def _prep_w_in(w_in):
    splits = np.cumsum([256, 256, 192, 128, 32, 384, 128, 128, 128, 128, 128, 128, 18])[:-1].tolist()
    (x_l, y_l, q_dn, kv_dn, k_pe, q_n, kc, vc, ks, vs, kw, vw, g_n) = jnp.split(w_in, splits, axis=1)
    z = lambda n: jnp.zeros((w_in.shape[0], n), w_in.dtype)

    def kpe_slot(w):
        return jnp.concatenate([z(64), w, z(32)], axis=1)

    def qn_slots(fn):
        out = []
        for p in range(NSA_GROUP):
            for h in (p, p + NSA_GROUP):
                out.append(fn(q_n[:, h * 64:(h + 1) * 64]))
        return jnp.concatenate(out, axis=1)

    def kv_rot(w):
        return jnp.concatenate([_rot_cols(w[:, :64], 32), _rot_cols(w[:, 64:], 32)], axis=1)

    cols = [x_l, y_l, _pad_cols(q_dn, 256), kv_dn, kpe_slot(k_pe), kpe_slot(_rot_cols(k_pe, 16)),
            qn_slots(lambda w: w), qn_slots(lambda w: _rot_cols(w, 32)),
            kc, kv_rot(kc), vc, ks, kv_rot(ks), kw, kv_rot(kw)]
    w = jnp.concatenate(cols, axis=1)
    assert w.shape[1] == IN_COLS_PAD
    w_t = jnp.concatenate([vs, vw, _pad_cols(g_n, GATE_ROWS)], axis=1).T
    return w.astype(BF16), w_t.astype(BF16)


def _rope_tables(seq):
    pos = jnp.arange(seq, dtype=F32)[:, None]
    inv64 = ROPE_THETA ** (-jnp.arange(32, dtype=F32) * 2.0 / 64)
    inv32 = ROPE_THETA ** (-jnp.arange(16, dtype=F32) * 2.0 / 32)
    a64 = pos * inv64[None, :]
    a32 = pos * inv32[None, :]
    c64, s64 = jnp.cos(a64), jnp.sin(a64)
    c32, s32 = jnp.cos(a32), jnp.sin(a32)
    cos_n = jnp.concatenate([c64, c64, c64, c64], axis=1)
    sin_n = jnp.concatenate([s64, s64, s64, s64], axis=1)
    one = jnp.ones((seq, 64), F32)
    cos_m = jnp.concatenate([one, c32, c32, one[:, :32]], axis=1)
    sin_m = jnp.concatenate([0 * one, s32, s32, 0 * one[:, :32]], axis=1)
    return cos_n, sin_n, cos_m, sin_m


def _in_proj_kernel(x_ref, g_ref, w_ref, wt_ref, cn_ref, sn_ref, cm_ref, sm_ref, gb_ref,
                    lru_ref, mla_ref, qn_ref, kc_ref, vc_ref, ksw_ref, vt_ref, gt_ref):
    x = x_ref[...]
    ms = jnp.mean(x * x, axis=-1, keepdims=True)
    h = ((x * lax.rsqrt(ms + EPS)) * g_ref[...]).astype(BF16)

    def proj(c0, n):
        return _dot(h, w_ref[:, c0:c0 + n])

    cn, sn = cn_ref[...], sn_ref[...]
    lru_ref[...] = proj(C_LRU, 512)
    mla_ref[:, 0:384] = proj(C_MLA, 384)
    mla_ref[:, 384:512] = proj(C_MLA + 384, 128) * cm_ref[...] + proj(C_KPE_ROT, 128) * sm_ref[...]
    scale = NSA_HEAD_DIM ** -0.5 * LOG2E
    low_half = lax.broadcasted_iota(jnp.int32, cn.shape, 1) < NSA_HEAD_DIM
    for p in range(NSA_GROUP):
        a = proj(C_QN + p * LANES, LANES)
        b = proj(C_QN_ROT + p * LANES, LANES)
        r = (a * cn + b * sn) * scale
        hi_hd = p + NSA_GROUP
        qn_ref[:, p * LANES:(p + 1) * LANES] = jnp.where(low_half, r, 0.0).astype(BF16)
        qn_ref[:, hi_hd * LANES:(hi_hd + 1) * LANES] = jnp.where(low_half, 0.0, r).astype(BF16)
    kc_ref[...] = proj(C_KC, 128) * cn + proj(C_KC_ROT, 128) * sn
    vc_ref[...] = proj(C_VC, 128)
    ksw_ref[:, 0:128] = (proj(C_KS, 128) * cn + proj(C_KS_ROT, 128) * sn).astype(BF16)
    ksw_ref[:, 128:256] = (proj(C_KW, 128) * cn + proj(C_KW_ROT, 128) * sn).astype(BF16)
    vt_ref[0] = _dot_nt(wt_ref[0:256, :], h).astype(BF16)
    gt_ref[0] = _sigmoid(_dot_nt(wt_ref[256:256 + GATE_ROWS, :], h) + gb_ref[...])


def _in_proj(x2, g, w_cat, w_t, tables, gate_b, batch, seq, tm=512):
    T = x2.shape[0]
    nt = T // tm
    npos = seq // tm
    cn, sn, cm, sm = tables
    row = lambda i: (i, 0)
    fixed = lambda i: (0, 0)
    posmap = lambda i: (i % npos, 0)
    tab_spec = pl.BlockSpec((tm, LANES), posmap)
    out_shapes = (
        jax.ShapeDtypeStruct((T, 512), F32),
        jax.ShapeDtypeStruct((T, 512), F32),
        jax.ShapeDtypeStruct((T, 768), BF16),
        jax.ShapeDtypeStruct((T, 128), F32),
        jax.ShapeDtypeStruct((T, 128), F32),
        jax.ShapeDtypeStruct((T, 256), BF16),
    )
    t_shapes = (jax.ShapeDtypeStruct((batch, 256, seq), BF16),
                jax.ShapeDtypeStruct((batch, GATE_ROWS, seq), F32))
    tmap = lambda i: (i // npos, 0, i % npos)
    gate_b_col = jnp.pad(gate_b, (0, GATE_ROWS - gate_b.shape[0])).reshape(GATE_ROWS, 1)
    return pl.pallas_call(
        _in_proj_kernel,
        grid=(nt,),
        in_specs=[pl.BlockSpec((tm, D_MODEL), row), pl.BlockSpec((1, D_MODEL), fixed),
                  pl.BlockSpec((D_MODEL, IN_COLS_PAD), fixed), pl.BlockSpec((256 + GATE_ROWS, D_MODEL), fixed),
                  tab_spec, tab_spec, tab_spec, tab_spec, pl.BlockSpec((GATE_ROWS, 1), fixed)],
        out_specs=[pl.BlockSpec((tm, s.shape[1]), row) for s in out_shapes]
        + [pl.BlockSpec((1, s.shape[1], tm), tmap) for s in t_shapes],
        out_shape=out_shapes + t_shapes,
        compiler_params=_cparams(("parallel",)),
        name="in_proj",
    )(x2, g.reshape(1, -1), w_cat, w_t, cn, sn, cm, sm, gate_b_col)


def _lru_kernel(xy_ref, cw_ref, cb_ref, wa_ref, ba_ref, wx_ref, bx_ref, lam_ref, out_ref,
                xe_sc, h_sc, *, tt):
    t = pl.program_id(1)

    @pl.when(t == 0)
    def _():
        xe_sc[0:8, :] = jnp.zeros((8, LRU_WIDTH), F32)
        h_sc[...] = jnp.zeros_like(h_sc)

    x = xy_ref[0, :, 0:LRU_WIDTH]
    y = xy_ref[0, :, LRU_WIDTH:2 * LRU_WIDTH]
    xe_sc[8:8 + tt, :] = x
    xc = cb_ref[...]
    for k in range(CONV_WIDTH):
        off = 8 - (CONV_WIDTH - 1) + k
        xc = xc + xe_sc[off:off + tt, :] * cw_ref[k:k + 1, :]
    xe_sc[0:8, :] = x[tt - 8:tt, :]

    xb = xc.astype(BF16)
    r = _sigmoid(_dot(xb, wa_ref[...]) + ba_ref[...])
    i = _sigmoid(_dot(xb, wx_ref[...]) + bx_ref[...])
    log_a = (-LRU_C * r) * jax.nn.softplus(-lam_ref[...])
    a = jnp.exp(log_a)
    b = jnp.sqrt(jnp.tanh(-log_a) * (a * a + 1.0)) * (i * xc)

    row = lax.broadcasted_iota(jnp.int32, (tt, LRU_WIDTH), 0)
    k = 1
    while k < tt:
        keep = row >= k
        a_sh = jnp.where(keep, pltpu.roll(a, k, 0), 1.0)
        b_sh = jnp.where(keep, pltpu.roll(b, k, 0), 0.0)
        b = a * b_sh + b
        a = a * a_sh
        k *= 2
    h = b + a * h_sc[0:1, :]
    h_sc[0:1, :] = h[tt - 1:tt, :]
    out_ref[0] = h * _gelu_tanh(y)


def _block_diag_dense(w):
    n, c, d = w.shape
    out = jnp.zeros((n * c, n * d), w.dtype)
    for j in range(n):
        out = out.at[j * c:(j + 1) * c, j * d:(j + 1) * d].set(w[j])
    return out


def _lru(lru_xy, conv_w, conv_b, wa, ba, wx, bx, lam, batch, seq, tt=512):
    xy = lru_xy.reshape(batch, seq, 512)
    fixed = lambda b, t: (0, 0)
    vec = pl.BlockSpec((1, LRU_WIDTH), fixed)
    mat = pl.BlockSpec((LRU_WIDTH, LRU_WIDTH), fixed)
    out = pl.pallas_call(
        functools.partial(_lru_kernel, tt=tt),
        grid=(batch, seq // tt),
        in_specs=[pl.BlockSpec((1, tt, 512), lambda b, t: (b, t, 0)),
                  pl.BlockSpec((CONV_WIDTH, LRU_WIDTH), fixed), vec, mat, vec, mat, vec, vec],
        out_specs=pl.BlockSpec((1, tt, LRU_WIDTH), lambda b, t: (b, t, 0)),
        out_shape=jax.ShapeDtypeStruct((batch, seq, LRU_WIDTH), F32),
        scratch_shapes=[pltpu.VMEM((tt + 8, LRU_WIDTH), F32), pltpu.VMEM((8, LRU_WIDTH), F32)],
        compiler_params=_cparams(("parallel", "arbitrary")),
        name="rg_lru",
    )(xy, conv_w, conv_b.reshape(1, -1), _block_diag_dense(wa).astype(BF16), ba.reshape(1, -1),
      _block_diag_dense(wx).astype(BF16), bx.reshape(1, -1), lam.reshape(1, -1))
    return out.reshape(batch * seq, LRU_WIDTH)


def _mla_prep_kernel(dn_ref, qn_ref, wqa_ref, wqb_ref, kvn_ref, wk_ref, wvt_ref, cm_ref, sm_ref,
                     q_ref, k_ref, vt_ref):
    q_dn = dn_ref[:, 0:256]
    ms = jnp.sum(q_dn * q_dn, axis=-1, keepdims=True) * (1.0 / MLA_Q_RANK)
    ql = ((q_dn * lax.rsqrt(ms + EPS)) * qn_ref[...]).astype(BF16)
    kv_dn = dn_ref[:, 256:384]
    ms = jnp.mean(kv_dn * kv_dn, axis=-1, keepdims=True)
    kvl = ((kv_dn * lax.rsqrt(ms + EPS)) * kvn_ref[...]).astype(BF16)
    kpe = dn_ref[:, 384:512]
    cm, sm = cm_ref[...], sm_ref[...]
    scale = (MLA_NOPE + MLA_ROPE) ** -0.5 * LOG2E
    for hd in range(MLA_HEADS):
        sl = slice(hd * LANES, (hd + 1) * LANES)
        q_ref[:, sl] = ((_dot(ql, wqa_ref[:, sl]) * cm + _dot(ql, wqb_ref[:, sl]) * sm) * scale).astype(BF16)
        k_ref[:, sl] = (_dot(kvl, wk_ref[:, sl]) + kpe).astype(BF16)
    vt_ref[0] = _dot_nt(wvt_ref[...], kvl).astype(BF16)


def _prep_mla_weights(w_uq, w_ukv):
    zq = jnp.zeros((MLA_Q_RANK, 32), w_uq.dtype)
    qa, qb, kk, vv = [], [], [], []
    for h in range(MLA_HEADS):
        wq = w_uq[:, h * 96:(h + 1) * 96]
        nope, ropew = wq[:, :64], wq[:, 64:]
        qa.append(jnp.concatenate([nope, ropew, zq], axis=1))
        qb.append(jnp.concatenate([0 * nope, _rot_cols(ropew, 16), zq], axis=1))
        wkv = w_ukv[:, h * 128:(h + 1) * 128]
        kk.append(_pad_cols(wkv[:, :64], LANES))
        vv.append(wkv[:, 64:])
    pad_rows = lambda w: jnp.pad(w, ((0, 256 - MLA_Q_RANK), (0, 0)))
    return (pad_rows(jnp.concatenate(qa, axis=1)).astype(BF16), pad_rows(jnp.concatenate(qb, axis=1)).astype(BF16),
            jnp.concatenate(kk, axis=1).astype(BF16), jnp.concatenate(vv, axis=1).T.astype(BF16))


def _mla_prep(mla_dn, q_norm, w_uq, kv_norm, w_ukv, tables, batch, seq, tm=512):
    T = mla_dn.shape[0]
    npos = seq // tm
    _, _, cm, sm = tables
    wqa, wqb, wk, wvt = _prep_mla_weights(w_uq, w_ukv)
    row = lambda i: (i, 0)
    fixed = lambda i: (0, 0)
    tab = pl.BlockSpec((tm, LANES), lambda i: (i % npos, 0))
    return pl.pallas_call(
        _mla_prep_kernel,
        grid=(T // tm,),
        in_specs=[pl.BlockSpec((tm, 512), row), pl.BlockSpec((1, 256), fixed),
                  pl.BlockSpec((256, 768), fixed), pl.BlockSpec((256, 768), fixed),
                  pl.BlockSpec((1, 128), fixed), pl.BlockSpec((128, 768), fixed),
                  pl.BlockSpec((384, 128), fixed), tab, tab],
        out_specs=[pl.BlockSpec((tm, 768), row), pl.BlockSpec((tm, 768), row),
                   pl.BlockSpec((1, 384, tm), lambda i: (i // npos, 0, i % npos))],
        out_shape=(jax.ShapeDtypeStruct((T, 768), BF16), jax.ShapeDtypeStruct((T, 768), BF16),
                   jax.ShapeDtypeStruct((batch, 384, seq), BF16)),
        compiler_params=_cparams(("parallel",)),
        name="mla_prep",
    )(mla_dn, _pad_cols(q_norm.reshape(1, -1), 256), wqa, wqb, kv_norm.reshape(1, -1), wk, wvt, cm, sm)


PROBE_KEYS = 128
LAG_LIMIT = 8.0


def _flash_kernel(qt_ref, kt_ref, first_ref, last_ref, par_ref, *refs, mode, k_per_head, v_per_head, gate_rows,
                  tq, tk):
    refs = list(refs)
    q_ref, k_ref, vt_ref = refs[:3]
    o_ref, m_st, acc_st = refs[-8:-5]
    s_sc, mx_sc, flag_sm = refs[-5:-3], refs[-3:-1], refs[-1]
    extra = refs[3:-8]
    sb_ref, oh_ref = (extra[0], extra[1]) if mode == "select" else (None, None)
    gt_ref = extra[-1] if gate_rows is not None else None
    step = pl.program_id(1)
    qt = qt_ref[step]
    kt = kt_ref[step]
    rd = par_ref[step]
    wr = 1 - rd

    def visible(nk):
        kpos = kt * tk + lax.broadcasted_iota(jnp.int32, (nk, tq), 0)
        qpos = qt * tq + lax.broadcasted_iota(jnp.int32, (nk, tq), 1)
        dist = qpos - kpos
        return (dist >= 0) & (dist < WINDOW) if mode == "window" else dist >= 0

    def score_tile(j, nk, valid):
        q = q_ref[0, :, j * LANES:(j + 1) * LANES]
        k = k_ref[0, 0:nk, j * LANES:(j + 1) * LANES] if k_per_head else k_ref[0, 0:nk, :]
        if mode == "select":
            q = jnp.concatenate([q, sb_ref[0, j // NSA_GROUP]], axis=1)
            k = jnp.concatenate([k, oh_ref[0:nk, :]], axis=1)
        s = _dot_nt(k, q)
        return s if valid is None else jnp.where(valid, s, NEG_INF)

    def weighted_values(j, nk, p):
        row = (j if v_per_head else j // NSA_GROUP) * HEAD_V
        vt = vt_ref[0, row:row + HEAD_V, 0:nk]
        return _dot(jnp.concatenate([vt, jnp.ones((SUM_ROWS, nk), BF16)], axis=0), p)

    def exact(nk, init):
        valid = visible(nk)

        def scores(j, slot):
            s = score_tile(j, nk, valid)
            s_sc[slot][0:nk, :] = s
            mx_sc[slot][...] = jnp.max(s, axis=0, keepdims=True)

        def accumulate(j, slot):
            if init:
                m_new = mx_sc[slot][...]
            else:
                m_prev = m_st[rd, j]
                m_new = jnp.maximum(m_prev, mx_sc[slot][...])
            p = jnp.exp2((s_sc[slot][0:nk, :] - m_new).astype(BF16))
            upd = weighted_values(j, nk, p)
            acc_st[wr, j] = upd if init else jnp.exp2(m_prev - m_new) * acc_st[rd, j] + upd
            m_st[wr, j] = m_new

        scores(0, 0)
        for j in range(N_ATT_HEADS):
            if j + 1 < N_ATT_HEADS:
                scores(j + 1, (j + 1) % 2)
            accumulate(j, j % 2)

    def lagged(nk, masked, init):
        valid = visible(nk) if masked else None
        rise = None
        for j in range(N_ATT_HEADS):
            if init:
                probe = score_tile(j, PROBE_KEYS, None if valid is None else valid[0:PROBE_KEYS])
                m_prev = jnp.max(probe, axis=0, keepdims=True)
            else:
                m_prev = m_st[rd, j]
            s = score_tile(j, nk, valid)
            p = jnp.exp2((s - m_prev).astype(BF16))
            tile_max = jnp.max(s, axis=0, keepdims=True)
            m_new = jnp.maximum(m_prev, tile_max)
            upd = weighted_values(j, nk, p)
            acc_st[wr, j] = (upd if init else acc_st[rd, j] + upd) * jnp.exp2(m_prev - m_new)
            m_st[wr, j] = m_new
            d = tile_max - m_prev
            rise = d if rise is None else jnp.maximum(rise, d)
        flag_sm[0] = (jnp.max(rise) > LAG_LIMIT).astype(jnp.int32)

    first = first_ref[step] == 1
    later = jnp.logical_not(first)
    if mode == "window":
        pl.when(first)(lambda: exact(tk, True))
        pl.when(later)(lambda: exact(tk, False))
    else:
        first_q, last_q = qt * tq, qt * tq + (tq - 1)
        crosses = kt * tk + (tk - 1) > first_q
        clear = jnp.logical_not(crosses)
        flag_sm[0] = 0
        for init, when in ((True, first), (False, later)):
            pl.when(when & clear)(functools.partial(lagged, tk, False, init))
            if tk > tq:
                half_only = kt * tk + tk // 2 > last_q
                pl.when(when & crosses & half_only)(functools.partial(lagged, tk // 2, True, init))
                pl.when(when & crosses & jnp.logical_not(half_only))(functools.partial(lagged, tk, True, init))
            else:
                pl.when(when & crosses)(functools.partial(lagged, tk, True, init))
        redo = flag_sm[0] == 1
        pl.when(redo & first)(lambda: exact(tk, True))
        pl.when(redo & later)(lambda: exact(tk, False))

    @pl.when(last_ref[step] == 1)
    def _():
        outs = []
        for j in range(N_ATT_HEADS):
            o = acc_st[wr, j, 0:HEAD_V, :] / acc_st[wr, j, HEAD_V:HEAD_V + 1, :]
            if gate_rows is not None:
                o = o * gt_ref[0, gate_rows[j]:gate_rows[j] + 1, :]
            outs.append(o)
        o_ref[0] = jnp.concatenate(outs, axis=0).T.astype(o_ref.dtype)


def _pair_tables(nq, mode, tq, tk):
    qts, kts, first, last, par = [], [], [], [], []
    for qt in range(nq):
        hi = (qt * tq + tq - 1) // tk
        lo = 0 if mode != "window" else max(0, (qt * tq - (WINDOW - 1)) // tk)
        for kt in range(lo, hi + 1):
            qts.append(qt)
            kts.append(kt)
            first.append(1 if kt == lo else 0)
            last.append(1 if kt == hi else 0)
            par.append((kt - lo) % 2)
    arr = lambda v: jnp.asarray(np.array(v, dtype=np.int32))
    return arr(qts), arr(kts), arr(first), arr(last), arr(par)


def _flash(q, k, vt, *, mode, batch, seq, per_head_kv, k_width, k_col, vt_rows, vt_row, out_dtype,
           selbias=None, onehot=None, gate_rows=None, gates_t=None, tq=512, tk=512):
    tk = min(tk, seq)
    nq = seq // tq
    tables = _pair_tables(nq, mode, tq, tk)
    n_steps = int(tables[0].shape[0])
    qw = N_ATT_HEADS * LANES
    ow = N_ATT_HEADS * HEAD_V
    in_specs = [
        pl.BlockSpec((1, tq, qw), lambda b, s, qt, kt, f, l, p: (b, qt[s], 0)),
        pl.BlockSpec((1, tk, k_width), lambda b, s, qt, kt, f, l, p: (b, kt[s], k_col)),
        pl.BlockSpec((1, vt_rows, tk), lambda b, s, qt, kt, f, l, p: (b, vt_row, kt[s])),
    ]
    args = [q, k, vt]
    if mode == "select":
        in_specs.append(pl.BlockSpec((1, NSA_KV_HEADS, tq, LANES), lambda b, s, qt, kt, f, l, p: (b, 0, qt[s], 0)))
        in_specs.append(pl.BlockSpec((tk, LANES), lambda b, s, qt, kt, f, l, p: (kt[s], 0)))
        args += [selbias, onehot]
    if gate_rows is not None:
        in_specs.append(pl.BlockSpec((1, GATE_ROWS, tq), lambda b, s, qt, kt, f, l, p: (b, 0, qt[s])))
        args.append(gates_t)
    kern = functools.partial(_flash_kernel, mode=mode, k_per_head=per_head_kv, v_per_head=per_head_kv,
                             gate_rows=gate_rows, tq=tq, tk=tk)
    return pl.pallas_call(
        kern,
        grid_spec=pltpu.PrefetchScalarGridSpec(
            num_scalar_prefetch=5,
            grid=(batch, n_steps),
            in_specs=in_specs,
            out_specs=pl.BlockSpec((1, tq, ow), lambda b, s, qt, kt, f, l, p: (b, qt[s], 0)),
            scratch_shapes=[pltpu.VMEM((2, N_ATT_HEADS, 1, tq), F32),
                            pltpu.VMEM((2, N_ATT_HEADS, HEAD_V + SUM_ROWS, tq), F32),
                            pltpu.VMEM((tk, tq), F32), pltpu.VMEM((tk, tq), F32),
                            pltpu.VMEM((1, tq), F32), pltpu.VMEM((1, tq), F32),
                            pltpu.SMEM((1,), jnp.int32)],
        ),
        out_shape=jax.ShapeDtypeStruct((batch, seq, ow), out_dtype),
        compiler_params=_cparams(("parallel", "arbitrary")),
        name="flash_" + mode,
    )(*tables, *args)


def _compress_kernel(kc_ref, vc_ref, pos_ref, w1a_ref, w1b_ref, w2_ref, ko_ref, vo_ref, *, nchunk):
    row = lax.broadcasted_iota(jnp.int32, (nchunk, LANES), 0)
    for br, src in enumerate((kc_ref, vc_ref)):
        r = src[0]
        pa = _dot((r + pos_ref[br, 0:1, :]).astype(BF16), w1a_ref[br])
        pb = _dot((r + pos_ref[br, 1:2, :]).astype(BF16), w1b_ref[br])
        hid = _gelu_tanh(pa + pltpu.roll(pb, nchunk - 1, 0))
        out = jnp.where(row < nchunk - 1, _dot(hid.astype(BF16), w2_ref[br]), 0.0)
        if br == 0:
            ko_ref[0] = out.astype(ko_ref.dtype)
        else:
            vo_ref[0] = out.T.astype(vo_ref.dtype)


def _prep_compress_weights(cmp_pos, cmp_w1, cmp_w2):
    half = CMP_LEN // 2
    pos, w1a, w1b, w2 = [], [], [], []
    for br in range(2):
        p = cmp_pos[br]
        tile = lambda ph: jnp.concatenate([ph, ph], axis=1).reshape(1, half * LANES)
        pos.append(jnp.concatenate([tile(p[:half]), tile(p[half:])], axis=0))
        w = cmp_w1[br].reshape(CMP_LEN, NSA_HEAD_DIM, CMP_HIDDEN)
        z = jnp.zeros_like(w[:half])

        def big(wh):
            g0 = jnp.concatenate([wh, z], axis=1)
            g1 = jnp.concatenate([z, wh], axis=1)
            return jnp.concatenate([g0.reshape(half * LANES, CMP_HIDDEN), g1.reshape(half * LANES, CMP_HIDDEN)], axis=1)

        w1a.append(big(w[:half]))
        w1b.append(big(w[half:]))
        w2.append(_block_diag_dense(jnp.stack([cmp_w2[br], cmp_w2[br]])))
    return (jnp.stack(pos), jnp.stack(w1a).astype(BF16), jnp.stack(w1b).astype(BF16), jnp.stack(w2).astype(BF16))


def _compress(kc, vc, cmp_pos, cmp_w1, cmp_w2, batch, seq):
    nchunk = seq // CMP_STRIDE
    width = CMP_STRIDE * LANES
    pos, w1a, w1b, w2 = _prep_compress_weights(cmp_pos, cmp_w1, cmp_w2)
    blk = pl.BlockSpec((1, nchunk, width), lambda b: (b, 0, 0))
    full3 = lambda shape: pl.BlockSpec(shape, lambda b: (0, 0, 0))
    return pl.pallas_call(
        functools.partial(_compress_kernel, nchunk=nchunk),
        grid=(batch,),
        in_specs=[blk, blk, full3(pos.shape), full3(w1a.shape), full3(w1b.shape), full3(w2.shape)],
        out_specs=[pl.BlockSpec((1, nchunk, LANES), lambda b: (b, 0, 0)),
                   pl.BlockSpec((1, LANES, nchunk), lambda b: (b, 0, 0))],
        out_shape=(jax.ShapeDtypeStruct((batch, nchunk, LANES), BF16),
                   jax.ShapeDtypeStruct((batch, LANES, nchunk), BF16)),
        compiler_params=_cparams(("parallel",)),
        name="nsa_compress",
    )(kc.reshape(batch, nchunk, width), vc.reshape(batch, nchunk, width), pos, w1a, w1b, w2)


def _select_kernel(q_ref, kc_ref, vct_ref, ovt_ref, gt_ref, oc_ref, sb_ref, ot_sc, imp_sc, *, tq, nchunk, variants):
    nsel = LANES
    qt = pl.program_id(1)
    col_ok = (qt * tq + lax.broadcasted_iota(jnp.int32, (1, tq), 1) >= CMP_LEN - 1).astype(F32)

    def attend(nc):
        kc = kc_ref[0, 0:nc, :]
        qpos = qt * tq + lax.broadcasted_iota(jnp.int32, (nc, tq), 1)
        cend = lax.broadcasted_iota(jnp.int32, (nc, tq), 0) * CMP_STRIDE + (CMP_LEN - 1)
        cvalid = cend <= qpos
        ovt = ovt_ref[:, 0:nc]
        for g in range(NSA_KV_HEADS):
            vct = vct_ref[0, g * HEAD_V:(g + 1) * HEAD_V, 0:nc]
            psum = jnp.zeros((nc, tq), F32)
            for pj in range(NSA_GROUP):
                hd = g * NSA_GROUP + pj
                q = q_ref[0, :, hd * LANES:(hd + 1) * LANES]
                s = jnp.where(cvalid, _dot_nt(kc, q), NEG_INF)
                m = jnp.max(s, axis=0, keepdims=True)
                e = jnp.exp2(s - m)
                p = e * (col_ok / jnp.maximum(jnp.sum(e, axis=0, keepdims=True), 1e-30))
                ot_sc[hd * HEAD_V:(hd + 1) * HEAD_V, :] = _dot(vct, p.astype(BF16)) * gt_ref[0, 3 * hd:3 * hd + 1, :]
                psum = psum + p
            hi = psum.astype(BF16)
            r1 = psum - hi.astype(F32)
            mid = r1.astype(BF16)
            lo = (r1 - mid.astype(F32)).astype(BF16)
            imp_sc[g] = _dot(ovt, hi) + _dot(ovt, mid) + _dot(ovt, lo)

    needed = (qt * tq + tq - CMP_STRIDE) // CMP_STRIDE
    lo_bound = 0
    for nc in variants:
        pl.when((needed > lo_bound) & (needed <= nc) if nc != variants[-1] else needed > lo_bound)(
            functools.partial(attend, nc))
        lo_bound = nc

    blk = lax.broadcasted_iota(jnp.int32, (nsel, tq), 0)
    blkf = blk.astype(F32)
    qp = qt * tq + lax.broadcasted_iota(jnp.int32, (nsel, tq), 1)
    valid = blk * SEL_LEN <= qp
    forced = (blk == 0) | (blk == qp // SEL_LEN)
    for g in range(NSA_KV_HEADS):
        score = jnp.where(valid, imp_sc[g], -1.0)
        score0 = jnp.where(forced, score + FORCE_BONUS, score)
        score = score0
        for _ in range(min(SEL_TOPK, nsel)):
            m = jnp.max(score, axis=0, keepdims=True)
            idx = jnp.min(jnp.where(score == m, blkf, float(nsel)), axis=0, keepdims=True)
            score = jnp.where(blkf == idx, -jnp.inf, score)
        bias = jnp.where(score0 >= 0.0, jnp.where(score == -jnp.inf, 0.0, NEG_INF), NEG_INF)
        sb_ref[0, g] = bias.T.astype(sb_ref.dtype)
    oc_ref[0] = ot_sc[...].T


def _overlap_matrix_t(nchunk, nsel):
    n = np.arange(nchunk)[None, :]
    s = np.arange(nsel)[:, None]
    cs = n * CMP_STRIDE
    ov = (cs < s * SEL_LEN + SEL_LEN) & (cs + CMP_LEN - 1 >= s * SEL_LEN) & (n < nchunk - 1)
    return jnp.asarray(ov.astype(np.float32)).astype(BF16)


def _select(qn, kcmp, vcmp_t, gates_t, batch, seq, tq=256):
    nchunk = seq // CMP_STRIDE
    nsel = LANES
    assert seq // SEL_LEN <= nsel
    qw = N_ATT_HEADS * LANES
    ow = N_ATT_HEADS * HEAD_V
    ovt = _overlap_matrix_t(nchunk, nsel)
    quarter = nchunk // 4
    variants = tuple(quarter * i for i in range(1, 5)) if quarter % LANES == 0 else (nchunk,)
    return pl.pallas_call(
        functools.partial(_select_kernel, tq=tq, nchunk=nchunk, variants=variants),
        grid=(batch, seq // tq),
        in_specs=[pl.BlockSpec((1, tq, qw), lambda b, t: (b, t, 0)),
                  pl.BlockSpec((1, nchunk, LANES), lambda b, t: (b, 0, 0)),
                  pl.BlockSpec((1, NSA_KV_HEADS * HEAD_V, nchunk), lambda b, t: (b, 0, 0)),
                  pl.BlockSpec((nsel, nchunk), lambda b, t: (0, 0)),
                  pl.BlockSpec((1, GATE_ROWS, tq), lambda b, t: (b, 0, t))],
        out_specs=[pl.BlockSpec((1, tq, ow), lambda b, t: (b, t, 0)),
                   pl.BlockSpec((1, NSA_KV_HEADS, tq, nsel), lambda b, t: (b, 0, t, 0))],
        out_shape=(jax.ShapeDtypeStruct((batch, seq, ow), F32),
                   jax.ShapeDtypeStruct((batch, NSA_KV_HEADS, seq, nsel), BF16)),
        scratch_shapes=[pltpu.VMEM((ow, tq), F32), pltpu.VMEM((NSA_KV_HEADS, nsel, tq), F32)],
        compiler_params=_cparams(("parallel", "parallel")),
        name="nsa_select",
    )(qn, kcmp, vcmp_t, ovt, gates_t)


def _out_proj_kernel(x_ref, a_ref, b_ref, oc_ref, os_ref, ow_ref, wa_ref, wb_ref, wc_ref, o_ref):
    acc = x_ref[...] + _dot(a_ref[...].astype(BF16), wa_ref[...])
    acc = acc + _dot(b_ref[...], wb_ref[...])
    c = (oc_ref[...] + os_ref[...]) + ow_ref[...]
    o_ref[...] = acc + _dot(c.astype(BF16), wc_ref[...])


def _out_proj(x2, a_out, b_out, o_c, o_s, o_w, w_out, tm=512):
    T = x2.shape[0]
    wa = w_out[0:256].astype(BF16)
    wb = w_out[256:640].astype(BF16)
    wc = w_out[640:1024].astype(BF16)
    row = lambda i: (i, 0)
    fixed = lambda i: (0, 0)
    act = lambda n: pl.BlockSpec((tm, n), row)
    return pl.pallas_call(
        _out_proj_kernel,
        grid=(T // tm,),
        in_specs=[act(D_MODEL), act(256), act(384), act(384), act(384), act(384),
                  pl.BlockSpec((256, D_MODEL), fixed), pl.BlockSpec((384, D_MODEL), fixed),
                  pl.BlockSpec((384, D_MODEL), fixed)],
        out_specs=pl.BlockSpec((tm, D_MODEL), row),
        out_shape=jax.ShapeDtypeStruct((T, D_MODEL), F32),
        compiler_params=_cparams(("parallel",)),
        name="out_proj",
    )(x2, a_out, b_out, o_c, o_s, o_w, wa, wb, wc)


def _final_norm(y, gf_ref):
    ms = jnp.mean(y * y, axis=-1, keepdims=True)
    return (y * lax.rsqrt(ms + EPS)) * gf_ref[...]


def _ffn_kernel(x_ref, g_ref, wg_ref, wu_ref, wd_ref, gf_ref, o_ref, h_sc, acc_sc, *, final_norm):
    f = pl.program_id(1)

    @pl.when(f == 0)
    def _():
        x = x_ref[...]
        ms = jnp.mean(x * x, axis=-1, keepdims=True)
        h_sc[...] = ((x * lax.rsqrt(ms + EPS)) * g_ref[...]).astype(BF16)
        acc_sc[...] = x

    h = h_sc[...]
    gte = _dot(h, wg_ref[...])
    up = _dot(h, wu_ref[...])
    act = (gte * _sigmoid(gte)) * up
    acc_sc[...] += _dot(act.astype(BF16), wd_ref[...])

    @pl.when(f == pl.num_programs(1) - 1)
    def _():
        y = acc_sc[...]
        o_ref[...] = _final_norm(y, gf_ref) if final_norm else y


def _ffn(x2, g, wg, wu, wd, g_final, final_norm, tm=512, tf=1408):
    T = x2.shape[0]
    nf = D_FF // tf
    return pl.pallas_call(
        functools.partial(_ffn_kernel, final_norm=final_norm),
        grid=(T // tm, nf),
        in_specs=[pl.BlockSpec((tm, D_MODEL), lambda i, f: (i, 0)), pl.BlockSpec((1, D_MODEL), lambda i, f: (0, 0)),
                  pl.BlockSpec((D_MODEL, tf), lambda i, f: (0, f)), pl.BlockSpec((D_MODEL, tf), lambda i, f: (0, f)),
                  pl.BlockSpec((tf, D_MODEL), lambda i, f: (f, 0)), pl.BlockSpec((1, D_MODEL), lambda i, f: (0, 0))],
        out_specs=pl.BlockSpec((tm, D_MODEL), lambda i, f: (i, 0)),
        out_shape=jax.ShapeDtypeStruct((T, D_MODEL), F32),
        scratch_shapes=[pltpu.VMEM((tm, D_MODEL), BF16), pltpu.VMEM((tm, D_MODEL), F32)],
        compiler_params=_cparams(("parallel", "arbitrary")),
        name="ffn_dense",
    )(x2, g.reshape(1, -1), wg.astype(BF16), wu.astype(BF16), wd.astype(BF16), g_final.reshape(1, -1))


MOE_TILE = 1024
MOE_CHUNK = 384
MOE_CHUNK_SMALL = 256


def _moe_kernel(x_ref, g_ref, rw_ref, tri_ref, wg_ref, wu_ref, wd_ref, gf_ref, o_ref,
                h_sc, gate_sc, key_sc, keyt_sc, acc_sc, *, final_norm, tm):
    e = pl.program_id(1)
    lane = lax.broadcasted_iota(jnp.int32, (tm, LANES), 1)

    @pl.when(e == 0)
    def _():
        x = x_ref[...]
        ms = jnp.mean(x * x, axis=-1, keepdims=True)
        hf = (x * lax.rsqrt(ms + EPS)) * g_ref[...]
        h_sc[...] = hf.astype(BF16)
        acc_sc[...] = x
        h_hi = hf.astype(BF16)
        h_lo = (hf - h_hi.astype(F32)).astype(BF16)
        logits = (_dot(h_hi, rw_ref[0]) + _dot(h_lo, rw_ref[0]) + _dot(h_hi, rw_ref[1]))
        logits = jnp.where(lane < N_EXPERTS, logits, -jnp.inf)
        m1 = jnp.max(logits, axis=1, keepdims=True)
        i1 = jnp.min(jnp.where(logits == m1, lane, LANES), axis=1, keepdims=True)
        rest = jnp.where(lane == i1, -jnp.inf, logits)
        m2 = jnp.max(rest, axis=1, keepdims=True)
        i2 = jnp.min(jnp.where(rest == m2, lane, LANES), axis=1, keepdims=True)
        e2 = jnp.exp(m2 - m1)
        den = 1.0 + e2
        gate_sc[...] = jnp.where(lane == i1, 1.0 / den, 0.0) + jnp.where(lane == i2, e2 / den, 0.0)
        routed = jnp.where((lane == i1) | (lane == i2), 1.0, 0.0)
        key = jnp.where(routed > 0.0, _dot(tri_ref[...], routed.astype(BF16)), -1.0)
        key_sc[...] = key
        keyt_sc[...] = key.T

    sel = lane == e
    kcol = jnp.max(jnp.where(sel, key_sc[...], -1.0), axis=1, keepdims=True)
    wcol = jnp.sum(jnp.where(sel, gate_sc[...], 0.0), axis=1, keepdims=True)
    krow = keyt_sc[pl.ds(e, 1), :]
    count = jnp.sum(jnp.where(kcol >= 0.0, 1.0, 0.0)).astype(jnp.int32)

    def chunk(rows, ci, carry):
        base = (ci * rows).astype(F32)
        row_id = lax.broadcasted_iota(jnp.int32, (rows, tm), 0).astype(F32)
        col_id = lax.broadcasted_iota(jnp.int32, (tm, rows), 1).astype(F32)
        gather = jnp.where(krow - base == row_id, 1.0, 0.0).astype(BF16)
        xg = _dot(gather, h_sc[...]).astype(BF16)
        gte = _dot(xg, wg_ref[0])
        up = _dot(xg, wu_ref[0])
        act = (gte * _sigmoid(gte)) * up
        y = _dot(act.astype(BF16), wd_ref[0])
        scatter = jnp.where(kcol - base == col_id, 1.0, 0.0).astype(BF16)
        acc_sc[...] += wcol * _dot(scatter, y.astype(BF16))
        return carry

    @pl.when((count > 0) & (count <= MOE_CHUNK_SMALL))
    def _():
        chunk(MOE_CHUNK_SMALL, jnp.int32(0), 0)

    n_chunks = jnp.where(count > MOE_CHUNK_SMALL, (count + (MOE_CHUNK - 1)) // MOE_CHUNK, 0)
    lax.fori_loop(0, n_chunks, functools.partial(chunk, MOE_CHUNK), 0)

    @pl.when(e == pl.num_programs(1) - 1)
    def _():
        out = acc_sc[...]
        o_ref[...] = _final_norm(out, gf_ref) if final_norm else out


def _moe(x2, g, router_w, wg, wu, wd, g_final, final_norm, tm=MOE_TILE):
    T = x2.shape[0]
    rw = _pad_cols(router_w, LANES)
    rw_hi = rw.astype(BF16)
    rw_lo = (rw - rw_hi.astype(F32)).astype(BF16)
    rw2 = jnp.stack([rw_hi, rw_lo])
    tri = jnp.asarray(np.tril(np.ones((tm, tm), np.float32), -1)).astype(BF16)
    return pl.pallas_call(
        functools.partial(_moe_kernel, final_norm=final_norm, tm=tm),
        grid=(T // tm, N_EXPERTS),
        in_specs=[pl.BlockSpec((tm, D_MODEL), lambda i, e: (i, 0)), pl.BlockSpec((1, D_MODEL), lambda i, e: (0, 0)),
                  pl.BlockSpec((2, D_MODEL, LANES), lambda i, e: (0, 0, 0)),
                  pl.BlockSpec((tm, tm), lambda i, e: (0, 0)),
                  pl.BlockSpec((1, D_MODEL, D_FF_EXPERT), lambda i, e: (e, 0, 0)),
                  pl.BlockSpec((1, D_MODEL, D_FF_EXPERT), lambda i, e: (e, 0, 0)),
                  pl.BlockSpec((1, D_FF_EXPERT, D_MODEL), lambda i, e: (e, 0, 0)),
                  pl.BlockSpec((1, D_MODEL), lambda i, e: (0, 0))],
        out_specs=pl.BlockSpec((tm, D_MODEL), lambda i, e: (i, 0)),
        out_shape=jax.ShapeDtypeStruct((T, D_MODEL), F32),
        scratch_shapes=[pltpu.VMEM((tm, D_MODEL), BF16), pltpu.VMEM((tm, LANES), F32),
                        pltpu.VMEM((tm, LANES), F32), pltpu.VMEM((LANES, tm), F32),
                        pltpu.VMEM((tm, D_MODEL), F32)],
        compiler_params=_cparams(("parallel", "arbitrary")),
        name="moe_routed",
    )(x2, g.reshape(1, -1), rw2, tri, wg.astype(BF16), wu.astype(BF16), wd.astype(BF16), g_final.reshape(1, -1))


def _mixer(x2, batch, seq, tables, onehot, norm_g, w_in, conv_w, conv_b, wa, ba, wx, bx, lam,
           q_norm, w_uq, kv_norm, w_ukv, cmp_pos, cmp_w1, cmp_w2, gate_b, w_out):
    w_cat, w_t = _prep_w_in(w_in)
    lru_xy, mla_dn, qn, kc, vc, ksw, vt_sw, gates_t = _in_proj(x2, norm_g, w_cat, w_t, tables, gate_b, batch, seq)
    a_out = _lru(lru_xy, conv_w, conv_b, wa, ba, wx, bx, lam, batch, seq)

    q_m, k_m, vt_m = _mla_prep(mla_dn, q_norm, w_uq, kv_norm, w_ukv, tables, batch, seq)
    b3 = lambda t: t.reshape(batch, seq, t.shape[-1])
    heads = tuple(range(N_ATT_HEADS))
    b_out = _flash(b3(q_m), b3(k_m), vt_m, mode="causal", batch=batch, seq=seq, per_head_kv=True,
                   k_width=N_ATT_HEADS * LANES, k_col=0, vt_rows=N_ATT_HEADS * HEAD_V, vt_row=0, out_dtype=BF16,
                   tk=LONG_KEY_TILE)

    kcmp, vcmp_t = _compress(kc, vc, cmp_pos, cmp_w1, cmp_w2, batch, seq)
    qn3 = b3(qn)
    o_c, selbias = _select(qn3, kcmp, vcmp_t, gates_t, batch, seq)
    ksw3 = b3(ksw)
    nsa_kw = dict(batch=batch, seq=seq, per_head_kv=False, k_width=LANES,
                  vt_rows=NSA_KV_HEADS * HEAD_V, out_dtype=F32, gates_t=gates_t)
    o_s = _flash(qn3, ksw3, vt_sw, mode="select", k_col=0, vt_row=0, selbias=selbias,
                 onehot=onehot, gate_rows=tuple(3 * h + 1 for h in heads), tk=LONG_KEY_TILE, **nsa_kw)
    o_w = _flash(qn3, ksw3, vt_sw, mode="window", k_col=1, vt_row=1,
                 gate_rows=tuple(3 * h + 2 for h in heads), **nsa_kw)
    T = batch * seq
    flat = lambda t: t.reshape(T, t.shape[-1])
    return _out_proj(x2, a_out, flat(b_out), flat(o_c), flat(o_s), flat(o_w), w_out)


def kernel(x, norm_mix, w_in, lru_conv_w, lru_conv_b, lru_wa, lru_ba, lru_wx, lru_bx, lru_lambda, mla_q_norm, mla_w_uq, mla_kv_norm, mla_w_ukv, nsa_cmp_pos, nsa_cmp_w1, nsa_cmp_w2, nsa_gate_b, w_out, norm_ffn, ffn_w_gate, ffn_w_up, ffn_w_down, router_w, moe_w_gate, moe_w_up, moe_w_down, norm_final):
    batch, seq, _ = x.shape
    depth = norm_mix.shape[0]
    tables = _rope_tables(seq)
    blk_id = np.arange(seq)[:, None] // SEL_LEN
    onehot = jnp.asarray((blk_id == np.arange(LANES)[None, :]).astype(np.float32)).astype(BF16)
    x2 = x.reshape(batch * seq, D_MODEL)
    for l in range(depth):
        x2 = _mixer(x2, batch, seq, tables, onehot, norm_mix[l], w_in[l], lru_conv_w[l], lru_conv_b[l],
                    lru_wa[l], lru_ba[l], lru_wx[l], lru_bx[l], lru_lambda[l], mla_q_norm[l], mla_w_uq[l],
                    mla_kv_norm[l], mla_w_ukv[l], nsa_cmp_pos[l], nsa_cmp_w1[l], nsa_cmp_w2[l],
                    nsa_gate_b[l], w_out[l])
        last = l == depth - 1
        j = l // 2
        if l % 2 == 0:
            x2 = _ffn(x2, norm_ffn[l], ffn_w_gate[j], ffn_w_up[j], ffn_w_down[j], norm_final, last)
        else:
            x2 = _moe(x2, norm_ffn[l], router_w[j], moe_w_gate[j], moe_w_up[j], moe_w_down[j], norm_final, last)
    if depth == 0:
        raise ValueError("depth must be positive")
    return x2.reshape(batch, seq, D_MODEL)
```

```python
import functools
import math

import numpy as np
import jax
import jax.numpy as jnp
from jax import lax
from jax.experimental import pallas as pl
from jax.experimental.pallas import tpu as pltpu

F32 = jnp.float32
BF16 = jnp.bfloat16

D_MODEL = 1024
LRU_WIDTH = 256
LRU_BLOCKS = 4
LRU_BLOCK_W = LRU_WIDTH // LRU_BLOCKS
CONV_WIDTH = 4
LRU_C = 8.0
MLA_HEADS = 6
MLA_NOPE = 64
MLA_ROPE = 32
MLA_V = 64
MLA_Q_RANK = 192
MLA_KV_RANK = 128
NSA_HEADS = 6
NSA_KV_HEADS = 2
NSA_GROUP = NSA_HEADS // NSA_KV_HEADS
NSA_HEAD_DIM = 64
CMP_LEN = 32
CMP_STRIDE = 16
CMP_HIDDEN = 128
SEL_LEN = 64
SEL_TOPK = 16
WINDOW = 512
FORCE_BONUS = 1.0e3
D_FF = 2816
N_EXPERTS = 8
D_FF_EXPERT = 1408
ROPE_THETA = 10000.0
EPS = 1e-6
NEG_INF = -1.0e30

LANES = 128
VMEM_LIMIT = 56 * 1024 * 1024

C_LRU = 0
C_MLA = 512
C_KPE_ROT = 1024
C_QN = 1152
C_QN_ROT = 1536
C_KC = 1920
C_KC_ROT = 2048
C_VC = 2176
C_KS = 2304
C_KS_ROT = 2432
C_KW = 2560
C_KW_ROT = 2688
IN_COLS_PAD = 2816
N_ATT_HEADS = 6
HEAD_V = 64
LONG_KEY_TILE = 1024
SUM_ROWS = 16
GATE_ROWS = 32
LOG2E = 1.4426950408889634


def _cparams(sem, flags=None):
    return pltpu.CompilerParams(dimension_semantics=sem, vmem_limit_bytes=VMEM_LIMIT, flags=flags)


def _gelu_tanh(x):
    return 0.5 * x * (1.0 + jnp.tanh(math.sqrt(2.0 / math.pi) * (x + 0.044715 * (x * x * x))))


def _sigmoid(x):
    return 1.0 / (1.0 + jnp.exp(-x))


def _dot(a, b):
    return jnp.dot(a, b, preferred_element_type=F32)


def _dot_nt(a, b):
    return lax.dot_general(a, b, (((1,), (1,)), ((), ())), preferred_element_type=F32)


def _rot_cols(w, half):
    return jnp.concatenate([-w[:, half:], w[:, :half]], axis=1)


def _pad_cols(w, n):
    return jnp.pad(w, ((0, 0), (0, n - w.shape[1])))


def _prep_w_in(w_in):
    splits = np.cumsum([256, 256, 192, 128, 32, 384, 128, 128, 128, 128, 128, 128, 18])[:-1].tolist()
    (x_l, y_l, q_dn, kv_dn, k_pe, q_n, kc, vc, ks, vs, kw, vw, g_n) = jnp.split(w_in, splits, axis=1)
    z = lambda n: jnp.zeros((w_in.shape[0], n), w_in.dtype)

    def kpe_slot(w):
        return jnp.concatenate([z(64), w, z(32)], axis=1)

    def qn_slots(fn):
        out = []
        for p in range(NSA_GROUP):
            for h in (p, p + NSA_GROUP):
                out.append(fn(q_n[:, h * 64:(h + 1) * 64]))
        return jnp.concatenate(out, axis=1)

    def kv_rot(w):
        return jnp.concatenate([_rot_cols(w[:, :64], 32), _rot_cols(w[:, 64:], 32)], axis=1)

    cols = [x_l, y_l, _pad_cols(q_dn, 256), kv_dn, kpe_slot(k_pe), kpe_slot(_rot_cols(k_pe, 16)),
            qn_slots(lambda w: w), qn_slots(lambda w: _rot_cols(w, 32)),
            kc, kv_rot(kc), vc, ks, kv_rot(ks), kw, kv_rot(kw)]
    w = jnp.concatenate(cols, axis=1)
    assert w.shape[1] == IN_COLS_PAD
    w_t = jnp.concatenate([vs, vw, _pad_cols(g_n, GATE_ROWS)], axis=1).T
    return w.astype(BF16), w_t.astype(BF16)


def _rope_tables(seq):
    pos = jnp.arange(seq, dtype=F32)[:, None]
    inv64 = ROPE_THETA ** (-jnp.arange(32, dtype=F32) * 2.0 / 64)
    inv32 = ROPE_THETA ** (-jnp.arange(16, dtype=F32) * 2.0 / 32)
    a64 = pos * inv64[None, :]
    a32 = pos * inv32[None, :]
    c64, s64 = jnp.cos(a64), jnp.sin(a64)
    c32, s32 = jnp.cos(a32), jnp.sin(a32)
    cos_n = jnp.concatenate([c64, c64, c64, c64], axis=1)
    sin_n = jnp.concatenate([s64, s64, s64, s64], axis=1)
    one = jnp.ones((seq, 64), F32)
    cos_m = jnp.concatenate([one, c32, c32, one[:, :32]], axis=1)
    sin_m = jnp.concatenate([0 * one, s32, s32, 0 * one[:, :32]], axis=1)
    return cos_n, sin_n, cos_m, sin_m


def _in_proj_kernel(x_ref, g_ref, w_ref, wt_ref, cn_ref, sn_ref, cm_ref, sm_ref, gb_ref,
                    qnorm_ref, wqa_ref, wqb_ref, kvnorm_ref, wk_ref, wvt_ref,
                    lru_ref, qm_ref, km_ref, qn_ref, kc_ref, vc_ref, ksw_ref, vt_ref, gt_ref, vtm_ref):
    x = x_ref[...]
    ms = jnp.mean(x * x, axis=-1, keepdims=True)
    h = ((x * lax.rsqrt(ms + EPS)) * g_ref[...]).astype(BF16)

    def proj(c0, n):
        return _dot(h, w_ref[:, c0:c0 + n])

    cn, sn = cn_ref[...], sn_ref[...]
    lru_ref[...] = proj(C_LRU, 512)
    cm, sm = cm_ref[...], sm_ref[...]
    kpe = proj(C_MLA + 384, 128) * cm + proj(C_KPE_ROT, 128) * sm
    _mla_heads(proj(C_MLA, 256), proj(C_MLA + 256, 128), kpe, cm, sm, qnorm_ref, wqa_ref, wqb_ref, kvnorm_ref,
               wk_ref, wvt_ref, qm_ref, km_ref, vtm_ref)
    scale = NSA_HEAD_DIM ** -0.5 * LOG2E
    low_half = lax.broadcasted_iota(jnp.int32, cn.shape, 1) < NSA_HEAD_DIM
    for p in range(NSA_GROUP):
        a = proj(C_QN + p * LANES, LANES)
        b = proj(C_QN_ROT + p * LANES, LANES)
        r = (a * cn + b * sn) * scale
        hi_hd = p + NSA_GROUP
        qn_ref[:, p * LANES:(p + 1) * LANES] = jnp.where(low_half, r, 0.0).astype(BF16)
        qn_ref[:, hi_hd * LANES:(hi_hd + 1) * LANES] = jnp.where(low_half, 0.0, r).astype(BF16)
    kc_ref[...] = proj(C_KC, 128) * cn + proj(C_KC_ROT, 128) * sn
    vc_ref[...] = proj(C_VC, 128)
    ksw_ref[:, 0:128] = (proj(C_KS, 128) * cn + proj(C_KS_ROT, 128) * sn).astype(BF16)
    ksw_ref[:, 128:256] = (proj(C_KW, 128) * cn + proj(C_KW_ROT, 128) * sn).astype(BF16)
    vt_ref[0] = _dot_nt(wt_ref[0:256, :], h).astype(BF16)
    gt_ref[0] = _sigmoid(_dot_nt(wt_ref[256:256 + GATE_ROWS, :], h) + gb_ref[...])


def _in_proj(x2, g, w_cat, w_t, tables, gate_b, q_norm, w_uq, kv_norm, w_ukv, batch, seq, tm=512):
    T = x2.shape[0]
    nt = T // tm
    npos = seq // tm
    cn, sn, cm, sm = tables
    wqa, wqb, wk, wvt = _prep_mla_weights(w_uq, w_ukv)
    row = lambda i: (i, 0)
    fixed = lambda i: (0, 0)
    posmap = lambda i: (i % npos, 0)
    tab_spec = pl.BlockSpec((tm, LANES), posmap)
    whole = lambda a: pl.BlockSpec(a.shape, fixed)
    out_shapes = (
        jax.ShapeDtypeStruct((T, 512), F32),
        jax.ShapeDtypeStruct((T, 768), BF16),
        jax.ShapeDtypeStruct((T, 768), BF16),
        jax.ShapeDtypeStruct((T, 768), BF16),
        jax.ShapeDtypeStruct((T, 128), F32),
        jax.ShapeDtypeStruct((T, 128), F32),
        jax.ShapeDtypeStruct((T, 256), BF16),
    )
    t_shapes = (jax.ShapeDtypeStruct((batch, 256, seq), BF16),
                jax.ShapeDtypeStruct((batch, GATE_ROWS, seq), F32),
                jax.ShapeDtypeStruct((batch, 384, seq), BF16))
    tmap = lambda i: (i // npos, 0, i % npos)
    gate_b_col = jnp.pad(gate_b, (0, GATE_ROWS - gate_b.shape[0])).reshape(GATE_ROWS, 1)
    q_norm_p = _pad_cols(q_norm.reshape(1, -1), 256)
    kv_norm_r = kv_norm.reshape(1, -1)
    return pl.pallas_call(
        _in_proj_kernel,
        grid=(nt,),
        in_specs=[pl.BlockSpec((tm, D_MODEL), row), pl.BlockSpec((1, D_MODEL), fixed),
                  pl.BlockSpec((D_MODEL, IN_COLS_PAD), fixed), pl.BlockSpec((256 + GATE_ROWS, D_MODEL), fixed),
                  tab_spec, tab_spec, tab_spec, tab_spec, pl.BlockSpec((GATE_ROWS, 1), fixed),
                  whole(q_norm_p), whole(wqa), whole(wqb), whole(kv_norm_r), whole(wk), whole(wvt)],
        out_specs=[pl.BlockSpec((tm, s.shape[1]), row) for s in out_shapes]
        + [pl.BlockSpec((1, s.shape[1], tm), tmap) for s in t_shapes],
        out_shape=out_shapes + t_shapes,
        compiler_params=_cparams(("parallel",)),
        name="in_proj",
    )(x2, g.reshape(1, -1), w_cat, w_t, cn, sn, cm, sm, gate_b_col, q_norm_p, wqa, wqb, kv_norm_r, wk, wvt)


def _lru_kernel(xy_ref, cw_ref, cb_ref, wa_ref, ba_ref, wx_ref, bx_ref, lam_ref, out_ref,
                xe_sc, h_sc, *, tt):
    t = pl.program_id(1)

    @pl.when(t == 0)
    def _():
        xe_sc[0:8, :] = jnp.zeros((8, LRU_WIDTH), F32)
        h_sc[...] = jnp.zeros_like(h_sc)

    x = xy_ref[0, :, 0:LRU_WIDTH]
    y = xy_ref[0, :, LRU_WIDTH:2 * LRU_WIDTH]
    xe_sc[8:8 + tt, :] = x
    xc = cb_ref[...]
    for k in range(CONV_WIDTH):
        off = 8 - (CONV_WIDTH - 1) + k
        xc = xc + xe_sc[off:off + tt, :] * cw_ref[k:k + 1, :]
    xe_sc[0:8, :] = x[tt - 8:tt, :]

    xb = xc.astype(BF16)
    r = _sigmoid(_dot(xb, wa_ref[...]) + ba_ref[...])
    i = _sigmoid(_dot(xb, wx_ref[...]) + bx_ref[...])
    log_a = (-LRU_C * r) * jax.nn.softplus(-lam_ref[...])
    a = jnp.exp(log_a)
    b = jnp.sqrt(jnp.tanh(-log_a) * (a * a + 1.0)) * (i * xc)

    row = lax.broadcasted_iota(jnp.int32, (tt, LRU_WIDTH), 0)
    k = 1
    while k < tt:
        keep = row >= k
        a_sh = jnp.where(keep, pltpu.roll(a, k, 0), 1.0)
        b_sh = jnp.where(keep, pltpu.roll(b, k, 0), 0.0)
        b = a * b_sh + b
        a = a * a_sh
        k *= 2
    h = b + a * h_sc[0:1, :]
    h_sc[0:1, :] = h[tt - 1:tt, :]
    out_ref[0] = h * _gelu_tanh(y)


def _block_diag_dense(w):
    n, c, d = w.shape
    z = jnp.zeros((c, d), w.dtype)
    return jnp.concatenate([jnp.concatenate([w[j] if i == j else z for i in range(n)], axis=1) for j in range(n)],
                           axis=0)


def _lru(lru_xy, conv_w, conv_b, wa, ba, wx, bx, lam, batch, seq, tt=512):
    xy = lru_xy.reshape(batch, seq, 512)
    fixed = lambda b, t: (0, 0)
    vec = pl.BlockSpec((1, LRU_WIDTH), fixed)
    mat = pl.BlockSpec((LRU_WIDTH, LRU_WIDTH), fixed)
    out = pl.pallas_call(
        functools.partial(_lru_kernel, tt=tt),
        grid=(batch, seq // tt),
        in_specs=[pl.BlockSpec((1, tt, 512), lambda b, t: (b, t, 0)),
                  pl.BlockSpec((CONV_WIDTH, LRU_WIDTH), fixed), vec, mat, vec, mat, vec, vec],
        out_specs=pl.BlockSpec((1, tt, LRU_WIDTH), lambda b, t: (b, t, 0)),
        out_shape=jax.ShapeDtypeStruct((batch, seq, LRU_WIDTH), F32),
        scratch_shapes=[pltpu.VMEM((tt + 8, LRU_WIDTH), F32), pltpu.VMEM((8, LRU_WIDTH), F32)],
        compiler_params=_cparams(("parallel", "arbitrary")),
        name="rg_lru",
    )(xy, conv_w, conv_b.reshape(1, -1), _block_diag_dense(wa).astype(BF16), ba.reshape(1, -1),
      _block_diag_dense(wx).astype(BF16), bx.reshape(1, -1), lam.reshape(1, -1))
    return out.reshape(batch * seq, LRU_WIDTH)


def _mla_heads(q_dn, kv_dn, kpe, cm, sm, qn_ref, wqa_ref, wqb_ref, kvn_ref, wk_ref, wvt_ref, q_ref, k_ref, vt_ref):
    ms = jnp.sum(q_dn * q_dn, axis=-1, keepdims=True) * (1.0 / MLA_Q_RANK)
    ql = ((q_dn * lax.rsqrt(ms + EPS)) * qn_ref[...]).astype(BF16)
    ms = jnp.mean(kv_dn * kv_dn, axis=-1, keepdims=True)
    kvl = ((kv_dn * lax.rsqrt(ms + EPS)) * kvn_ref[...]).astype(BF16)
    scale = (MLA_NOPE + MLA_ROPE) ** -0.5 * LOG2E
    for hd in range(MLA_HEADS):
        sl = slice(hd * LANES, (hd + 1) * LANES)
        q_ref[:, sl] = ((_dot(ql, wqa_ref[:, sl]) * cm + _dot(ql, wqb_ref[:, sl]) * sm) * scale).astype(BF16)
        k_ref[:, sl] = (_dot(kvl, wk_ref[:, sl]) + kpe).astype(BF16)
    vt_ref[0] = _dot_nt(wvt_ref[...], kvl).astype(BF16)


def _prep_mla_weights(w_uq, w_ukv):
    zq = jnp.zeros((MLA_Q_RANK, 32), w_uq.dtype)
    qa, qb, kk, vv = [], [], [], []
    for h in range(MLA_HEADS):
        wq = w_uq[:, h * 96:(h + 1) * 96]
        nope, ropew = wq[:, :64], wq[:, 64:]
        qa.append(jnp.concatenate([nope, ropew, zq], axis=1))
        qb.append(jnp.concatenate([0 * nope, _rot_cols(ropew, 16), zq], axis=1))
        wkv = w_ukv[:, h * 128:(h + 1) * 128]
        kk.append(_pad_cols(wkv[:, :64], LANES))
        vv.append(wkv[:, 64:])
    pad_rows = lambda w: jnp.pad(w, ((0, 256 - MLA_Q_RANK), (0, 0)))
    return (pad_rows(jnp.concatenate(qa, axis=1)).astype(BF16), pad_rows(jnp.concatenate(qb, axis=1)).astype(BF16),
            jnp.concatenate(kk, axis=1).astype(BF16), jnp.concatenate(vv, axis=1).T.astype(BF16))


PROBE_KEYS = 128
LAG_LIMIT = 8.0


def _flash_kernel(qt_ref, kt_ref, first_ref, last_ref, par_ref, *refs, mode, k_per_head, v_per_head, gate_rows,
                  tq, tk):
    refs = list(refs)
    q_ref, k_ref, vt_ref = refs[:3]
    o_ref, m_st, acc_st = refs[-8:-5]
    s_sc, mx_sc, flag_sm = refs[-5:-3], refs[-3:-1], refs[-1]
    extra = refs[3:-8]
    sb_ref, oh_ref = (extra[0], extra[1]) if mode == "select" else (None, None)
    gt_ref = extra[-1] if gate_rows is not None else None
    step = pl.program_id(1)
    qt = qt_ref[step]
    kt = kt_ref[step]
    rd = par_ref[step]
    wr = 1 - rd

    def visible(nk):
        kpos = kt * tk + lax.broadcasted_iota(jnp.int32, (nk, tq), 0)
        qpos = qt * tq + lax.broadcasted_iota(jnp.int32, (nk, tq), 1)
        dist = qpos - kpos
        return (dist >= 0) & (dist < WINDOW) if mode == "window" else dist >= 0

    def score_tile(j, nk, valid):
        q = q_ref[0, :, j * LANES:(j + 1) * LANES]
        k = k_ref[0, 0:nk, j * LANES:(j + 1) * LANES] if k_per_head else k_ref[0, 0:nk, :]
        if mode == "select":
            q = jnp.concatenate([q, sb_ref[0, j // NSA_GROUP]], axis=1)
            k = jnp.concatenate([k, oh_ref[0:nk, :]], axis=1)
        s = _dot_nt(k, q)
        return s if valid is None else jnp.where(valid, s, NEG_INF)

    def weighted_values(j, nk, p):
        row = (j if v_per_head else j // NSA_GROUP) * HEAD_V
        vt = vt_ref[0, row:row + HEAD_V, 0:nk]
        return _dot(jnp.concatenate([vt, jnp.ones((SUM_ROWS, nk), BF16)], axis=0), p)

    def exact(nk, init):
        valid = visible(nk)

        def scores(j, slot):
            s = score_tile(j, nk, valid)
            s_sc[slot][0:nk, :] = s
            mx_sc[slot][...] = jnp.max(s, axis=0, keepdims=True)

        def accumulate(j, slot):
            if init:
                m_new = mx_sc[slot][...]
            else:
                m_prev = m_st[rd, j]
                m_new = jnp.maximum(m_prev, mx_sc[slot][...])
            p = jnp.exp2((s_sc[slot][0:nk, :] - m_new).astype(BF16))
            upd = weighted_values(j, nk, p)
            acc_st[wr, j] = upd if init else jnp.exp2(m_prev - m_new) * acc_st[rd, j] + upd
            m_st[wr, j] = m_new

        scores(0, 0)
        for j in range(N_ATT_HEADS):
            if j + 1 < N_ATT_HEADS:
                scores(j + 1, (j + 1) % 2)
            accumulate(j, j % 2)

    def lagged(nk, masked, init):
        valid = visible(nk) if masked else None
        rise = None
        for j in range(N_ATT_HEADS):
            if init:
                probe = score_tile(j, PROBE_KEYS, None if valid is None else valid[0:PROBE_KEYS])
                m_prev = jnp.max(probe, axis=0, keepdims=True)
            else:
                m_prev = m_st[rd, j]
            s = score_tile(j, nk, valid)
            p = jnp.exp2((s - m_prev).astype(BF16))
            tile_max = jnp.max(s, axis=0, keepdims=True)
            m_new = jnp.maximum(m_prev, tile_max)
            upd = weighted_values(j, nk, p)
            acc_st[wr, j] = (upd if init else acc_st[rd, j] + upd) * jnp.exp2(m_prev - m_new)
            m_st[wr, j] = m_new
            d = tile_max - m_prev
            rise = d if rise is None else jnp.maximum(rise, d)
        flag_sm[0] = (jnp.max(rise) > LAG_LIMIT).astype(jnp.int32)

    first = first_ref[step] == 1
    later = jnp.logical_not(first)
    if mode == "window":
        pl.when(first)(lambda: exact(tk, True))
        pl.when(later)(lambda: exact(tk, False))
    else:
        first_q, last_q = qt * tq, qt * tq + (tq - 1)
        crosses = kt * tk + (tk - 1) > first_q
        clear = jnp.logical_not(crosses)
        flag_sm[0] = 0
        for init, when in ((True, first), (False, later)):
            pl.when(when & clear)(functools.partial(lagged, tk, False, init))
            if tk > tq:
                half_only = kt * tk + tk // 2 > last_q
                pl.when(when & crosses & half_only)(functools.partial(lagged, tk // 2, True, init))
                pl.when(when & crosses & jnp.logical_not(half_only))(functools.partial(lagged, tk, True, init))
            else:
                pl.when(when & crosses)(functools.partial(lagged, tk, True, init))
        redo = flag_sm[0] == 1
        pl.when(redo & first)(lambda: exact(tk, True))
        pl.when(redo & later)(lambda: exact(tk, False))

    @pl.when(last_ref[step] == 1)
    def _():
        outs = []
        for j in range(N_ATT_HEADS):
            o = acc_st[wr, j, 0:HEAD_V, :] / acc_st[wr, j, HEAD_V:HEAD_V + 1, :]
            if gate_rows is not None:
                o = o * gt_ref[0, gate_rows[j]:gate_rows[j] + 1, :]
            outs.append(o)
        o_ref[0] = jnp.concatenate(outs, axis=0).T.astype(o_ref.dtype)


def _pair_tables(nq, mode, tq, tk):
    qts, kts, first, last, par = [], [], [], [], []
    for qt in range(nq):
        hi = (qt * tq + tq - 1) // tk
        lo = 0 if mode != "window" else max(0, (qt * tq - (WINDOW - 1)) // tk)
        for kt in range(lo, hi + 1):
            qts.append(qt)
            kts.append(kt)
            first.append(1 if kt == lo else 0)
            last.append(1 if kt == hi else 0)
            par.append((kt - lo) % 2)
    arr = lambda v: jnp.asarray(np.array(v, dtype=np.int32))
    return arr(qts), arr(kts), arr(first), arr(last), arr(par)


def _flash(q, k, vt, *, mode, batch, seq, per_head_kv, k_width, k_col, vt_rows, vt_row, out_dtype,
           selbias=None, onehot=None, gate_rows=None, gates_t=None, tq=512, tk=512):
    tk = min(tk, seq)
    nq = seq // tq
    tables = _pair_tables(nq, mode, tq, tk)
    n_steps = int(tables[0].shape[0])
    qw = N_ATT_HEADS * LANES
    ow = N_ATT_HEADS * HEAD_V
    in_specs = [
        pl.BlockSpec((1, tq, qw), lambda b, s, qt, kt, f, l, p: (b, qt[s], 0)),
        pl.BlockSpec((1, tk, k_width), lambda b, s, qt, kt, f, l, p: (b, kt[s], k_col)),
        pl.BlockSpec((1, vt_rows, tk), lambda b, s, qt, kt, f, l, p: (b, vt_row, kt[s])),
    ]
    args = [q, k, vt]
    if mode == "select":
        in_specs.append(pl.BlockSpec((1, NSA_KV_HEADS, tq, LANES), lambda b, s, qt, kt, f, l, p: (b, 0, qt[s], 0)))
        in_specs.append(pl.BlockSpec((tk, LANES), lambda b, s, qt, kt, f, l, p: (kt[s], 0)))
        args += [selbias, onehot]
    if gate_rows is not None:
        in_specs.append(pl.BlockSpec((1, GATE_ROWS, tq), lambda b, s, qt, kt, f, l, p: (b, 0, qt[s])))
        args.append(gates_t)
    kern = functools.partial(_flash_kernel, mode=mode, k_per_head=per_head_kv, v_per_head=per_head_kv,
                             gate_rows=gate_rows, tq=tq, tk=tk)
    return pl.pallas_call(
        kern,
        grid_spec=pltpu.PrefetchScalarGridSpec(
            num_scalar_prefetch=5,
            grid=(batch, n_steps),
            in_specs=in_specs,
            out_specs=pl.BlockSpec((1, tq, ow), lambda b, s, qt, kt, f, l, p: (b, qt[s], 0)),
            scratch_shapes=[pltpu.VMEM((2, N_ATT_HEADS, 1, tq), F32),
                            pltpu.VMEM((2, N_ATT_HEADS, HEAD_V + SUM_ROWS, tq), F32),
                            pltpu.VMEM((tk, tq), F32), pltpu.VMEM((tk, tq), F32),
                            pltpu.VMEM((1, tq), F32), pltpu.VMEM((1, tq), F32),
                            pltpu.SMEM((1,), jnp.int32)],
        ),
        out_shape=jax.ShapeDtypeStruct((batch, seq, ow), out_dtype),
        compiler_params=_cparams(("parallel", "arbitrary")),
        name="flash_" + mode,
    )(*tables, *args)


def _compress_kernel(kc_ref, vc_ref, pos_ref, w1a_ref, w1b_ref, w2_ref, ko_ref, vo_ref, *, nchunk):
    row = lax.broadcasted_iota(jnp.int32, (nchunk, LANES), 0)
    for br, src in enumerate((kc_ref, vc_ref)):
        pa = pb = None
        for l in range(CMP_STRIDE):
            r = src[0, pl.ds(l, nchunk, stride=CMP_STRIDE), :]
            sl = slice(l * LANES, (l + 1) * LANES)
            ta = _dot((r + pos_ref[br, 0:1, sl]).astype(BF16), w1a_ref[br, sl, :])
            tb = _dot((r + pos_ref[br, 1:2, sl]).astype(BF16), w1b_ref[br, sl, :])
            pa = ta if pa is None else pa + ta
            pb = tb if pb is None else pb + tb
        hid = _gelu_tanh(pa + pltpu.roll(pb, nchunk - 1, 0))
        out = jnp.where(row < nchunk - 1, _dot(hid.astype(BF16), w2_ref[br]), 0.0)
        if br == 0:
            ko_ref[0] = out.astype(ko_ref.dtype)
        else:
            vo_ref[0] = out.T.astype(vo_ref.dtype)


def _prep_compress_weights(cmp_pos, cmp_w1, cmp_w2):
    half = CMP_LEN // 2
    pos, w1a, w1b, w2 = [], [], [], []
    for br in range(2):
        p = cmp_pos[br]
        tile = lambda ph: jnp.concatenate([ph, ph], axis=1).reshape(1, half * LANES)
        pos.append(jnp.concatenate([tile(p[:half]), tile(p[half:])], axis=0))
        w = cmp_w1[br].reshape(CMP_LEN, NSA_HEAD_DIM, CMP_HIDDEN)
        z = jnp.zeros_like(w[:half])

        def big(wh):
            g0 = jnp.concatenate([wh, z], axis=1)
            g1 = jnp.concatenate([z, wh], axis=1)
            return jnp.concatenate([g0.reshape(half * LANES, CMP_HIDDEN), g1.reshape(half * LANES, CMP_HIDDEN)], axis=1)

        w1a.append(big(w[:half]))
        w1b.append(big(w[half:]))
        w2.append(_block_diag_dense(jnp.stack([cmp_w2[br], cmp_w2[br]])))
    return (jnp.stack(pos), jnp.stack(w1a).astype(BF16), jnp.stack(w1b).astype(BF16), jnp.stack(w2).astype(BF16))


def _compress(kc, vc, cmp_pos, cmp_w1, cmp_w2, batch, seq):
    nchunk = seq // CMP_STRIDE
    pos, w1a, w1b, w2 = _prep_compress_weights(cmp_pos, cmp_w1, cmp_w2)
    blk = pl.BlockSpec((1, seq, LANES), lambda b: (b, 0, 0))
    full3 = lambda shape: pl.BlockSpec(shape, lambda b: (0, 0, 0))
    return pl.pallas_call(
        functools.partial(_compress_kernel, nchunk=nchunk),
        grid=(batch,),
        in_specs=[blk, blk, full3(pos.shape), full3(w1a.shape), full3(w1b.shape), full3(w2.shape)],
        out_specs=[pl.BlockSpec((1, nchunk, LANES), lambda b: (b, 0, 0)),
                   pl.BlockSpec((1, LANES, nchunk), lambda b: (b, 0, 0))],
        out_shape=(jax.ShapeDtypeStruct((batch, nchunk, LANES), BF16),
                   jax.ShapeDtypeStruct((batch, LANES, nchunk), BF16)),
        compiler_params=_cparams(("parallel",)),
        name="nsa_compress",
    )(kc.reshape(batch, seq, LANES), vc.reshape(batch, seq, LANES), pos, w1a, w1b, w2)


def _select_kernel(q_ref, kc_ref, vct_ref, ovt_ref, gt_ref, oc_ref, sb_ref, ot_sc, imp_sc, *, tq, nchunk, variants):
    nsel = LANES
    qt = pl.program_id(1)
    col_ok = (qt * tq + lax.broadcasted_iota(jnp.int32, (1, tq), 1) >= CMP_LEN - 1).astype(F32)

    def attend(nc):
        kc = kc_ref[0, 0:nc, :]
        qpos = qt * tq + lax.broadcasted_iota(jnp.int32, (nc, tq), 1)
        cend = lax.broadcasted_iota(jnp.int32, (nc, tq), 0) * CMP_STRIDE + (CMP_LEN - 1)
        cvalid = cend <= qpos
        ovt = ovt_ref[:, 0:nc]
        for g in range(NSA_KV_HEADS):
            vct = vct_ref[0, g * HEAD_V:(g + 1) * HEAD_V, 0:nc]
            psum = jnp.zeros((nc, tq), F32)
            for pj in range(NSA_GROUP):
                hd = g * NSA_GROUP + pj
                q = q_ref[0, :, hd * LANES:(hd + 1) * LANES]
                s = jnp.where(cvalid, _dot_nt(kc, q), NEG_INF)
                m = jnp.max(s, axis=0, keepdims=True)
                e = jnp.exp2(s - m)
                p = e * (col_ok / jnp.maximum(jnp.sum(e, axis=0, keepdims=True), 1e-30))
                ot_sc[hd * HEAD_V:(hd + 1) * HEAD_V, :] = _dot(vct, p.astype(BF16)) * gt_ref[0, 3 * hd:3 * hd + 1, :]
                psum = psum + p
            hi = psum.astype(BF16)
            r1 = psum - hi.astype(F32)
            mid = r1.astype(BF16)
            lo = (r1 - mid.astype(F32)).astype(BF16)
            imp_sc[g] = _dot(ovt, hi) + _dot(ovt, mid) + _dot(ovt, lo)

    needed = (qt * tq + tq - CMP_STRIDE) // CMP_STRIDE
    lo_bound = 0
    for nc in variants:
        pl.when((needed > lo_bound) & (needed <= nc) if nc != variants[-1] else needed > lo_bound)(
            functools.partial(attend, nc))
        lo_bound = nc

    blk = lax.broadcasted_iota(jnp.int32, (nsel, tq), 0)
    blkf = blk.astype(F32)
    qp = qt * tq + lax.broadcasted_iota(jnp.int32, (nsel, tq), 1)
    valid = blk * SEL_LEN <= qp
    forced = (blk == 0) | (blk == qp // SEL_LEN)
    for g in range(NSA_KV_HEADS):
        score = jnp.where(valid, imp_sc[g], -1.0)
        score0 = jnp.where(forced, score + FORCE_BONUS, score)
        score = score0
        for _ in range(min(SEL_TOPK, nsel)):
            m = jnp.max(score, axis=0, keepdims=True)
            idx = jnp.min(jnp.where(score == m, blkf, float(nsel)), axis=0, keepdims=True)
            score = jnp.where(blkf == idx, -jnp.inf, score)
        bias = jnp.where(score0 >= 0.0, jnp.where(score == -jnp.inf, 0.0, NEG_INF), NEG_INF)
        sb_ref[0, g] = bias.T.astype(sb_ref.dtype)
    oc_ref[0] = ot_sc[...].T


def _overlap_matrix_t(nchunk, nsel):
    n = np.arange(nchunk)[None, :]
    s = np.arange(nsel)[:, None]
    cs = n * CMP_STRIDE
    ov = (cs < s * SEL_LEN + SEL_LEN) & (cs + CMP_LEN - 1 >= s * SEL_LEN) & (n < nchunk - 1)
    return jnp.asarray(ov.astype(np.float32)).astype(BF16)


def _select(qn, kcmp, vcmp_t, gates_t, batch, seq, tq=256):
    nchunk = seq // CMP_STRIDE
    nsel = LANES
    assert seq // SEL_LEN <= nsel
    qw = N_ATT_HEADS * LANES
    ow = N_ATT_HEADS * HEAD_V
    ovt = _overlap_matrix_t(nchunk, nsel)
    quarter = nchunk // 4
    variants = tuple(quarter * i for i in range(1, 5)) if quarter % LANES == 0 else (nchunk,)
    return pl.pallas_call(
        functools.partial(_select_kernel, tq=tq, nchunk=nchunk, variants=variants),
        grid=(batch, seq // tq),
        in_specs=[pl.BlockSpec((1, tq, qw), lambda b, t: (b, t, 0)),
                  pl.BlockSpec((1, nchunk, LANES), lambda b, t: (b, 0, 0)),
                  pl.BlockSpec((1, NSA_KV_HEADS * HEAD_V, nchunk), lambda b, t: (b, 0, 0)),
                  pl.BlockSpec((nsel, nchunk), lambda b, t: (0, 0)),
                  pl.BlockSpec((1, GATE_ROWS, tq), lambda b, t: (b, 0, t))],
        out_specs=[pl.BlockSpec((1, tq, ow), lambda b, t: (b, t, 0)),
                   pl.BlockSpec((1, NSA_KV_HEADS, tq, nsel), lambda b, t: (b, 0, t, 0))],
        out_shape=(jax.ShapeDtypeStruct((batch, seq, ow), F32),
                   jax.ShapeDtypeStruct((batch, NSA_KV_HEADS, seq, nsel), BF16)),
        scratch_shapes=[pltpu.VMEM((ow, tq), F32), pltpu.VMEM((NSA_KV_HEADS, nsel, tq), F32)],
        compiler_params=_cparams(("parallel", "parallel")),
        name="nsa_select",
    )(qn, kcmp, vcmp_t, ovt, gates_t)


def _out_proj_kernel(x_ref, a_ref, b_ref, oc_ref, os_ref, ow_ref, wa_ref, wb_ref, wc_ref, o_ref):
    acc = x_ref[...] + _dot(a_ref[...].astype(BF16), wa_ref[...])
    acc = acc + _dot(b_ref[...], wb_ref[...])
    c = (oc_ref[...] + os_ref[...]) + ow_ref[...]
    o_ref[...] = acc + _dot(c.astype(BF16), wc_ref[...])


def _out_proj(x2, a_out, b_out, o_c, o_s, o_w, w_out, tm=512):
    T = x2.shape[0]
    wa = w_out[0:256].astype(BF16)
    wb = w_out[256:640].astype(BF16)
    wc = w_out[640:1024].astype(BF16)
    row = lambda i: (i, 0)
    fixed = lambda i: (0, 0)
    act = lambda n: pl.BlockSpec((tm, n), row)
    return pl.pallas_call(
        _out_proj_kernel,
        grid=(T // tm,),
        in_specs=[act(D_MODEL), act(256), act(384), act(384), act(384), act(384),
                  pl.BlockSpec((256, D_MODEL), fixed), pl.BlockSpec((384, D_MODEL), fixed),
                  pl.BlockSpec((384, D_MODEL), fixed)],
        out_specs=pl.BlockSpec((tm, D_MODEL), row),
        out_shape=jax.ShapeDtypeStruct((T, D_MODEL), F32),
        compiler_params=_cparams(("parallel",)),
        name="out_proj",
    )(x2, a_out, b_out, o_c, o_s, o_w, wa, wb, wc)


def _final_norm(y, gf_ref):
    ms = jnp.mean(y * y, axis=-1, keepdims=True)
    return (y * lax.rsqrt(ms + EPS)) * gf_ref[...]


def _ffn_kernel(x_ref, g_ref, wg_ref, wu_ref, wd_ref, gf_ref, o_ref, h_sc, acc_sc, *, final_norm):
    f = pl.program_id(1)

    @pl.when(f == 0)
    def _():
        x = x_ref[...]
        ms = jnp.mean(x * x, axis=-1, keepdims=True)
        h_sc[...] = ((x * lax.rsqrt(ms + EPS)) * g_ref[...]).astype(BF16)
        acc_sc[...] = x

    h = h_sc[...]
    gte = _dot(h, wg_ref[...])
    up = _dot(h, wu_ref[...])
    act = (gte * _sigmoid(gte)) * up
    acc_sc[...] += _dot(act.astype(BF16), wd_ref[...])

    @pl.when(f == pl.num_programs(1) - 1)
    def _():
        y = acc_sc[...]
        o_ref[...] = _final_norm(y, gf_ref) if final_norm else y


def _ffn(x2, g, wg, wu, wd, g_final, final_norm, tm=512, tf=1408):
    T = x2.shape[0]
    nf = D_FF // tf
    return pl.pallas_call(
        functools.partial(_ffn_kernel, final_norm=final_norm),
        grid=(T // tm, nf),
        in_specs=[pl.BlockSpec((tm, D_MODEL), lambda i, f: (i, 0)), pl.BlockSpec((1, D_MODEL), lambda i, f: (0, 0)),
                  pl.BlockSpec((D_MODEL, tf), lambda i, f: (0, f)), pl.BlockSpec((D_MODEL, tf), lambda i, f: (0, f)),
                  pl.BlockSpec((tf, D_MODEL), lambda i, f: (f, 0)), pl.BlockSpec((1, D_MODEL), lambda i, f: (0, 0))],
        out_specs=pl.BlockSpec((tm, D_MODEL), lambda i, f: (i, 0)),
        out_shape=jax.ShapeDtypeStruct((T, D_MODEL), F32),
        scratch_shapes=[pltpu.VMEM((tm, D_MODEL), BF16), pltpu.VMEM((tm, D_MODEL), F32)],
        compiler_params=_cparams(("parallel", "arbitrary")),
        name="ffn_dense",
    )(x2, g.reshape(1, -1), wg.astype(BF16), wu.astype(BF16), wd.astype(BF16), g_final.reshape(1, -1))


MOE_TILE = 1024
MOE_CHUNK = 384
MOE_CHUNK_SMALL = 256


def _moe_kernel(x_ref, g_ref, rw_ref, tri_ref, wg_ref, wu_ref, wd_ref, gf_ref, o_ref,
                h_sc, gate_sc, key_sc, keyt_sc, acc_sc, *, final_norm, tm):
    e = pl.program_id(1)
    lane = lax.broadcasted_iota(jnp.int32, (tm, LANES), 1)

    @pl.when(e == 0)
    def _():
        x = x_ref[...]
        ms = jnp.mean(x * x, axis=-1, keepdims=True)
        hf = (x * lax.rsqrt(ms + EPS)) * g_ref[...]
        h_sc[...] = hf.astype(BF16)
        acc_sc[...] = x
        h_hi = hf.astype(BF16)
        h_lo = (hf - h_hi.astype(F32)).astype(BF16)
        logits = (_dot(h_hi, rw_ref[0]) + _dot(h_lo, rw_ref[0]) + _dot(h_hi, rw_ref[1]))
        logits = jnp.where(lane < N_EXPERTS, logits, -jnp.inf)
        m1 = jnp.max(logits, axis=1, keepdims=True)
        i1 = jnp.min(jnp.where(logits == m1, lane, LANES), axis=1, keepdims=True)
        rest = jnp.where(lane == i1, -jnp.inf, logits)
        m2 = jnp.max(rest, axis=1, keepdims=True)
        i2 = jnp.min(jnp.where(rest == m2, lane, LANES), axis=1, keepdims=True)
        e2 = jnp.exp(m2 - m1)
        den = 1.0 + e2
        gate_sc[...] = jnp.where(lane == i1, 1.0 / den, 0.0) + jnp.where(lane == i2, e2 / den, 0.0)
        routed = jnp.where((lane == i1) | (lane == i2), 1.0, 0.0)
        key = jnp.where(routed > 0.0, _dot(tri_ref[...], routed.astype(BF16)), -1.0)
        key_sc[...] = key
        keyt_sc[...] = key.T

    sel = lane == e
    kcol = jnp.max(jnp.where(sel, key_sc[...], -1.0), axis=1, keepdims=True)
    wcol = jnp.sum(jnp.where(sel, gate_sc[...], 0.0), axis=1, keepdims=True)
    krow = keyt_sc[pl.ds(e, 1), :]
    count = jnp.sum(jnp.where(kcol >= 0.0, 1.0, 0.0)).astype(jnp.int32)

    def chunk(rows, ci, carry):
        base = (ci * rows).astype(F32)
        row_id = lax.broadcasted_iota(jnp.int32, (rows, tm), 0).astype(F32)
        col_id = lax.broadcasted_iota(jnp.int32, (tm, rows), 1).astype(F32)
        gather = jnp.where(krow - base == row_id, 1.0, 0.0).astype(BF16)
        xg = _dot(gather, h_sc[...]).astype(BF16)
        gte = _dot(xg, wg_ref[0])
        up = _dot(xg, wu_ref[0])
        act = (gte * _sigmoid(gte)) * up
        y = _dot(act.astype(BF16), wd_ref[0])
        scatter = jnp.where(kcol - base == col_id, 1.0, 0.0).astype(BF16)
        acc_sc[...] += wcol * _dot(scatter, y.astype(BF16))
        return carry

    @pl.when((count > 0) & (count <= MOE_CHUNK_SMALL))
    def _():
        chunk(MOE_CHUNK_SMALL, jnp.int32(0), 0)

    n_chunks = jnp.where(count > MOE_CHUNK_SMALL, (count + (MOE_CHUNK - 1)) // MOE_CHUNK, 0)
    lax.fori_loop(0, n_chunks, functools.partial(chunk, MOE_CHUNK), 0)

    @pl.when(e == pl.num_programs(1) - 1)
    def _():
        out = acc_sc[...]
        o_ref[...] = _final_norm(out, gf_ref) if final_norm else out


def _moe(x2, g, router_w, wg, wu, wd, g_final, final_norm, tm=MOE_TILE):
    T = x2.shape[0]
    rw = _pad_cols(router_w, LANES)
    rw_hi = rw.astype(BF16)
    rw_lo = (rw - rw_hi.astype(F32)).astype(BF16)
    rw2 = jnp.stack([rw_hi, rw_lo])
    tri = jnp.asarray(np.tril(np.ones((tm, tm), np.float32), -1)).astype(BF16)
    return pl.pallas_call(
        functools.partial(_moe_kernel, final_norm=final_norm, tm=tm),
        grid=(T // tm, N_EXPERTS),
        in_specs=[pl.BlockSpec((tm, D_MODEL), lambda i, e: (i, 0)), pl.BlockSpec((1, D_MODEL), lambda i, e: (0, 0)),
                  pl.BlockSpec((2, D_MODEL, LANES), lambda i, e: (0, 0, 0)),
                  pl.BlockSpec((tm, tm), lambda i, e: (0, 0)),
                  pl.BlockSpec((1, D_MODEL, D_FF_EXPERT), lambda i, e: (e, 0, 0)),
                  pl.BlockSpec((1, D_MODEL, D_FF_EXPERT), lambda i, e: (e, 0, 0)),
                  pl.BlockSpec((1, D_FF_EXPERT, D_MODEL), lambda i, e: (e, 0, 0)),
                  pl.BlockSpec((1, D_MODEL), lambda i, e: (0, 0))],
        out_specs=pl.BlockSpec((tm, D_MODEL), lambda i, e: (i, 0)),
        out_shape=jax.ShapeDtypeStruct((T, D_MODEL), F32),
        scratch_shapes=[pltpu.VMEM((tm, D_MODEL), BF16), pltpu.VMEM((tm, LANES), F32),
                        pltpu.VMEM((tm, LANES), F32), pltpu.VMEM((LANES, tm), F32),
                        pltpu.VMEM((tm, D_MODEL), F32)],
        compiler_params=_cparams(("parallel", "arbitrary")),
        name="moe_routed",
    )(x2, g.reshape(1, -1), rw2, tri, wg.astype(BF16), wu.astype(BF16), wd.astype(BF16), g_final.reshape(1, -1))


def _mixer(x2, batch, seq, tables, onehot, norm_g, w_in, conv_w, conv_b, wa, ba, wx, bx, lam,
           q_norm, w_uq, kv_norm, w_ukv, cmp_pos, cmp_w1, cmp_w2, gate_b, w_out):
    w_cat, w_t = _prep_w_in(w_in)
    lru_xy, q_m, k_m, qn, kc, vc, ksw, vt_sw, gates_t, vt_m = _in_proj(
        x2, norm_g, w_cat, w_t, tables, gate_b, q_norm, w_uq, kv_norm, w_ukv, batch, seq)
    a_out = _lru(lru_xy, conv_w, conv_b, wa, ba, wx, bx, lam, batch, seq)
    b3 = lambda t: t.reshape(batch, seq, t.shape[-1])
    heads = tuple(range(N_ATT_HEADS))
    b_out = _flash(b3(q_m), b3(k_m), vt_m, mode="causal", batch=batch, seq=seq, per_head_kv=True,
                   k_width=N_ATT_HEADS * LANES, k_col=0, vt_rows=N_ATT_HEADS * HEAD_V, vt_row=0, out_dtype=BF16,
                   tk=LONG_KEY_TILE)

    kcmp, vcmp_t = _compress(kc, vc, cmp_pos, cmp_w1, cmp_w2, batch, seq)
    qn3 = b3(qn)
    o_c, selbias = _select(qn3, kcmp, vcmp_t, gates_t, batch, seq)
    ksw3 = b3(ksw)
    nsa_kw = dict(batch=batch, seq=seq, per_head_kv=False, k_width=LANES,
                  vt_rows=NSA_KV_HEADS * HEAD_V, out_dtype=F32, gates_t=gates_t)
    o_s = _flash(qn3, ksw3, vt_sw, mode="select", k_col=0, vt_row=0, selbias=selbias,
                 onehot=onehot, gate_rows=tuple(3 * h + 1 for h in heads), tk=LONG_KEY_TILE, **nsa_kw)
    o_w = _flash(qn3, ksw3, vt_sw, mode="window", k_col=1, vt_row=1,
                 gate_rows=tuple(3 * h + 2 for h in heads), **nsa_kw)
    T = batch * seq
    flat = lambda t: t.reshape(T, t.shape[-1])
    return _out_proj(x2, a_out, flat(b_out), flat(o_c), flat(o_s), flat(o_w), w_out)


def kernel(x, norm_mix, w_in, lru_conv_w, lru_conv_b, lru_wa, lru_ba, lru_wx, lru_bx, lru_lambda, mla_q_norm, mla_w_uq, mla_kv_norm, mla_w_ukv, nsa_cmp_pos, nsa_cmp_w1, nsa_cmp_w2, nsa_gate_b, w_out, norm_ffn, ffn_w_gate, ffn_w_up, ffn_w_down, router_w, moe_w_gate, moe_w_up, moe_w_down, norm_final):
    batch, seq, _ = x.shape
    depth = norm_mix.shape[0]
    tables = _rope_tables(seq)
    blk_id = np.arange(seq)[:, None] // SEL_LEN
    onehot = jnp.asarray((blk_id == np.arange(LANES)[None, :]).astype(np.float32)).astype(BF16)
    x2 = x.reshape(batch * seq, D_MODEL)
    for l in range(depth):
        x2 = _mixer(x2, batch, seq, tables, onehot, norm_mix[l], w_in[l], lru_conv_w[l], lru_conv_b[l],
                    lru_wa[l], lru_ba[l], lru_wx[l], lru_bx[l], lru_lambda[l], mla_q_norm[l], mla_w_uq[l],
                    mla_kv_norm[l], mla_w_ukv[l], nsa_cmp_pos[l], nsa_cmp_w1[l], nsa_cmp_w2[l],
                    nsa_gate_b[l], w_out[l])
        last = l == depth - 1
        j = l // 2
        if l % 2 == 0:
            x2 = _ffn(x2, norm_ffn[l], ffn_w_gate[j], ffn_w_up[j], ffn_w_down[j], norm_final, last)
        else:
            x2 = _moe(x2, norm_ffn[l], router_w[j], moe_w_gate[j], moe_w_up[j], moe_w_down[j], norm_final, last)
    if depth == 0:
        raise ValueError("depth must be positive")
    return x2.reshape(batch, seq, D_MODEL)
```

```python
import functools
import math

import numpy as np
import jax
import jax.numpy as jnp
from jax import lax
from jax.experimental import pallas as pl
from jax.experimental.pallas import tpu as pltpu

F32 = jnp.float32
BF16 = jnp.bfloat16

D_MODEL = 1024
LRU_WIDTH = 256
LRU_BLOCKS = 4
LRU_BLOCK_W = LRU_WIDTH // LRU_BLOCKS
CONV_WIDTH = 4
LRU_C = 8.0
MLA_HEADS = 6
MLA_NOPE = 64
MLA_ROPE = 32
MLA_V = 64
MLA_Q_RANK = 192
MLA_KV_RANK = 128
NSA_HEADS = 6
NSA_KV_HEADS = 2
NSA_GROUP = NSA_HEADS // NSA_KV_HEADS
NSA_HEAD_DIM = 64
CMP_LEN = 32
CMP_STRIDE = 16
CMP_HIDDEN = 128
SEL_LEN = 64
SEL_TOPK = 16
WINDOW = 512
FORCE_BONUS = 1.0e3
D_FF = 2816
N_EXPERTS = 8
D_FF_EXPERT = 1408
ROPE_THETA = 10000.0
EPS = 1e-6
NEG_INF = -1.0e30

LANES = 128
VMEM_LIMIT = 56 * 1024 * 1024

C_LRU = 0
C_MLA = 512
C_KPE_ROT = 1024
C_QN = 1152
C_QN_ROT = 1536
C_KC = 1920
C_KC_ROT = 2048
C_VC = 2176
C_KS = 2304
C_KS_ROT = 2432
C_KW = 2560
C_KW_ROT = 2688
IN_COLS_PAD = 2816
N_ATT_HEADS = 6
HEAD_V = 64
LONG_KEY_TILE = 1024
SUM_ROWS = 16
GATE_ROWS = 32
LOG2E = 1.4426950408889634


def _cparams(sem, flags=None):
    return pltpu.CompilerParams(dimension_semantics=sem, vmem_limit_bytes=VMEM_LIMIT, flags=flags)


def _gelu_tanh(x):
    return 0.5 * x * (1.0 + jnp.tanh(math.sqrt(2.0 / math.pi) * (x + 0.044715 * (x * x * x))))


def _sigmoid(x):
    return 1.0 / (1.0 + jnp.exp(-x))


def _dot(a, b):
    return jnp.dot(a, b, preferred_element_type=F32)


def _dot_nt(a, b):
    return lax.dot_general(a, b, (((1,), (1,)), ((), ())), preferred_element_type=F32)


def _rot_cols(w, half):
    return jnp.concatenate([-w[:, half:], w[:, :half]], axis=1)


def _pad_cols(w, n):
    return jnp.pad(w, ((0, 0), (0, n - w.shape[1])))


def _prep_w_in(w_in):
    splits = np.cumsum([256, 256, 192, 128, 32, 384, 128, 128, 128, 128, 128, 128, 18])[:-1].tolist()
    (x_l, y_l, q_dn, kv_dn, k_pe, q_n, kc, vc, ks, vs, kw, vw, g_n) = jnp.split(w_in, splits, axis=1)
    z = lambda n: jnp.zeros((w_in.shape[0], n), w_in.dtype)

    def kpe_slot(w):
        return jnp.concatenate([z(64), w, z(32)], axis=1)

    def qn_slots(fn):
        out = []
        for p in range(NSA_GROUP):
            for h in (p, p + NSA_GROUP):
                out.append(fn(q_n[:, h * 64:(h + 1) * 64]))
        return jnp.concatenate(out, axis=1)

    def kv_rot(w):
        return jnp.concatenate([_rot_cols(w[:, :64], 32), _rot_cols(w[:, 64:], 32)], axis=1)

    cols = [x_l, y_l, _pad_cols(q_dn, 256), kv_dn, kpe_slot(k_pe), kpe_slot(_rot_cols(k_pe, 16)),
            qn_slots(lambda w: w), qn_slots(lambda w: _rot_cols(w, 32)),
            kc, kv_rot(kc), vc, ks, kv_rot(ks), kw, kv_rot(kw)]
    w = jnp.concatenate(cols, axis=1)
    assert w.shape[1] == IN_COLS_PAD
    w_t = jnp.concatenate([vs, vw, _pad_cols(g_n, GATE_ROWS)], axis=1).T
    return w.astype(BF16), w_t.astype(BF16)


def _rope_tables(seq):
    pos = jnp.arange(seq, dtype=F32)[:, None]
    inv64 = ROPE_THETA ** (-jnp.arange(32, dtype=F32) * 2.0 / 64)
    inv32 = ROPE_THETA ** (-jnp.arange(16, dtype=F32) * 2.0 / 32)
    a64 = pos * inv64[None, :]
    a32 = pos * inv32[None, :]
    c64, s64 = jnp.cos(a64), jnp.sin(a64)
    c32, s32 = jnp.cos(a32), jnp.sin(a32)
    cos_n = jnp.concatenate([c64, c64, c64, c64], axis=1)
    sin_n = jnp.concatenate([s64, s64, s64, s64], axis=1)
    one = jnp.ones((seq, 64), F32)
    cos_m = jnp.concatenate([one, c32, c32, one[:, :32]], axis=1)
    sin_m = jnp.concatenate([0 * one, s32, s32, 0 * one[:, :32]], axis=1)
    return cos_n, sin_n, cos_m, sin_m


def _in_proj_kernel(x_ref, g_ref, w_ref, wt_ref, cn_ref, sn_ref, cm_ref, sm_ref, gb_ref,
                    qnorm_ref, wqa_ref, wqb_ref, kvnorm_ref, wk_ref, wvt_ref,
                    lru_ref, qm_ref, km_ref, qn_ref, kc_ref, vc_ref, ksw_ref, vt_ref, gt_ref, vtm_ref):
    x = x_ref[...]
    ms = jnp.mean(x * x, axis=-1, keepdims=True)
    h = ((x * lax.rsqrt(ms + EPS)) * g_ref[...]).astype(BF16)

    def proj(c0, n):
        return _dot(h, w_ref[:, c0:c0 + n])

    cn, sn = cn_ref[...], sn_ref[...]
    lru_ref[...] = proj(C_LRU, 512)
    cm, sm = cm_ref[...], sm_ref[...]
    kpe = proj(C_MLA + 384, 128) * cm + proj(C_KPE_ROT, 128) * sm
    _mla_heads(proj(C_MLA, 256), proj(C_MLA + 256, 128), kpe, cm, sm, qnorm_ref, wqa_ref, wqb_ref, kvnorm_ref,
               wk_ref, wvt_ref, qm_ref, km_ref, vtm_ref)
    scale = NSA_HEAD_DIM ** -0.5 * LOG2E
    low_half = lax.broadcasted_iota(jnp.int32, cn.shape, 1) < NSA_HEAD_DIM
    for p in range(NSA_GROUP):
        a = proj(C_QN + p * LANES, LANES)
        b = proj(C_QN_ROT + p * LANES, LANES)
        r = (a * cn + b * sn) * scale
        hi_hd = p + NSA_GROUP
        qn_ref[:, p * LANES:(p + 1) * LANES] = jnp.where(low_half, r, 0.0).astype(BF16)
        qn_ref[:, hi_hd * LANES:(hi_hd + 1) * LANES] = jnp.where(low_half, 0.0, r).astype(BF16)
    kc_ref[...] = proj(C_KC, 128) * cn + proj(C_KC_ROT, 128) * sn
    vc_ref[...] = proj(C_VC, 128)
    ksw_ref[:, 0:128] = (proj(C_KS, 128) * cn + proj(C_KS_ROT, 128) * sn).astype(BF16)
    ksw_ref[:, 128:256] = (proj(C_KW, 128) * cn + proj(C_KW_ROT, 128) * sn).astype(BF16)
    vt_ref[0] = _dot_nt(wt_ref[0:256, :], h).astype(BF16)
    gt_ref[0] = _sigmoid(_dot_nt(wt_ref[256:256 + GATE_ROWS, :], h) + gb_ref[...])


def _in_proj(x2, g, w_cat, w_t, tables, gate_b, q_norm, w_uq, kv_norm, w_ukv, batch, seq, tm=512):
    T = x2.shape[0]
    nt = T // tm
    npos = seq // tm
    cn, sn, cm, sm = tables
    wqa, wqb, wk, wvt = _prep_mla_weights(w_uq, w_ukv)
    row = lambda i: (i, 0)
    fixed = lambda i: (0, 0)
    posmap = lambda i: (i % npos, 0)
    tab_spec = pl.BlockSpec((tm, LANES), posmap)
    whole = lambda a: pl.BlockSpec(a.shape, fixed)
    out_shapes = (
        jax.ShapeDtypeStruct((T, 512), F32),
        jax.ShapeDtypeStruct((T, 768), BF16),
        jax.ShapeDtypeStruct((T, 768), BF16),
        jax.ShapeDtypeStruct((T, 768), BF16),
        jax.ShapeDtypeStruct((T, 128), F32),
        jax.ShapeDtypeStruct((T, 128), F32),
        jax.ShapeDtypeStruct((T, 256), BF16),
    )
    t_shapes = (jax.ShapeDtypeStruct((batch, 256, seq), BF16),
                jax.ShapeDtypeStruct((batch, GATE_ROWS, seq), F32),
                jax.ShapeDtypeStruct((batch, 384, seq), BF16))
    tmap = lambda i: (i // npos, 0, i % npos)
    gate_b_col = jnp.pad(gate_b, (0, GATE_ROWS - gate_b.shape[0])).reshape(GATE_ROWS, 1)
    q_norm_p = _pad_cols(q_norm.reshape(1, -1), 256)
    kv_norm_r = kv_norm.reshape(1, -1)
    return pl.pallas_call(
        _in_proj_kernel,
        grid=(nt,),
        in_specs=[pl.BlockSpec((tm, D_MODEL), row), pl.BlockSpec((1, D_MODEL), fixed),
                  pl.BlockSpec((D_MODEL, IN_COLS_PAD), fixed), pl.BlockSpec((256 + GATE_ROWS, D_MODEL), fixed),
                  tab_spec, tab_spec, tab_spec, tab_spec, pl.BlockSpec((GATE_ROWS, 1), fixed),
                  whole(q_norm_p), whole(wqa), whole(wqb), whole(kv_norm_r), whole(wk), whole(wvt)],
        out_specs=[pl.BlockSpec((tm, s.shape[1]), row) for s in out_shapes]
        + [pl.BlockSpec((1, s.shape[1], tm), tmap) for s in t_shapes],
        out_shape=out_shapes + t_shapes,
        compiler_params=_cparams(("parallel",)),
        name="in_proj",
    )(x2, g.reshape(1, -1), w_cat, w_t, cn, sn, cm, sm, gate_b_col, q_norm_p, wqa, wqb, kv_norm_r, wk, wvt)


def _lru_kernel(xy_ref, cw_ref, cb_ref, wa_ref, ba_ref, wx_ref, bx_ref, lam_ref, out_ref,
                xe_sc, h_sc, *, tt):
    t = pl.program_id(1)

    @pl.when(t == 0)
    def _():
        xe_sc[0:8, :] = jnp.zeros((8, LRU_WIDTH), F32)
        h_sc[...] = jnp.zeros_like(h_sc)

    x = xy_ref[0, :, 0:LRU_WIDTH]
    y = xy_ref[0, :, LRU_WIDTH:2 * LRU_WIDTH]
    xe_sc[8:8 + tt, :] = x
    xc = cb_ref[...]
    for k in range(CONV_WIDTH):
        off = 8 - (CONV_WIDTH - 1) + k
        xc = xc + xe_sc[off:off + tt, :] * cw_ref[k:k + 1, :]
    xe_sc[0:8, :] = x[tt - 8:tt, :]

    xb = xc.astype(BF16)
    r = _sigmoid(_dot(xb, wa_ref[...]) + ba_ref[...])
    i = _sigmoid(_dot(xb, wx_ref[...]) + bx_ref[...])
    log_a = (-LRU_C * r) * jax.nn.softplus(-lam_ref[...])
    a = jnp.exp(log_a)
    b = jnp.sqrt(jnp.tanh(-log_a) * (a * a + 1.0)) * (i * xc)

    row = lax.broadcasted_iota(jnp.int32, (tt, LRU_WIDTH), 0)
    k = 1
    while k < tt:
        keep = row >= k
        a_sh = jnp.where(keep, pltpu.roll(a, k, 0), 1.0)
        b_sh = jnp.where(keep, pltpu.roll(b, k, 0), 0.0)
        b = a * b_sh + b
        a = a * a_sh
        k *= 2
    h = b + a * h_sc[0:1, :]
    h_sc[0:1, :] = h[tt - 1:tt, :]
    out_ref[0] = h * _gelu_tanh(y)


def _block_diag_dense(w):
    n, c, d = w.shape
    z = jnp.zeros((c, d), w.dtype)
    return jnp.concatenate([jnp.concatenate([w[j] if i == j else z for i in range(n)], axis=1) for j in range(n)],
                           axis=0)


def _lru(lru_xy, conv_w, conv_b, wa, ba, wx, bx, lam, batch, seq, tt=512):
    xy = lru_xy.reshape(batch, seq, 512)
    fixed = lambda b, t: (0, 0)
    vec = pl.BlockSpec((1, LRU_WIDTH), fixed)
    mat = pl.BlockSpec((LRU_WIDTH, LRU_WIDTH), fixed)
    out = pl.pallas_call(
        functools.partial(_lru_kernel, tt=tt),
        grid=(batch, seq // tt),
        in_specs=[pl.BlockSpec((1, tt, 512), lambda b, t: (b, t, 0)),
                  pl.BlockSpec((CONV_WIDTH, LRU_WIDTH), fixed), vec, mat, vec, mat, vec, vec],
        out_specs=pl.BlockSpec((1, tt, LRU_WIDTH), lambda b, t: (b, t, 0)),
        out_shape=jax.ShapeDtypeStruct((batch, seq, LRU_WIDTH), F32),
        scratch_shapes=[pltpu.VMEM((tt + 8, LRU_WIDTH), F32), pltpu.VMEM((8, LRU_WIDTH), F32)],
        compiler_params=_cparams(("parallel", "arbitrary")),
        name="rg_lru",
    )(xy, conv_w, conv_b.reshape(1, -1), _block_diag_dense(wa).astype(BF16), ba.reshape(1, -1),
      _block_diag_dense(wx).astype(BF16), bx.reshape(1, -1), lam.reshape(1, -1))
    return out.reshape(batch * seq, LRU_WIDTH)


def _mla_heads(q_dn, kv_dn, kpe, cm, sm, qn_ref, wqa_ref, wqb_ref, kvn_ref, wk_ref, wvt_ref, q_ref, k_ref, vt_ref):
    ms = jnp.sum(q_dn * q_dn, axis=-1, keepdims=True) * (1.0 / MLA_Q_RANK)
    ql = ((q_dn * lax.rsqrt(ms + EPS)) * qn_ref[...]).astype(BF16)
    ms = jnp.mean(kv_dn * kv_dn, axis=-1, keepdims=True)
    kvl = ((kv_dn * lax.rsqrt(ms + EPS)) * kvn_ref[...]).astype(BF16)
    scale = (MLA_NOPE + MLA_ROPE) ** -0.5 * LOG2E
    for hd in range(MLA_HEADS):
        sl = slice(hd * LANES, (hd + 1) * LANES)
        q_ref[:, sl] = ((_dot(ql, wqa_ref[:, sl]) * cm + _dot(ql, wqb_ref[:, sl]) * sm) * scale).astype(BF16)
        k_ref[:, sl] = (_dot(kvl, wk_ref[:, sl]) + kpe).astype(BF16)
    vt_ref[0] = _dot_nt(wvt_ref[...], kvl).astype(BF16)


def _prep_mla_weights(w_uq, w_ukv):
    zq = jnp.zeros((MLA_Q_RANK, 32), w_uq.dtype)
    qa, qb, kk, vv = [], [], [], []
    for h in range(MLA_HEADS):
        wq = w_uq[:, h * 96:(h + 1) * 96]
        nope, ropew = wq[:, :64], wq[:, 64:]
        qa.append(jnp.concatenate([nope, ropew, zq], axis=1))
        qb.append(jnp.concatenate([0 * nope, _rot_cols(ropew, 16), zq], axis=1))
        wkv = w_ukv[:, h * 128:(h + 1) * 128]
        kk.append(_pad_cols(wkv[:, :64], LANES))
        vv.append(wkv[:, 64:])
    pad_rows = lambda w: jnp.pad(w, ((0, 256 - MLA_Q_RANK), (0, 0)))
    return (pad_rows(jnp.concatenate(qa, axis=1)).astype(BF16), pad_rows(jnp.concatenate(qb, axis=1)).astype(BF16),
            jnp.concatenate(kk, axis=1).astype(BF16), jnp.concatenate(vv, axis=1).T.astype(BF16))


PROBE_KEYS = 128
LAG_LIMIT = 8.0


def _flash_kernel(qt_ref, kt_ref, first_ref, last_ref, par_ref, *refs, mode, k_per_head, v_per_head, gate_rows,
                  tq, tk):
    refs = list(refs)
    q_ref, k_ref, vt_ref = refs[:3]
    o_ref, m_st, acc_st = refs[-8:-5]
    s_sc, mx_sc, flag_sm = refs[-5:-3], refs[-3:-1], refs[-1]
    extra = refs[3:-8]
    sb_ref, oh_ref = (extra[0], extra[1]) if mode == "select" else (None, None)
    gt_ref = extra[-1] if gate_rows is not None else None
    step = pl.program_id(1)
    qt = qt_ref[step]
    kt = kt_ref[step]
    rd = par_ref[step]
    wr = 1 - rd

    def visible(nk):
        kpos = kt * tk + lax.broadcasted_iota(jnp.int32, (nk, tq), 0)
        qpos = qt * tq + lax.broadcasted_iota(jnp.int32, (nk, tq), 1)
        dist = qpos - kpos
        return (dist >= 0) & (dist < WINDOW) if mode == "window" else dist >= 0

    def score_tile(j, nk, valid):
        q = q_ref[0, :, j * LANES:(j + 1) * LANES]
        k = k_ref[0, 0:nk, j * LANES:(j + 1) * LANES] if k_per_head else k_ref[0, 0:nk, :]
        if mode == "select":
            q = jnp.concatenate([q, sb_ref[0, j // NSA_GROUP]], axis=1)
            k = jnp.concatenate([k, oh_ref[0:nk, :]], axis=1)
        s = _dot_nt(k, q)
        return s if valid is None else jnp.where(valid, s, NEG_INF)

    def weighted_values(j, nk, p):
        row = (j if v_per_head else j // NSA_GROUP) * HEAD_V
        vt = vt_ref[0, row:row + HEAD_V, 0:nk]
        return _dot(jnp.concatenate([vt, jnp.ones((SUM_ROWS, nk), BF16)], axis=0), p)

    def exact(nk, init):
        valid = visible(nk)

        def scores(j, slot):
            s = score_tile(j, nk, valid)
            s_sc[slot][0:nk, :] = s
            mx_sc[slot][...] = jnp.max(s, axis=0, keepdims=True)

        def accumulate(j, slot):
            if init:
                m_new = mx_sc[slot][...]
            else:
                m_prev = m_st[rd, j]
                m_new = jnp.maximum(m_prev, mx_sc[slot][...])
            p = jnp.exp2((s_sc[slot][0:nk, :] - m_new).astype(BF16))
            upd = weighted_values(j, nk, p)
            acc_st[wr, j] = upd if init else jnp.exp2(m_prev - m_new) * acc_st[rd, j] + upd
            m_st[wr, j] = m_new

        scores(0, 0)
        for j in range(N_ATT_HEADS):
            if j + 1 < N_ATT_HEADS:
                scores(j + 1, (j + 1) % 2)
            accumulate(j, j % 2)

    def lagged(nk, masked, init):
        valid = visible(nk) if masked else None
        rise = None
        for j in range(N_ATT_HEADS):
            if init:
                probe = score_tile(j, PROBE_KEYS, None if valid is None else valid[0:PROBE_KEYS])
                m_prev = jnp.max(probe, axis=0, keepdims=True)
            else:
                m_prev = m_st[rd, j]
            s = score_tile(j, nk, valid)
            p = jnp.exp2((s - m_prev).astype(BF16))
            tile_max = jnp.max(s, axis=0, keepdims=True)
            m_new = jnp.maximum(m_prev, tile_max)
            upd = weighted_values(j, nk, p)
            acc_st[wr, j] = (upd if init else acc_st[rd, j] + upd) * jnp.exp2(m_prev - m_new)
            m_st[wr, j] = m_new
            d = tile_max - m_prev
            rise = d if rise is None else jnp.maximum(rise, d)
        flag_sm[0] = (jnp.max(rise) > LAG_LIMIT).astype(jnp.int32)

    first = first_ref[step] == 1
    later = jnp.logical_not(first)
    if mode == "window":
        flag_sm[0] = 0
        pl.when(first)(functools.partial(lagged, tk, True, True))
        pl.when(later)(functools.partial(lagged, tk, True, False))
        redo = flag_sm[0] == 1
        pl.when(redo & first)(lambda: exact(tk, True))
        pl.when(redo & later)(lambda: exact(tk, False))
    else:
        first_q, last_q = qt * tq, qt * tq + (tq - 1)
        crosses = kt * tk + (tk - 1) > first_q
        clear = jnp.logical_not(crosses)
        flag_sm[0] = 0
        for init, when in ((True, first), (False, later)):
            pl.when(when & clear)(functools.partial(lagged, tk, False, init))
            if tk > tq:
                half_only = kt * tk + tk // 2 > last_q
                pl.when(when & crosses & half_only)(functools.partial(lagged, tk // 2, True, init))
                pl.when(when & crosses & jnp.logical_not(half_only))(functools.partial(lagged, tk, True, init))
            else:
                pl.when(when & crosses)(functools.partial(lagged, tk, True, init))
        redo = flag_sm[0] == 1
        pl.when(redo & first)(lambda: exact(tk, True))
        pl.when(redo & later)(lambda: exact(tk, False))

    @pl.when(last_ref[step] == 1)
    def _():
        outs = []
        for j in range(N_ATT_HEADS):
            o = acc_st[wr, j, 0:HEAD_V, :] / acc_st[wr, j, HEAD_V:HEAD_V + 1, :]
            if gate_rows is not None:
                o = o * gt_ref[0, gate_rows[j]:gate_rows[j] + 1, :]
            outs.append(o)
        o_ref[0] = jnp.concatenate(outs, axis=0).T.astype(o_ref.dtype)


def _pair_tables(nq, mode, tq, tk):
    qts, kts, first, last, par = [], [], [], [], []
    for qt in range(nq):
        hi = (qt * tq + tq - 1) // tk
        lo = 0 if mode != "window" else max(0, (qt * tq - (WINDOW - 1)) // tk)
        order = list(range(lo, hi + 1))
        if mode == "window":
            order.reverse()
        for n, kt in enumerate(order):
            qts.append(qt)
            kts.append(kt)
            first.append(1 if n == 0 else 0)
            last.append(1 if n == len(order) - 1 else 0)
            par.append(n % 2)
    arr = lambda v: jnp.asarray(np.array(v, dtype=np.int32))
    return arr(qts), arr(kts), arr(first), arr(last), arr(par)


def _flash(q, k, vt, *, mode, batch, seq, per_head_kv, k_width, k_col, vt_rows, vt_row, out_dtype,
           selbias=None, onehot=None, gate_rows=None, gates_t=None, tq=512, tk=512):
    tk = min(tk, seq)
    nq = seq // tq
    tables = _pair_tables(nq, mode, tq, tk)
    n_steps = int(tables[0].shape[0])
    qw = N_ATT_HEADS * LANES
    ow = N_ATT_HEADS * HEAD_V
    in_specs = [
        pl.BlockSpec((1, tq, qw), lambda b, s, qt, kt, f, l, p: (b, qt[s], 0)),
        pl.BlockSpec((1, tk, k_width), lambda b, s, qt, kt, f, l, p: (b, kt[s], k_col)),
        pl.BlockSpec((1, vt_rows, tk), lambda b, s, qt, kt, f, l, p: (b, vt_row, kt[s])),
    ]
    args = [q, k, vt]
    if mode == "select":
        in_specs.append(pl.BlockSpec((1, NSA_KV_HEADS, tq, LANES), lambda b, s, qt, kt, f, l, p: (b, 0, qt[s], 0)))
        in_specs.append(pl.BlockSpec((tk, LANES), lambda b, s, qt, kt, f, l, p: (kt[s], 0)))
        args += [selbias, onehot]
    if gate_rows is not None:
        in_specs.append(pl.BlockSpec((1, GATE_ROWS, tq), lambda b, s, qt, kt, f, l, p: (b, 0, qt[s])))
        args.append(gates_t)
    kern = functools.partial(_flash_kernel, mode=mode, k_per_head=per_head_kv, v_per_head=per_head_kv,
                             gate_rows=gate_rows, tq=tq, tk=tk)
    return pl.pallas_call(
        kern,
        grid_spec=pltpu.PrefetchScalarGridSpec(
            num_scalar_prefetch=5,
            grid=(batch, n_steps),
            in_specs=in_specs,
            out_specs=pl.BlockSpec((1, tq, ow), lambda b, s, qt, kt, f, l, p: (b, qt[s], 0)),
            scratch_shapes=[pltpu.VMEM((2, N_ATT_HEADS, 1, tq), F32),
                            pltpu.VMEM((2, N_ATT_HEADS, HEAD_V + SUM_ROWS, tq), F32),
                            pltpu.VMEM((tk, tq), F32), pltpu.VMEM((tk, tq), F32),
                            pltpu.VMEM((1, tq), F32), pltpu.VMEM((1, tq), F32),
                            pltpu.SMEM((1,), jnp.int32)],
        ),
        out_shape=jax.ShapeDtypeStruct((batch, seq, ow), out_dtype),
        compiler_params=_cparams(("parallel", "arbitrary")),
        name="flash_" + mode,
    )(*tables, *args)


def _compress_kernel(kc_ref, vc_ref, pos_ref, w1a_ref, w1b_ref, w2_ref, ko_ref, vo_ref, *, nchunk):
    row = lax.broadcasted_iota(jnp.int32, (nchunk, LANES), 0)
    for br, src in enumerate((kc_ref, vc_ref)):
        pa = pb = None
        for l in range(CMP_STRIDE):
            r = src[0, pl.ds(l, nchunk, stride=CMP_STRIDE), :]
            sl = slice(l * LANES, (l + 1) * LANES)
            ta = _dot((r + pos_ref[br, 0:1, sl]).astype(BF16), w1a_ref[br, sl, :])
            tb = _dot((r + pos_ref[br, 1:2, sl]).astype(BF16), w1b_ref[br, sl, :])
            pa = ta if pa is None else pa + ta
            pb = tb if pb is None else pb + tb
        hid = _gelu_tanh(pa + pltpu.roll(pb, nchunk - 1, 0))
        out = jnp.where(row < nchunk - 1, _dot(hid.astype(BF16), w2_ref[br]), 0.0)
        if br == 0:
            ko_ref[0] = out.astype(ko_ref.dtype)
        else:
            vo_ref[0] = out.T.astype(vo_ref.dtype)


def _prep_compress_weights(cmp_pos, cmp_w1, cmp_w2):
    half = CMP_LEN // 2
    pos, w1a, w1b, w2 = [], [], [], []
    for br in range(2):
        p = cmp_pos[br]
        tile = lambda ph: jnp.concatenate([ph, ph], axis=1).reshape(1, half * LANES)
        pos.append(jnp.concatenate([tile(p[:half]), tile(p[half:])], axis=0))
        w = cmp_w1[br].reshape(CMP_LEN, NSA_HEAD_DIM, CMP_HIDDEN)
        z = jnp.zeros_like(w[:half])

        def big(wh):
            g0 = jnp.concatenate([wh, z], axis=1)
            g1 = jnp.concatenate([z, wh], axis=1)
            return jnp.concatenate([g0.reshape(half * LANES, CMP_HIDDEN), g1.reshape(half * LANES, CMP_HIDDEN)], axis=1)

        w1a.append(big(w[:half]))
        w1b.append(big(w[half:]))
        w2.append(_block_diag_dense(jnp.stack([cmp_w2[br], cmp_w2[br]])))
    return (jnp.stack(pos), jnp.stack(w1a).astype(BF16), jnp.stack(w1b).astype(BF16), jnp.stack(w2).astype(BF16))


def _compress(kc, vc, cmp_pos, cmp_w1, cmp_w2, batch, seq):
    nchunk = seq // CMP_STRIDE
    pos, w1a, w1b, w2 = _prep_compress_weights(cmp_pos, cmp_w1, cmp_w2)
    blk = pl.BlockSpec((1, seq, LANES), lambda b: (b, 0, 0))
    full3 = lambda shape: pl.BlockSpec(shape, lambda b: (0, 0, 0))
    return pl.pallas_call(
        functools.partial(_compress_kernel, nchunk=nchunk),
        grid=(batch,),
        in_specs=[blk, blk, full3(pos.shape), full3(w1a.shape), full3(w1b.shape), full3(w2.shape)],
        out_specs=[pl.BlockSpec((1, nchunk, LANES), lambda b: (b, 0, 0)),
                   pl.BlockSpec((1, LANES, nchunk), lambda b: (b, 0, 0))],
        out_shape=(jax.ShapeDtypeStruct((batch, nchunk, LANES), BF16),
                   jax.ShapeDtypeStruct((batch, LANES, nchunk), BF16)),
        compiler_params=_cparams(("parallel",)),
        name="nsa_compress",
    )(kc.reshape(batch, seq, LANES), vc.reshape(batch, seq, LANES), pos, w1a, w1b, w2)


def _select_kernel(q_ref, kc_ref, vct_ref, ovt_ref, gt_ref, oc_ref, sb_ref, ot_sc, imp_sc, *, tq, nchunk, variants):
    nsel = LANES
    qt = pl.program_id(1)
    col_ok = (qt * tq + lax.broadcasted_iota(jnp.int32, (1, tq), 1) >= CMP_LEN - 1).astype(F32)

    def attend(nc):
        kc = kc_ref[0, 0:nc, :]
        qpos = qt * tq + lax.broadcasted_iota(jnp.int32, (nc, tq), 1)
        cend = lax.broadcasted_iota(jnp.int32, (nc, tq), 0) * CMP_STRIDE + (CMP_LEN - 1)
        cvalid = cend <= qpos
        ovt = ovt_ref[:, 0:nc]
        for g in range(NSA_KV_HEADS):
            vct = vct_ref[0, g * HEAD_V:(g + 1) * HEAD_V, 0:nc]
            psum = jnp.zeros((nc, tq), F32)
            for pj in range(NSA_GROUP):
                hd = g * NSA_GROUP + pj
                q = q_ref[0, :, hd * LANES:(hd + 1) * LANES]
                s = jnp.where(cvalid, _dot_nt(kc, q), NEG_INF)
                m = jnp.max(s, axis=0, keepdims=True)
                e = jnp.exp2(s - m)
                p = e * (col_ok / jnp.maximum(jnp.sum(e, axis=0, keepdims=True), 1e-30))
                ot_sc[hd * HEAD_V:(hd + 1) * HEAD_V, :] = _dot(vct, p.astype(BF16)) * gt_ref[0, 3 * hd:3 * hd + 1, :]
                psum = psum + p
            hi = psum.astype(BF16)
            r1 = psum - hi.astype(F32)
            mid = r1.astype(BF16)
            lo = (r1 - mid.astype(F32)).astype(BF16)
            imp_sc[g] = _dot(ovt, hi) + _dot(ovt, mid) + _dot(ovt, lo)

    needed = (qt * tq + tq - CMP_STRIDE) // CMP_STRIDE
    lo_bound = 0
    for nc in variants:
        pl.when((needed > lo_bound) & (needed <= nc) if nc != variants[-1] else needed > lo_bound)(
            functools.partial(attend, nc))
        lo_bound = nc

    blk = lax.broadcasted_iota(jnp.int32, (nsel, tq), 0)
    blkf = blk.astype(F32)
    qp = qt * tq + lax.broadcasted_iota(jnp.int32, (nsel, tq), 1)
    valid = blk * SEL_LEN <= qp
    forced = (blk == 0) | (blk == qp // SEL_LEN)
    for g in range(NSA_KV_HEADS):
        score = jnp.where(valid, imp_sc[g], -1.0)
        score0 = jnp.where(forced, score + FORCE_BONUS, score)
        score = score0
        for _ in range(min(SEL_TOPK, nsel)):
            m = jnp.max(score, axis=0, keepdims=True)
            idx = jnp.min(jnp.where(score == m, blkf, float(nsel)), axis=0, keepdims=True)
            score = jnp.where(blkf == idx, -jnp.inf, score)
        bias = jnp.where(score0 >= 0.0, jnp.where(score == -jnp.inf, 0.0, NEG_INF), NEG_INF)
        sb_ref[0, g] = bias.T.astype(sb_ref.dtype)
    oc_ref[0] = ot_sc[...].T


def _overlap_matrix_t(nchunk, nsel):
    n = np.arange(nchunk)[None, :]
    s = np.arange(nsel)[:, None]
    cs = n * CMP_STRIDE
    ov = (cs < s * SEL_LEN + SEL_LEN) & (cs + CMP_LEN - 1 >= s * SEL_LEN) & (n < nchunk - 1)
    return jnp.asarray(ov.astype(np.float32)).astype(BF16)


def _select(qn, kcmp, vcmp_t, gates_t, batch, seq, tq=512):
    nchunk = seq // CMP_STRIDE
    nsel = LANES
    assert seq // SEL_LEN <= nsel
    qw = N_ATT_HEADS * LANES
    ow = N_ATT_HEADS * HEAD_V
    ovt = _overlap_matrix_t(nchunk, nsel)
    quarter = nchunk // 4
    variants = tuple(quarter * i for i in range(1, 5)) if quarter % LANES == 0 else (nchunk,)
    return pl.pallas_call(
        functools.partial(_select_kernel, tq=tq, nchunk=nchunk, variants=variants),
        grid=(batch, seq // tq),
        in_specs=[pl.BlockSpec((1, tq, qw), lambda b, t: (b, t, 0)),
                  pl.BlockSpec((1, nchunk, LANES), lambda b, t: (b, 0, 0)),
                  pl.BlockSpec((1, NSA_KV_HEADS * HEAD_V, nchunk), lambda b, t: (b, 0, 0)),
                  pl.BlockSpec((nsel, nchunk), lambda b, t: (0, 0)),
                  pl.BlockSpec((1, GATE_ROWS, tq), lambda b, t: (b, 0, t))],
        out_specs=[pl.BlockSpec((1, tq, ow), lambda b, t: (b, t, 0)),
                   pl.BlockSpec((1, NSA_KV_HEADS, tq, nsel), lambda b, t: (b, 0, t, 0))],
        out_shape=(jax.ShapeDtypeStruct((batch, seq, ow), F32),
                   jax.ShapeDtypeStruct((batch, NSA_KV_HEADS, seq, nsel), BF16)),
        scratch_shapes=[pltpu.VMEM((ow, tq), F32), pltpu.VMEM((NSA_KV_HEADS, nsel, tq), F32)],
        compiler_params=_cparams(("parallel", "parallel")),
        name="nsa_select",
    )(qn, kcmp, vcmp_t, ovt, gates_t)


def _out_proj_kernel(x_ref, a_ref, b_ref, oc_ref, os_ref, ow_ref, wa_ref, wb_ref, wc_ref, o_ref):
    acc = x_ref[...] + _dot(a_ref[...].astype(BF16), wa_ref[...])
    acc = acc + _dot(b_ref[...], wb_ref[...])
    c = (oc_ref[...] + os_ref[...]) + ow_ref[...]
    o_ref[...] = acc + _dot(c.astype(BF16), wc_ref[...])


def _out_proj(x2, a_out, b_out, o_c, o_s, o_w, w_out, tm=512):
    T = x2.shape[0]
    wa = w_out[0:256].astype(BF16)
    wb = w_out[256:640].astype(BF16)
    wc = w_out[640:1024].astype(BF16)
    row = lambda i: (i, 0)
    fixed = lambda i: (0, 0)
    act = lambda n: pl.BlockSpec((tm, n), row)
    return pl.pallas_call(
        _out_proj_kernel,
        grid=(T // tm,),
        in_specs=[act(D_MODEL), act(256), act(384), act(384), act(384), act(384),
                  pl.BlockSpec((256, D_MODEL), fixed), pl.BlockSpec((384, D_MODEL), fixed),
                  pl.BlockSpec((384, D_MODEL), fixed)],
        out_specs=pl.BlockSpec((tm, D_MODEL), row),
        out_shape=jax.ShapeDtypeStruct((T, D_MODEL), F32),
        compiler_params=_cparams(("parallel",)),
        name="out_proj",
    )(x2, a_out, b_out, o_c, o_s, o_w, wa, wb, wc)


def _final_norm(y, gf_ref):
    ms = jnp.mean(y * y, axis=-1, keepdims=True)
    return (y * lax.rsqrt(ms + EPS)) * gf_ref[...]


def _ffn_kernel(x_ref, g_ref, wg_ref, wu_ref, wd_ref, gf_ref, o_ref, h_sc, acc_sc, *, final_norm):
    f = pl.program_id(1)

    @pl.when(f == 0)
    def _():
        x = x_ref[...]
        ms = jnp.mean(x * x, axis=-1, keepdims=True)
        h_sc[...] = ((x * lax.rsqrt(ms + EPS)) * g_ref[...]).astype(BF16)
        acc_sc[...] = x

    h = h_sc[...]
    gte = _dot(h, wg_ref[...])
    up = _dot(h, wu_ref[...])
    act = (gte * _sigmoid(gte)) * up
    acc_sc[...] += _dot(act.astype(BF16), wd_ref[...])

    @pl.when(f == pl.num_programs(1) - 1)
    def _():
        y = acc_sc[...]
        o_ref[...] = _final_norm(y, gf_ref) if final_norm else y


def _ffn(x2, g, wg, wu, wd, g_final, final_norm, tm=512, tf=1408):
    T = x2.shape[0]
    nf = D_FF // tf
    return pl.pallas_call(
        functools.partial(_ffn_kernel, final_norm=final_norm),
        grid=(T // tm, nf),
        in_specs=[pl.BlockSpec((tm, D_MODEL), lambda i, f: (i, 0)), pl.BlockSpec((1, D_MODEL), lambda i, f: (0, 0)),
                  pl.BlockSpec((D_MODEL, tf), lambda i, f: (0, f)), pl.BlockSpec((D_MODEL, tf), lambda i, f: (0, f)),
                  pl.BlockSpec((tf, D_MODEL), lambda i, f: (f, 0)), pl.BlockSpec((1, D_MODEL), lambda i, f: (0, 0))],
        out_specs=pl.BlockSpec((tm, D_MODEL), lambda i, f: (i, 0)),
        out_shape=jax.ShapeDtypeStruct((T, D_MODEL), F32),
        scratch_shapes=[pltpu.VMEM((tm, D_MODEL), BF16), pltpu.VMEM((tm, D_MODEL), F32)],
        compiler_params=_cparams(("parallel", "arbitrary")),
        name="ffn_dense",
    )(x2, g.reshape(1, -1), wg.astype(BF16), wu.astype(BF16), wd.astype(BF16), g_final.reshape(1, -1))


MOE_TILE = 1024
MOE_CHUNK = 384
MOE_CHUNK_SMALL = 256


def _moe_kernel(x_ref, g_ref, rw_ref, tri_ref, wg_ref, wu_ref, wd_ref, gf_ref, o_ref,
                h_sc, gate_sc, key_sc, keyt_sc, acc_sc, *, final_norm, tm):
    e = pl.program_id(1)
    lane = lax.broadcasted_iota(jnp.int32, (tm, LANES), 1)

    @pl.when(e == 0)
    def _():
        x = x_ref[...]
        ms = jnp.mean(x * x, axis=-1, keepdims=True)
        hf = (x * lax.rsqrt(ms + EPS)) * g_ref[...]
        h_sc[...] = hf.astype(BF16)
        acc_sc[...] = x
        h_hi = hf.astype(BF16)
        h_lo = (hf - h_hi.astype(F32)).astype(BF16)
        logits = (_dot(h_hi, rw_ref[0]) + _dot(h_lo, rw_ref[0]) + _dot(h_hi, rw_ref[1]))
        logits = jnp.where(lane < N_EXPERTS, logits, -jnp.inf)
        m1 = jnp.max(logits, axis=1, keepdims=True)
        i1 = jnp.min(jnp.where(logits == m1, lane, LANES), axis=1, keepdims=True)
        rest = jnp.where(lane == i1, -jnp.inf, logits)
        m2 = jnp.max(rest, axis=1, keepdims=True)
        i2 = jnp.min(jnp.where(rest == m2, lane, LANES), axis=1, keepdims=True)
        e2 = jnp.exp(m2 - m1)
        den = 1.0 + e2
        gate_sc[...] = jnp.where(lane == i1, 1.0 / den, 0.0) + jnp.where(lane == i2, e2 / den, 0.0)
        routed = jnp.where((lane == i1) | (lane == i2), 1.0, 0.0)
        key = jnp.where(routed > 0.0, _dot(tri_ref[...], routed.astype(BF16)), -1.0)
        key_sc[...] = key
        keyt_sc[...] = key.T

    sel = lane == e
    kcol = jnp.max(jnp.where(sel, key_sc[...], -1.0), axis=1, keepdims=True)
    wcol = jnp.sum(jnp.where(sel, gate_sc[...], 0.0), axis=1, keepdims=True)
    krow = keyt_sc[pl.ds(e, 1), :]
    count = jnp.sum(jnp.where(kcol >= 0.0, 1.0, 0.0)).astype(jnp.int32)

    def chunk(rows, ci, carry):
        base = (ci * rows).astype(F32)
        row_id = lax.broadcasted_iota(jnp.int32, (rows, tm), 0).astype(F32)
        col_id = lax.broadcasted_iota(jnp.int32, (tm, rows), 1).astype(F32)
        gather = jnp.where(krow - base == row_id, 1.0, 0.0).astype(BF16)
        xg = _dot(gather, h_sc[...]).astype(BF16)
        gte = _dot(xg, wg_ref[0])
        up = _dot(xg, wu_ref[0])
        act = (gte * _sigmoid(gte)) * up
        y = _dot(act.astype(BF16), wd_ref[0])
        scatter = jnp.where(kcol - base == col_id, 1.0, 0.0).astype(BF16)
        acc_sc[...] += wcol * _dot(scatter, y.astype(BF16))
        return carry

    @pl.when((count > 0) & (count <= MOE_CHUNK_SMALL))
    def _():
        chunk(MOE_CHUNK_SMALL, jnp.int32(0), 0)

    n_chunks = jnp.where(count > MOE_CHUNK_SMALL, (count + (MOE_CHUNK - 1)) // MOE_CHUNK, 0)
    lax.fori_loop(0, n_chunks, functools.partial(chunk, MOE_CHUNK), 0)

    @pl.when(e == pl.num_programs(1) - 1)
    def _():
        out = acc_sc[...]
        o_ref[...] = _final_norm(out, gf_ref) if final_norm else out


def _moe(x2, g, router_w, wg, wu, wd, g_final, final_norm, tm=MOE_TILE):
    T = x2.shape[0]
    rw = _pad_cols(router_w, LANES)
    rw_hi = rw.astype(BF16)
    rw_lo = (rw - rw_hi.astype(F32)).astype(BF16)
    rw2 = jnp.stack([rw_hi, rw_lo])
    tri = jnp.asarray(np.tril(np.ones((tm, tm), np.float32), -1)).astype(BF16)
    return pl.pallas_call(
        functools.partial(_moe_kernel, final_norm=final_norm, tm=tm),
        grid=(T // tm, N_EXPERTS),
        in_specs=[pl.BlockSpec((tm, D_MODEL), lambda i, e: (i, 0)), pl.BlockSpec((1, D_MODEL), lambda i, e: (0, 0)),
                  pl.BlockSpec((2, D_MODEL, LANES), lambda i, e: (0, 0, 0)),
                  pl.BlockSpec((tm, tm), lambda i, e: (0, 0)),
                  pl.BlockSpec((1, D_MODEL, D_FF_EXPERT), lambda i, e: (e, 0, 0)),
                  pl.BlockSpec((1, D_MODEL, D_FF_EXPERT), lambda i, e: (e, 0, 0)),
                  pl.BlockSpec((1, D_FF_EXPERT, D_MODEL), lambda i, e: (e, 0, 0)),
                  pl.BlockSpec((1, D_MODEL), lambda i, e: (0, 0))],
        out_specs=pl.BlockSpec((tm, D_MODEL), lambda i, e: (i, 0)),
        out_shape=jax.ShapeDtypeStruct((T, D_MODEL), F32),
        scratch_shapes=[pltpu.VMEM((tm, D_MODEL), BF16), pltpu.VMEM((tm, LANES), F32),
                        pltpu.VMEM((tm, LANES), F32), pltpu.VMEM((LANES, tm), F32),
                        pltpu.VMEM((tm, D_MODEL), F32)],
        compiler_params=_cparams(("parallel", "arbitrary")),
        name="moe_routed",
    )(x2, g.reshape(1, -1), rw2, tri, wg.astype(BF16), wu.astype(BF16), wd.astype(BF16), g_final.reshape(1, -1))


def _mixer(x2, batch, seq, tables, onehot, norm_g, w_in, conv_w, conv_b, wa, ba, wx, bx, lam,
           q_norm, w_uq, kv_norm, w_ukv, cmp_pos, cmp_w1, cmp_w2, gate_b, w_out):
    w_cat, w_t = _prep_w_in(w_in)
    lru_xy, q_m, k_m, qn, kc, vc, ksw, vt_sw, gates_t, vt_m = _in_proj(
        x2, norm_g, w_cat, w_t, tables, gate_b, q_norm, w_uq, kv_norm, w_ukv, batch, seq)
    a_out = _lru(lru_xy, conv_w, conv_b, wa, ba, wx, bx, lam, batch, seq)
    b3 = lambda t: t.reshape(batch, seq, t.shape[-1])
    heads = tuple(range(N_ATT_HEADS))
    b_out = _flash(b3(q_m), b3(k_m), vt_m, mode="causal", batch=batch, seq=seq, per_head_kv=True,
                   k_width=N_ATT_HEADS * LANES, k_col=0, vt_rows=N_ATT_HEADS * HEAD_V, vt_row=0, out_dtype=BF16,
                   tk=LONG_KEY_TILE)

    kcmp, vcmp_t = _compress(kc, vc, cmp_pos, cmp_w1, cmp_w2, batch, seq)
    qn3 = b3(qn)
    o_c, selbias = _select(qn3, kcmp, vcmp_t, gates_t, batch, seq)
    ksw3 = b3(ksw)
    nsa_kw = dict(batch=batch, seq=seq, per_head_kv=False, k_width=LANES,
                  vt_rows=NSA_KV_HEADS * HEAD_V, out_dtype=F32, gates_t=gates_t)
    o_s = _flash(qn3, ksw3, vt_sw, mode="select", k_col=0, vt_row=0, selbias=selbias,
                 onehot=onehot, gate_rows=tuple(3 * h + 1 for h in heads), tk=LONG_KEY_TILE, **nsa_kw)
    o_w = _flash(qn3, ksw3, vt_sw, mode="window", k_col=1, vt_row=1,
                 gate_rows=tuple(3 * h + 2 for h in heads), **nsa_kw)
    T = batch * seq
    flat = lambda t: t.reshape(T, t.shape[-1])
    return _out_proj(x2, a_out, flat(b_out), flat(o_c), flat(o_s), flat(o_w), w_out)


def kernel(x, norm_mix, w_in, lru_conv_w, lru_conv_b, lru_wa, lru_ba, lru_wx, lru_bx, lru_lambda, mla_q_norm, mla_w_uq, mla_kv_norm, mla_w_ukv, nsa_cmp_pos, nsa_cmp_w1, nsa_cmp_w2, nsa_gate_b, w_out, norm_ffn, ffn_w_gate, ffn_w_up, ffn_w_down, router_w, moe_w_gate, moe_w_up, moe_w_down, norm_final):
    batch, seq, _ = x.shape
    depth = norm_mix.shape[0]
    tables = _rope_tables(seq)
    blk_id = np.arange(seq)[:, None] // SEL_LEN
    onehot = jnp.asarray((blk_id == np.arange(LANES)[None, :]).astype(np.float32)).astype(BF16)
    x2 = x.reshape(batch * seq, D_MODEL)
    for l in range(depth):
        x2 = _mixer(x2, batch, seq, tables, onehot, norm_mix[l], w_in[l], lru_conv_w[l], lru_conv_b[l],
                    lru_wa[l], lru_ba[l], lru_wx[l], lru_bx[l], lru_lambda[l], mla_q_norm[l], mla_w_uq[l],
                    mla_kv_norm[l], mla_w_ukv[l], nsa_cmp_pos[l], nsa_cmp_w1[l], nsa_cmp_w2[l],
                    nsa_gate_b[l], w_out[l])
        last = l == depth - 1
        j = l // 2
        if l % 2 == 0:
            x2 = _ffn(x2, norm_ffn[l], ffn_w_gate[j], ffn_w_up[j], ffn_w_down[j], norm_final, last)
        else:
            x2 = _moe(x2, norm_ffn[l], router_w[j], moe_w_gate[j], moe_w_up[j], moe_w_down[j], norm_final, last)
    if depth == 0:
        raise ValueError("depth must be positive")
    return x2.reshape(batch, seq, D_MODEL)
```

```python
import functools
import math

import numpy as np
import jax
import jax.numpy as jnp
from jax import lax
from jax.experimental import pallas as pl
from jax.experimental.pallas import tpu as pltpu

F32 = jnp.float32
BF16 = jnp.bfloat16

D_MODEL = 1024
LRU_WIDTH = 256
CONV_WIDTH = 4
LRU_C = 8.0
MLA_HEADS = 6
MLA_NOPE = 64
MLA_ROPE = 32
MLA_Q_RANK = 192
NSA_HEADS = 6
NSA_KV_HEADS = 2
NSA_GROUP = NSA_HEADS // NSA_KV_HEADS
NSA_HEAD_DIM = 64
CMP_LEN = 32
CMP_STRIDE = 16
CMP_HIDDEN = 128
SEL_LEN = 64
SEL_TOPK = 16
WINDOW = 512
FORCE_BONUS = 1.0e3
D_FF = 2816
N_EXPERTS = 8
D_FF_EXPERT = 1408
ROPE_THETA = 10000.0
EPS = 1e-6
NEG_INF = -1.0e30

LANES = 128
VMEM_LIMIT = 56 * 1024 * 1024

C_LRU = 0
C_MLA = 512
C_KPE_ROT = 1024
C_QN = 1152
C_QN_ROT = 1536
C_KC = 1920
C_KC_ROT = 2048
C_VC = 2176
C_KS = 2304
C_KS_ROT = 2432
C_KW = 2560
C_KW_ROT = 2688
IN_COLS_PAD = 2816
N_ATT_HEADS = 6
HEAD_V = 64
LONG_KEY_TILE = 1024
SUM_ROWS = 16
GATE_ROWS = 32
LOG2E = 1.4426950408889634


def _cparams(sem):
    return pltpu.CompilerParams(dimension_semantics=sem, vmem_limit_bytes=VMEM_LIMIT)


def _gelu_tanh(x):
    return 0.5 * x * (1.0 + jnp.tanh(math.sqrt(2.0 / math.pi) * (x + 0.044715 * (x * x * x))))


def _sigmoid(x):
    return 1.0 / (1.0 + jnp.exp(-x))


def _dot(a, b):
    return jnp.dot(a, b, preferred_element_type=F32)


def _dot_nt(a, b):
    return lax.dot_general(a, b, (((1,), (1,)), ((), ())), preferred_element_type=F32)


def _rot_cols(w, half):
    return jnp.concatenate([-w[:, half:], w[:, :half]], axis=1)


def _pad_cols(w, n):
    return jnp.pad(w, ((0, 0), (0, n - w.shape[1])))


def _prep_w_in(w_in):
    splits = np.cumsum([256, 256, 192, 128, 32, 384, 128, 128, 128, 128, 128, 128, 18])[:-1].tolist()
    (x_l, y_l, q_dn, kv_dn, k_pe, q_n, kc, vc, ks, vs, kw, vw, g_n) = jnp.split(w_in, splits, axis=1)
    z = lambda n: jnp.zeros((w_in.shape[0], n), w_in.dtype)

    def kpe_slot(w):
        return jnp.concatenate([z(64), w, z(32)], axis=1)

    def qn_slots(fn):
        out = []
        for p in range(NSA_GROUP):
            for h in (p, p + NSA_GROUP):
                out.append(fn(q_n[:, h * 64:(h + 1) * 64]))
        return jnp.concatenate(out, axis=1)

    def kv_rot(w):
        return jnp.concatenate([_rot_cols(w[:, :64], 32), _rot_cols(w[:, 64:], 32)], axis=1)

    cols = [x_l, y_l, _pad_cols(q_dn, 256), kv_dn, kpe_slot(k_pe), kpe_slot(_rot_cols(k_pe, 16)),
            qn_slots(lambda w: w), qn_slots(lambda w: _rot_cols(w, 32)),
            kc, kv_rot(kc), vc, ks, kv_rot(ks), kw, kv_rot(kw)]
    w = jnp.concatenate(cols, axis=1)
    assert w.shape[1] == IN_COLS_PAD
    w_t = jnp.concatenate([vs, vw, _pad_cols(g_n, GATE_ROWS)], axis=1).T
    return w.astype(BF16), w_t.astype(BF16)


def _rope_tables(seq):
    pos = jnp.arange(seq, dtype=F32)[:, None]
    inv64 = ROPE_THETA ** (-jnp.arange(32, dtype=F32) * 2.0 / 64)
    inv32 = ROPE_THETA ** (-jnp.arange(16, dtype=F32) * 2.0 / 32)
    a64 = pos * inv64[None, :]
    a32 = pos * inv32[None, :]
    c64, s64 = jnp.cos(a64), jnp.sin(a64)
    c32, s32 = jnp.cos(a32), jnp.sin(a32)
    cos_n = jnp.concatenate([c64, c64, c64, c64], axis=1)
    sin_n = jnp.concatenate([s64, s64, s64, s64], axis=1)
    one = jnp.ones((seq, 64), F32)
    cos_m = jnp.concatenate([one, c32, c32, one[:, :32]], axis=1)
    sin_m = jnp.concatenate([0 * one, s32, s32, 0 * one[:, :32]], axis=1)
    return cos_n, sin_n, cos_m, sin_m


def _in_proj_kernel(x_ref, g_ref, w_ref, wt_ref, cn_ref, sn_ref, cm_ref, sm_ref, gb_ref,
                    qnorm_ref, wqa_ref, wqb_ref, kvnorm_ref, wk_ref, wvt_ref,
                    lru_ref, qm_ref, km_ref, qn_ref, kc_ref, vc_ref, ksw_ref, vt_ref, gt_ref, vtm_ref):
    x = x_ref[...]
    ms = jnp.mean(x * x, axis=-1, keepdims=True)
    h = ((x * lax.rsqrt(ms + EPS)) * g_ref[...]).astype(BF16)

    def proj(c0, n):
        return _dot(h, w_ref[:, c0:c0 + n])

    cn, sn = cn_ref[...], sn_ref[...]
    lru_ref[...] = proj(C_LRU, 512)
    cm, sm = cm_ref[...], sm_ref[...]
    kpe = proj(C_MLA + 384, 128) * cm + proj(C_KPE_ROT, 128) * sm
    _mla_heads(proj(C_MLA, 256), proj(C_MLA + 256, 128), kpe, cm, sm, qnorm_ref, wqa_ref, wqb_ref, kvnorm_ref,
               wk_ref, wvt_ref, qm_ref, km_ref, vtm_ref)
    scale = NSA_HEAD_DIM ** -0.5 * LOG2E
    low_half = lax.broadcasted_iota(jnp.int32, cn.shape, 1) < NSA_HEAD_DIM
    for p in range(NSA_GROUP):
        a = proj(C_QN + p * LANES, LANES)
        b = proj(C_QN_ROT + p * LANES, LANES)
        r = (a * cn + b * sn) * scale
        hi_hd = p + NSA_GROUP
        qn_ref[:, p * LANES:(p + 1) * LANES] = jnp.where(low_half, r, 0.0).astype(BF16)
        qn_ref[:, hi_hd * LANES:(hi_hd + 1) * LANES] = jnp.where(low_half, 0.0, r).astype(BF16)
    kc_ref[...] = proj(C_KC, 128) * cn + proj(C_KC_ROT, 128) * sn
    vc_ref[...] = proj(C_VC, 128)
    ksw_ref[:, 0:128] = (proj(C_KS, 128) * cn + proj(C_KS_ROT, 128) * sn).astype(BF16)
    ksw_ref[:, 128:256] = (proj(C_KW, 128) * cn + proj(C_KW_ROT, 128) * sn).astype(BF16)
    vt_ref[0] = _dot_nt(wt_ref[0:256, :], h).astype(BF16)
    gt_ref[0] = _sigmoid(_dot_nt(wt_ref[256:256 + GATE_ROWS, :], h) + gb_ref[...])


def _in_proj(x2, g, w_cat, w_t, tables, gate_b, q_norm, w_uq, kv_norm, w_ukv, batch, seq, tm=512):
    T = x2.shape[0]
    nt = T // tm
    npos = seq // tm
    cn, sn, cm, sm = tables
    wqa, wqb, wk, wvt = _prep_mla_weights(w_uq, w_ukv)
    row = lambda i: (i, 0)
    fixed = lambda i: (0, 0)
    posmap = lambda i: (i % npos, 0)
    tab_spec = pl.BlockSpec((tm, LANES), posmap)
    whole = lambda a: pl.BlockSpec(a.shape, fixed)
    out_shapes = (
        jax.ShapeDtypeStruct((T, 512), F32),
        jax.ShapeDtypeStruct((T, 768), BF16),
        jax.ShapeDtypeStruct((T, 768), BF16),
        jax.ShapeDtypeStruct((T, 768), BF16),
        jax.ShapeDtypeStruct((T, 128), F32),
        jax.ShapeDtypeStruct((T, 128), F32),
        jax.ShapeDtypeStruct((T, 256), BF16),
    )
    t_shapes = (jax.ShapeDtypeStruct((batch, 256, seq), BF16),
                jax.ShapeDtypeStruct((batch, GATE_ROWS, seq), F32),
                jax.ShapeDtypeStruct((batch, 384, seq), BF16))
    tmap = lambda i: (i // npos, 0, i % npos)
    gate_b_col = jnp.pad(gate_b, (0, GATE_ROWS - gate_b.shape[0])).reshape(GATE_ROWS, 1)
    q_norm_p = _pad_cols(q_norm.reshape(1, -1), 256)
    kv_norm_r = kv_norm.reshape(1, -1)
    return pl.pallas_call(
        _in_proj_kernel,
        grid=(nt,),
        in_specs=[pl.BlockSpec((tm, D_MODEL), row), pl.BlockSpec((1, D_MODEL), fixed),
                  pl.BlockSpec((D_MODEL, IN_COLS_PAD), fixed), pl.BlockSpec((256 + GATE_ROWS, D_MODEL), fixed),
                  tab_spec, tab_spec, tab_spec, tab_spec, pl.BlockSpec((GATE_ROWS, 1), fixed),
                  whole(q_norm_p), whole(wqa), whole(wqb), whole(kv_norm_r), whole(wk), whole(wvt)],
        out_specs=[pl.BlockSpec((tm, s.shape[1]), row) for s in out_shapes]
        + [pl.BlockSpec((1, s.shape[1], tm), tmap) for s in t_shapes],
        out_shape=out_shapes + t_shapes,
        compiler_params=_cparams(("parallel",)),
        name="in_proj",
    )(x2, g.reshape(1, -1), w_cat, w_t, cn, sn, cm, sm, gate_b_col, q_norm_p, wqa, wqb, kv_norm_r, wk, wvt)


def _lru_kernel(xy_ref, cw_ref, cb_ref, wa_ref, ba_ref, wx_ref, bx_ref, lam_ref, out_ref,
                xe_sc, h_sc, *, tt):
    t = pl.program_id(1)

    @pl.when(t == 0)
    def _():
        xe_sc[0:8, :] = jnp.zeros((8, LRU_WIDTH), F32)
        h_sc[...] = jnp.zeros_like(h_sc)

    x = xy_ref[0, :, 0:LRU_WIDTH]
    y = xy_ref[0, :, LRU_WIDTH:2 * LRU_WIDTH]
    xe_sc[8:8 + tt, :] = x
    xc = cb_ref[...]
    for k in range(CONV_WIDTH):
        off = 8 - (CONV_WIDTH - 1) + k
        xc = xc + xe_sc[off:off + tt, :] * cw_ref[k:k + 1, :]
    xe_sc[0:8, :] = x[tt - 8:tt, :]

    xb = xc.astype(BF16)
    r = _sigmoid(_dot(xb, wa_ref[...]) + ba_ref[...])
    i = _sigmoid(_dot(xb, wx_ref[...]) + bx_ref[...])
    log_a = (-LRU_C * r) * jax.nn.softplus(-lam_ref[...])
    a = jnp.exp(log_a)
    b = jnp.sqrt(jnp.tanh(-log_a) * (a * a + 1.0)) * (i * xc)

    row = lax.broadcasted_iota(jnp.int32, (tt, LRU_WIDTH), 0)
    k = 1
    while k < tt:
        keep = row >= k
        a_sh = jnp.where(keep, pltpu.roll(a, k, 0), 1.0)
        b_sh = jnp.where(keep, pltpu.roll(b, k, 0), 0.0)
        b = a * b_sh + b
        a = a * a_sh
        k *= 2
    h = b + a * h_sc[0:1, :]
    h_sc[0:1, :] = h[tt - 1:tt, :]
    out_ref[0] = h * _gelu_tanh(y)


def _block_diag_dense(w):
    n, c, d = w.shape
    z = jnp.zeros((c, d), w.dtype)
    return jnp.concatenate([jnp.concatenate([w[j] if i == j else z for i in range(n)], axis=1) for j in range(n)],
                           axis=0)


def _lru(lru_xy, conv_w, conv_b, wa, ba, wx, bx, lam, batch, seq, tt=512):
    xy = lru_xy.reshape(batch, seq, 512)
    fixed = lambda b, t: (0, 0)
    vec = pl.BlockSpec((1, LRU_WIDTH), fixed)
    mat = pl.BlockSpec((LRU_WIDTH, LRU_WIDTH), fixed)
    out = pl.pallas_call(
        functools.partial(_lru_kernel, tt=tt),
        grid=(batch, seq // tt),
        in_specs=[pl.BlockSpec((1, tt, 512), lambda b, t: (b, t, 0)),
                  pl.BlockSpec((CONV_WIDTH, LRU_WIDTH), fixed), vec, mat, vec, mat, vec, vec],
        out_specs=pl.BlockSpec((1, tt, LRU_WIDTH), lambda b, t: (b, t, 0)),
        out_shape=jax.ShapeDtypeStruct((batch, seq, LRU_WIDTH), F32),
        scratch_shapes=[pltpu.VMEM((tt + 8, LRU_WIDTH), F32), pltpu.VMEM((8, LRU_WIDTH), F32)],
        compiler_params=_cparams(("parallel", "arbitrary")),
        name="rg_lru",
    )(xy, conv_w, conv_b.reshape(1, -1), _block_diag_dense(wa).astype(BF16), ba.reshape(1, -1),
      _block_diag_dense(wx).astype(BF16), bx.reshape(1, -1), lam.reshape(1, -1))
    return out.reshape(batch * seq, LRU_WIDTH)


def _mla_heads(q_dn, kv_dn, kpe, cm, sm, qn_ref, wqa_ref, wqb_ref, kvn_ref, wk_ref, wvt_ref, q_ref, k_ref, vt_ref):
    ms = jnp.sum(q_dn * q_dn, axis=-1, keepdims=True) * (1.0 / MLA_Q_RANK)
    ql = ((q_dn * lax.rsqrt(ms + EPS)) * qn_ref[...]).astype(BF16)
    ms = jnp.mean(kv_dn * kv_dn, axis=-1, keepdims=True)
    kvl = ((kv_dn * lax.rsqrt(ms + EPS)) * kvn_ref[...]).astype(BF16)
    scale = (MLA_NOPE + MLA_ROPE) ** -0.5 * LOG2E
    for hd in range(MLA_HEADS):
        sl = slice(hd * LANES, (hd + 1) * LANES)
        q_ref[:, sl] = ((_dot(ql, wqa_ref[:, sl]) * cm + _dot(ql, wqb_ref[:, sl]) * sm) * scale).astype(BF16)
        k_ref[:, sl] = (_dot(kvl, wk_ref[:, sl]) + kpe).astype(BF16)
    vt_ref[0] = _dot_nt(wvt_ref[...], kvl).astype(BF16)


def _prep_mla_weights(w_uq, w_ukv):
    zq = jnp.zeros((MLA_Q_RANK, 32), w_uq.dtype)
    qa, qb, kk, vv = [], [], [], []
    for h in range(MLA_HEADS):
        wq = w_uq[:, h * 96:(h + 1) * 96]
        nope, ropew = wq[:, :64], wq[:, 64:]
        qa.append(jnp.concatenate([nope, ropew, zq], axis=1))
        qb.append(jnp.concatenate([0 * nope, _rot_cols(ropew, 16), zq], axis=1))
        wkv = w_ukv[:, h * 128:(h + 1) * 128]
        kk.append(_pad_cols(wkv[:, :64], LANES))
        vv.append(wkv[:, 64:])
    pad_rows = lambda w: jnp.pad(w, ((0, 256 - MLA_Q_RANK), (0, 0)))
    return (pad_rows(jnp.concatenate(qa, axis=1)).astype(BF16), pad_rows(jnp.concatenate(qb, axis=1)).astype(BF16),
            jnp.concatenate(kk, axis=1).astype(BF16), jnp.concatenate(vv, axis=1).T.astype(BF16))


PROBE_KEYS = 128
LAG_LIMIT = 8.0


def _flash_kernel(qt_ref, kt_ref, first_ref, last_ref, par_ref, *refs, mode, k_per_head, v_per_head, gate_rows,
                  tq, tk):
    refs = list(refs)
    q_ref, k_ref, vt_ref = refs[:3]
    o_ref, m_st, acc_st = refs[-8:-5]
    s_sc, mx_sc, flag_sm = refs[-5:-3], refs[-3:-1], refs[-1]
    extra = refs[3:-8]
    sb_ref, oh_ref = (extra[0], extra[1]) if mode == "select" else (None, None)
    gt_ref = extra[-1] if gate_rows is not None else None
    step = pl.program_id(1)
    qt = qt_ref[step]
    kt = kt_ref[step]
    rd = par_ref[step]
    wr = 1 - rd

    def visible(nk):
        kpos = kt * tk + lax.broadcasted_iota(jnp.int32, (nk, tq), 0)
        qpos = qt * tq + lax.broadcasted_iota(jnp.int32, (nk, tq), 1)
        dist = qpos - kpos
        return (dist >= 0) & (dist < WINDOW) if mode == "window" else dist >= 0

    def score_tile(j, nk, valid):
        q = q_ref[0, :, j * LANES:(j + 1) * LANES]
        k = k_ref[0, 0:nk, j * LANES:(j + 1) * LANES] if k_per_head else k_ref[0, 0:nk, :]
        if mode == "select":
            q = jnp.concatenate([q, sb_ref[0, j // NSA_GROUP]], axis=1)
            k = jnp.concatenate([k, oh_ref[0:nk, :]], axis=1)
        s = _dot_nt(k, q)
        return s if valid is None else jnp.where(valid, s, NEG_INF)

    def weighted_values(j, nk, p):
        row = (j if v_per_head else j // NSA_GROUP) * HEAD_V
        vt = vt_ref[0, row:row + HEAD_V, 0:nk]
        return _dot(jnp.concatenate([vt, jnp.ones((SUM_ROWS, nk), BF16)], axis=0), p)

    def exact(nk, init):
        valid = visible(nk)

        def scores(j, slot):
            s = score_tile(j, nk, valid)
            s_sc[slot][0:nk, :] = s
            mx_sc[slot][...] = jnp.max(s, axis=0, keepdims=True)

        def accumulate(j, slot):
            if init:
                m_new = mx_sc[slot][...]
            else:
                m_prev = m_st[rd, j]
                m_new = jnp.maximum(m_prev, mx_sc[slot][...])
            p = jnp.exp2((s_sc[slot][0:nk, :] - m_new).astype(BF16))
            upd = weighted_values(j, nk, p)
            acc_st[wr, j] = upd if init else jnp.exp2(m_prev - m_new) * acc_st[rd, j] + upd
            m_st[wr, j] = m_new

        scores(0, 0)
        for j in range(N_ATT_HEADS):
            if j + 1 < N_ATT_HEADS:
                scores(j + 1, (j + 1) % 2)
            accumulate(j, j % 2)

    def lagged(nk, masked, init):
        valid = visible(nk) if masked else None
        rise = None
        for j in range(N_ATT_HEADS):
            if init:
                probe = score_tile(j, PROBE_KEYS, None if valid is None else valid[0:PROBE_KEYS])
                m_prev = jnp.max(probe, axis=0, keepdims=True)
            else:
                m_prev = m_st[rd, j]
            s = score_tile(j, nk, valid)
            p = jnp.exp2((s - m_prev).astype(BF16))
            tile_max = jnp.max(s, axis=0, keepdims=True)
            m_new = jnp.maximum(m_prev, tile_max)
            upd = weighted_values(j, nk, p)
            acc_st[wr, j] = (upd if init else acc_st[rd, j] + upd) * jnp.exp2(m_prev - m_new)
            m_st[wr, j] = m_new
            d = tile_max - m_prev
            rise = d if rise is None else jnp.maximum(rise, d)
        flag_sm[0] = (jnp.max(rise) > LAG_LIMIT).astype(jnp.int32)

    first = first_ref[step] == 1
    later = jnp.logical_not(first)
    if mode == "window":
        pl.when(first)(lambda: exact(tk, True))
        pl.when(later)(lambda: exact(tk, False))
    else:
        first_q, last_q = qt * tq, qt * tq + (tq - 1)
        crosses = kt * tk + (tk - 1) > first_q
        clear = jnp.logical_not(crosses)
        flag_sm[0] = 0
        for init, when in ((True, first), (False, later)):
            pl.when(when & clear)(functools.partial(lagged, tk, False, init))
            if tk > tq:
                half_only = kt * tk + tk // 2 > last_q
                pl.when(when & crosses & half_only)(functools.partial(lagged, tk // 2, True, init))
                pl.when(when & crosses & jnp.logical_not(half_only))(functools.partial(lagged, tk, True, init))
            else:
                pl.when(when & crosses)(functools.partial(lagged, tk, True, init))
        redo = flag_sm[0] == 1
        pl.when(redo & first)(lambda: exact(tk, True))
        pl.when(redo & later)(lambda: exact(tk, False))

    @pl.when(last_ref[step] == 1)
    def _():
        outs = []
        for j in range(N_ATT_HEADS):
            o = acc_st[wr, j, 0:HEAD_V, :] / acc_st[wr, j, HEAD_V:HEAD_V + 1, :]
            if gate_rows is not None:
                o = o * gt_ref[0, gate_rows[j]:gate_rows[j] + 1, :]
            outs.append(o)
        o_ref[0] = jnp.concatenate(outs, axis=0).T.astype(o_ref.dtype)


def _pair_tables(nq, mode, tq, tk):
    qts, kts, first, last, par = [], [], [], [], []
    for qt in range(nq):
        hi = (qt * tq + tq - 1) // tk
        lo = 0 if mode != "window" else max(0, (qt * tq - (WINDOW - 1)) // tk)
        for n, kt in enumerate(range(lo, hi + 1)):
            qts.append(qt)
            kts.append(kt)
            first.append(1 if kt == lo else 0)
            last.append(1 if kt == hi else 0)
            par.append(n % 2)
    arr = lambda v: jnp.asarray(np.array(v, dtype=np.int32))
    return arr(qts), arr(kts), arr(first), arr(last), arr(par)


def _flash(q, k, vt, *, mode, batch, seq, per_head_kv, k_width, k_col, vt_rows, vt_row, out_dtype,
           selbias=None, onehot=None, gate_rows=None, gates_t=None, tq=512, tk=512):
    tk = min(tk, seq)
    nq = seq // tq
    tables = _pair_tables(nq, mode, tq, tk)
    n_steps = int(tables[0].shape[0])
    qw = N_ATT_HEADS * LANES
    ow = N_ATT_HEADS * HEAD_V
    in_specs = [
        pl.BlockSpec((1, tq, qw), lambda b, s, qt, kt, f, l, p: (b, qt[s], 0)),
        pl.BlockSpec((1, tk, k_width), lambda b, s, qt, kt, f, l, p: (b, kt[s], k_col)),
        pl.BlockSpec((1, vt_rows, tk), lambda b, s, qt, kt, f, l, p: (b, vt_row, kt[s])),
    ]
    args = [q, k, vt]
    if mode == "select":
        in_specs.append(pl.BlockSpec((1, NSA_KV_HEADS, tq, LANES), lambda b, s, qt, kt, f, l, p: (b, 0, qt[s], 0)))
        in_specs.append(pl.BlockSpec((tk, LANES), lambda b, s, qt, kt, f, l, p: (kt[s], 0)))
        args += [selbias, onehot]
    if gate_rows is not None:
        in_specs.append(pl.BlockSpec((1, GATE_ROWS, tq), lambda b, s, qt, kt, f, l, p: (b, 0, qt[s])))
        args.append(gates_t)
    kern = functools.partial(_flash_kernel, mode=mode, k_per_head=per_head_kv, v_per_head=per_head_kv,
                             gate_rows=gate_rows, tq=tq, tk=tk)
    return pl.pallas_call(
        kern,
        grid_spec=pltpu.PrefetchScalarGridSpec(
            num_scalar_prefetch=5,
            grid=(batch, n_steps),
            in_specs=in_specs,
            out_specs=pl.BlockSpec((1, tq, ow), lambda b, s, qt, kt, f, l, p: (b, qt[s], 0)),
            scratch_shapes=[pltpu.VMEM((2, N_ATT_HEADS, 1, tq), F32),
                            pltpu.VMEM((2, N_ATT_HEADS, HEAD_V + SUM_ROWS, tq), F32),
                            pltpu.VMEM((tk, tq), F32), pltpu.VMEM((tk, tq), F32),
                            pltpu.VMEM((1, tq), F32), pltpu.VMEM((1, tq), F32),
                            pltpu.SMEM((1,), jnp.int32)],
        ),
        out_shape=jax.ShapeDtypeStruct((batch, seq, ow), out_dtype),
        compiler_params=_cparams(("parallel", "arbitrary")),
        name="flash_" + mode,
    )(*tables, *args)


def _compress_kernel(kc_ref, vc_ref, pos_ref, w1a_ref, w1b_ref, w2_ref, ko_ref, vo_ref, *, nchunk):
    row = lax.broadcasted_iota(jnp.int32, (nchunk, LANES), 0)
    for br, src in enumerate((kc_ref, vc_ref)):
        pa = pb = None
        for l in range(CMP_STRIDE):
            r = src[0, pl.ds(l, nchunk, stride=CMP_STRIDE), :]
            sl = slice(l * LANES, (l + 1) * LANES)
            ta = _dot((r + pos_ref[br, 0:1, sl]).astype(BF16), w1a_ref[br, sl, :])
            tb = _dot((r + pos_ref[br, 1:2, sl]).astype(BF16), w1b_ref[br, sl, :])
            pa = ta if pa is None else pa + ta
            pb = tb if pb is None else pb + tb
        hid = _gelu_tanh(pa + pltpu.roll(pb, nchunk - 1, 0))
        out = jnp.where(row < nchunk - 1, _dot(hid.astype(BF16), w2_ref[br]), 0.0)
        if br == 0:
            ko_ref[0] = out.astype(ko_ref.dtype)
        else:
            vo_ref[0] = out.T.astype(vo_ref.dtype)


def _prep_compress_weights(cmp_pos, cmp_w1, cmp_w2):
    half = CMP_LEN // 2
    pos, w1a, w1b, w2 = [], [], [], []
    for br in range(2):
        p = cmp_pos[br]
        tile = lambda ph: jnp.concatenate([ph, ph], axis=1).reshape(1, half * LANES)
        pos.append(jnp.concatenate([tile(p[:half]), tile(p[half:])], axis=0))
        w = cmp_w1[br].reshape(CMP_LEN, NSA_HEAD_DIM, CMP_HIDDEN)
        z = jnp.zeros_like(w[:half])

        def big(wh):
            g0 = jnp.concatenate([wh, z], axis=1)
            g1 = jnp.concatenate([z, wh], axis=1)
            return jnp.concatenate([g0.reshape(half * LANES, CMP_HIDDEN), g1.reshape(half * LANES, CMP_HIDDEN)], axis=1)

        w1a.append(big(w[:half]))
        w1b.append(big(w[half:]))
        w2.append(_block_diag_dense(jnp.stack([cmp_w2[br], cmp_w2[br]])))
    return (jnp.stack(pos), jnp.stack(w1a).astype(BF16), jnp.stack(w1b).astype(BF16), jnp.stack(w2).astype(BF16))


def _compress(kc, vc, cmp_pos, cmp_w1, cmp_w2, batch, seq):
    nchunk = seq // CMP_STRIDE
    pos, w1a, w1b, w2 = _prep_compress_weights(cmp_pos, cmp_w1, cmp_w2)
    blk = pl.BlockSpec((1, seq, LANES), lambda b: (b, 0, 0))
    full3 = lambda shape: pl.BlockSpec(shape, lambda b: (0, 0, 0))
    return pl.pallas_call(
        functools.partial(_compress_kernel, nchunk=nchunk),
        grid=(batch,),
        in_specs=[blk, blk, full3(pos.shape), full3(w1a.shape), full3(w1b.shape), full3(w2.shape)],
        out_specs=[pl.BlockSpec((1, nchunk, LANES), lambda b: (b, 0, 0)),
                   pl.BlockSpec((1, LANES, nchunk), lambda b: (b, 0, 0))],
        out_shape=(jax.ShapeDtypeStruct((batch, nchunk, LANES), BF16),
                   jax.ShapeDtypeStruct((batch, LANES, nchunk), BF16)),
        compiler_params=_cparams(("parallel",)),
        name="nsa_compress",
    )(kc.reshape(batch, seq, LANES), vc.reshape(batch, seq, LANES), pos, w1a, w1b, w2)


def _select_kernel(q_ref, kc_ref, vct_ref, ovt_ref, gt_ref, oc_ref, sb_ref, ot_sc, imp_sc, *, tq, nchunk, variants):
    nsel = LANES
    qt = pl.program_id(1)
    col_ok = (qt * tq + lax.broadcasted_iota(jnp.int32, (1, tq), 1) >= CMP_LEN - 1).astype(F32)

    def attend(nc):
        kc = kc_ref[0, 0:nc, :]
        qpos = qt * tq + lax.broadcasted_iota(jnp.int32, (nc, tq), 1)
        cend = lax.broadcasted_iota(jnp.int32, (nc, tq), 0) * CMP_STRIDE + (CMP_LEN - 1)
        cvalid = cend <= qpos
        ovt = ovt_ref[:, 0:nc]
        for g in range(NSA_KV_HEADS):
            vct = vct_ref[0, g * HEAD_V:(g + 1) * HEAD_V, 0:nc]
            psum = jnp.zeros((nc, tq), F32)
            for pj in range(NSA_GROUP):
                hd = g * NSA_GROUP + pj
                q = q_ref[0, :, hd * LANES:(hd + 1) * LANES]
                s = jnp.where(cvalid, _dot_nt(kc, q), NEG_INF)
                m = jnp.max(s, axis=0, keepdims=True)
                e = jnp.exp2(s - m)
                p = e * (col_ok / jnp.maximum(jnp.sum(e, axis=0, keepdims=True), 1e-30))
                ot_sc[hd * HEAD_V:(hd + 1) * HEAD_V, :] = _dot(vct, p.astype(BF16)) * gt_ref[0, 3 * hd:3 * hd + 1, :]
                psum = psum + p
            hi = psum.astype(BF16)
            r1 = psum - hi.astype(F32)
            mid = r1.astype(BF16)
            lo = (r1 - mid.astype(F32)).astype(BF16)
            imp_sc[g] = _dot(ovt, hi) + _dot(ovt, mid) + _dot(ovt, lo)

    needed = (qt * tq + tq - CMP_STRIDE) // CMP_STRIDE
    lo_bound = 0
    for nc in variants:
        pl.when((needed > lo_bound) & (needed <= nc) if nc != variants[-1] else needed > lo_bound)(
            functools.partial(attend, nc))
        lo_bound = nc

    blk = lax.broadcasted_iota(jnp.int32, (nsel, tq), 0)
    blkf = blk.astype(F32)
    qp = qt * tq + lax.broadcasted_iota(jnp.int32, (nsel, tq), 1)
    valid = blk * SEL_LEN <= qp
    forced = (blk == 0) | (blk == qp // SEL_LEN)
    for g in range(NSA_KV_HEADS):
        score = jnp.where(valid, imp_sc[g], -1.0)
        score0 = jnp.where(forced, score + FORCE_BONUS, score)
        score = score0
        for _ in range(min(SEL_TOPK, nsel)):
            m = jnp.max(score, axis=0, keepdims=True)
            idx = jnp.min(jnp.where(score == m, blkf, float(nsel)), axis=0, keepdims=True)
            score = jnp.where(blkf == idx, -jnp.inf, score)
        bias = jnp.where(score0 >= 0.0, jnp.where(score == -jnp.inf, 0.0, NEG_INF), NEG_INF)
        sb_ref[0, g] = bias.T.astype(sb_ref.dtype)
    oc_ref[0] = ot_sc[...].T


def _overlap_matrix_t(nchunk, nsel):
    n = np.arange(nchunk)[None, :]
    s = np.arange(nsel)[:, None]
    cs = n * CMP_STRIDE
    ov = (cs < s * SEL_LEN + SEL_LEN) & (cs + CMP_LEN - 1 >= s * SEL_LEN) & (n < nchunk - 1)
    return jnp.asarray(ov.astype(np.float32)).astype(BF16)


def _select(qn, kcmp, vcmp_t, gates_t, batch, seq, tq=512):
    nchunk = seq // CMP_STRIDE
    nsel = LANES
    assert seq // SEL_LEN <= nsel
    qw = N_ATT_HEADS * LANES
    ow = N_ATT_HEADS * HEAD_V
    ovt = _overlap_matrix_t(nchunk, nsel)
    quarter = nchunk // 4
    variants = tuple(quarter * i for i in range(1, 5)) if quarter % LANES == 0 else (nchunk,)
    return pl.pallas_call(
        functools.partial(_select_kernel, tq=tq, nchunk=nchunk, variants=variants),
        grid=(batch, seq // tq),
        in_specs=[pl.BlockSpec((1, tq, qw), lambda b, t: (b, t, 0)),
                  pl.BlockSpec((1, nchunk, LANES), lambda b, t: (b, 0, 0)),
                  pl.BlockSpec((1, NSA_KV_HEADS * HEAD_V, nchunk), lambda b, t: (b, 0, 0)),
                  pl.BlockSpec((nsel, nchunk), lambda b, t: (0, 0)),
                  pl.BlockSpec((1, GATE_ROWS, tq), lambda b, t: (b, 0, t))],
        out_specs=[pl.BlockSpec((1, tq, ow), lambda b, t: (b, t, 0)),
                   pl.BlockSpec((1, NSA_KV_HEADS, tq, nsel), lambda b, t: (b, 0, t, 0))],
        out_shape=(jax.ShapeDtypeStruct((batch, seq, ow), F32),
                   jax.ShapeDtypeStruct((batch, NSA_KV_HEADS, seq, nsel), BF16)),
        scratch_shapes=[pltpu.VMEM((ow, tq), F32), pltpu.VMEM((NSA_KV_HEADS, nsel, tq), F32)],
        compiler_params=_cparams(("parallel", "parallel")),
        name="nsa_select",
    )(qn, kcmp, vcmp_t, ovt, gates_t)


def _out_proj_kernel(x_ref, a_ref, b_ref, oc_ref, os_ref, ow_ref, wa_ref, wb_ref, wc_ref, o_ref):
    acc = x_ref[...] + _dot(a_ref[...].astype(BF16), wa_ref[...])
    acc = acc + _dot(b_ref[...], wb_ref[...])
    c = (oc_ref[...] + os_ref[...]) + ow_ref[...]
    o_ref[...] = acc + _dot(c.astype(BF16), wc_ref[...])


def _out_proj(x2, a_out, b_out, o_c, o_s, o_w, w_out, tm=512):
    T = x2.shape[0]
    wa = w_out[0:256].astype(BF16)
    wb = w_out[256:640].astype(BF16)
    wc = w_out[640:1024].astype(BF16)
    row = lambda i: (i, 0)
    fixed = lambda i: (0, 0)
    act = lambda n: pl.BlockSpec((tm, n), row)
    return pl.pallas_call(
        _out_proj_kernel,
        grid=(T // tm,),
        in_specs=[act(D_MODEL), act(256), act(384), act(384), act(384), act(384),
                  pl.BlockSpec((256, D_MODEL), fixed), pl.BlockSpec((384, D_MODEL), fixed),
                  pl.BlockSpec((384, D_MODEL), fixed)],
        out_specs=pl.BlockSpec((tm, D_MODEL), row),
        out_shape=jax.ShapeDtypeStruct((T, D_MODEL), F32),
        compiler_params=_cparams(("parallel",)),
        name="out_proj",
    )(x2, a_out, b_out, o_c, o_s, o_w, wa, wb, wc)


def _final_norm(y, gf_ref):
    ms = jnp.mean(y * y, axis=-1, keepdims=True)
    return (y * lax.rsqrt(ms + EPS)) * gf_ref[...]


def _ffn_kernel(x_ref, g_ref, wg_ref, wu_ref, wd_ref, gf_ref, o_ref, h_sc, acc_sc, *, final_norm):
    f = pl.program_id(1)

    @pl.when(f == 0)
    def _():
        x = x_ref[...]
        ms = jnp.mean(x * x, axis=-1, keepdims=True)
        h_sc[...] = ((x * lax.rsqrt(ms + EPS)) * g_ref[...]).astype(BF16)
        acc_sc[...] = x

    h = h_sc[...]
    gte = _dot(h, wg_ref[...])
    up = _dot(h, wu_ref[...])
    act = (gte * _sigmoid(gte)) * up
    acc_sc[...] += _dot(act.astype(BF16), wd_ref[...])

    @pl.when(f == pl.num_programs(1) - 1)
    def _():
        y = acc_sc[...]
        o_ref[...] = _final_norm(y, gf_ref) if final_norm else y


def _ffn(x2, g, wg, wu, wd, g_final, final_norm, tm=512, tf=1408):
    T = x2.shape[0]
    nf = D_FF // tf
    return pl.pallas_call(
        functools.partial(_ffn_kernel, final_norm=final_norm),
        grid=(T // tm, nf),
        in_specs=[pl.BlockSpec((tm, D_MODEL), lambda i, f: (i, 0)), pl.BlockSpec((1, D_MODEL), lambda i, f: (0, 0)),
                  pl.BlockSpec((D_MODEL, tf), lambda i, f: (0, f)), pl.BlockSpec((D_MODEL, tf), lambda i, f: (0, f)),
                  pl.BlockSpec((tf, D_MODEL), lambda i, f: (f, 0)), pl.BlockSpec((1, D_MODEL), lambda i, f: (0, 0))],
        out_specs=pl.BlockSpec((tm, D_MODEL), lambda i, f: (i, 0)),
        out_shape=jax.ShapeDtypeStruct((T, D_MODEL), F32),
        scratch_shapes=[pltpu.VMEM((tm, D_MODEL), BF16), pltpu.VMEM((tm, D_MODEL), F32)],
        compiler_params=_cparams(("parallel", "arbitrary")),
        name="ffn_dense",
    )(x2, g.reshape(1, -1), wg.astype(BF16), wu.astype(BF16), wd.astype(BF16), g_final.reshape(1, -1))


MOE_TILE = 1024
MOE_CHUNK = 384
MOE_CHUNK_SMALL = 256


def _moe_kernel(x_ref, g_ref, rw_ref, tri_ref, wg_ref, wu_ref, wd_ref, gf_ref, o_ref,
                h_sc, gate_sc, key_sc, keyt_sc, acc_sc, *, final_norm, tm):
    e = pl.program_id(1)
    lane = lax.broadcasted_iota(jnp.int32, (tm, LANES), 1)

    @pl.when(e == 0)
    def _():
        x = x_ref[...]
        ms = jnp.mean(x * x, axis=-1, keepdims=True)
        hf = (x * lax.rsqrt(ms + EPS)) * g_ref[...]
        h_sc[...] = hf.astype(BF16)
        acc_sc[...] = x
        h_hi = hf.astype(BF16)
        h_lo = (hf - h_hi.astype(F32)).astype(BF16)
        logits = (_dot(h_hi, rw_ref[0]) + _dot(h_lo, rw_ref[0]) + _dot(h_hi, rw_ref[1]))
        logits = jnp.where(lane < N_EXPERTS, logits, -jnp.inf)
        m1 = jnp.max(logits, axis=1, keepdims=True)
        i1 = jnp.min(jnp.where(logits == m1, lane, LANES), axis=1, keepdims=True)
        rest = jnp.where(lane == i1, -jnp.inf, logits)
        m2 = jnp.max(rest, axis=1, keepdims=True)
        i2 = jnp.min(jnp.where(rest == m2, lane, LANES), axis=1, keepdims=True)
        e2 = jnp.exp(m2 - m1)
        den = 1.0 + e2
        gate_sc[...] = jnp.where(lane == i1, 1.0 / den, 0.0) + jnp.where(lane == i2, e2 / den, 0.0)
        routed = jnp.where((lane == i1) | (lane == i2), 1.0, 0.0)
        key = jnp.where(routed > 0.0, _dot(tri_ref[...], routed.astype(BF16)), -1.0)
        key_sc[...] = key
        keyt_sc[...] = key.T

    sel = lane == e
    kcol = jnp.max(jnp.where(sel, key_sc[...], -1.0), axis=1, keepdims=True)
    wcol = jnp.sum(jnp.where(sel, gate_sc[...], 0.0), axis=1, keepdims=True)
    krow = keyt_sc[pl.ds(e, 1), :]
    count = jnp.sum(jnp.where(kcol >= 0.0, 1.0, 0.0)).astype(jnp.int32)

    def chunk(rows, ci, carry):
        base = (ci * rows).astype(F32)
        row_id = lax.broadcasted_iota(jnp.int32, (rows, tm), 0).astype(F32)
        col_id = lax.broadcasted_iota(jnp.int32, (tm, rows), 1).astype(F32)
        gather = jnp.where(krow - base == row_id, 1.0, 0.0).astype(BF16)
        xg = _dot(gather, h_sc[...]).astype(BF16)
        gte = _dot(xg, wg_ref[0])
        up = _dot(xg, wu_ref[0])
        act = (gte * _sigmoid(gte)) * up
        y = _dot(act.astype(BF16), wd_ref[0])
        scatter = jnp.where(kcol - base == col_id, 1.0, 0.0).astype(BF16)
        acc_sc[...] += wcol * _dot(scatter, y.astype(BF16))
        return carry

    @pl.when((count > 0) & (count <= MOE_CHUNK_SMALL))
    def _():
        chunk(MOE_CHUNK_SMALL, jnp.int32(0), 0)

    n_chunks = jnp.where(count > MOE_CHUNK_SMALL, (count + (MOE_CHUNK - 1)) // MOE_CHUNK, 0)
    lax.fori_loop(0, n_chunks, functools.partial(chunk, MOE_CHUNK), 0)

    @pl.when(e == pl.num_programs(1) - 1)
    def _():
        out = acc_sc[...]
        o_ref[...] = _final_norm(out, gf_ref) if final_norm else out


def _moe(x2, g, router_w, wg, wu, wd, g_final, final_norm, tm=MOE_TILE):
    T = x2.shape[0]
    rw = _pad_cols(router_w, LANES)
    rw_hi = rw.astype(BF16)
    rw_lo = (rw - rw_hi.astype(F32)).astype(BF16)
    rw2 = jnp.stack([rw_hi, rw_lo])
    tri = jnp.asarray(np.tril(np.ones((tm, tm), np.float32), -1)).astype(BF16)
    return pl.pallas_call(
        functools.partial(_moe_kernel, final_norm=final_norm, tm=tm),
        grid=(T // tm, N_EXPERTS),
        in_specs=[pl.BlockSpec((tm, D_MODEL), lambda i, e: (i, 0)), pl.BlockSpec((1, D_MODEL), lambda i, e: (0, 0)),
                  pl.BlockSpec((2, D_MODEL, LANES), lambda i, e: (0, 0, 0)),
                  pl.BlockSpec((tm, tm), lambda i, e: (0, 0)),
                  pl.BlockSpec((1, D_MODEL, D_FF_EXPERT), lambda i, e: (e, 0, 0)),
                  pl.BlockSpec((1, D_MODEL, D_FF_EXPERT), lambda i, e: (e, 0, 0)),
                  pl.BlockSpec((1, D_FF_EXPERT, D_MODEL), lambda i, e: (e, 0, 0)),
                  pl.BlockSpec((1, D_MODEL), lambda i, e: (0, 0))],
        out_specs=pl.BlockSpec((tm, D_MODEL), lambda i, e: (i, 0)),
        out_shape=jax.ShapeDtypeStruct((T, D_MODEL), F32),
        scratch_shapes=[pltpu.VMEM((tm, D_MODEL), BF16), pltpu.VMEM((tm, LANES), F32),
                        pltpu.VMEM((tm, LANES), F32), pltpu.VMEM((LANES, tm), F32),
                        pltpu.VMEM((tm, D_MODEL), F32)],
        compiler_params=_cparams(("parallel", "arbitrary")),
        name="moe_routed",
    )(x2, g.reshape(1, -1), rw2, tri, wg.astype(BF16), wu.astype(BF16), wd.astype(BF16), g_final.reshape(1, -1))


def _mixer(x2, batch, seq, tables, onehot, norm_g, w_in, conv_w, conv_b, wa, ba, wx, bx, lam,
           q_norm, w_uq, kv_norm, w_ukv, cmp_pos, cmp_w1, cmp_w2, gate_b, w_out):
    w_cat, w_t = _prep_w_in(w_in)
    lru_xy, q_m, k_m, qn, kc, vc, ksw, vt_sw, gates_t, vt_m = _in_proj(
        x2, norm_g, w_cat, w_t, tables, gate_b, q_norm, w_uq, kv_norm, w_ukv, batch, seq)
    a_out = _lru(lru_xy, conv_w, conv_b, wa, ba, wx, bx, lam, batch, seq)
    b3 = lambda t: t.reshape(batch, seq, t.shape[-1])
    heads = tuple(range(N_ATT_HEADS))
    b_out = _flash(b3(q_m), b3(k_m), vt_m, mode="causal", batch=batch, seq=seq, per_head_kv=True,
                   k_width=N_ATT_HEADS * LANES, k_col=0, vt_rows=N_ATT_HEADS * HEAD_V, vt_row=0, out_dtype=BF16,
                   tk=LONG_KEY_TILE)

    kcmp, vcmp_t = _compress(kc, vc, cmp_pos, cmp_w1, cmp_w2, batch, seq)
    qn3 = b3(qn)
    o_c, selbias = _select(qn3, kcmp, vcmp_t, gates_t, batch, seq)
    ksw3 = b3(ksw)
    nsa_kw = dict(batch=batch, seq=seq, per_head_kv=False, k_width=LANES,
                  vt_rows=NSA_KV_HEADS * HEAD_V, out_dtype=F32, gates_t=gates_t)
    o_s = _flash(qn3, ksw3, vt_sw, mode="select", k_col=0, vt_row=0, selbias=selbias,
                 onehot=onehot, gate_rows=tuple(3 * h + 1 for h in heads), tk=LONG_KEY_TILE, **nsa_kw)
    o_w = _flash(qn3, ksw3, vt_sw, mode="window", k_col=1, vt_row=1,
                 gate_rows=tuple(3 * h + 2 for h in heads), **nsa_kw)
    T = batch * seq
    flat = lambda t: t.reshape(T, t.shape[-1])
    return _out_proj(x2, a_out, flat(b_out), flat(o_c), flat(o_s), flat(o_w), w_out)


def kernel(x, norm_mix, w_in, lru_conv_w, lru_conv_b, lru_wa, lru_ba, lru_wx, lru_bx, lru_lambda, mla_q_norm, mla_w_uq, mla_kv_norm, mla_w_ukv, nsa_cmp_pos, nsa_cmp_w1, nsa_cmp_w2, nsa_gate_b, w_out, norm_ffn, ffn_w_gate, ffn_w_up, ffn_w_down, router_w, moe_w_gate, moe_w_up, moe_w_down, norm_final):
    batch, seq, _ = x.shape
    depth = norm_mix.shape[0]
    tables = _rope_tables(seq)
    blk_id = np.arange(seq)[:, None] // SEL_LEN
    onehot = jnp.asarray((blk_id == np.arange(LANES)[None, :]).astype(np.float32)).astype(BF16)
    x2 = x.reshape(batch * seq, D_MODEL)
    for l in range(depth):
        x2 = _mixer(x2, batch, seq, tables, onehot, norm_mix[l], w_in[l], lru_conv_w[l], lru_conv_b[l],
                    lru_wa[l], lru_ba[l], lru_wx[l], lru_bx[l], lru_lambda[l], mla_q_norm[l], mla_w_uq[l],
                    mla_kv_norm[l], mla_w_ukv[l], nsa_cmp_pos[l], nsa_cmp_w1[l], nsa_cmp_w2[l],
                    nsa_gate_b[l], w_out[l])
        last = l == depth - 1
        j = l // 2
        if l % 2 == 0:
            x2 = _ffn(x2, norm_ffn[l], ffn_w_gate[j], ffn_w_up[j], ffn_w_down[j], norm_final, last)
        else:
            x2 = _moe(x2, norm_ffn[l], router_w[j], moe_w_gate[j], moe_w_up[j], moe_w_down[j], norm_final, last)
    if depth == 0:
        raise ValueError("depth must be positive")
    return x2.reshape(batch, seq, D_MODEL)
```

```python
import functools
import math

import numpy as np
import jax
import jax.numpy as jnp
from jax import lax
from jax.experimental import pallas as pl
from jax.experimental.pallas import tpu as pltpu

F32 = jnp.float32
BF16 = jnp.bfloat16

D_MODEL = 1024
LRU_WIDTH = 256
CONV_WIDTH = 4
LRU_C = 8.0
MLA_HEADS = 6
MLA_NOPE = 64
MLA_ROPE = 32
MLA_Q_RANK = 192
NSA_HEADS = 6
NSA_KV_HEADS = 2
NSA_GROUP = NSA_HEADS // NSA_KV_HEADS
NSA_HEAD_DIM = 64
CMP_LEN = 32
CMP_STRIDE = 16
CMP_HIDDEN = 128
SEL_LEN = 64
SEL_TOPK = 16
WINDOW = 512
FORCE_BONUS = 1.0e3
D_FF = 2816
N_EXPERTS = 8
D_FF_EXPERT = 1408
ROPE_THETA = 10000.0
EPS = 1e-6
NEG_INF = -1.0e30

LANES = 128
VMEM_LIMIT = 56 * 1024 * 1024

C_LRU = 0
C_MLA = 512
C_KPE_ROT = 1024
C_QN = 1152
C_QN_ROT = 1536
C_KC = 1920
C_KC_ROT = 2048
C_VC = 2176
C_KS = 2304
C_KS_ROT = 2432
C_KW = 2560
C_KW_ROT = 2688
IN_COLS_PAD = 2816
N_ATT_HEADS = 6
HEAD_V = 64
LONG_KEY_TILE = 1024
SUM_ROWS = 16
GATE_ROWS = 32
LOG2E = 1.4426950408889634


def _cparams(sem):
    return pltpu.CompilerParams(dimension_semantics=sem, vmem_limit_bytes=VMEM_LIMIT)


def _gelu_tanh(x):
    return 0.5 * x * (1.0 + jnp.tanh(math.sqrt(2.0 / math.pi) * (x + 0.044715 * (x * x * x))))


def _sigmoid(x):
    return 1.0 / (1.0 + jnp.exp(-x))


def _dot(a, b):
    return jnp.dot(a, b, preferred_element_type=F32)


def _dot_nt(a, b):
    return lax.dot_general(a, b, (((1,), (1,)), ((), ())), preferred_element_type=F32)


def _rot_cols(w, half):
    return jnp.concatenate([-w[:, half:], w[:, :half]], axis=1)


def _pad_cols(w, n):
    return jnp.pad(w, ((0, 0), (0, n - w.shape[1])))


def _prep_w_in(w_in):
    splits = np.cumsum([256, 256, 192, 128, 32, 384, 128, 128, 128, 128, 128, 128, 18])[:-1].tolist()
    (x_l, y_l, q_dn, kv_dn, k_pe, q_n, kc, vc, ks, vs, kw, vw, g_n) = jnp.split(w_in, splits, axis=1)
    z = lambda n: jnp.zeros((w_in.shape[0], n), w_in.dtype)

    def kpe_slot(w):
        return jnp.concatenate([z(64), w, z(32)], axis=1)

    def qn_slots(fn):
        out = []
        for p in range(NSA_GROUP):
            for h in (p, p + NSA_GROUP):
                out.append(fn(q_n[:, h * 64:(h + 1) * 64]))
        return jnp.concatenate(out, axis=1)

    def kv_rot(w):
        return jnp.concatenate([_rot_cols(w[:, :64], 32), _rot_cols(w[:, 64:], 32)], axis=1)

    cols = [x_l, y_l, _pad_cols(q_dn, 256), kv_dn, kpe_slot(k_pe), kpe_slot(_rot_cols(k_pe, 16)),
            qn_slots(lambda w: w), qn_slots(lambda w: _rot_cols(w, 32)),
            kc, kv_rot(kc), vc, ks, kv_rot(ks), kw, kv_rot(kw)]
    w = jnp.concatenate(cols, axis=1)
    assert w.shape[1] == IN_COLS_PAD
    w_t = jnp.concatenate([vs, vw, _pad_cols(g_n, GATE_ROWS)], axis=1).T
    return w.astype(BF16), w_t.astype(BF16)


def _rope_tables(seq):
    pos = jnp.arange(seq, dtype=F32)[:, None]
    inv64 = ROPE_THETA ** (-jnp.arange(32, dtype=F32) * 2.0 / 64)
    inv32 = ROPE_THETA ** (-jnp.arange(16, dtype=F32) * 2.0 / 32)
    a64 = pos * inv64[None, :]
    a32 = pos * inv32[None, :]
    c64, s64 = jnp.cos(a64), jnp.sin(a64)
    c32, s32 = jnp.cos(a32), jnp.sin(a32)
    cos_n = jnp.concatenate([c64, c64, c64, c64], axis=1)
    sin_n = jnp.concatenate([s64, s64, s64, s64], axis=1)
    one = jnp.ones((seq, 64), F32)
    cos_m = jnp.concatenate([one, c32, c32, one[:, :32]], axis=1)
    sin_m = jnp.concatenate([0 * one, s32, s32, 0 * one[:, :32]], axis=1)
    return cos_n, sin_n, cos_m, sin_m


def _in_proj_kernel(x_ref, g_ref, w_ref, wt_ref, cn_ref, sn_ref, cm_ref, sm_ref, gb_ref,
                    qnorm_ref, wqa_ref, wqb_ref, kvnorm_ref, wk_ref, wvt_ref,
                    lru_ref, qm_ref, km_ref, qn_ref, kc_ref, vc_ref, ksw_ref, vt_ref, gt_ref, vtm_ref):
    x = x_ref[...]
    ms = jnp.mean(x * x, axis=-1, keepdims=True)
    h = ((x * lax.rsqrt(ms + EPS)) * g_ref[...]).astype(BF16)

    def proj(c0, n):
        return _dot(h, w_ref[:, c0:c0 + n])

    cn, sn = cn_ref[...], sn_ref[...]
    lru_ref[...] = proj(C_LRU, 512)
    cm, sm = cm_ref[...], sm_ref[...]
    kpe = proj(C_MLA + 384, 128) * cm + proj(C_KPE_ROT, 128) * sm
    _mla_heads(proj(C_MLA, 256), proj(C_MLA + 256, 128), kpe, cm, sm, qnorm_ref, wqa_ref, wqb_ref, kvnorm_ref,
               wk_ref, wvt_ref, qm_ref, km_ref, vtm_ref)
    scale = NSA_HEAD_DIM ** -0.5 * LOG2E
    low_half = lax.broadcasted_iota(jnp.int32, cn.shape, 1) < NSA_HEAD_DIM
    for p in range(NSA_GROUP):
        a = proj(C_QN + p * LANES, LANES)
        b = proj(C_QN_ROT + p * LANES, LANES)
        r = (a * cn + b * sn) * scale
        hi_hd = p + NSA_GROUP
        qn_ref[:, p * LANES:(p + 1) * LANES] = jnp.where(low_half, r, 0.0).astype(BF16)
        qn_ref[:, hi_hd * LANES:(hi_hd + 1) * LANES] = jnp.where(low_half, 0.0, r).astype(BF16)
    kc_ref[...] = proj(C_KC, 128) * cn + proj(C_KC_ROT, 128) * sn
    vc_ref[...] = proj(C_VC, 128)
    ksw_ref[:, 0:128] = (proj(C_KS, 128) * cn + proj(C_KS_ROT, 128) * sn).astype(BF16)
    ksw_ref[:, 128:256] = (proj(C_KW, 128) * cn + proj(C_KW_ROT, 128) * sn).astype(BF16)
    vt_ref[0] = _dot_nt(wt_ref[0:256, :], h).astype(BF16)
    gt_ref[0] = _sigmoid(_dot_nt(wt_ref[256:256 + GATE_ROWS, :], h) + gb_ref[...])


def _in_proj(x2, g, w_cat, w_t, tables, gate_b, q_norm, w_uq, kv_norm, w_ukv, batch, seq, tm=512):
    T = x2.shape[0]
    nt = T // tm
    npos = seq // tm
    cn, sn, cm, sm = tables
    wqa, wqb, wk, wvt = _prep_mla_weights(w_uq, w_ukv)
    row = lambda i: (i, 0)
    fixed = lambda i: (0, 0)
    posmap = lambda i: (i % npos, 0)
    tab_spec = pl.BlockSpec((tm, LANES), posmap)
    whole = lambda a: pl.BlockSpec(a.shape, fixed)
    out_shapes = (
        jax.ShapeDtypeStruct((T, 512), F32),
        jax.ShapeDtypeStruct((T, 768), BF16),
        jax.ShapeDtypeStruct((T, 768), BF16),
        jax.ShapeDtypeStruct((T, 768), BF16),
        jax.ShapeDtypeStruct((T, 128), F32),
        jax.ShapeDtypeStruct((T, 128), F32),
        jax.ShapeDtypeStruct((T, 256), BF16),
    )
    t_shapes = (jax.ShapeDtypeStruct((batch, 256, seq), BF16),
                jax.ShapeDtypeStruct((batch, GATE_ROWS, seq), F32),
                jax.ShapeDtypeStruct((batch, 384, seq), BF16))
    tmap = lambda i: (i // npos, 0, i % npos)
    gate_b_col = jnp.pad(gate_b, (0, GATE_ROWS - gate_b.shape[0])).reshape(GATE_ROWS, 1)
    q_norm_p = _pad_cols(q_norm.reshape(1, -1), 256)
    kv_norm_r = kv_norm.reshape(1, -1)
    return pl.pallas_call(
        _in_proj_kernel,
        grid=(nt,),
        in_specs=[pl.BlockSpec((tm, D_MODEL), row), pl.BlockSpec((1, D_MODEL), fixed),
                  pl.BlockSpec((D_MODEL, IN_COLS_PAD), fixed), pl.BlockSpec((256 + GATE_ROWS, D_MODEL), fixed),
                  tab_spec, tab_spec, tab_spec, tab_spec, pl.BlockSpec((GATE_ROWS, 1), fixed),
                  whole(q_norm_p), whole(wqa), whole(wqb), whole(kv_norm_r), whole(wk), whole(wvt)],
        out_specs=[pl.BlockSpec((tm, s.shape[1]), row) for s in out_shapes]
        + [pl.BlockSpec((1, s.shape[1], tm), tmap) for s in t_shapes],
        out_shape=out_shapes + t_shapes,
        compiler_params=_cparams(("parallel",)),
        name="in_proj",
    )(x2, g.reshape(1, -1), w_cat, w_t, cn, sn, cm, sm, gate_b_col, q_norm_p, wqa, wqb, kv_norm_r, wk, wvt)


def _lru_kernel(xy_ref, cw_ref, cb_ref, wa_ref, ba_ref, wx_ref, bx_ref, lam_ref, out_ref,
                xe_sc, h_sc, *, tt):
    t = pl.program_id(1)

    @pl.when(t == 0)
    def _():
        xe_sc[0:8, :] = jnp.zeros((8, LRU_WIDTH), F32)
        h_sc[...] = jnp.zeros_like(h_sc)

    x = xy_ref[0, :, 0:LRU_WIDTH]
    y = xy_ref[0, :, LRU_WIDTH:2 * LRU_WIDTH]
    xe_sc[8:8 + tt, :] = x
    xc = cb_ref[...]
    for k in range(CONV_WIDTH):
        off = 8 - (CONV_WIDTH - 1) + k
        xc = xc + xe_sc[off:off + tt, :] * cw_ref[k:k + 1, :]
    xe_sc[0:8, :] = x[tt - 8:tt, :]

    xb = xc.astype(BF16)
    r = _sigmoid(_dot(xb, wa_ref[...]) + ba_ref[...])
    i = _sigmoid(_dot(xb, wx_ref[...]) + bx_ref[...])
    log_a = (-LRU_C * r) * jax.nn.softplus(-lam_ref[...])
    a = jnp.exp(log_a)
    b = jnp.sqrt(jnp.tanh(-log_a) * (a * a + 1.0)) * (i * xc)

    row = lax.broadcasted_iota(jnp.int32, (tt, LRU_WIDTH), 0)
    k = 1
    while k < tt:
        keep = row >= k
        a_sh = jnp.where(keep, pltpu.roll(a, k, 0), 1.0)
        b_sh = jnp.where(keep, pltpu.roll(b, k, 0), 0.0)
        b = a * b_sh + b
        a = a * a_sh
        k *= 2
    h = b + a * h_sc[0:1, :]
    h_sc[0:1, :] = h[tt - 1:tt, :]
    out_ref[0] = h * _gelu_tanh(y)


def _block_diag_dense(w):
    n, c, d = w.shape
    z = jnp.zeros((c, d), w.dtype)
    return jnp.concatenate([jnp.concatenate([w[j] if i == j else z for i in range(n)], axis=1) for j in range(n)],
                           axis=0)


def _lru(lru_xy, conv_w, conv_b, wa, ba, wx, bx, lam, batch, seq, tt=512):
    xy = lru_xy.reshape(batch, seq, 512)
    fixed = lambda b, t: (0, 0)
    vec = pl.BlockSpec((1, LRU_WIDTH), fixed)
    mat = pl.BlockSpec((LRU_WIDTH, LRU_WIDTH), fixed)
    out = pl.pallas_call(
        functools.partial(_lru_kernel, tt=tt),
        grid=(batch, seq // tt),
        in_specs=[pl.BlockSpec((1, tt, 512), lambda b, t: (b, t, 0)),
                  pl.BlockSpec((CONV_WIDTH, LRU_WIDTH), fixed), vec, mat, vec, mat, vec, vec],
        out_specs=pl.BlockSpec((1, tt, LRU_WIDTH), lambda b, t: (b, t, 0)),
        out_shape=jax.ShapeDtypeStruct((batch, seq, LRU_WIDTH), F32),
        scratch_shapes=[pltpu.VMEM((tt + 8, LRU_WIDTH), F32), pltpu.VMEM((8, LRU_WIDTH), F32)],
        compiler_params=_cparams(("parallel", "arbitrary")),
        name="rg_lru",
    )(xy, conv_w, conv_b.reshape(1, -1), _block_diag_dense(wa).astype(BF16), ba.reshape(1, -1),
      _block_diag_dense(wx).astype(BF16), bx.reshape(1, -1), lam.reshape(1, -1))
    return out.reshape(batch * seq, LRU_WIDTH)


def _mla_heads(q_dn, kv_dn, kpe, cm, sm, qn_ref, wqa_ref, wqb_ref, kvn_ref, wk_ref, wvt_ref, q_ref, k_ref, vt_ref):
    ms = jnp.sum(q_dn * q_dn, axis=-1, keepdims=True) * (1.0 / MLA_Q_RANK)
    ql = ((q_dn * lax.rsqrt(ms + EPS)) * qn_ref[...]).astype(BF16)
    ms = jnp.mean(kv_dn * kv_dn, axis=-1, keepdims=True)
    kvl = ((kv_dn * lax.rsqrt(ms + EPS)) * kvn_ref[...]).astype(BF16)
    scale = (MLA_NOPE + MLA_ROPE) ** -0.5 * LOG2E
    for hd in range(MLA_HEADS):
        sl = slice(hd * LANES, (hd + 1) * LANES)
        q_ref[:, sl] = ((_dot(ql, wqa_ref[:, sl]) * cm + _dot(ql, wqb_ref[:, sl]) * sm) * scale).astype(BF16)
        k_ref[:, sl] = (_dot(kvl, wk_ref[:, sl]) + kpe).astype(BF16)
    vt_ref[0] = _dot_nt(wvt_ref[...], kvl).astype(BF16)


def _prep_mla_weights(w_uq, w_ukv):
    zq = jnp.zeros((MLA_Q_RANK, 32), w_uq.dtype)
    qa, qb, kk, vv = [], [], [], []
    for h in range(MLA_HEADS):
        wq = w_uq[:, h * 96:(h + 1) * 96]
        nope, ropew = wq[:, :64], wq[:, 64:]
        qa.append(jnp.concatenate([nope, ropew, zq], axis=1))
        qb.append(jnp.concatenate([0 * nope, _rot_cols(ropew, 16), zq], axis=1))
        wkv = w_ukv[:, h * 128:(h + 1) * 128]
        kk.append(_pad_cols(wkv[:, :64], LANES))
        vv.append(wkv[:, 64:])
    pad_rows = lambda w: jnp.pad(w, ((0, 256 - MLA_Q_RANK), (0, 0)))
    return (pad_rows(jnp.concatenate(qa, axis=1)).astype(BF16), pad_rows(jnp.concatenate(qb, axis=1)).astype(BF16),
            jnp.concatenate(kk, axis=1).astype(BF16), jnp.concatenate(vv, axis=1).T.astype(BF16))


PROBE_KEYS = 128
LAG_LIMIT = 8.0


def _flash_kernel(qt_ref, kt_ref, first_ref, last_ref, par_ref, *refs, mode, k_per_head, v_per_head, gate_rows,
                  tq, tk):
    refs = list(refs)
    q_ref, k_ref, vt_ref = refs[:3]
    o_ref, m_st, acc_st = refs[-8:-5]
    s_sc, mx_sc, flag_sm = refs[-5:-3], refs[-3:-1], refs[-1]
    extra = refs[3:-8]
    sb_ref, oh_ref = (extra[0], extra[1]) if mode == "select" else (None, None)
    gt_ref = extra[-1] if gate_rows is not None else None
    step = pl.program_id(1)
    qt = qt_ref[step]
    kt = kt_ref[step]
    rd = par_ref[step]
    wr = 1 - rd

    def visible(nk):
        kpos = kt * tk + lax.broadcasted_iota(jnp.int32, (nk, tq), 0)
        qpos = qt * tq + lax.broadcasted_iota(jnp.int32, (nk, tq), 1)
        return kpos <= qpos

    def score_tile(j, nk, valid):
        q = q_ref[0, :, j * LANES:(j + 1) * LANES]
        k = k_ref[0, 0:nk, j * LANES:(j + 1) * LANES] if k_per_head else k_ref[0, 0:nk, :]
        if mode == "select":
            q = jnp.concatenate([q, sb_ref[0, j // NSA_GROUP]], axis=1)
            k = jnp.concatenate([k, oh_ref[0:nk, :]], axis=1)
        s = _dot_nt(k, q)
        return s if valid is None else jnp.where(valid, s, NEG_INF)

    def weighted_values(j, nk, p):
        row = (j if v_per_head else j // NSA_GROUP) * HEAD_V
        vt = vt_ref[0, row:row + HEAD_V, 0:nk]
        return _dot(jnp.concatenate([vt, jnp.ones((SUM_ROWS, nk), BF16)], axis=0), p)

    def exact(nk, init):
        valid = visible(nk)

        def scores(j, slot):
            s = score_tile(j, nk, valid)
            s_sc[slot][0:nk, :] = s
            mx_sc[slot][...] = jnp.max(s, axis=0, keepdims=True)

        def accumulate(j, slot):
            if init:
                m_new = mx_sc[slot][...]
            else:
                m_prev = m_st[rd, j]
                m_new = jnp.maximum(m_prev, mx_sc[slot][...])
            p = jnp.exp2((s_sc[slot][0:nk, :] - m_new).astype(BF16))
            upd = weighted_values(j, nk, p)
            acc_st[wr, j] = upd if init else jnp.exp2(m_prev - m_new) * acc_st[rd, j] + upd
            m_st[wr, j] = m_new

        scores(0, 0)
        for j in range(N_ATT_HEADS):
            if j + 1 < N_ATT_HEADS:
                scores(j + 1, (j + 1) % 2)
            accumulate(j, j % 2)

    def lagged(nk, masked, init):
        valid = visible(nk) if masked else None
        rise = None
        for j in range(N_ATT_HEADS):
            if init:
                probe = score_tile(j, PROBE_KEYS, None if valid is None else valid[0:PROBE_KEYS])
                m_prev = jnp.max(probe, axis=0, keepdims=True)
            else:
                m_prev = m_st[rd, j]
            s = score_tile(j, nk, valid)
            p = jnp.exp2((s - m_prev).astype(BF16))
            tile_max = jnp.max(s, axis=0, keepdims=True)
            m_new = jnp.maximum(m_prev, tile_max)
            upd = weighted_values(j, nk, p)
            acc_st[wr, j] = (upd if init else acc_st[rd, j] + upd) * jnp.exp2(m_prev - m_new)
            m_st[wr, j] = m_new
            d = tile_max - m_prev
            rise = d if rise is None else jnp.maximum(rise, d)
        flag_sm[0] = (jnp.max(rise) > LAG_LIMIT).astype(jnp.int32)

    first = first_ref[step] == 1
    later = jnp.logical_not(first)
    first_q, last_q = qt * tq, qt * tq + (tq - 1)
    crosses = kt * tk + (tk - 1) > first_q
    clear = jnp.logical_not(crosses)
    flag_sm[0] = 0
    for init, when in ((True, first), (False, later)):
        pl.when(when & clear)(functools.partial(lagged, tk, False, init))
        if tk > tq:
            half_only = kt * tk + tk // 2 > last_q
            pl.when(when & crosses & half_only)(functools.partial(lagged, tk // 2, True, init))
            pl.when(when & crosses & jnp.logical_not(half_only))(functools.partial(lagged, tk, True, init))
        else:
            pl.when(when & crosses)(functools.partial(lagged, tk, True, init))
    redo = flag_sm[0] == 1
    pl.when(redo & first)(lambda: exact(tk, True))
    pl.when(redo & later)(lambda: exact(tk, False))

    @pl.when(last_ref[step] == 1)
    def _():
        outs = []
        for j in range(N_ATT_HEADS):
            o = acc_st[wr, j, 0:HEAD_V, :] / acc_st[wr, j, HEAD_V:HEAD_V + 1, :]
            if gate_rows is not None:
                o = o * gt_ref[0, gate_rows[j]:gate_rows[j] + 1, :]
            outs.append(o)
        o_ref[0] = jnp.concatenate(outs, axis=0).T.astype(o_ref.dtype)


def _pair_tables(nq, tq, tk):
    qts, kts, first, last, par = [], [], [], [], []
    for qt in range(nq):
        hi = (qt * tq + tq - 1) // tk
        for kt in range(hi + 1):
            qts.append(qt)
            kts.append(kt)
            first.append(1 if kt == 0 else 0)
            last.append(1 if kt == hi else 0)
            par.append(kt % 2)
    arr = lambda v: jnp.asarray(np.array(v, dtype=np.int32))
    return arr(qts), arr(kts), arr(first), arr(last), arr(par)


def _flash(q, k, vt, *, mode, batch, seq, per_head_kv, k_width, k_col, vt_rows, vt_row, out_dtype,
           selbias=None, onehot=None, gate_rows=None, gates_t=None, tq=512, tk=512):
    tk = min(tk, seq)
    nq = seq // tq
    tables = _pair_tables(nq, tq, tk)
    n_steps = int(tables[0].shape[0])
    qw = N_ATT_HEADS * LANES
    ow = N_ATT_HEADS * HEAD_V
    in_specs = [
        pl.BlockSpec((1, tq, qw), lambda b, s, qt, kt, f, l, p: (b, qt[s], 0)),
        pl.BlockSpec((1, tk, k_width), lambda b, s, qt, kt, f, l, p: (b, kt[s], k_col)),
        pl.BlockSpec((1, vt_rows, tk), lambda b, s, qt, kt, f, l, p: (b, vt_row, kt[s])),
    ]
    args = [q, k, vt]
    if mode == "select":
        in_specs.append(pl.BlockSpec((1, NSA_KV_HEADS, tq, LANES), lambda b, s, qt, kt, f, l, p: (b, 0, qt[s], 0)))
        in_specs.append(pl.BlockSpec((tk, LANES), lambda b, s, qt, kt, f, l, p: (kt[s], 0)))
        args += [selbias, onehot]
    if gate_rows is not None:
        in_specs.append(pl.BlockSpec((1, GATE_ROWS, tq), lambda b, s, qt, kt, f, l, p: (b, 0, qt[s])))
        args.append(gates_t)
    kern = functools.partial(_flash_kernel, mode=mode, k_per_head=per_head_kv, v_per_head=per_head_kv,
                             gate_rows=gate_rows, tq=tq, tk=tk)
    return pl.pallas_call(
        kern,
        grid_spec=pltpu.PrefetchScalarGridSpec(
            num_scalar_prefetch=5,
            grid=(batch, n_steps),
            in_specs=in_specs,
            out_specs=pl.BlockSpec((1, tq, ow), lambda b, s, qt, kt, f, l, p: (b, qt[s], 0)),
            scratch_shapes=[pltpu.VMEM((2, N_ATT_HEADS, 1, tq), F32),
                            pltpu.VMEM((2, N_ATT_HEADS, HEAD_V + SUM_ROWS, tq), F32),
                            pltpu.VMEM((tk, tq), F32), pltpu.VMEM((tk, tq), F32),
                            pltpu.VMEM((1, tq), F32), pltpu.VMEM((1, tq), F32),
                            pltpu.SMEM((1,), jnp.int32)],
        ),
        out_shape=jax.ShapeDtypeStruct((batch, seq, ow), out_dtype),
        compiler_params=_cparams(("parallel", "arbitrary")),
        name="flash_" + mode,
    )(*tables, *args)


def _window_kernel(q_ref, kp_ref, kc_ref, vp_ref, vc_ref, gt_ref, o_ref, s_a, s_b, *, tq, gate_rows):
    qt = pl.program_id(1)
    nk = 2 * tq
    kpos = (qt - 1) * tq + lax.broadcasted_iota(jnp.int32, (nk, tq), 0)
    qpos = qt * tq + lax.broadcasted_iota(jnp.int32, (nk, tq), 1)
    dist = qpos - kpos
    valid = (kpos >= 0) & (dist >= 0) & (dist < WINDOW)
    ones = jnp.ones((SUM_ROWS, nk), BF16)
    slots = (s_a, s_b)

    def scores(j):
        q = q_ref[0, :, j * LANES:(j + 1) * LANES]
        s = jnp.concatenate([_dot_nt(kp_ref[0], q), _dot_nt(kc_ref[0], q)], axis=0)
        slots[j % 2][...] = jnp.where(valid, s, NEG_INF)

    def output(j):
        s = slots[j % 2][...]
        p = jnp.exp2((s - jnp.max(s, axis=0, keepdims=True)).astype(BF16))
        g = j // NSA_GROUP
        vt = jnp.concatenate([vp_ref[0, g * HEAD_V:(g + 1) * HEAD_V, :], vc_ref[0, g * HEAD_V:(g + 1) * HEAD_V, :]],
                             axis=1)
        acc = _dot(jnp.concatenate([vt, ones], axis=0), p)
        return (acc[0:HEAD_V] / acc[HEAD_V:HEAD_V + 1]) * gt_ref[0, gate_rows[j]:gate_rows[j] + 1, :]

    outs = []
    scores(0)
    for j in range(N_ATT_HEADS):
        if j + 1 < N_ATT_HEADS:
            scores(j + 1)
        outs.append(output(j))
    o_ref[0] = jnp.concatenate(outs, axis=0).T.astype(o_ref.dtype)


def _window(q, k, vt, gates_t, *, batch, seq, k_col, vt_row, gate_rows, tq=512):
    assert tq == WINDOW
    qw = N_ATT_HEADS * LANES
    ow = N_ATT_HEADS * HEAD_V
    prev = lambda t: jnp.maximum(t - 1, 0)
    vrows = NSA_KV_HEADS * HEAD_V
    return pl.pallas_call(
        functools.partial(_window_kernel, tq=tq, gate_rows=gate_rows),
        grid=(batch, seq // tq),
        in_specs=[pl.BlockSpec((1, tq, qw), lambda b, t: (b, t, 0)),
                  pl.BlockSpec((1, tq, LANES), lambda b, t: (b, prev(t), k_col)),
                  pl.BlockSpec((1, tq, LANES), lambda b, t: (b, t, k_col)),
                  pl.BlockSpec((1, vrows, tq), lambda b, t: (b, vt_row, prev(t))),
                  pl.BlockSpec((1, vrows, tq), lambda b, t: (b, vt_row, t)),
                  pl.BlockSpec((1, GATE_ROWS, tq), lambda b, t: (b, 0, t))],
        out_specs=pl.BlockSpec((1, tq, ow), lambda b, t: (b, t, 0)),
        out_shape=jax.ShapeDtypeStruct((batch, seq, ow), F32),
        scratch_shapes=[pltpu.VMEM((2 * tq, tq), F32), pltpu.VMEM((2 * tq, tq), F32)],
        compiler_params=_cparams(("parallel", "parallel")),
        name="nsa_window",
    )(q, k, k, vt, vt, gates_t)


def _compress_kernel(kc_ref, vc_ref, pos_ref, w1a_ref, w1b_ref, w2_ref, ko_ref, vo_ref, *, nchunk):
    row = lax.broadcasted_iota(jnp.int32, (nchunk, LANES), 0)
    for br, src in enumerate((kc_ref, vc_ref)):
        pa = pb = None
        for l in range(CMP_STRIDE):
            r = src[0, pl.ds(l, nchunk, stride=CMP_STRIDE), :]
            sl = slice(l * LANES, (l + 1) * LANES)
            ta = _dot((r + pos_ref[br, 0:1, sl]).astype(BF16), w1a_ref[br, sl, :])
            tb = _dot((r + pos_ref[br, 1:2, sl]).astype(BF16), w1b_ref[br, sl, :])
            pa = ta if pa is None else pa + ta
            pb = tb if pb is None else pb + tb
        hid = _gelu_tanh(pa + pltpu.roll(pb, nchunk - 1, 0))
        out = jnp.where(row < nchunk - 1, _dot(hid.astype(BF16), w2_ref[br]), 0.0)
        if br == 0:
            ko_ref[0] = out.astype(ko_ref.dtype)
        else:
            vo_ref[0] = out.T.astype(vo_ref.dtype)


def _prep_compress_weights(cmp_pos, cmp_w1, cmp_w2):
    half = CMP_LEN // 2
    pos, w1a, w1b, w2 = [], [], [], []
    for br in range(2):
        p = cmp_pos[br]
        tile = lambda ph: jnp.concatenate([ph, ph], axis=1).reshape(1, half * LANES)
        pos.append(jnp.concatenate([tile(p[:half]), tile(p[half:])], axis=0))
        w = cmp_w1[br].reshape(CMP_LEN, NSA_HEAD_DIM, CMP_HIDDEN)
        z = jnp.zeros_like(w[:half])

        def big(wh):
            g0 = jnp.concatenate([wh, z], axis=1)
            g1 = jnp.concatenate([z, wh], axis=1)
            return jnp.concatenate([g0.reshape(half * LANES, CMP_HIDDEN), g1.reshape(half * LANES, CMP_HIDDEN)], axis=1)

        w1a.append(big(w[:half]))
        w1b.append(big(w[half:]))
        w2.append(_block_diag_dense(jnp.stack([cmp_w2[br], cmp_w2[br]])))
    return (jnp.stack(pos), jnp.stack(w1a).astype(BF16), jnp.stack(w1b).astype(BF16), jnp.stack(w2).astype(BF16))


def _compress(kc, vc, cmp_pos, cmp_w1, cmp_w2, batch, seq):
    nchunk = seq // CMP_STRIDE
    pos, w1a, w1b, w2 = _prep_compress_weights(cmp_pos, cmp_w1, cmp_w2)
    blk = pl.BlockSpec((1, seq, LANES), lambda b: (b, 0, 0))
    full3 = lambda shape: pl.BlockSpec(shape, lambda b: (0, 0, 0))
    return pl.pallas_call(
        functools.partial(_compress_kernel, nchunk=nchunk),
        grid=(batch,),
        in_specs=[blk, blk, full3(pos.shape), full3(w1a.shape), full3(w1b.shape), full3(w2.shape)],
        out_specs=[pl.BlockSpec((1, nchunk, LANES), lambda b: (b, 0, 0)),
                   pl.BlockSpec((1, LANES, nchunk), lambda b: (b, 0, 0))],
        out_shape=(jax.ShapeDtypeStruct((batch, nchunk, LANES), BF16),
                   jax.ShapeDtypeStruct((batch, LANES, nchunk), BF16)),
        compiler_params=_cparams(("parallel",)),
        name="nsa_compress",
    )(kc.reshape(batch, seq, LANES), vc.reshape(batch, seq, LANES), pos, w1a, w1b, w2)


def _select_kernel(q_ref, kc_ref, vct_ref, ovt_ref, gt_ref, oc_ref, sb_ref, ot_sc, imp_sc, *, tq, nchunk, variants):
    nsel = LANES
    qt = pl.program_id(1)
    col_ok = (qt * tq + lax.broadcasted_iota(jnp.int32, (1, tq), 1) >= CMP_LEN - 1).astype(F32)

    def attend(nc):
        kc = kc_ref[0, 0:nc, :]
        qpos = qt * tq + lax.broadcasted_iota(jnp.int32, (nc, tq), 1)
        cend = lax.broadcasted_iota(jnp.int32, (nc, tq), 0) * CMP_STRIDE + (CMP_LEN - 1)
        cvalid = cend <= qpos
        ovt = ovt_ref[:, 0:nc]
        for g in range(NSA_KV_HEADS):
            vct = vct_ref[0, g * HEAD_V:(g + 1) * HEAD_V, 0:nc]
            psum = jnp.zeros((nc, tq), F32)
            for pj in range(NSA_GROUP):
                hd = g * NSA_GROUP + pj
                q = q_ref[0, :, hd * LANES:(hd + 1) * LANES]
                s = jnp.where(cvalid, _dot_nt(kc, q), NEG_INF)
                m = jnp.max(s, axis=0, keepdims=True)
                e = jnp.exp2(s - m)
                p = e * (col_ok / jnp.maximum(jnp.sum(e, axis=0, keepdims=True), 1e-30))
                ot_sc[hd * HEAD_V:(hd + 1) * HEAD_V, :] = _dot(vct, p.astype(BF16)) * gt_ref[0, 3 * hd:3 * hd + 1, :]
                psum = psum + p
            hi = psum.astype(BF16)
            r1 = psum - hi.astype(F32)
            mid = r1.astype(BF16)
            lo = (r1 - mid.astype(F32)).astype(BF16)
            imp_sc[g] = _dot(ovt, hi) + _dot(ovt, mid) + _dot(ovt, lo)

    needed = (qt * tq + tq - CMP_STRIDE) // CMP_STRIDE
    lo_bound = 0
    for nc in variants:
        pl.when((needed > lo_bound) & (needed <= nc) if nc != variants[-1] else needed > lo_bound)(
            functools.partial(attend, nc))
        lo_bound = nc

    blk = lax.broadcasted_iota(jnp.int32, (nsel, tq), 0)
    blkf = blk.astype(F32)
    qp = qt * tq + lax.broadcasted_iota(jnp.int32, (nsel, tq), 1)
    valid = blk * SEL_LEN <= qp
    forced = (blk == 0) | (blk == qp // SEL_LEN)
    for g in range(NSA_KV_HEADS):
        score = jnp.where(valid, imp_sc[g], -1.0)
        score0 = jnp.where(forced, score + FORCE_BONUS, score)
        score = score0
        for _ in range(min(SEL_TOPK, nsel)):
            m = jnp.max(score, axis=0, keepdims=True)
            idx = jnp.min(jnp.where(score == m, blkf, float(nsel)), axis=0, keepdims=True)
            score = jnp.where(blkf == idx, -jnp.inf, score)
        bias = jnp.where(score0 >= 0.0, jnp.where(score == -jnp.inf, 0.0, NEG_INF), NEG_INF)
        sb_ref[0, g] = bias.T.astype(sb_ref.dtype)
    oc_ref[0] = ot_sc[...].T


def _overlap_matrix_t(nchunk, nsel):
    n = np.arange(nchunk)[None, :]
    s = np.arange(nsel)[:, None]
    cs = n * CMP_STRIDE
    ov = (cs < s * SEL_LEN + SEL_LEN) & (cs + CMP_LEN - 1 >= s * SEL_LEN) & (n < nchunk - 1)
    return jnp.asarray(ov.astype(np.float32)).astype(BF16)


def _select(qn, kcmp, vcmp_t, gates_t, batch, seq, tq=512):
    nchunk = seq // CMP_STRIDE
    nsel = LANES
    assert seq // SEL_LEN <= nsel
    qw = N_ATT_HEADS * LANES
    ow = N_ATT_HEADS * HEAD_V
    ovt = _overlap_matrix_t(nchunk, nsel)
    quarter = nchunk // 4
    variants = tuple(quarter * i for i in range(1, 5)) if quarter % LANES == 0 else (nchunk,)
    return pl.pallas_call(
        functools.partial(_select_kernel, tq=tq, nchunk=nchunk, variants=variants),
        grid=(batch, seq // tq),
        in_specs=[pl.BlockSpec((1, tq, qw), lambda b, t: (b, t, 0)),
                  pl.BlockSpec((1, nchunk, LANES), lambda b, t: (b, 0, 0)),
                  pl.BlockSpec((1, NSA_KV_HEADS * HEAD_V, nchunk), lambda b, t: (b, 0, 0)),
                  pl.BlockSpec((nsel, nchunk), lambda b, t: (0, 0)),
                  pl.BlockSpec((1, GATE_ROWS, tq), lambda b, t: (b, 0, t))],
        out_specs=[pl.BlockSpec((1, tq, ow), lambda b, t: (b, t, 0)),
                   pl.BlockSpec((1, NSA_KV_HEADS, tq, nsel), lambda b, t: (b, 0, t, 0))],
        out_shape=(jax.ShapeDtypeStruct((batch, seq, ow), F32),
                   jax.ShapeDtypeStruct((batch, NSA_KV_HEADS, seq, nsel), BF16)),
        scratch_shapes=[pltpu.VMEM((ow, tq), F32), pltpu.VMEM((NSA_KV_HEADS, nsel, tq), F32)],
        compiler_params=_cparams(("parallel", "parallel")),
        name="nsa_select",
    )(qn, kcmp, vcmp_t, ovt, gates_t)


def _out_proj_kernel(x_ref, a_ref, b_ref, oc_ref, os_ref, ow_ref, wa_ref, wb_ref, wc_ref, o_ref):
    acc = x_ref[...] + _dot(a_ref[...].astype(BF16), wa_ref[...])
    acc = acc + _dot(b_ref[...], wb_ref[...])
    c = (oc_ref[...] + os_ref[...]) + ow_ref[...]
    o_ref[...] = acc + _dot(c.astype(BF16), wc_ref[...])


def _out_proj(x2, a_out, b_out, o_c, o_s, o_w, w_out, tm=512):
    T = x2.shape[0]
    wa = w_out[0:256].astype(BF16)
    wb = w_out[256:640].astype(BF16)
    wc = w_out[640:1024].astype(BF16)
    row = lambda i: (i, 0)
    fixed = lambda i: (0, 0)
    act = lambda n: pl.BlockSpec((tm, n), row)
    return pl.pallas_call(
        _out_proj_kernel,
        grid=(T // tm,),
        in_specs=[act(D_MODEL), act(256), act(384), act(384), act(384), act(384),
                  pl.BlockSpec((256, D_MODEL), fixed), pl.BlockSpec((384, D_MODEL), fixed),
                  pl.BlockSpec((384, D_MODEL), fixed)],
        out_specs=pl.BlockSpec((tm, D_MODEL), row),
        out_shape=jax.ShapeDtypeStruct((T, D_MODEL), F32),
        compiler_params=_cparams(("parallel",)),
        name="out_proj",
    )(x2, a_out, b_out, o_c, o_s, o_w, wa, wb, wc)


def _final_norm(y, gf_ref):
    ms = jnp.mean(y * y, axis=-1, keepdims=True)
    return (y * lax.rsqrt(ms + EPS)) * gf_ref[...]


def _ffn_kernel(x_ref, g_ref, wg_ref, wu_ref, wd_ref, gf_ref, o_ref, h_sc, acc_sc, *, final_norm):
    f = pl.program_id(1)

    @pl.when(f == 0)
    def _():
        x = x_ref[...]
        ms = jnp.mean(x * x, axis=-1, keepdims=True)
        h_sc[...] = ((x * lax.rsqrt(ms + EPS)) * g_ref[...]).astype(BF16)
        acc_sc[...] = x

    h = h_sc[...]
    gte = _dot(h, wg_ref[...])
    up = _dot(h, wu_ref[...])
    act = (gte * _sigmoid(gte)) * up
    acc_sc[...] += _dot(act.astype(BF16), wd_ref[...])

    @pl.when(f == pl.num_programs(1) - 1)
    def _():
        y = acc_sc[...]
        o_ref[...] = _final_norm(y, gf_ref) if final_norm else y


def _ffn(x2, g, wg, wu, wd, g_final, final_norm, tm=512, tf=1408):
    T = x2.shape[0]
    nf = D_FF // tf
    return pl.pallas_call(
        functools.partial(_ffn_kernel, final_norm=final_norm),
        grid=(T // tm, nf),
        in_specs=[pl.BlockSpec((tm, D_MODEL), lambda i, f: (i, 0)), pl.BlockSpec((1, D_MODEL), lambda i, f: (0, 0)),
                  pl.BlockSpec((D_MODEL, tf), lambda i, f: (0, f)), pl.BlockSpec((D_MODEL, tf), lambda i, f: (0, f)),
                  pl.BlockSpec((tf, D_MODEL), lambda i, f: (f, 0)), pl.BlockSpec((1, D_MODEL), lambda i, f: (0, 0))],
        out_specs=pl.BlockSpec((tm, D_MODEL), lambda i, f: (i, 0)),
        out_shape=jax.ShapeDtypeStruct((T, D_MODEL), F32),
        scratch_shapes=[pltpu.VMEM((tm, D_MODEL), BF16), pltpu.VMEM((tm, D_MODEL), F32)],
        compiler_params=_cparams(("parallel", "arbitrary")),
        name="ffn_dense",
    )(x2, g.reshape(1, -1), wg.astype(BF16), wu.astype(BF16), wd.astype(BF16), g_final.reshape(1, -1))


MOE_TILE = 1024
MOE_CHUNK = 384
MOE_CHUNK_SMALL = 256
MOE_CHUNK_MID = 320


def _moe_kernel(x_ref, g_ref, rw_ref, tri_ref, wg_ref, wu_ref, wd_ref, gf_ref, o_ref,
                h_sc, gate_sc, key_sc, keyt_sc, acc_sc, *, final_norm, tm):
    e = pl.program_id(1)
    lane = lax.broadcasted_iota(jnp.int32, (tm, LANES), 1)

    @pl.when(e == 0)
    def _():
        x = x_ref[...]
        ms = jnp.mean(x * x, axis=-1, keepdims=True)
        hf = (x * lax.rsqrt(ms + EPS)) * g_ref[...]
        h_sc[...] = hf.astype(BF16)
        acc_sc[...] = x
        h_hi = hf.astype(BF16)
        h_lo = (hf - h_hi.astype(F32)).astype(BF16)
        logits = (_dot(h_hi, rw_ref[0]) + _dot(h_lo, rw_ref[0]) + _dot(h_hi, rw_ref[1]))
        logits = jnp.where(lane < N_EXPERTS, logits, -jnp.inf)
        m1 = jnp.max(logits, axis=1, keepdims=True)
        i1 = jnp.min(jnp.where(logits == m1, lane, LANES), axis=1, keepdims=True)
        rest = jnp.where(lane == i1, -jnp.inf, logits)
        m2 = jnp.max(rest, axis=1, keepdims=True)
        i2 = jnp.min(jnp.where(rest == m2, lane, LANES), axis=1, keepdims=True)
        e2 = jnp.exp(m2 - m1)
        den = 1.0 + e2
        gate_sc[...] = jnp.where(lane == i1, 1.0 / den, 0.0) + jnp.where(lane == i2, e2 / den, 0.0)
        routed = jnp.where((lane == i1) | (lane == i2), 1.0, 0.0)
        key = jnp.where(routed > 0.0, _dot(tri_ref[...], routed.astype(BF16)), -1.0)
        key_sc[...] = key
        keyt_sc[...] = key.T

    sel = lane == e
    kcol = jnp.max(jnp.where(sel, key_sc[...], -1.0), axis=1, keepdims=True)
    wcol = jnp.sum(jnp.where(sel, gate_sc[...], 0.0), axis=1, keepdims=True)
    krow = keyt_sc[pl.ds(e, 1), :]
    count = jnp.sum(jnp.where(kcol >= 0.0, 1.0, 0.0)).astype(jnp.int32)

    def chunk(rows, ci, carry):
        base = (ci * rows).astype(F32)
        row_id = lax.broadcasted_iota(jnp.int32, (rows, tm), 0).astype(F32)
        col_id = lax.broadcasted_iota(jnp.int32, (tm, rows), 1).astype(F32)
        gather = jnp.where(krow - base == row_id, 1.0, 0.0).astype(BF16)
        xg = _dot(gather, h_sc[...]).astype(BF16)
        gte = _dot(xg, wg_ref[0])
        up = _dot(xg, wu_ref[0])
        act = (gte * _sigmoid(gte)) * up
        y = _dot(act.astype(BF16), wd_ref[0])
        scatter = jnp.where(kcol - base == col_id, 1.0, 0.0).astype(BF16)
        acc_sc[...] += wcol * _dot(scatter, y.astype(BF16))
        return carry

    @pl.when((count > 0) & (count <= MOE_CHUNK_SMALL))
    def _():
        chunk(MOE_CHUNK_SMALL, jnp.int32(0), 0)

    @pl.when((count > MOE_CHUNK_SMALL) & (count <= MOE_CHUNK_MID))
    def _():
        chunk(MOE_CHUNK_MID, jnp.int32(0), 0)

    n_chunks = jnp.where(count > MOE_CHUNK_MID, (count + (MOE_CHUNK - 1)) // MOE_CHUNK, 0)
    lax.fori_loop(0, n_chunks, functools.partial(chunk, MOE_CHUNK), 0)

    @pl.when(e == pl.num_programs(1) - 1)
    def _():
        out = acc_sc[...]
        o_ref[...] = _final_norm(out, gf_ref) if final_norm else out


def _moe(x2, g, router_w, wg, wu, wd, g_final, final_norm, tm=MOE_TILE):
    T = x2.shape[0]
    rw = _pad_cols(router_w, LANES)
    rw_hi = rw.astype(BF16)
    rw_lo = (rw - rw_hi.astype(F32)).astype(BF16)
    rw2 = jnp.stack([rw_hi, rw_lo])
    tri = jnp.asarray(np.tril(np.ones((tm, tm), np.float32), -1)).astype(BF16)
    return pl.pallas_call(
        functools.partial(_moe_kernel, final_norm=final_norm, tm=tm),
        grid=(T // tm, N_EXPERTS),
        in_specs=[pl.BlockSpec((tm, D_MODEL), lambda i, e: (i, 0)), pl.BlockSpec((1, D_MODEL), lambda i, e: (0, 0)),
                  pl.BlockSpec((2, D_MODEL, LANES), lambda i, e: (0, 0, 0)),
                  pl.BlockSpec((tm, tm), lambda i, e: (0, 0)),
                  pl.BlockSpec((1, D_MODEL, D_FF_EXPERT), lambda i, e: (e, 0, 0)),
                  pl.BlockSpec((1, D_MODEL, D_FF_EXPERT), lambda i, e: (e, 0, 0)),
                  pl.BlockSpec((1, D_FF_EXPERT, D_MODEL), lambda i, e: (e, 0, 0)),
                  pl.BlockSpec((1, D_MODEL), lambda i, e: (0, 0))],
        out_specs=pl.BlockSpec((tm, D_MODEL), lambda i, e: (i, 0)),
        out_shape=jax.ShapeDtypeStruct((T, D_MODEL), F32),
        scratch_shapes=[pltpu.VMEM((tm, D_MODEL), BF16), pltpu.VMEM((tm, LANES), F32),
                        pltpu.VMEM((tm, LANES), F32), pltpu.VMEM((LANES, tm), F32),
                        pltpu.VMEM((tm, D_MODEL), F32)],
        compiler_params=_cparams(("parallel", "arbitrary")),
        name="moe_routed",
    )(x2, g.reshape(1, -1), rw2, tri, wg.astype(BF16), wu.astype(BF16), wd.astype(BF16), g_final.reshape(1, -1))


def _mixer(x2, batch, seq, tables, onehot, norm_g, w_in, conv_w, conv_b, wa, ba, wx, bx, lam,
           q_norm, w_uq, kv_norm, w_ukv, cmp_pos, cmp_w1, cmp_w2, gate_b, w_out):
    w_cat, w_t = _prep_w_in(w_in)
    lru_xy, q_m, k_m, qn, kc, vc, ksw, vt_sw, gates_t, vt_m = _in_proj(
        x2, norm_g, w_cat, w_t, tables, gate_b, q_norm, w_uq, kv_norm, w_ukv, batch, seq)
    a_out = _lru(lru_xy, conv_w, conv_b, wa, ba, wx, bx, lam, batch, seq)
    b3 = lambda t: t.reshape(batch, seq, t.shape[-1])
    heads = tuple(range(N_ATT_HEADS))
    b_out = _flash(b3(q_m), b3(k_m), vt_m, mode="causal", batch=batch, seq=seq, per_head_kv=True,
                   k_width=N_ATT_HEADS * LANES, k_col=0, vt_rows=N_ATT_HEADS * HEAD_V, vt_row=0, out_dtype=BF16,
                   tk=LONG_KEY_TILE)

    kcmp, vcmp_t = _compress(kc, vc, cmp_pos, cmp_w1, cmp_w2, batch, seq)
    qn3 = b3(qn)
    o_c, selbias = _select(qn3, kcmp, vcmp_t, gates_t, batch, seq)
    ksw3 = b3(ksw)
    nsa_kw = dict(batch=batch, seq=seq, per_head_kv=False, k_width=LANES,
                  vt_rows=NSA_KV_HEADS * HEAD_V, out_dtype=F32, gates_t=gates_t)
    o_s = _flash(qn3, ksw3, vt_sw, mode="select", k_col=0, vt_row=0, selbias=selbias,
                 onehot=onehot, gate_rows=tuple(3 * h + 1 for h in heads), tk=LONG_KEY_TILE, **nsa_kw)
    o_w = _window(qn3, ksw3, vt_sw, gates_t, batch=batch, seq=seq, k_col=1, vt_row=1,
                  gate_rows=tuple(3 * h + 2 for h in heads))
    T = batch * seq
    flat = lambda t: t.reshape(T, t.shape[-1])
    return _out_proj(x2, a_out, flat(b_out), flat(o_c), flat(o_s), flat(o_w), w_out)


def kernel(x, norm_mix, w_in, lru_conv_w, lru_conv_b, lru_wa, lru_ba, lru_wx, lru_bx, lru_lambda, mla_q_norm, mla_w_uq, mla_kv_norm, mla_w_ukv, nsa_cmp_pos, nsa_cmp_w1, nsa_cmp_w2, nsa_gate_b, w_out, norm_ffn, ffn_w_gate, ffn_w_up, ffn_w_down, router_w, moe_w_gate, moe_w_up, moe_w_down, norm_final):
    batch, seq, _ = x.shape
    depth = norm_mix.shape[0]
    tables = _rope_tables(seq)
    blk_id = np.arange(seq)[:, None] // SEL_LEN
    onehot = jnp.asarray((blk_id == np.arange(LANES)[None, :]).astype(np.float32)).astype(BF16)
    x2 = x.reshape(batch * seq, D_MODEL)
    for l in range(depth):
        x2 = _mixer(x2, batch, seq, tables, onehot, norm_mix[l], w_in[l], lru_conv_w[l], lru_conv_b[l],
                    lru_wa[l], lru_ba[l], lru_wx[l], lru_bx[l], lru_lambda[l], mla_q_norm[l], mla_w_uq[l],
                    mla_kv_norm[l], mla_w_ukv[l], nsa_cmp_pos[l], nsa_cmp_w1[l], nsa_cmp_w2[l],
                    nsa_gate_b[l], w_out[l])
        last = l == depth - 1
        j = l // 2
        if l % 2 == 0:
            x2 = _ffn(x2, norm_ffn[l], ffn_w_gate[j], ffn_w_up[j], ffn_w_down[j], norm_final, last)
        else:
            x2 = _moe(x2, norm_ffn[l], router_w[j], moe_w_gate[j], moe_w_up[j], moe_w_down[j], norm_final, last)
    if depth == 0:
        raise ValueError("depth must be positive")
    return x2.reshape(batch, seq, D_MODEL)
```

```python
import functools
import math

import numpy as np
import jax
import jax.numpy as jnp
from jax import lax
from jax.experimental import pallas as pl
from jax.experimental.pallas import tpu as pltpu

F32 = jnp.float32
BF16 = jnp.bfloat16

D_MODEL = 1024
LRU_WIDTH = 256
CONV_WIDTH = 4
LRU_C = 8.0
MLA_HEADS = 6
MLA_NOPE = 64
MLA_ROPE = 32
MLA_Q_RANK = 192
NSA_HEADS = 6
NSA_KV_HEADS = 2
NSA_GROUP = NSA_HEADS // NSA_KV_HEADS
NSA_HEAD_DIM = 64
CMP_LEN = 32
CMP_STRIDE = 16
CMP_HIDDEN = 128
SEL_LEN = 64
SEL_TOPK = 16
WINDOW = 512
FORCE_BONUS = 1.0e3
D_FF = 2816
N_EXPERTS = 8
D_FF_EXPERT = 1408
ROPE_THETA = 10000.0
EPS = 1e-6
NEG_INF = -1.0e30

LANES = 128
VMEM_LIMIT = 56 * 1024 * 1024

C_LRU = 0
C_MLA = 512
C_KPE_ROT = 1024
C_QN = 1152
C_QN_ROT = 1536
C_KC = 1920
C_KC_ROT = 2048
C_VC = 2176
C_KS = 2304
C_KS_ROT = 2432
C_KW = 2560
C_KW_ROT = 2688
IN_COLS_PAD = 2816
N_ATT_HEADS = 6
HEAD_V = 64
LONG_KEY_TILE = 1024
SUM_ROWS = 16
GATE_ROWS = 32
LOG2E = 1.4426950408889634


def _cparams(sem):
    return pltpu.CompilerParams(dimension_semantics=sem, vmem_limit_bytes=VMEM_LIMIT)


def _gelu_tanh(x):
    return 0.5 * x * (1.0 + jnp.tanh(math.sqrt(2.0 / math.pi) * (x + 0.044715 * (x * x * x))))


def _sigmoid(x):
    return 1.0 / (1.0 + jnp.exp(-x))


def _dot(a, b):
    return jnp.dot(a, b, preferred_element_type=F32)


def _dot_nt(a, b):
    return lax.dot_general(a, b, (((1,), (1,)), ((), ())), preferred_element_type=F32)


def _rot_cols(w, half):
    return jnp.concatenate([-w[:, half:], w[:, :half]], axis=1)


def _pad_cols(w, n):
    return jnp.pad(w, ((0, 0), (0, n - w.shape[1])))


def _prep_w_in(w_in):
    splits = np.cumsum([256, 256, 192, 128, 32, 384, 128, 128, 128, 128, 128, 128, 18])[:-1].tolist()
    (x_l, y_l, q_dn, kv_dn, k_pe, q_n, kc, vc, ks, vs, kw, vw, g_n) = jnp.split(w_in, splits, axis=1)
    z = lambda n: jnp.zeros((w_in.shape[0], n), w_in.dtype)

    def kpe_slot(w):
        return jnp.concatenate([z(64), w, z(32)], axis=1)

    def qn_slots(fn):
        out = []
        for p in range(NSA_GROUP):
            for h in (p, p + NSA_GROUP):
                out.append(fn(q_n[:, h * 64:(h + 1) * 64]))
        return jnp.concatenate(out, axis=1)

    def kv_rot(w):
        return jnp.concatenate([_rot_cols(w[:, :64], 32), _rot_cols(w[:, 64:], 32)], axis=1)

    cols = [x_l, y_l, _pad_cols(q_dn, 256), kv_dn, kpe_slot(k_pe), kpe_slot(_rot_cols(k_pe, 16)),
            qn_slots(lambda w: w), qn_slots(lambda w: _rot_cols(w, 32)),
            kc, kv_rot(kc), vc, ks, kv_rot(ks), kw, kv_rot(kw)]
    w = jnp.concatenate(cols, axis=1)
    assert w.shape[1] == IN_COLS_PAD
    w_t = jnp.concatenate([vs, vw, _pad_cols(g_n, GATE_ROWS)], axis=1).T
    return w.astype(BF16), w_t.astype(BF16)


def _rope_tables(seq):
    pos = jnp.arange(seq, dtype=F32)[:, None]
    inv64 = ROPE_THETA ** (-jnp.arange(32, dtype=F32) * 2.0 / 64)
    inv32 = ROPE_THETA ** (-jnp.arange(16, dtype=F32) * 2.0 / 32)
    a64 = pos * inv64[None, :]
    a32 = pos * inv32[None, :]
    c64, s64 = jnp.cos(a64), jnp.sin(a64)
    c32, s32 = jnp.cos(a32), jnp.sin(a32)
    cos_n = jnp.concatenate([c64, c64, c64, c64], axis=1)
    sin_n = jnp.concatenate([s64, s64, s64, s64], axis=1)
    one = jnp.ones((seq, 64), F32)
    cos_m = jnp.concatenate([one, c32, c32, one[:, :32]], axis=1)
    sin_m = jnp.concatenate([0 * one, s32, s32, 0 * one[:, :32]], axis=1)
    return cos_n, sin_n, cos_m, sin_m


def _in_proj_kernel(x_ref, g_ref, w_ref, wt_ref, cn_ref, sn_ref, cm_ref, sm_ref, gb_ref,
                    qnorm_ref, wqa_ref, wqb_ref, kvnorm_ref, wk_ref, wvt_ref,
                    lru_ref, qm_ref, km_ref, qn_ref, kc_ref, vc_ref, ksw_ref, vt_ref, gt_ref, vtm_ref):
    x = x_ref[...]
    ms = jnp.mean(x * x, axis=-1, keepdims=True)
    h = ((x * lax.rsqrt(ms + EPS)) * g_ref[...]).astype(BF16)

    def proj(c0, n):
        return _dot(h, w_ref[:, c0:c0 + n])

    cn, sn = cn_ref[...], sn_ref[...]
    lru_ref[...] = proj(C_LRU, 512)
    cm, sm = cm_ref[...], sm_ref[...]
    kpe = proj(C_MLA + 384, 128) * cm + proj(C_KPE_ROT, 128) * sm
    _mla_heads(proj(C_MLA, 256), proj(C_MLA + 256, 128), kpe, cm, sm, qnorm_ref, wqa_ref, wqb_ref, kvnorm_ref,
               wk_ref, wvt_ref, qm_ref, km_ref, vtm_ref)
    scale = NSA_HEAD_DIM ** -0.5 * LOG2E
    low_half = lax.broadcasted_iota(jnp.int32, cn.shape, 1) < NSA_HEAD_DIM
    for p in range(NSA_GROUP):
        a = proj(C_QN + p * LANES, LANES)
        b = proj(C_QN_ROT + p * LANES, LANES)
        r = (a * cn + b * sn) * scale
        hi_hd = p + NSA_GROUP
        qn_ref[:, p * LANES:(p + 1) * LANES] = jnp.where(low_half, r, 0.0).astype(BF16)
        qn_ref[:, hi_hd * LANES:(hi_hd + 1) * LANES] = jnp.where(low_half, 0.0, r).astype(BF16)
    kc_ref[...] = proj(C_KC, 128) * cn + proj(C_KC_ROT, 128) * sn
    vc_ref[...] = proj(C_VC, 128)
    ksw_ref[:, 0:128] = (proj(C_KS, 128) * cn + proj(C_KS_ROT, 128) * sn).astype(BF16)
    ksw_ref[:, 128:256] = (proj(C_KW, 128) * cn + proj(C_KW_ROT, 128) * sn).astype(BF16)
    vt_ref[0] = _dot_nt(wt_ref[0:256, :], h).astype(BF16)
    gt_ref[0] = _sigmoid(_dot_nt(wt_ref[256:256 + GATE_ROWS, :], h) + gb_ref[...])


def _in_proj(x2, g, w_cat, w_t, tables, gate_b, q_norm, w_uq, kv_norm, w_ukv, batch, seq, tm=512):
    T = x2.shape[0]
    nt = T // tm
    npos = seq // tm
    cn, sn, cm, sm = tables
    wqa, wqb, wk, wvt = _prep_mla_weights(w_uq, w_ukv)
    row = lambda i: (i, 0)
    fixed = lambda i: (0, 0)
    posmap = lambda i: (i % npos, 0)
    tab_spec = pl.BlockSpec((tm, LANES), posmap)
    whole = lambda a: pl.BlockSpec(a.shape, fixed)
    out_shapes = (
        jax.ShapeDtypeStruct((T, 512), F32),
        jax.ShapeDtypeStruct((T, 768), BF16),
        jax.ShapeDtypeStruct((T, 768), BF16),
        jax.ShapeDtypeStruct((T, 768), BF16),
        jax.ShapeDtypeStruct((T, 128), F32),
        jax.ShapeDtypeStruct((T, 128), F32),
        jax.ShapeDtypeStruct((T, 256), BF16),
    )
    t_shapes = (jax.ShapeDtypeStruct((batch, 256, seq), BF16),
                jax.ShapeDtypeStruct((batch, GATE_ROWS, seq), F32),
                jax.ShapeDtypeStruct((batch, 384, seq), BF16))
    tmap = lambda i: (i // npos, 0, i % npos)
    gate_b_col = jnp.pad(gate_b, (0, GATE_ROWS - gate_b.shape[0])).reshape(GATE_ROWS, 1)
    q_norm_p = _pad_cols(q_norm.reshape(1, -1), 256)
    kv_norm_r = kv_norm.reshape(1, -1)
    return pl.pallas_call(
        _in_proj_kernel,
        grid=(nt,),
        in_specs=[pl.BlockSpec((tm, D_MODEL), row), pl.BlockSpec((1, D_MODEL), fixed),
                  pl.BlockSpec((D_MODEL, IN_COLS_PAD), fixed), pl.BlockSpec((256 + GATE_ROWS, D_MODEL), fixed),
                  tab_spec, tab_spec, tab_spec, tab_spec, pl.BlockSpec((GATE_ROWS, 1), fixed),
                  whole(q_norm_p), whole(wqa), whole(wqb), whole(kv_norm_r), whole(wk), whole(wvt)],
        out_specs=[pl.BlockSpec((tm, s.shape[1]), row) for s in out_shapes]
        + [pl.BlockSpec((1, s.shape[1], tm), tmap) for s in t_shapes],
        out_shape=out_shapes + t_shapes,
        compiler_params=_cparams(("parallel",)),
        name="in_proj",
    )(x2, g.reshape(1, -1), w_cat, w_t, cn, sn, cm, sm, gate_b_col, q_norm_p, wqa, wqb, kv_norm_r, wk, wvt)


def _lru_kernel(xy_ref, cw_ref, cb_ref, wa_ref, ba_ref, wx_ref, bx_ref, lam_ref, out_ref,
                xe_sc, h_sc, *, tt):
    t = pl.program_id(1)

    @pl.when(t == 0)
    def _():
        xe_sc[0:8, :] = jnp.zeros((8, LRU_WIDTH), F32)
        h_sc[...] = jnp.zeros_like(h_sc)

    x = xy_ref[0, :, 0:LRU_WIDTH]
    y = xy_ref[0, :, LRU_WIDTH:2 * LRU_WIDTH]
    xe_sc[8:8 + tt, :] = x
    xc = cb_ref[...]
    for k in range(CONV_WIDTH):
        off = 8 - (CONV_WIDTH - 1) + k
        xc = xc + xe_sc[off:off + tt, :] * cw_ref[k:k + 1, :]
    xe_sc[0:8, :] = x[tt - 8:tt, :]

    xb = xc.astype(BF16)
    r = _sigmoid(_dot(xb, wa_ref[...]) + ba_ref[...])
    i = _sigmoid(_dot(xb, wx_ref[...]) + bx_ref[...])
    log_a = (-LRU_C * r) * jax.nn.softplus(-lam_ref[...])
    a = jnp.exp(log_a)
    b = jnp.sqrt(jnp.tanh(-log_a) * (a * a + 1.0)) * (i * xc)

    row = lax.broadcasted_iota(jnp.int32, (tt, LRU_WIDTH), 0)
    k = 1
    while k < tt:
        keep = row >= k
        a_sh = jnp.where(keep, pltpu.roll(a, k, 0), 1.0)
        b_sh = jnp.where(keep, pltpu.roll(b, k, 0), 0.0)
        b = a * b_sh + b
        a = a * a_sh
        k *= 2
    h = b + a * h_sc[0:1, :]
    h_sc[0:1, :] = h[tt - 1:tt, :]
    out_ref[0] = h * _gelu_tanh(y)


def _block_diag_dense(w):
    n, c, d = w.shape
    z = jnp.zeros((c, d), w.dtype)
    return jnp.concatenate([jnp.concatenate([w[j] if i == j else z for i in range(n)], axis=1) for j in range(n)],
                           axis=0)


def _lru(lru_xy, conv_w, conv_b, wa, ba, wx, bx, lam, batch, seq, tt=512):
    xy = lru_xy.reshape(batch, seq, 512)
    fixed = lambda b, t: (0, 0)
    vec = pl.BlockSpec((1, LRU_WIDTH), fixed)
    mat = pl.BlockSpec((LRU_WIDTH, LRU_WIDTH), fixed)
    out = pl.pallas_call(
        functools.partial(_lru_kernel, tt=tt),
        grid=(batch, seq // tt),
        in_specs=[pl.BlockSpec((1, tt, 512), lambda b, t: (b, t, 0)),
                  pl.BlockSpec((CONV_WIDTH, LRU_WIDTH), fixed), vec, mat, vec, mat, vec, vec],
        out_specs=pl.BlockSpec((1, tt, LRU_WIDTH), lambda b, t: (b, t, 0)),
        out_shape=jax.ShapeDtypeStruct((batch, seq, LRU_WIDTH), F32),
        scratch_shapes=[pltpu.VMEM((tt + 8, LRU_WIDTH), F32), pltpu.VMEM((8, LRU_WIDTH), F32)],
        compiler_params=_cparams(("parallel", "arbitrary")),
        name="rg_lru",
    )(xy, conv_w, conv_b.reshape(1, -1), _block_diag_dense(wa).astype(BF16), ba.reshape(1, -1),
      _block_diag_dense(wx).astype(BF16), bx.reshape(1, -1), lam.reshape(1, -1))
    return out.reshape(batch * seq, LRU_WIDTH)


def _mla_heads(q_dn, kv_dn, kpe, cm, sm, qn_ref, wqa_ref, wqb_ref, kvn_ref, wk_ref, wvt_ref, q_ref, k_ref, vt_ref):
    ms = jnp.sum(q_dn * q_dn, axis=-1, keepdims=True) * (1.0 / MLA_Q_RANK)
    ql = ((q_dn * lax.rsqrt(ms + EPS)) * qn_ref[...]).astype(BF16)
    ms = jnp.mean(kv_dn * kv_dn, axis=-1, keepdims=True)
    kvl = ((kv_dn * lax.rsqrt(ms + EPS)) * kvn_ref[...]).astype(BF16)
    scale = (MLA_NOPE + MLA_ROPE) ** -0.5 * LOG2E
    for hd in range(MLA_HEADS):
        sl = slice(hd * LANES, (hd + 1) * LANES)
        q_ref[:, sl] = ((_dot(ql, wqa_ref[:, sl]) * cm + _dot(ql, wqb_ref[:, sl]) * sm) * scale).astype(BF16)
        k_ref[:, sl] = (_dot(kvl, wk_ref[:, sl]) + kpe).astype(BF16)
    vt_ref[0] = _dot_nt(wvt_ref[...], kvl).astype(BF16)


def _prep_mla_weights(w_uq, w_ukv):
    zq = jnp.zeros((MLA_Q_RANK, 32), w_uq.dtype)
    qa, qb, kk, vv = [], [], [], []
    for h in range(MLA_HEADS):
        wq = w_uq[:, h * 96:(h + 1) * 96]
        nope, ropew = wq[:, :64], wq[:, 64:]
        qa.append(jnp.concatenate([nope, ropew, zq], axis=1))
        qb.append(jnp.concatenate([0 * nope, _rot_cols(ropew, 16), zq], axis=1))
        wkv = w_ukv[:, h * 128:(h + 1) * 128]
        kk.append(_pad_cols(wkv[:, :64], LANES))
        vv.append(wkv[:, 64:])
    pad_rows = lambda w: jnp.pad(w, ((0, 256 - MLA_Q_RANK), (0, 0)))
    return (pad_rows(jnp.concatenate(qa, axis=1)).astype(BF16), pad_rows(jnp.concatenate(qb, axis=1)).astype(BF16),
            jnp.concatenate(kk, axis=1).astype(BF16), jnp.concatenate(vv, axis=1).T.astype(BF16))


PROBE_KEYS = 128
LAG_LIMIT = 8.0


def _flash_kernel(qt_ref, kt_ref, first_ref, last_ref, par_ref, *refs, mode, k_per_head, v_per_head, gate_rows,
                  tq, tk):
    refs = list(refs)
    q_ref, k_ref, vt_ref = refs[:3]
    o_ref, m_st, acc_st = refs[-8:-5]
    s_sc, mx_sc, flag_sm = refs[-5:-3], refs[-3:-1], refs[-1]
    extra = refs[3:-8]
    sb_ref, oh_ref = (extra[0], extra[1]) if mode == "select" else (None, None)
    gt_ref = extra[-1] if gate_rows is not None else None
    step = pl.program_id(1)
    qt = qt_ref[step]
    kt = kt_ref[step]
    rd = par_ref[step]
    wr = 1 - rd

    def visible(nk):
        kpos = kt * tk + lax.broadcasted_iota(jnp.int32, (nk, tq), 0)
        qpos = qt * tq + lax.broadcasted_iota(jnp.int32, (nk, tq), 1)
        return kpos <= qpos

    def score_tile(j, nk, valid):
        q = q_ref[0, :, j * LANES:(j + 1) * LANES]
        k = k_ref[0, 0:nk, j * LANES:(j + 1) * LANES] if k_per_head else k_ref[0, 0:nk, :]
        if mode == "select":
            q = jnp.concatenate([q, sb_ref[0, j // NSA_GROUP]], axis=1)
            k = jnp.concatenate([k, oh_ref[0:nk, :]], axis=1)
        s = _dot_nt(k, q)
        return s if valid is None else jnp.where(valid, s, NEG_INF)

    def weighted_values(j, nk, p):
        row = (j if v_per_head else j // NSA_GROUP) * HEAD_V
        vt = vt_ref[0, row:row + HEAD_V, 0:nk]
        return _dot(jnp.concatenate([vt, jnp.ones((SUM_ROWS, nk), BF16)], axis=0), p)

    def exact(nk, init):
        valid = visible(nk)

        def scores(j, slot):
            s = score_tile(j, nk, valid)
            s_sc[slot][0:nk, :] = s
            mx_sc[slot][...] = jnp.max(s, axis=0, keepdims=True)

        def accumulate(j, slot):
            if init:
                m_new = mx_sc[slot][...]
            else:
                m_prev = m_st[rd, j]
                m_new = jnp.maximum(m_prev, mx_sc[slot][...])
            p = jnp.exp2((s_sc[slot][0:nk, :] - m_new).astype(BF16))
            upd = weighted_values(j, nk, p)
            acc_st[wr, j] = upd if init else jnp.exp2(m_prev - m_new) * acc_st[rd, j] + upd
            m_st[wr, j] = m_new

        scores(0, 0)
        for j in range(N_ATT_HEADS):
            if j + 1 < N_ATT_HEADS:
                scores(j + 1, (j + 1) % 2)
            accumulate(j, j % 2)

    def lagged(nk, masked, init):
        valid = visible(nk) if masked else None
        rise = None
        for j in range(N_ATT_HEADS):
            if init:
                probe = score_tile(j, PROBE_KEYS, None if valid is None else valid[0:PROBE_KEYS])
                m_prev = jnp.max(probe, axis=0, keepdims=True)
            else:
                m_prev = m_st[rd, j]
            s = score_tile(j, nk, valid)
            p = jnp.exp2((s - m_prev).astype(BF16))
            tile_max = jnp.max(s, axis=0, keepdims=True)
            m_new = jnp.maximum(m_prev, tile_max)
            upd = weighted_values(j, nk, p)
            acc_st[wr, j] = (upd if init else acc_st[rd, j] + upd) * jnp.exp2(m_prev - m_new)
            m_st[wr, j] = m_new
            d = tile_max - m_prev
            rise = d if rise is None else jnp.maximum(rise, d)
        flag_sm[0] = (jnp.max(rise) > LAG_LIMIT).astype(jnp.int32)

    first = first_ref[step] == 1
    later = jnp.logical_not(first)
    first_q, last_q = qt * tq, qt * tq + (tq - 1)
    crosses = kt * tk + (tk - 1) > first_q
    clear = jnp.logical_not(crosses)
    flag_sm[0] = 0
    for init, when in ((True, first), (False, later)):
        pl.when(when & clear)(functools.partial(lagged, tk, False, init))
        if tk > tq:
            half_only = kt * tk + tk // 2 > last_q
            pl.when(when & crosses & half_only)(functools.partial(lagged, tk // 2, True, init))
            pl.when(when & crosses & jnp.logical_not(half_only))(functools.partial(lagged, tk, True, init))
        else:
            pl.when(when & crosses)(functools.partial(lagged, tk, True, init))
    redo = flag_sm[0] == 1
    pl.when(redo & first)(lambda: exact(tk, True))
    pl.when(redo & later)(lambda: exact(tk, False))

    @pl.when(last_ref[step] == 1)
    def _():
        outs = []
        for j in range(N_ATT_HEADS):
            o = acc_st[wr, j, 0:HEAD_V, :] / acc_st[wr, j, HEAD_V:HEAD_V + 1, :]
            if gate_rows is not None:
                o = o * gt_ref[0, gate_rows[j]:gate_rows[j] + 1, :]
            outs.append(o)
        o_ref[0] = jnp.concatenate(outs, axis=0).T.astype(o_ref.dtype)


def _pair_tables(nq, tq, tk):
    qts, kts, first, last, par = [], [], [], [], []
    for qt in range(nq):
        hi = (qt * tq + tq - 1) // tk
        for kt in range(hi + 1):
            qts.append(qt)
            kts.append(kt)
            first.append(1 if kt == 0 else 0)
            last.append(1 if kt == hi else 0)
            par.append(kt % 2)
    arr = lambda v: jnp.asarray(np.array(v, dtype=np.int32))
    return arr(qts), arr(kts), arr(first), arr(last), arr(par)


def _flash(q, k, vt, *, mode, batch, seq, per_head_kv, k_width, k_col, vt_rows, vt_row, out_dtype,
           selbias=None, onehot=None, gate_rows=None, gates_t=None, tq=512, tk=512):
    tk = min(tk, seq)
    nq = seq // tq
    tables = _pair_tables(nq, tq, tk)
    n_steps = int(tables[0].shape[0])
    qw = N_ATT_HEADS * LANES
    ow = N_ATT_HEADS * HEAD_V
    in_specs = [
        pl.BlockSpec((1, tq, qw), lambda b, s, qt, kt, f, l, p: (b, qt[s], 0)),
        pl.BlockSpec((1, tk, k_width), lambda b, s, qt, kt, f, l, p: (b, kt[s], k_col)),
        pl.BlockSpec((1, vt_rows, tk), lambda b, s, qt, kt, f, l, p: (b, vt_row, kt[s])),
    ]
    args = [q, k, vt]
    if mode == "select":
        in_specs.append(pl.BlockSpec((1, NSA_KV_HEADS, tq, LANES), lambda b, s, qt, kt, f, l, p: (b, 0, qt[s], 0)))
        in_specs.append(pl.BlockSpec((tk, LANES), lambda b, s, qt, kt, f, l, p: (kt[s], 0)))
        args += [selbias, onehot]
    if gate_rows is not None:
        in_specs.append(pl.BlockSpec((1, GATE_ROWS, tq), lambda b, s, qt, kt, f, l, p: (b, 0, qt[s])))
        args.append(gates_t)
    kern = functools.partial(_flash_kernel, mode=mode, k_per_head=per_head_kv, v_per_head=per_head_kv,
                             gate_rows=gate_rows, tq=tq, tk=tk)
    return pl.pallas_call(
        kern,
        grid_spec=pltpu.PrefetchScalarGridSpec(
            num_scalar_prefetch=5,
            grid=(batch, n_steps),
            in_specs=in_specs,
            out_specs=pl.BlockSpec((1, tq, ow), lambda b, s, qt, kt, f, l, p: (b, qt[s], 0)),
            scratch_shapes=[pltpu.VMEM((2, N_ATT_HEADS, 1, tq), F32),
                            pltpu.VMEM((2, N_ATT_HEADS, HEAD_V + SUM_ROWS, tq), F32),
                            pltpu.VMEM((tk, tq), F32), pltpu.VMEM((tk, tq), F32),
                            pltpu.VMEM((1, tq), F32), pltpu.VMEM((1, tq), F32),
                            pltpu.SMEM((1,), jnp.int32)],
        ),
        out_shape=jax.ShapeDtypeStruct((batch, seq, ow), out_dtype),
        compiler_params=_cparams(("parallel", "arbitrary")),
        name="flash_" + mode,
    )(*tables, *args)


def _window_kernel(q_ref, kp_ref, kc_ref, vp_ref, vc_ref, gt_ref, o_ref, s_a, s_b, mx_a, mx_b, *, tq, gate_rows):
    qt = pl.program_id(1)
    krow = lax.broadcasted_iota(jnp.int32, (tq, tq), 0)
    qcol = lax.broadcasted_iota(jnp.int32, (tq, tq), 1)
    valid_prev = (qt > 0) & (qcol - krow < WINDOW - tq)
    valid_own = krow <= qcol
    ones = jnp.ones((SUM_ROWS, tq), BF16)
    slots, maxes = (s_a, s_b), (mx_a, mx_b)

    def scores(j):
        q = q_ref[0, :, j * LANES:(j + 1) * LANES]
        s_prev = jnp.where(valid_prev, _dot_nt(kp_ref[0], q), NEG_INF)
        s_own = jnp.where(valid_own, _dot_nt(kc_ref[0], q), NEG_INF)
        slots[j % 2][0:tq, :] = s_prev
        slots[j % 2][tq:2 * tq, :] = s_own
        maxes[j % 2][...] = jnp.maximum(jnp.max(s_prev, axis=0, keepdims=True), jnp.max(s_own, axis=0, keepdims=True))

    def output(j):
        m = maxes[j % 2][...]
        g = j // NSA_GROUP
        acc = None
        for half, v_ref in enumerate((vp_ref, vc_ref)):
            p = jnp.exp2((slots[j % 2][half * tq:(half + 1) * tq, :] - m).astype(BF16))
            vt = v_ref[0, g * HEAD_V:(g + 1) * HEAD_V, :]
            part = _dot(jnp.concatenate([vt, ones], axis=0), p)
            acc = part if acc is None else acc + part
        return (acc[0:HEAD_V] / acc[HEAD_V:HEAD_V + 1]) * gt_ref[0, gate_rows[j]:gate_rows[j] + 1, :]

    outs = []
    scores(0)
    for j in range(N_ATT_HEADS):
        if j + 1 < N_ATT_HEADS:
            scores(j + 1)
        outs.append(output(j))
    o_ref[0] = jnp.concatenate(outs, axis=0).T.astype(o_ref.dtype)


def _window(q, k, vt, gates_t, *, batch, seq, k_col, vt_row, gate_rows, tq=512):
    assert tq == WINDOW
    qw = N_ATT_HEADS * LANES
    ow = N_ATT_HEADS * HEAD_V
    prev = lambda t: jnp.maximum(t - 1, 0)
    vrows = NSA_KV_HEADS * HEAD_V
    return pl.pallas_call(
        functools.partial(_window_kernel, tq=tq, gate_rows=gate_rows),
        grid=(batch, seq // tq),
        in_specs=[pl.BlockSpec((1, tq, qw), lambda b, t: (b, t, 0)),
                  pl.BlockSpec((1, tq, LANES), lambda b, t: (b, prev(t), k_col)),
                  pl.BlockSpec((1, tq, LANES), lambda b, t: (b, t, k_col)),
                  pl.BlockSpec((1, vrows, tq), lambda b, t: (b, vt_row, prev(t))),
                  pl.BlockSpec((1, vrows, tq), lambda b, t: (b, vt_row, t)),
                  pl.BlockSpec((1, GATE_ROWS, tq), lambda b, t: (b, 0, t))],
        out_specs=pl.BlockSpec((1, tq, ow), lambda b, t: (b, t, 0)),
        out_shape=jax.ShapeDtypeStruct((batch, seq, ow), F32),
        scratch_shapes=[pltpu.VMEM((2 * tq, tq), F32), pltpu.VMEM((2 * tq, tq), F32),
                        pltpu.VMEM((1, tq), F32), pltpu.VMEM((1, tq), F32)],
        compiler_params=_cparams(("parallel", "parallel")),
        name="nsa_window",
    )(q, k, k, vt, vt, gates_t)


def _compress_kernel(kc_ref, vc_ref, pos_ref, w1a_ref, w1b_ref, w2_ref, ko_ref, vo_ref, *, nchunk):
    row = lax.broadcasted_iota(jnp.int32, (nchunk, LANES), 0)
    for br, src in enumerate((kc_ref, vc_ref)):
        pa = pb = None
        for l in range(CMP_STRIDE):
            r = src[0, pl.ds(l, nchunk, stride=CMP_STRIDE), :]
            sl = slice(l * LANES, (l + 1) * LANES)
            ta = _dot((r + pos_ref[br, 0:1, sl]).astype(BF16), w1a_ref[br, sl, :])
            tb = _dot((r + pos_ref[br, 1:2, sl]).astype(BF16), w1b_ref[br, sl, :])
            pa = ta if pa is None else pa + ta
            pb = tb if pb is None else pb + tb
        hid = _gelu_tanh(pa + pltpu.roll(pb, nchunk - 1, 0))
        out = jnp.where(row < nchunk - 1, _dot(hid.astype(BF16), w2_ref[br]), 0.0)
        if br == 0:
            ko_ref[0] = out.astype(ko_ref.dtype)
        else:
            vo_ref[0] = out.T.astype(vo_ref.dtype)


def _prep_compress_weights(cmp_pos, cmp_w1, cmp_w2):
    half = CMP_LEN // 2
    pos, w1a, w1b, w2 = [], [], [], []
    for br in range(2):
        p = cmp_pos[br]
        tile = lambda ph: jnp.concatenate([ph, ph], axis=1).reshape(1, half * LANES)
        pos.append(jnp.concatenate([tile(p[:half]), tile(p[half:])], axis=0))
        w = cmp_w1[br].reshape(CMP_LEN, NSA_HEAD_DIM, CMP_HIDDEN)
        z = jnp.zeros_like(w[:half])

        def big(wh):
            g0 = jnp.concatenate([wh, z], axis=1)
            g1 = jnp.concatenate([z, wh], axis=1)
            return jnp.concatenate([g0.reshape(half * LANES, CMP_HIDDEN), g1.reshape(half * LANES, CMP_HIDDEN)], axis=1)

        w1a.append(big(w[:half]))
        w1b.append(big(w[half:]))
        w2.append(_block_diag_dense(jnp.stack([cmp_w2[br], cmp_w2[br]])))
    return (jnp.stack(pos), jnp.stack(w1a).astype(BF16), jnp.stack(w1b).astype(BF16), jnp.stack(w2).astype(BF16))


def _compress(kc, vc, cmp_pos, cmp_w1, cmp_w2, batch, seq):
    nchunk = seq // CMP_STRIDE
    pos, w1a, w1b, w2 = _prep_compress_weights(cmp_pos, cmp_w1, cmp_w2)
    blk = pl.BlockSpec((1, seq, LANES), lambda b: (b, 0, 0))
    full3 = lambda shape: pl.BlockSpec(shape, lambda b: (0, 0, 0))
    return pl.pallas_call(
        functools.partial(_compress_kernel, nchunk=nchunk),
        grid=(batch,),
        in_specs=[blk, blk, full3(pos.shape), full3(w1a.shape), full3(w1b.shape), full3(w2.shape)],
        out_specs=[pl.BlockSpec((1, nchunk, LANES), lambda b: (b, 0, 0)),
                   pl.BlockSpec((1, LANES, nchunk), lambda b: (b, 0, 0))],
        out_shape=(jax.ShapeDtypeStruct((batch, nchunk, LANES), BF16),
                   jax.ShapeDtypeStruct((batch, LANES, nchunk), BF16)),
        compiler_params=_cparams(("parallel",)),
        name="nsa_compress",
    )(kc.reshape(batch, seq, LANES), vc.reshape(batch, seq, LANES), pos, w1a, w1b, w2)


def _select_kernel(q_ref, kc_ref, vct_ref, ovt_ref, gt_ref, oc_ref, sb_ref, ot_sc, imp_sc, *, tq, nchunk, variants):
    nsel = LANES
    qt = pl.program_id(1)
    col_ok = (qt * tq + lax.broadcasted_iota(jnp.int32, (1, tq), 1) >= CMP_LEN - 1).astype(F32)

    def attend(nc):
        kc = kc_ref[0, 0:nc, :]
        qpos = qt * tq + lax.broadcasted_iota(jnp.int32, (nc, tq), 1)
        cend = lax.broadcasted_iota(jnp.int32, (nc, tq), 0) * CMP_STRIDE + (CMP_LEN - 1)
        cvalid = cend <= qpos
        ovt = ovt_ref[:, 0:nc]
        for g in range(NSA_KV_HEADS):
            vct = vct_ref[0, g * HEAD_V:(g + 1) * HEAD_V, 0:nc]
            psum = jnp.zeros((nc, tq), F32)
            for pj in range(NSA_GROUP):
                hd = g * NSA_GROUP + pj
                q = q_ref[0, :, hd * LANES:(hd + 1) * LANES]
                s = jnp.where(cvalid, _dot_nt(kc, q), NEG_INF)
                m = jnp.max(s, axis=0, keepdims=True)
                e = jnp.exp2(s - m)
                p = e * (col_ok / jnp.maximum(jnp.sum(e, axis=0, keepdims=True), 1e-30))
                ot_sc[hd * HEAD_V:(hd + 1) * HEAD_V, :] = _dot(vct, p.astype(BF16)) * gt_ref[0, 3 * hd:3 * hd + 1, :]
                psum = psum + p
            hi = psum.astype(BF16)
            r1 = psum - hi.astype(F32)
            mid = r1.astype(BF16)
            lo = (r1 - mid.astype(F32)).astype(BF16)
            imp_sc[g] = _dot(ovt, hi) + _dot(ovt, mid) + _dot(ovt, lo)

    needed = (qt * tq + tq - CMP_STRIDE) // CMP_STRIDE
    lo_bound = 0
    for nc in variants:
        pl.when((needed > lo_bound) & (needed <= nc) if nc != variants[-1] else needed > lo_bound)(
            functools.partial(attend, nc))
        lo_bound = nc

    blk = lax.broadcasted_iota(jnp.int32, (nsel, tq), 0)
    blkf = blk.astype(F32)
    qp = qt * tq + lax.broadcasted_iota(jnp.int32, (nsel, tq), 1)
    valid = blk * SEL_LEN <= qp
    forced = (blk == 0) | (blk == qp // SEL_LEN)
    for g in range(NSA_KV_HEADS):
        score = jnp.where(valid, imp_sc[g], -1.0)
        score0 = jnp.where(forced, score + FORCE_BONUS, score)
        score = score0
        for _ in range(min(SEL_TOPK, nsel)):
            m = jnp.max(score, axis=0, keepdims=True)
            idx = jnp.min(jnp.where(score == m, blkf, float(nsel)), axis=0, keepdims=True)
            score = jnp.where(blkf == idx, -jnp.inf, score)
        bias = jnp.where(score0 >= 0.0, jnp.where(score == -jnp.inf, 0.0, NEG_INF), NEG_INF)
        sb_ref[0, g] = bias.T.astype(sb_ref.dtype)
    oc_ref[0] = ot_sc[...].T


def _overlap_matrix_t(nchunk, nsel):
    n = np.arange(nchunk)[None, :]
    s = np.arange(nsel)[:, None]
    cs = n * CMP_STRIDE
    ov = (cs < s * SEL_LEN + SEL_LEN) & (cs + CMP_LEN - 1 >= s * SEL_LEN) & (n < nchunk - 1)
    return jnp.asarray(ov.astype(np.float32)).astype(BF16)


def _select(qn, kcmp, vcmp_t, gates_t, batch, seq, tq=512):
    nchunk = seq // CMP_STRIDE
    nsel = LANES
    assert seq // SEL_LEN <= nsel
    qw = N_ATT_HEADS * LANES
    ow = N_ATT_HEADS * HEAD_V
    ovt = _overlap_matrix_t(nchunk, nsel)
    quarter = nchunk // 4
    variants = tuple(quarter * i for i in range(1, 5)) if quarter % LANES == 0 else (nchunk,)
    return pl.pallas_call(
        functools.partial(_select_kernel, tq=tq, nchunk=nchunk, variants=variants),
        grid=(batch, seq // tq),
        in_specs=[pl.BlockSpec((1, tq, qw), lambda b, t: (b, t, 0)),
                  pl.BlockSpec((1, nchunk, LANES), lambda b, t: (b, 0, 0)),
                  pl.BlockSpec((1, NSA_KV_HEADS * HEAD_V, nchunk), lambda b, t: (b, 0, 0)),
                  pl.BlockSpec((nsel, nchunk), lambda b, t: (0, 0)),
                  pl.BlockSpec((1, GATE_ROWS, tq), lambda b, t: (b, 0, t))],
        out_specs=[pl.BlockSpec((1, tq, ow), lambda b, t: (b, t, 0)),
                   pl.BlockSpec((1, NSA_KV_HEADS, tq, nsel), lambda b, t: (b, 0, t, 0))],
        out_shape=(jax.ShapeDtypeStruct((batch, seq, ow), F32),
                   jax.ShapeDtypeStruct((batch, NSA_KV_HEADS, seq, nsel), BF16)),
        scratch_shapes=[pltpu.VMEM((ow, tq), F32), pltpu.VMEM((NSA_KV_HEADS, nsel, tq), F32)],
        compiler_params=_cparams(("parallel", "parallel")),
        name="nsa_select",
    )(qn, kcmp, vcmp_t, ovt, gates_t)


def _out_proj_kernel(x_ref, a_ref, b_ref, oc_ref, os_ref, ow_ref, wa_ref, wb_ref, wc_ref, o_ref):
    acc = x_ref[...] + _dot(a_ref[...].astype(BF16), wa_ref[...])
    acc = acc + _dot(b_ref[...], wb_ref[...])
    c = (oc_ref[...] + os_ref[...]) + ow_ref[...]
    o_ref[...] = acc + _dot(c.astype(BF16), wc_ref[...])


def _out_proj(x2, a_out, b_out, o_c, o_s, o_w, w_out, tm=512):
    T = x2.shape[0]
    wa = w_out[0:256].astype(BF16)
    wb = w_out[256:640].astype(BF16)
    wc = w_out[640:1024].astype(BF16)
    row = lambda i: (i, 0)
    fixed = lambda i: (0, 0)
    act = lambda n: pl.BlockSpec((tm, n), row)
    return pl.pallas_call(
        _out_proj_kernel,
        grid=(T // tm,),
        in_specs=[act(D_MODEL), act(256), act(384), act(384), act(384), act(384),
                  pl.BlockSpec((256, D_MODEL), fixed), pl.BlockSpec((384, D_MODEL), fixed),
                  pl.BlockSpec((384, D_MODEL), fixed)],
        out_specs=pl.BlockSpec((tm, D_MODEL), row),
        out_shape=jax.ShapeDtypeStruct((T, D_MODEL), F32),
        compiler_params=_cparams(("parallel",)),
        name="out_proj",
    )(x2, a_out, b_out, o_c, o_s, o_w, wa, wb, wc)


def _final_norm(y, gf_ref):
    ms = jnp.mean(y * y, axis=-1, keepdims=True)
    return (y * lax.rsqrt(ms + EPS)) * gf_ref[...]


def _ffn_kernel(x_ref, g_ref, wg_ref, wu_ref, wd_ref, gf_ref, o_ref, h_sc, acc_sc, *, final_norm):
    f = pl.program_id(1)

    @pl.when(f == 0)
    def _():
        x = x_ref[...]
        ms = jnp.mean(x * x, axis=-1, keepdims=True)
        h_sc[...] = ((x * lax.rsqrt(ms + EPS)) * g_ref[...]).astype(BF16)
        acc_sc[...] = x

    h = h_sc[...]
    gte = _dot(h, wg_ref[...])
    up = _dot(h, wu_ref[...])
    act = (gte * _sigmoid(gte)) * up
    acc_sc[...] += _dot(act.astype(BF16), wd_ref[...])

    @pl.when(f == pl.num_programs(1) - 1)
    def _():
        y = acc_sc[...]
        o_ref[...] = _final_norm(y, gf_ref) if final_norm else y


def _ffn(x2, g, wg, wu, wd, g_final, final_norm, tm=512, tf=1408):
    T = x2.shape[0]
    nf = D_FF // tf
    return pl.pallas_call(
        functools.partial(_ffn_kernel, final_norm=final_norm),
        grid=(T // tm, nf),
        in_specs=[pl.BlockSpec((tm, D_MODEL), lambda i, f: (i, 0)), pl.BlockSpec((1, D_MODEL), lambda i, f: (0, 0)),
                  pl.BlockSpec((D_MODEL, tf), lambda i, f: (0, f)), pl.BlockSpec((D_MODEL, tf), lambda i, f: (0, f)),
                  pl.BlockSpec((tf, D_MODEL), lambda i, f: (f, 0)), pl.BlockSpec((1, D_MODEL), lambda i, f: (0, 0))],
        out_specs=pl.BlockSpec((tm, D_MODEL), lambda i, f: (i, 0)),
        out_shape=jax.ShapeDtypeStruct((T, D_MODEL), F32),
        scratch_shapes=[pltpu.VMEM((tm, D_MODEL), BF16), pltpu.VMEM((tm, D_MODEL), F32)],
        compiler_params=_cparams(("parallel", "arbitrary")),
        name="ffn_dense",
    )(x2, g.reshape(1, -1), wg.astype(BF16), wu.astype(BF16), wd.astype(BF16), g_final.reshape(1, -1))


MOE_TILE = 1024
MOE_CHUNK = 384
MOE_CHUNK_SMALL = 256
MOE_CHUNK_MID = 320


def _moe_kernel(x_ref, g_ref, rw_ref, tri_ref, wg_ref, wu_ref, wd_ref, gf_ref, o_ref,
                h_sc, gate_sc, key_sc, keyt_sc, acc_sc, *, final_norm, tm):
    e = pl.program_id(1)
    lane = lax.broadcasted_iota(jnp.int32, (tm, LANES), 1)

    @pl.when(e == 0)
    def _():
        x = x_ref[...]
        ms = jnp.mean(x * x, axis=-1, keepdims=True)
        hf = (x * lax.rsqrt(ms + EPS)) * g_ref[...]
        h_sc[...] = hf.astype(BF16)
        acc_sc[...] = x
        h_hi = hf.astype(BF16)
        h_lo = (hf - h_hi.astype(F32)).astype(BF16)
        logits = (_dot(h_hi, rw_ref[0]) + _dot(h_lo, rw_ref[0]) + _dot(h_hi, rw_ref[1]))
        logits = jnp.where(lane < N_EXPERTS, logits, -jnp.inf)
        m1 = jnp.max(logits, axis=1, keepdims=True)
        i1 = jnp.min(jnp.where(logits == m1, lane, LANES), axis=1, keepdims=True)
        rest = jnp.where(lane == i1, -jnp.inf, logits)
        m2 = jnp.max(rest, axis=1, keepdims=True)
        i2 = jnp.min(jnp.where(rest == m2, lane, LANES), axis=1, keepdims=True)
        e2 = jnp.exp(m2 - m1)
        den = 1.0 + e2
        gate_sc[...] = jnp.where(lane == i1, 1.0 / den, 0.0) + jnp.where(lane == i2, e2 / den, 0.0)
        routed = jnp.where((lane == i1) | (lane == i2), 1.0, 0.0)
        key = jnp.where(routed > 0.0, _dot(tri_ref[...], routed.astype(BF16)), -1.0)
        key_sc[...] = key
        keyt_sc[...] = key.T

    sel = lane == e
    kcol = jnp.max(jnp.where(sel, key_sc[...], -1.0), axis=1, keepdims=True)
    wcol = jnp.sum(jnp.where(sel, gate_sc[...], 0.0), axis=1, keepdims=True)
    krow = keyt_sc[pl.ds(e, 1), :]
    count = jnp.sum(jnp.where(kcol >= 0.0, 1.0, 0.0)).astype(jnp.int32)

    def chunk(rows, ci, carry):
        base = (ci * rows).astype(F32)
        row_id = lax.broadcasted_iota(jnp.int32, (rows, tm), 0).astype(F32)
        col_id = lax.broadcasted_iota(jnp.int32, (tm, rows), 1).astype(F32)
        gather = jnp.where(krow - base == row_id, 1.0, 0.0).astype(BF16)
        xg = _dot(gather, h_sc[...]).astype(BF16)
        gte = _dot(xg, wg_ref[0])
        up = _dot(xg, wu_ref[0])
        act = (gte * _sigmoid(gte)) * up
        y = _dot(act.astype(BF16), wd_ref[0])
        scatter = jnp.where(kcol - base == col_id, 1.0, 0.0).astype(BF16)
        acc_sc[...] += wcol * _dot(scatter, y.astype(BF16))
        return carry

    @pl.when((count > 0) & (count <= MOE_CHUNK_SMALL))
    def _():
        chunk(MOE_CHUNK_SMALL, jnp.int32(0), 0)

    @pl.when((count > MOE_CHUNK_SMALL) & (count <= MOE_CHUNK_MID))
    def _():
        chunk(MOE_CHUNK_MID, jnp.int32(0), 0)

    n_chunks = jnp.where(count > MOE_CHUNK_MID, (count + (MOE_CHUNK - 1)) // MOE_CHUNK, 0)
    lax.fori_loop(0, n_chunks, functools.partial(chunk, MOE_CHUNK), 0)

    @pl.when(e == pl.num_programs(1) - 1)
    def _():
        out = acc_sc[...]
        o_ref[...] = _final_norm(out, gf_ref) if final_norm else out


def _moe(x2, g, router_w, wg, wu, wd, g_final, final_norm, tm=MOE_TILE):
    T = x2.shape[0]
    rw = _pad_cols(router_w, LANES)
    rw_hi = rw.astype(BF16)
    rw_lo = (rw - rw_hi.astype(F32)).astype(BF16)
    rw2 = jnp.stack([rw_hi, rw_lo])
    tri = jnp.asarray(np.tril(np.ones((tm, tm), np.float32), -1)).astype(BF16)
    return pl.pallas_call(
        functools.partial(_moe_kernel, final_norm=final_norm, tm=tm),
        grid=(T // tm, N_EXPERTS),
        in_specs=[pl.BlockSpec((tm, D_MODEL), lambda i, e: (i, 0)), pl.BlockSpec((1, D_MODEL), lambda i, e: (0, 0)),
                  pl.BlockSpec((2, D_MODEL, LANES), lambda i, e: (0, 0, 0)),
                  pl.BlockSpec((tm, tm), lambda i, e: (0, 0)),
                  pl.BlockSpec((1, D_MODEL, D_FF_EXPERT), lambda i, e: (e, 0, 0)),
                  pl.BlockSpec((1, D_MODEL, D_FF_EXPERT), lambda i, e: (e, 0, 0)),
                  pl.BlockSpec((1, D_FF_EXPERT, D_MODEL), lambda i, e: (e, 0, 0)),
                  pl.BlockSpec((1, D_MODEL), lambda i, e: (0, 0))],
        out_specs=pl.BlockSpec((tm, D_MODEL), lambda i, e: (i, 0)),
        out_shape=jax.ShapeDtypeStruct((T, D_MODEL), F32),
        scratch_shapes=[pltpu.VMEM((tm, D_MODEL), BF16), pltpu.VMEM((tm, LANES), F32),
                        pltpu.VMEM((tm, LANES), F32), pltpu.VMEM((LANES, tm), F32),
                        pltpu.VMEM((tm, D_MODEL), F32)],
        compiler_params=_cparams(("parallel", "arbitrary")),
        name="moe_routed",
    )(x2, g.reshape(1, -1), rw2, tri, wg.astype(BF16), wu.astype(BF16), wd.astype(BF16), g_final.reshape(1, -1))


def _mixer(x2, batch, seq, tables, onehot, norm_g, w_in, conv_w, conv_b, wa, ba, wx, bx, lam,
           q_norm, w_uq, kv_norm, w_ukv, cmp_pos, cmp_w1, cmp_w2, gate_b, w_out):
    w_cat, w_t = _prep_w_in(w_in)
    lru_xy, q_m, k_m, qn, kc, vc, ksw, vt_sw, gates_t, vt_m = _in_proj(
        x2, norm_g, w_cat, w_t, tables, gate_b, q_norm, w_uq, kv_norm, w_ukv, batch, seq)
    a_out = _lru(lru_xy, conv_w, conv_b, wa, ba, wx, bx, lam, batch, seq)
    b3 = lambda t: t.reshape(batch, seq, t.shape[-1])
    heads = tuple(range(N_ATT_HEADS))
    b_out = _flash(b3(q_m), b3(k_m), vt_m, mode="causal", batch=batch, seq=seq, per_head_kv=True,
                   k_width=N_ATT_HEADS * LANES, k_col=0, vt_rows=N_ATT_HEADS * HEAD_V, vt_row=0, out_dtype=BF16,
                   tk=LONG_KEY_TILE)

    kcmp, vcmp_t = _compress(kc, vc, cmp_pos, cmp_w1, cmp_w2, batch, seq)
    qn3 = b3(qn)
    o_c, selbias = _select(qn3, kcmp, vcmp_t, gates_t, batch, seq)
    ksw3 = b3(ksw)
    nsa_kw = dict(batch=batch, seq=seq, per_head_kv=False, k_width=LANES,
                  vt_rows=NSA_KV_HEADS * HEAD_V, out_dtype=F32, gates_t=gates_t)
    o_s = _flash(qn3, ksw3, vt_sw, mode="select", k_col=0, vt_row=0, selbias=selbias,
                 onehot=onehot, gate_rows=tuple(3 * h + 1 for h in heads), tk=LONG_KEY_TILE, **nsa_kw)
    o_w = _window(qn3, ksw3, vt_sw, gates_t, batch=batch, seq=seq, k_col=1, vt_row=1,
                  gate_rows=tuple(3 * h + 2 for h in heads))
    T = batch * seq
    flat = lambda t: t.reshape(T, t.shape[-1])
    return _out_proj(x2, a_out, flat(b_out), flat(o_c), flat(o_s), flat(o_w), w_out)


def kernel(x, norm_mix, w_in, lru_conv_w, lru_conv_b, lru_wa, lru_ba, lru_wx, lru_bx, lru_lambda, mla_q_norm, mla_w_uq, mla_kv_norm, mla_w_ukv, nsa_cmp_pos, nsa_cmp_w1, nsa_cmp_w2, nsa_gate_b, w_out, norm_ffn, ffn_w_gate, ffn_w_up, ffn_w_down, router_w, moe_w_gate, moe_w_up, moe_w_down, norm_final):
    batch, seq, _ = x.shape
    depth = norm_mix.shape[0]
    tables = _rope_tables(seq)
    blk_id = np.arange(seq)[:, None] // SEL_LEN
    onehot = jnp.asarray((blk_id == np.arange(LANES)[None, :]).astype(np.float32)).astype(BF16)
    x2 = x.reshape(batch * seq, D_MODEL)
    for l in range(depth):
        x2 = _mixer(x2, batch, seq, tables, onehot, norm_mix[l], w_in[l], lru_conv_w[l], lru_conv_b[l],
                    lru_wa[l], lru_ba[l], lru_wx[l], lru_bx[l], lru_lambda[l], mla_q_norm[l], mla_w_uq[l],
                    mla_kv_norm[l], mla_w_ukv[l], nsa_cmp_pos[l], nsa_cmp_w1[l], nsa_cmp_w2[l],
                    nsa_gate_b[l], w_out[l])
        last = l == depth - 1
        j = l // 2
        if l % 2 == 0:
            x2 = _ffn(x2, norm_ffn[l], ffn_w_gate[j], ffn_w_up[j], ffn_w_down[j], norm_final, last)
        else:
            x2 = _moe(x2, norm_ffn[l], router_w[j], moe_w_gate[j], moe_w_up[j], moe_w_down[j], norm_final, last)
    if depth == 0:
        raise ValueError("depth must be positive")
    return x2.reshape(batch, seq, D_MODEL)
```

```python
import functools
import math

import numpy as np
import jax
import jax.numpy as jnp
from jax import lax
from jax.experimental import pallas as pl
from jax.experimental.pallas import tpu as pltpu

F32 = jnp.float32
BF16 = jnp.bfloat16

D_MODEL = 1024
LRU_WIDTH = 256
CONV_WIDTH = 4
LRU_C = 8.0
MLA_HEADS = 6
MLA_NOPE = 64
MLA_ROPE = 32
MLA_Q_RANK = 192
NSA_HEADS = 6
NSA_KV_HEADS = 2
NSA_GROUP = NSA_HEADS // NSA_KV_HEADS
NSA_HEAD_DIM = 64
CMP_LEN = 32
CMP_STRIDE = 16
CMP_HIDDEN = 128
SEL_LEN = 64
SEL_TOPK = 16
WINDOW = 512
FORCE_BONUS = 1.0e3
D_FF = 2816
N_EXPERTS = 8
D_FF_EXPERT = 1408
ROPE_THETA = 10000.0
EPS = 1e-6
NEG_INF = -1.0e30

LANES = 128
VMEM_LIMIT = 56 * 1024 * 1024

C_LRU = 0
C_MLA = 512
C_KPE_ROT = 1024
C_QN = 1152
C_KC = 1536
C_VC = 1664
C_KS = 1792
C_KW = 1920
IN_COLS_PAD = 2048
N_ATT_HEADS = 6
HEAD_V = 64
LONG_KEY_TILE = 1024
SUM_ROWS = 16
GATE_ROWS = 32
LOG2E = 1.4426950408889634


def _cparams(sem):
    return pltpu.CompilerParams(dimension_semantics=sem, vmem_limit_bytes=VMEM_LIMIT)


def _gelu_tanh(x):
    return 0.5 * x * (1.0 + jnp.tanh(math.sqrt(2.0 / math.pi) * (x + 0.044715 * (x * x * x))))


def _sigmoid(x):
    return 1.0 / (1.0 + jnp.exp(-x))


def _dot(a, b):
    return jnp.dot(a, b, preferred_element_type=F32)


def _dot_nt(a, b):
    return lax.dot_general(a, b, (((1,), (1,)), ((), ())), preferred_element_type=F32)


def _rot_cols(w, half):
    return jnp.concatenate([-w[:, half:], w[:, :half]], axis=1)


def _pad_cols(w, n):
    return jnp.pad(w, ((0, 0), (0, n - w.shape[1])))


def _prep_w_in(w_in):
    splits = np.cumsum([256, 256, 192, 128, 32, 384, 128, 128, 128, 128, 128, 128, 18])[:-1].tolist()
    (x_l, y_l, q_dn, kv_dn, k_pe, q_n, kc, vc, ks, vs, kw, vw, g_n) = jnp.split(w_in, splits, axis=1)
    z = lambda n: jnp.zeros((w_in.shape[0], n), w_in.dtype)

    def kpe_slot(w):
        return jnp.concatenate([z(64), w, z(32)], axis=1)

    def qn_slots(fn):
        out = []
        for p in range(NSA_GROUP):
            for h in (p, p + NSA_GROUP):
                out.append(fn(q_n[:, h * 64:(h + 1) * 64]))
        return jnp.concatenate(out, axis=1)

    cols = [x_l, y_l, _pad_cols(q_dn, 256), kv_dn, kpe_slot(k_pe), kpe_slot(_rot_cols(k_pe, 16)),
            qn_slots(lambda w: w), kc, vc, ks, kw]
    w = jnp.concatenate(cols, axis=1)
    assert w.shape[1] == IN_COLS_PAD
    w_t = jnp.concatenate([vs, vw, _pad_cols(g_n, GATE_ROWS)], axis=1).T
    return w.astype(BF16), w_t.astype(BF16)


def _rope_tables(seq):
    pos = jnp.arange(seq, dtype=F32)[:, None]
    inv64 = ROPE_THETA ** (-jnp.arange(32, dtype=F32) * 2.0 / 64)
    inv32 = ROPE_THETA ** (-jnp.arange(16, dtype=F32) * 2.0 / 32)
    a64 = pos * inv64[None, :]
    a32 = pos * inv32[None, :]
    c64, s64 = jnp.cos(a64), jnp.sin(a64)
    c32, s32 = jnp.cos(a32), jnp.sin(a32)
    cos_n = jnp.concatenate([c64, c64, c64, c64], axis=1)
    sin_n = jnp.concatenate([-s64, s64, -s64, s64], axis=1)
    one = jnp.ones((seq, 64), F32)
    cos_m = jnp.concatenate([one, c32, c32, one[:, :32]], axis=1)
    sin_m = jnp.concatenate([0 * one, s32, s32, 0 * one[:, :32]], axis=1)
    return cos_n, sin_n, cos_m, sin_m


def _in_proj_kernel(x_ref, g_ref, w_ref, wt_ref, cn_ref, sn_ref, cm_ref, sm_ref, gb_ref,
                    qnorm_ref, wqa_ref, wqb_ref, kvnorm_ref, wk_ref, wvt_ref,
                    lru_ref, qm_ref, km_ref, qn_ref, kc_ref, vc_ref, ksw_ref, vt_ref, gt_ref, vtm_ref):
    x = x_ref[...]
    ms = jnp.mean(x * x, axis=-1, keepdims=True)
    h = ((x * lax.rsqrt(ms + EPS)) * g_ref[...]).astype(BF16)

    def proj(c0, n):
        return _dot(h, w_ref[:, c0:c0 + n])

    cn, sn = cn_ref[...], sn_ref[...]
    lru_ref[...] = proj(C_LRU, 512)
    cm, sm = cm_ref[...], sm_ref[...]
    kpe = proj(C_MLA + 384, 128) * cm + proj(C_KPE_ROT, 128) * sm
    _mla_heads(proj(C_MLA, 256), proj(C_MLA + 256, 128), kpe, cm, sm, qnorm_ref, wqa_ref, wqb_ref, kvnorm_ref,
               wk_ref, wvt_ref, qm_ref, km_ref, vtm_ref)
    scale = NSA_HEAD_DIM ** -0.5 * LOG2E
    lane = lax.broadcasted_iota(jnp.int32, cn.shape, 1)
    low_half = lane < NSA_HEAD_DIM
    first_half = (lane % NSA_HEAD_DIM) < NSA_HEAD_DIM // 2

    def rope_n(c0):
        a = proj(c0, LANES)
        swapped = jnp.where(first_half, pltpu.roll(a, LANES - NSA_HEAD_DIM // 2, 1), pltpu.roll(a, NSA_HEAD_DIM // 2, 1))
        return a * cn + swapped * sn

    for p in range(NSA_GROUP):
        r = rope_n(C_QN + p * LANES) * scale
        hi_hd = p + NSA_GROUP
        qn_ref[:, p * LANES:(p + 1) * LANES] = jnp.where(low_half, r, 0.0).astype(BF16)
        qn_ref[:, hi_hd * LANES:(hi_hd + 1) * LANES] = jnp.where(low_half, 0.0, r).astype(BF16)
    kc_ref[...] = rope_n(C_KC)
    vc_ref[...] = proj(C_VC, 128)
    ksw_ref[:, 0:128] = rope_n(C_KS).astype(BF16)
    ksw_ref[:, 128:256] = rope_n(C_KW).astype(BF16)
    vt_ref[0] = _dot_nt(wt_ref[0:256, :], h).astype(BF16)
    gt_ref[0] = _sigmoid(_dot_nt(wt_ref[256:256 + GATE_ROWS, :], h) + gb_ref[...])


def _in_proj(x2, g, w_cat, w_t, tables, gate_b, q_norm, w_uq, kv_norm, w_ukv, batch, seq, tm=512):
    T = x2.shape[0]
    nt = T // tm
    npos = seq // tm
    cn, sn, cm, sm = tables
    wqa, wqb, wk, wvt = _prep_mla_weights(w_uq, w_ukv)
    row = lambda i: (i, 0)
    fixed = lambda i: (0, 0)
    posmap = lambda i: (i % npos, 0)
    tab_spec = pl.BlockSpec((tm, LANES), posmap)
    whole = lambda a: pl.BlockSpec(a.shape, fixed)
    out_shapes = (
        jax.ShapeDtypeStruct((T, 512), F32),
        jax.ShapeDtypeStruct((T, 768), BF16),
        jax.ShapeDtypeStruct((T, 768), BF16),
        jax.ShapeDtypeStruct((T, 768), BF16),
        jax.ShapeDtypeStruct((T, 128), F32),
        jax.ShapeDtypeStruct((T, 128), F32),
        jax.ShapeDtypeStruct((T, 256), BF16),
    )
    t_shapes = (jax.ShapeDtypeStruct((batch, 256, seq), BF16),
                jax.ShapeDtypeStruct((batch, GATE_ROWS, seq), F32),
                jax.ShapeDtypeStruct((batch, 384, seq), BF16))
    tmap = lambda i: (i // npos, 0, i % npos)
    gate_b_col = jnp.pad(gate_b, (0, GATE_ROWS - gate_b.shape[0])).reshape(GATE_ROWS, 1)
    q_norm_p = _pad_cols(q_norm.reshape(1, -1), 256)
    kv_norm_r = kv_norm.reshape(1, -1)
    return pl.pallas_call(
        _in_proj_kernel,
        grid=(nt,),
        in_specs=[pl.BlockSpec((tm, D_MODEL), row), pl.BlockSpec((1, D_MODEL), fixed),
                  pl.BlockSpec((D_MODEL, IN_COLS_PAD), fixed), pl.BlockSpec((256 + GATE_ROWS, D_MODEL), fixed),
                  tab_spec, tab_spec, tab_spec, tab_spec, pl.BlockSpec((GATE_ROWS, 1), fixed),
                  whole(q_norm_p), whole(wqa), whole(wqb), whole(kv_norm_r), whole(wk), whole(wvt)],
        out_specs=[pl.BlockSpec((tm, s.shape[1]), row) for s in out_shapes]
        + [pl.BlockSpec((1, s.shape[1], tm), tmap) for s in t_shapes],
        out_shape=out_shapes + t_shapes,
        compiler_params=_cparams(("parallel",)),
        name="in_proj",
    )(x2, g.reshape(1, -1), w_cat, w_t, cn, sn, cm, sm, gate_b_col, q_norm_p, wqa, wqb, kv_norm_r, wk, wvt)


def _lru_kernel(xy_ref, cw_ref, cb_ref, wa_ref, ba_ref, wx_ref, bx_ref, lam_ref, out_ref,
                xe_sc, h_sc, *, tt):
    t = pl.program_id(1)

    @pl.when(t == 0)
    def _():
        xe_sc[0:8, :] = jnp.zeros((8, LRU_WIDTH), F32)
        h_sc[...] = jnp.zeros_like(h_sc)

    x = xy_ref[0, :, 0:LRU_WIDTH]
    y = xy_ref[0, :, LRU_WIDTH:2 * LRU_WIDTH]
    xe_sc[8:8 + tt, :] = x
    xc = cb_ref[...]
    for k in range(CONV_WIDTH):
        off = 8 - (CONV_WIDTH - 1) + k
        xc = xc + xe_sc[off:off + tt, :] * cw_ref[k:k + 1, :]
    xe_sc[0:8, :] = x[tt - 8:tt, :]

    xb = xc.astype(BF16)
    r = _sigmoid(_dot(xb, wa_ref[...]) + ba_ref[...])
    i = _sigmoid(_dot(xb, wx_ref[...]) + bx_ref[...])
    log_a = (-LRU_C * r) * jax.nn.softplus(-lam_ref[...])
    a = jnp.exp(log_a)
    b = jnp.sqrt(jnp.tanh(-log_a) * (a * a + 1.0)) * (i * xc)

    row = lax.broadcasted_iota(jnp.int32, (tt, LRU_WIDTH), 0)
    k = 1
    while k < tt:
        keep = row >= k
        a_sh = jnp.where(keep, pltpu.roll(a, k, 0), 1.0)
        b_sh = jnp.where(keep, pltpu.roll(b, k, 0), 0.0)
        b = a * b_sh + b
        a = a * a_sh
        k *= 2
    h = b + a * h_sc[0:1, :]
    h_sc[0:1, :] = h[tt - 1:tt, :]
    out_ref[0] = h * _gelu_tanh(y)


def _block_diag_dense(w):
    n, c, d = w.shape
    z = jnp.zeros((c, d), w.dtype)
    return jnp.concatenate([jnp.concatenate([w[j] if i == j else z for i in range(n)], axis=1) for j in range(n)],
                           axis=0)


def _lru(lru_xy, conv_w, conv_b, wa, ba, wx, bx, lam, batch, seq, tt=512):
    xy = lru_xy.reshape(batch, seq, 512)
    fixed = lambda b, t: (0, 0)
    vec = pl.BlockSpec((1, LRU_WIDTH), fixed)
    mat = pl.BlockSpec((LRU_WIDTH, LRU_WIDTH), fixed)
    out = pl.pallas_call(
        functools.partial(_lru_kernel, tt=tt),
        grid=(batch, seq // tt),
        in_specs=[pl.BlockSpec((1, tt, 512), lambda b, t: (b, t, 0)),
                  pl.BlockSpec((CONV_WIDTH, LRU_WIDTH), fixed), vec, mat, vec, mat, vec, vec],
        out_specs=pl.BlockSpec((1, tt, LRU_WIDTH), lambda b, t: (b, t, 0)),
        out_shape=jax.ShapeDtypeStruct((batch, seq, LRU_WIDTH), F32),
        scratch_shapes=[pltpu.VMEM((tt + 8, LRU_WIDTH), F32), pltpu.VMEM((8, LRU_WIDTH), F32)],
        compiler_params=_cparams(("parallel", "arbitrary")),
        name="rg_lru",
    )(xy, conv_w, conv_b.reshape(1, -1), _block_diag_dense(wa).astype(BF16), ba.reshape(1, -1),
      _block_diag_dense(wx).astype(BF16), bx.reshape(1, -1), lam.reshape(1, -1))
    return out.reshape(batch * seq, LRU_WIDTH)


def _mla_heads(q_dn, kv_dn, kpe, cm, sm, qn_ref, wqa_ref, wqb_ref, kvn_ref, wk_ref, wvt_ref, q_ref, k_ref, vt_ref):
    ms = jnp.sum(q_dn * q_dn, axis=-1, keepdims=True) * (1.0 / MLA_Q_RANK)
    ql = ((q_dn * lax.rsqrt(ms + EPS)) * qn_ref[...]).astype(BF16)
    ms = jnp.mean(kv_dn * kv_dn, axis=-1, keepdims=True)
    kvl = ((kv_dn * lax.rsqrt(ms + EPS)) * kvn_ref[...]).astype(BF16)
    scale = (MLA_NOPE + MLA_ROPE) ** -0.5 * LOG2E
    for hd in range(MLA_HEADS):
        sl = slice(hd * LANES, (hd + 1) * LANES)
        q_ref[:, sl] = ((_dot(ql, wqa_ref[:, sl]) * cm + _dot(ql, wqb_ref[:, sl]) * sm) * scale).astype(BF16)
        k_ref[:, sl] = (_dot(kvl, wk_ref[:, sl]) + kpe).astype(BF16)
    vt_ref[0] = _dot_nt(wvt_ref[...], kvl).astype(BF16)


def _prep_mla_weights(w_uq, w_ukv):
    zq = jnp.zeros((MLA_Q_RANK, 32), w_uq.dtype)
    qa, qb, kk, vv = [], [], [], []
    for h in range(MLA_HEADS):
        wq = w_uq[:, h * 96:(h + 1) * 96]
        nope, ropew = wq[:, :64], wq[:, 64:]
        qa.append(jnp.concatenate([nope, ropew, zq], axis=1))
        qb.append(jnp.concatenate([0 * nope, _rot_cols(ropew, 16), zq], axis=1))
        wkv = w_ukv[:, h * 128:(h + 1) * 128]
        kk.append(_pad_cols(wkv[:, :64], LANES))
        vv.append(wkv[:, 64:])
    pad_rows = lambda w: jnp.pad(w, ((0, 256 - MLA_Q_RANK), (0, 0)))
    return (pad_rows(jnp.concatenate(qa, axis=1)).astype(BF16), pad_rows(jnp.concatenate(qb, axis=1)).astype(BF16),
            jnp.concatenate(kk, axis=1).astype(BF16), jnp.concatenate(vv, axis=1).T.astype(BF16))


PROBE_KEYS = 128
LAG_LIMIT = 8.0


def _flash_kernel(qt_ref, kt_ref, first_ref, last_ref, par_ref, *refs, mode, k_per_head, v_per_head, gate_rows,
                  tq, tk):
    refs = list(refs)
    q_ref, k_ref, vt_ref = refs[:3]
    o_ref, m_st, acc_st = refs[-8:-5]
    s_sc, mx_sc, flag_sm = refs[-5:-3], refs[-3:-1], refs[-1]
    extra = refs[3:-8]
    sb_ref, oh_ref = (extra[0], extra[1]) if mode == "select" else (None, None)
    gt_ref = extra[-1] if gate_rows is not None else None
    step = pl.program_id(1)
    qt = qt_ref[step]
    kt = kt_ref[step]
    rd = par_ref[step]
    wr = 1 - rd

    def visible(nk):
        kpos = kt * tk + lax.broadcasted_iota(jnp.int32, (nk, tq), 0)
        qpos = qt * tq + lax.broadcasted_iota(jnp.int32, (nk, tq), 1)
        return kpos <= qpos

    def score_tile(j, nk, valid):
        q = q_ref[0, :, j * LANES:(j + 1) * LANES]
        k = k_ref[0, 0:nk, j * LANES:(j + 1) * LANES] if k_per_head else k_ref[0, 0:nk, :]
        if mode == "select":
            q = jnp.concatenate([q, sb_ref[0, j // NSA_GROUP]], axis=1)
            k = jnp.concatenate([k, oh_ref[0:nk, :]], axis=1)
        s = _dot_nt(k, q)
        return s if valid is None else jnp.where(valid, s, NEG_INF)

    def weighted_values(j, nk, p):
        row = (j if v_per_head else j // NSA_GROUP) * HEAD_V
        vt = vt_ref[0, row:row + HEAD_V, 0:nk]
        return _dot(jnp.concatenate([vt, jnp.ones((SUM_ROWS, nk), BF16)], axis=0), p)

    def exact(nk, init):
        valid = visible(nk)

        def scores(j, slot):
            s = score_tile(j, nk, valid)
            s_sc[slot][0:nk, :] = s
            mx_sc[slot][...] = jnp.max(s, axis=0, keepdims=True)

        def accumulate(j, slot):
            if init:
                m_new = mx_sc[slot][...]
            else:
                m_prev = m_st[rd, j]
                m_new = jnp.maximum(m_prev, mx_sc[slot][...])
            p = jnp.exp2((s_sc[slot][0:nk, :] - m_new).astype(BF16))
            upd = weighted_values(j, nk, p)
            acc_st[wr, j] = upd if init else jnp.exp2(m_prev - m_new) * acc_st[rd, j] + upd
            m_st[wr, j] = m_new

        scores(0, 0)
        for j in range(N_ATT_HEADS):
            if j + 1 < N_ATT_HEADS:
                scores(j + 1, (j + 1) % 2)
            accumulate(j, j % 2)

    def lagged(nk, masked, init):
        valid = visible(nk) if masked else None
        rise = None
        for j in range(N_ATT_HEADS):
            if init:
                probe = score_tile(j, PROBE_KEYS, None if valid is None else valid[0:PROBE_KEYS])
                m_prev = jnp.max(probe, axis=0, keepdims=True)
            else:
                m_prev = m_st[rd, j]
            s = score_tile(j, nk, valid)
            p = jnp.exp2((s - m_prev).astype(BF16))
            tile_max = jnp.max(s, axis=0, keepdims=True)
            m_new = jnp.maximum(m_prev, tile_max)
            upd = weighted_values(j, nk, p)
            acc_st[wr, j] = (upd if init else acc_st[rd, j] + upd) * jnp.exp2(m_prev - m_new)
            m_st[wr, j] = m_new
            d = tile_max - m_prev
            rise = d if rise is None else jnp.maximum(rise, d)
        flag_sm[0] = (jnp.max(rise) > LAG_LIMIT).astype(jnp.int32)

    first = first_ref[step] == 1
    later = jnp.logical_not(first)
    first_q, last_q = qt * tq, qt * tq + (tq - 1)
    crosses = kt * tk + (tk - 1) > first_q
    clear = jnp.logical_not(crosses)
    flag_sm[0] = 0
    for init, when in ((True, first), (False, later)):
        pl.when(when & clear)(functools.partial(lagged, tk, False, init))
        if tk > tq:
            half_only = kt * tk + tk // 2 > last_q
            pl.when(when & crosses & half_only)(functools.partial(lagged, tk // 2, True, init))
            pl.when(when & crosses & jnp.logical_not(half_only))(functools.partial(lagged, tk, True, init))
        else:
            pl.when(when & crosses)(functools.partial(lagged, tk, True, init))
    redo = flag_sm[0] == 1
    pl.when(redo & first)(lambda: exact(tk, True))
    pl.when(redo & later)(lambda: exact(tk, False))

    @pl.when(last_ref[step] == 1)
    def _():
        outs = []
        for j in range(N_ATT_HEADS):
            o = acc_st[wr, j, 0:HEAD_V, :] / acc_st[wr, j, HEAD_V:HEAD_V + 1, :]
            if gate_rows is not None:
                o = o * gt_ref[0, gate_rows[j]:gate_rows[j] + 1, :]
            outs.append(o)
        o_ref[0] = jnp.concatenate(outs, axis=0).T.astype(o_ref.dtype)


def _pair_tables(nq, tq, tk):
    qts, kts, first, last, par = [], [], [], [], []
    for qt in range(nq):
        hi = (qt * tq + tq - 1) // tk
        for kt in range(hi + 1):
            qts.append(qt)
            kts.append(kt)
            first.append(1 if kt == 0 else 0)
            last.append(1 if kt == hi else 0)
            par.append(kt % 2)
    arr = lambda v: jnp.asarray(np.array(v, dtype=np.int32))
    return arr(qts), arr(kts), arr(first), arr(last), arr(par)


def _flash(q, k, vt, *, mode, batch, seq, per_head_kv, k_width, k_col, vt_rows, vt_row, out_dtype,
           selbias=None, onehot=None, gate_rows=None, gates_t=None, tq=512, tk=512):
    tk = min(tk, seq)
    nq = seq // tq
    tables = _pair_tables(nq, tq, tk)
    n_steps = int(tables[0].shape[0])
    qw = N_ATT_HEADS * LANES
    ow = N_ATT_HEADS * HEAD_V
    in_specs = [
        pl.BlockSpec((1, tq, qw), lambda b, s, qt, kt, f, l, p: (b, qt[s], 0)),
        pl.BlockSpec((1, tk, k_width), lambda b, s, qt, kt, f, l, p: (b, kt[s], k_col)),
        pl.BlockSpec((1, vt_rows, tk), lambda b, s, qt, kt, f, l, p: (b, vt_row, kt[s])),
    ]
    args = [q, k, vt]
    if mode == "select":
        in_specs.append(pl.BlockSpec((1, NSA_KV_HEADS, tq, LANES), lambda b, s, qt, kt, f, l, p: (b, 0, qt[s], 0)))
        in_specs.append(pl.BlockSpec((tk, LANES), lambda b, s, qt, kt, f, l, p: (kt[s], 0)))
        args += [selbias, onehot]
    if gate_rows is not None:
        in_specs.append(pl.BlockSpec((1, GATE_ROWS, tq), lambda b, s, qt, kt, f, l, p: (b, 0, qt[s])))
        args.append(gates_t)
    kern = functools.partial(_flash_kernel, mode=mode, k_per_head=per_head_kv, v_per_head=per_head_kv,
                             gate_rows=gate_rows, tq=tq, tk=tk)
    return pl.pallas_call(
        kern,
        grid_spec=pltpu.PrefetchScalarGridSpec(
            num_scalar_prefetch=5,
            grid=(batch, n_steps),
            in_specs=in_specs,
            out_specs=pl.BlockSpec((1, tq, ow), lambda b, s, qt, kt, f, l, p: (b, qt[s], 0)),
            scratch_shapes=[pltpu.VMEM((2, N_ATT_HEADS, 1, tq), F32),
                            pltpu.VMEM((2, N_ATT_HEADS, HEAD_V + SUM_ROWS, tq), F32),
                            pltpu.VMEM((tk, tq), F32), pltpu.VMEM((tk, tq), F32),
                            pltpu.VMEM((1, tq), F32), pltpu.VMEM((1, tq), F32),
                            pltpu.SMEM((1,), jnp.int32)],
        ),
        out_shape=jax.ShapeDtypeStruct((batch, seq, ow), out_dtype),
        compiler_params=_cparams(("parallel", "arbitrary")),
        name="flash_" + mode,
    )(*tables, *args)


def _window_kernel(q_ref, kp_ref, kc_ref, vp_ref, vc_ref, gt_ref, o_ref, s_a, s_b, mx_a, mx_b, *, tq, gate_rows):
    qt = pl.program_id(1)
    krow = lax.broadcasted_iota(jnp.int32, (tq, tq), 0)
    qcol = lax.broadcasted_iota(jnp.int32, (tq, tq), 1)
    valid_prev = (qt > 0) & (qcol - krow < WINDOW - tq)
    valid_own = krow <= qcol
    ones = jnp.ones((SUM_ROWS, tq), BF16)
    slots, maxes = (s_a, s_b), (mx_a, mx_b)

    def scores(j):
        q = q_ref[0, :, j * LANES:(j + 1) * LANES]
        s_prev = jnp.where(valid_prev, _dot_nt(kp_ref[0], q), NEG_INF)
        s_own = jnp.where(valid_own, _dot_nt(kc_ref[0], q), NEG_INF)
        slots[j % 2][0:tq, :] = s_prev
        slots[j % 2][tq:2 * tq, :] = s_own
        maxes[j % 2][...] = jnp.maximum(jnp.max(s_prev, axis=0, keepdims=True), jnp.max(s_own, axis=0, keepdims=True))

    def output(j):
        m = maxes[j % 2][...]
        g = j // NSA_GROUP
        acc = None
        for half, v_ref in enumerate((vp_ref, vc_ref)):
            p = jnp.exp2((slots[j % 2][half * tq:(half + 1) * tq, :] - m).astype(BF16))
            vt = v_ref[0, g * HEAD_V:(g + 1) * HEAD_V, :]
            part = _dot(jnp.concatenate([vt, ones], axis=0), p)
            acc = part if acc is None else acc + part
        return (acc[0:HEAD_V] / acc[HEAD_V:HEAD_V + 1]) * gt_ref[0, gate_rows[j]:gate_rows[j] + 1, :]

    outs = []
    scores(0)
    for j in range(N_ATT_HEADS):
        if j + 1 < N_ATT_HEADS:
            scores(j + 1)
        outs.append(output(j))
    o_ref[0] = jnp.concatenate(outs, axis=0).T.astype(o_ref.dtype)


def _window(q, k, vt, gates_t, *, batch, seq, k_col, vt_row, gate_rows, tq=512):
    assert tq == WINDOW
    qw = N_ATT_HEADS * LANES
    ow = N_ATT_HEADS * HEAD_V
    prev = lambda t: jnp.maximum(t - 1, 0)
    vrows = NSA_KV_HEADS * HEAD_V
    return pl.pallas_call(
        functools.partial(_window_kernel, tq=tq, gate_rows=gate_rows),
        grid=(batch, seq // tq),
        in_specs=[pl.BlockSpec((1, tq, qw), lambda b, t: (b, t, 0)),
                  pl.BlockSpec((1, tq, LANES), lambda b, t: (b, prev(t), k_col)),
                  pl.BlockSpec((1, tq, LANES), lambda b, t: (b, t, k_col)),
                  pl.BlockSpec((1, vrows, tq), lambda b, t: (b, vt_row, prev(t))),
                  pl.BlockSpec((1, vrows, tq), lambda b, t: (b, vt_row, t)),
                  pl.BlockSpec((1, GATE_ROWS, tq), lambda b, t: (b, 0, t))],
        out_specs=pl.BlockSpec((1, tq, ow), lambda b, t: (b, t, 0)),
        out_shape=jax.ShapeDtypeStruct((batch, seq, ow), F32),
        scratch_shapes=[pltpu.VMEM((2 * tq, tq), F32), pltpu.VMEM((2 * tq, tq), F32),
                        pltpu.VMEM((1, tq), F32), pltpu.VMEM((1, tq), F32)],
        compiler_params=_cparams(("parallel", "parallel")),
        name="nsa_window",
    )(q, k, k, vt, vt, gates_t)


def _compress_kernel(kc_ref, vc_ref, pos_ref, w1a_ref, w1b_ref, w2_ref, ko_ref, vo_ref, *, nchunk):
    row = lax.broadcasted_iota(jnp.int32, (nchunk, LANES), 0)
    for br, src in enumerate((kc_ref, vc_ref)):
        pa = pb = None
        for l in range(CMP_STRIDE):
            r = src[0, pl.ds(l, nchunk, stride=CMP_STRIDE), :]
            sl = slice(l * LANES, (l + 1) * LANES)
            ta = _dot((r + pos_ref[br, 0:1, sl]).astype(BF16), w1a_ref[br, sl, :])
            tb = _dot((r + pos_ref[br, 1:2, sl]).astype(BF16), w1b_ref[br, sl, :])
            pa = ta if pa is None else pa + ta
            pb = tb if pb is None else pb + tb
        hid = _gelu_tanh(pa + pltpu.roll(pb, nchunk - 1, 0))
        out = jnp.where(row < nchunk - 1, _dot(hid.astype(BF16), w2_ref[br]), 0.0)
        if br == 0:
            ko_ref[0] = out.astype(ko_ref.dtype)
        else:
            vo_ref[0] = out.T.astype(vo_ref.dtype)


def _prep_compress_weights(cmp_pos, cmp_w1, cmp_w2):
    half = CMP_LEN // 2
    pos, w1a, w1b, w2 = [], [], [], []
    for br in range(2):
        p = cmp_pos[br]
        tile = lambda ph: jnp.concatenate([ph, ph], axis=1).reshape(1, half * LANES)
        pos.append(jnp.concatenate([tile(p[:half]), tile(p[half:])], axis=0))
        w = cmp_w1[br].reshape(CMP_LEN, NSA_HEAD_DIM, CMP_HIDDEN)
        z = jnp.zeros_like(w[:half])

        def big(wh):
            g0 = jnp.concatenate([wh, z], axis=1)
            g1 = jnp.concatenate([z, wh], axis=1)
            return jnp.concatenate([g0.reshape(half * LANES, CMP_HIDDEN), g1.reshape(half * LANES, CMP_HIDDEN)], axis=1)

        w1a.append(big(w[:half]))
        w1b.append(big(w[half:]))
        w2.append(_block_diag_dense(jnp.stack([cmp_w2[br], cmp_w2[br]])))
    return (jnp.stack(pos), jnp.stack(w1a).astype(BF16), jnp.stack(w1b).astype(BF16), jnp.stack(w2).astype(BF16))


def _compress(kc, vc, cmp_pos, cmp_w1, cmp_w2, batch, seq):
    nchunk = seq // CMP_STRIDE
    pos, w1a, w1b, w2 = _prep_compress_weights(cmp_pos, cmp_w1, cmp_w2)
    blk = pl.BlockSpec((1, seq, LANES), lambda b: (b, 0, 0))
    full3 = lambda shape: pl.BlockSpec(shape, lambda b: (0, 0, 0))
    return pl.pallas_call(
        functools.partial(_compress_kernel, nchunk=nchunk),
        grid=(batch,),
        in_specs=[blk, blk, full3(pos.shape), full3(w1a.shape), full3(w1b.shape), full3(w2.shape)],
        out_specs=[pl.BlockSpec((1, nchunk, LANES), lambda b: (b, 0, 0)),
                   pl.BlockSpec((1, LANES, nchunk), lambda b: (b, 0, 0))],
        out_shape=(jax.ShapeDtypeStruct((batch, nchunk, LANES), BF16),
                   jax.ShapeDtypeStruct((batch, LANES, nchunk), BF16)),
        compiler_params=_cparams(("parallel",)),
        name="nsa_compress",
    )(kc.reshape(batch, seq, LANES), vc.reshape(batch, seq, LANES), pos, w1a, w1b, w2)


def _select_kernel(q_ref, kc_ref, vct_ref, ovt_ref, gt_ref, oc_ref, sb_ref, ot_sc, imp_sc, *, tq, nchunk, variants):
    nsel = LANES
    qt = pl.program_id(1)
    col_ok = (qt * tq + lax.broadcasted_iota(jnp.int32, (1, tq), 1) >= CMP_LEN - 1).astype(F32)

    def attend(nc):
        kc = kc_ref[0, 0:nc, :]
        qpos = qt * tq + lax.broadcasted_iota(jnp.int32, (nc, tq), 1)
        cend = lax.broadcasted_iota(jnp.int32, (nc, tq), 0) * CMP_STRIDE + (CMP_LEN - 1)
        cvalid = cend <= qpos
        ovt = ovt_ref[:, 0:nc]
        for g in range(NSA_KV_HEADS):
            vct = vct_ref[0, g * HEAD_V:(g + 1) * HEAD_V, 0:nc]
            psum = jnp.zeros((nc, tq), F32)
            for pj in range(NSA_GROUP):
                hd = g * NSA_GROUP + pj
                q = q_ref[0, :, hd * LANES:(hd + 1) * LANES]
                s = jnp.where(cvalid, _dot_nt(kc, q), NEG_INF)
                m = jnp.max(s, axis=0, keepdims=True)
                e = jnp.exp2(s - m)
                p = e * (col_ok / jnp.maximum(jnp.sum(e, axis=0, keepdims=True), 1e-30))
                ot_sc[hd * HEAD_V:(hd + 1) * HEAD_V, :] = _dot(vct, p.astype(BF16)) * gt_ref[0, 3 * hd:3 * hd + 1, :]
                psum = psum + p
            hi = psum.astype(BF16)
            r1 = psum - hi.astype(F32)
            mid = r1.astype(BF16)
            lo = (r1 - mid.astype(F32)).astype(BF16)
            imp_sc[g] = _dot(ovt, hi) + _dot(ovt, mid) + _dot(ovt, lo)

    needed = (qt * tq + tq - CMP_STRIDE) // CMP_STRIDE
    lo_bound = 0
    for nc in variants:
        pl.when((needed > lo_bound) & (needed <= nc) if nc != variants[-1] else needed > lo_bound)(
            functools.partial(attend, nc))
        lo_bound = nc

    blk = lax.broadcasted_iota(jnp.int32, (nsel, tq), 0)
    blkf = blk.astype(F32)
    qp = qt * tq + lax.broadcasted_iota(jnp.int32, (nsel, tq), 1)
    valid = blk * SEL_LEN <= qp
    forced = (blk == 0) | (blk == qp // SEL_LEN)
    for g in range(NSA_KV_HEADS):
        score = jnp.where(valid, imp_sc[g], -1.0)
        score0 = jnp.where(forced, score + FORCE_BONUS, score)
        score = score0
        for _ in range(min(SEL_TOPK, nsel)):
            m = jnp.max(score, axis=0, keepdims=True)
            idx = jnp.min(jnp.where(score == m, blkf, float(nsel)), axis=0, keepdims=True)
            score = jnp.where(blkf == idx, -jnp.inf, score)
        bias = jnp.where(score0 >= 0.0, jnp.where(score == -jnp.inf, 0.0, NEG_INF), NEG_INF)
        sb_ref[0, g] = bias.T.astype(sb_ref.dtype)
    oc_ref[0] = ot_sc[...].T


def _overlap_matrix_t(nchunk, nsel):
    n = np.arange(nchunk)[None, :]
    s = np.arange(nsel)[:, None]
    cs = n * CMP_STRIDE
    ov = (cs < s * SEL_LEN + SEL_LEN) & (cs + CMP_LEN - 1 >= s * SEL_LEN) & (n < nchunk - 1)
    return jnp.asarray(ov.astype(np.float32)).astype(BF16)


def _select(qn, kcmp, vcmp_t, gates_t, batch, seq, tq=512):
    nchunk = seq // CMP_STRIDE
    nsel = LANES
    assert seq // SEL_LEN <= nsel
    qw = N_ATT_HEADS * LANES
    ow = N_ATT_HEADS * HEAD_V
    ovt = _overlap_matrix_t(nchunk, nsel)
    quarter = nchunk // 4
    variants = tuple(quarter * i for i in range(1, 5)) if quarter % LANES == 0 else (nchunk,)
    return pl.pallas_call(
        functools.partial(_select_kernel, tq=tq, nchunk=nchunk, variants=variants),
        grid=(batch, seq // tq),
        in_specs=[pl.BlockSpec((1, tq, qw), lambda b, t: (b, t, 0)),
                  pl.BlockSpec((1, nchunk, LANES), lambda b, t: (b, 0, 0)),
                  pl.BlockSpec((1, NSA_KV_HEADS * HEAD_V, nchunk), lambda b, t: (b, 0, 0)),
                  pl.BlockSpec((nsel, nchunk), lambda b, t: (0, 0)),
                  pl.BlockSpec((1, GATE_ROWS, tq), lambda b, t: (b, 0, t))],
        out_specs=[pl.BlockSpec((1, tq, ow), lambda b, t: (b, t, 0)),
                   pl.BlockSpec((1, NSA_KV_HEADS, tq, nsel), lambda b, t: (b, 0, t, 0))],
        out_shape=(jax.ShapeDtypeStruct((batch, seq, ow), F32),
                   jax.ShapeDtypeStruct((batch, NSA_KV_HEADS, seq, nsel), BF16)),
        scratch_shapes=[pltpu.VMEM((ow, tq), F32), pltpu.VMEM((NSA_KV_HEADS, nsel, tq), F32)],
        compiler_params=_cparams(("parallel", "parallel")),
        name="nsa_select",
    )(qn, kcmp, vcmp_t, ovt, gates_t)


def _out_proj_kernel(x_ref, a_ref, b_ref, oc_ref, os_ref, ow_ref, wa_ref, wb_ref, wc_ref, o_ref):
    acc = x_ref[...] + _dot(a_ref[...].astype(BF16), wa_ref[...])
    acc = acc + _dot(b_ref[...], wb_ref[...])
    c = (oc_ref[...] + os_ref[...]) + ow_ref[...]
    o_ref[...] = acc + _dot(c.astype(BF16), wc_ref[...])


def _out_proj(x2, a_out, b_out, o_c, o_s, o_w, w_out, tm=512):
    T = x2.shape[0]
    wa = w_out[0:256].astype(BF16)
    wb = w_out[256:640].astype(BF16)
    wc = w_out[640:1024].astype(BF16)
    row = lambda i: (i, 0)
    fixed = lambda i: (0, 0)
    act = lambda n: pl.BlockSpec((tm, n), row)
    return pl.pallas_call(
        _out_proj_kernel,
        grid=(T // tm,),
        in_specs=[act(D_MODEL), act(256), act(384), act(384), act(384), act(384),
                  pl.BlockSpec((256, D_MODEL), fixed), pl.BlockSpec((384, D_MODEL), fixed),
                  pl.BlockSpec((384, D_MODEL), fixed)],
        out_specs=pl.BlockSpec((tm, D_MODEL), row),
        out_shape=jax.ShapeDtypeStruct((T, D_MODEL), F32),
        compiler_params=_cparams(("parallel",)),
        name="out_proj",
    )(x2, a_out, b_out, o_c, o_s, o_w, wa, wb, wc)


def _final_norm(y, gf_ref):
    ms = jnp.mean(y * y, axis=-1, keepdims=True)
    return (y * lax.rsqrt(ms + EPS)) * gf_ref[...]


def _ffn_kernel(x_ref, g_ref, wg_ref, wu_ref, wd_ref, gf_ref, o_ref, h_sc, acc_sc, *, final_norm):
    f = pl.program_id(1)

    @pl.when(f == 0)
    def _():
        x = x_ref[...]
        ms = jnp.mean(x * x, axis=-1, keepdims=True)
        h_sc[...] = ((x * lax.rsqrt(ms + EPS)) * g_ref[...]).astype(BF16)
        acc_sc[...] = x

    h = h_sc[...]
    gte = _dot(h, wg_ref[...])
    up = _dot(h, wu_ref[...])
    act = (gte * _sigmoid(gte)) * up
    acc_sc[...] += _dot(act.astype(BF16), wd_ref[...])

    @pl.when(f == pl.num_programs(1) - 1)
    def _():
        y = acc_sc[...]
        o_ref[...] = _final_norm(y, gf_ref) if final_norm else y


def _ffn(x2, g, wg, wu, wd, g_final, final_norm, tm=512, tf=1408):
    T = x2.shape[0]
    nf = D_FF // tf
    return pl.pallas_call(
        functools.partial(_ffn_kernel, final_norm=final_norm),
        grid=(T // tm, nf),
        in_specs=[pl.BlockSpec((tm, D_MODEL), lambda i, f: (i, 0)), pl.BlockSpec((1, D_MODEL), lambda i, f: (0, 0)),
                  pl.BlockSpec((D_MODEL, tf), lambda i, f: (0, f)), pl.BlockSpec((D_MODEL, tf), lambda i, f: (0, f)),
                  pl.BlockSpec((tf, D_MODEL), lambda i, f: (f, 0)), pl.BlockSpec((1, D_MODEL), lambda i, f: (0, 0))],
        out_specs=pl.BlockSpec((tm, D_MODEL), lambda i, f: (i, 0)),
        out_shape=jax.ShapeDtypeStruct((T, D_MODEL), F32),
        scratch_shapes=[pltpu.VMEM((tm, D_MODEL), BF16), pltpu.VMEM((tm, D_MODEL), F32)],
        compiler_params=_cparams(("parallel", "arbitrary")),
        name="ffn_dense",
    )(x2, g.reshape(1, -1), wg.astype(BF16), wu.astype(BF16), wd.astype(BF16), g_final.reshape(1, -1))


MOE_TILE = 1024
MOE_CHUNK = 384
MOE_CHUNK_SMALL = 256
MOE_CHUNK_MID = 320


def _moe_kernel(x_ref, g_ref, rw_ref, tri_ref, wg_ref, wu_ref, wd_ref, gf_ref, o_ref,
                h_sc, gate_sc, key_sc, keyt_sc, acc_sc, *, final_norm, tm):
    e = pl.program_id(1)
    lane = lax.broadcasted_iota(jnp.int32, (tm, LANES), 1)

    @pl.when(e == 0)
    def _():
        x = x_ref[...]
        ms = jnp.mean(x * x, axis=-1, keepdims=True)
        hf = (x * lax.rsqrt(ms + EPS)) * g_ref[...]
        h_sc[...] = hf.astype(BF16)
        acc_sc[...] = x
        h_hi = hf.astype(BF16)
        h_lo = (hf - h_hi.astype(F32)).astype(BF16)
        logits = (_dot(h_hi, rw_ref[0]) + _dot(h_lo, rw_ref[0]) + _dot(h_hi, rw_ref[1]))
        logits = jnp.where(lane < N_EXPERTS, logits, -jnp.inf)
        m1 = jnp.max(logits, axis=1, keepdims=True)
        i1 = jnp.min(jnp.where(logits == m1, lane, LANES), axis=1, keepdims=True)
        rest = jnp.where(lane == i1, -jnp.inf, logits)
        m2 = jnp.max(rest, axis=1, keepdims=True)
        i2 = jnp.min(jnp.where(rest == m2, lane, LANES), axis=1, keepdims=True)
        e2 = jnp.exp(m2 - m1)
        den = 1.0 + e2
        gate_sc[...] = jnp.where(lane == i1, 1.0 / den, 0.0) + jnp.where(lane == i2, e2 / den, 0.0)
        routed = jnp.where((lane == i1) | (lane == i2), 1.0, 0.0)
        key = jnp.where(routed > 0.0, _dot(tri_ref[...], routed.astype(BF16)), -1.0)
        key_sc[...] = key
        keyt_sc[...] = key.T

    sel = lane == e
    kcol = jnp.max(jnp.where(sel, key_sc[...], -1.0), axis=1, keepdims=True)
    wcol = jnp.sum(jnp.where(sel, gate_sc[...], 0.0), axis=1, keepdims=True)
    krow = keyt_sc[pl.ds(e, 1), :]
    count = jnp.sum(jnp.where(kcol >= 0.0, 1.0, 0.0)).astype(jnp.int32)

    def chunk(rows, ci, carry):
        base = (ci * rows).astype(F32)
        row_id = lax.broadcasted_iota(jnp.int32, (rows, tm), 0).astype(F32)
        col_id = lax.broadcasted_iota(jnp.int32, (tm, rows), 1).astype(F32)
        gather = jnp.where(krow - base == row_id, 1.0, 0.0).astype(BF16)
        xg = _dot(gather, h_sc[...]).astype(BF16)
        gte = _dot(xg, wg_ref[0])
        up = _dot(xg, wu_ref[0])
        act = (gte * _sigmoid(gte)) * up
        y = _dot(act.astype(BF16), wd_ref[0])
        scatter = jnp.where(kcol - base == col_id, 1.0, 0.0).astype(BF16)
        acc_sc[...] += wcol * _dot(scatter, y.astype(BF16))
        return carry

    @pl.when((count > 0) & (count <= MOE_CHUNK_SMALL))
    def _():
        chunk(MOE_CHUNK_SMALL, jnp.int32(0), 0)

    @pl.when((count > MOE_CHUNK_SMALL) & (count <= MOE_CHUNK_MID))
    def _():
        chunk(MOE_CHUNK_MID, jnp.int32(0), 0)

    n_chunks = jnp.where(count > MOE_CHUNK_MID, (count + (MOE_CHUNK - 1)) // MOE_CHUNK, 0)
    lax.fori_loop(0, n_chunks, functools.partial(chunk, MOE_CHUNK), 0)

    @pl.when(e == pl.num_programs(1) - 1)
    def _():
        out = acc_sc[...]
        o_ref[...] = _final_norm(out, gf_ref) if final_norm else out


def _moe(x2, g, router_w, wg, wu, wd, g_final, final_norm, tm=MOE_TILE):
    T = x2.shape[0]
    rw = _pad_cols(router_w, LANES)
    rw_hi = rw.astype(BF16)
    rw_lo = (rw - rw_hi.astype(F32)).astype(BF16)
    rw2 = jnp.stack([rw_hi, rw_lo])
    tri = jnp.asarray(np.tril(np.ones((tm, tm), np.float32), -1)).astype(BF16)
    return pl.pallas_call(
        functools.partial(_moe_kernel, final_norm=final_norm, tm=tm),
        grid=(T // tm, N_EXPERTS),
        in_specs=[pl.BlockSpec((tm, D_MODEL), lambda i, e: (i, 0)), pl.BlockSpec((1, D_MODEL), lambda i, e: (0, 0)),
                  pl.BlockSpec((2, D_MODEL, LANES), lambda i, e: (0, 0, 0)),
                  pl.BlockSpec((tm, tm), lambda i, e: (0, 0)),
                  pl.BlockSpec((1, D_MODEL, D_FF_EXPERT), lambda i, e: (e, 0, 0)),
                  pl.BlockSpec((1, D_MODEL, D_FF_EXPERT), lambda i, e: (e, 0, 0)),
                  pl.BlockSpec((1, D_FF_EXPERT, D_MODEL), lambda i, e: (e, 0, 0)),
                  pl.BlockSpec((1, D_MODEL), lambda i, e: (0, 0))],
        out_specs=pl.BlockSpec((tm, D_MODEL), lambda i, e: (i, 0)),
        out_shape=jax.ShapeDtypeStruct((T, D_MODEL), F32),
        scratch_shapes=[pltpu.VMEM((tm, D_MODEL), BF16), pltpu.VMEM((tm, LANES), F32),
                        pltpu.VMEM((tm, LANES), F32), pltpu.VMEM((LANES, tm), F32),
                        pltpu.VMEM((tm, D_MODEL), F32)],
        compiler_params=_cparams(("parallel", "arbitrary")),
        name="moe_routed",
    )(x2, g.reshape(1, -1), rw2, tri, wg.astype(BF16), wu.astype(BF16), wd.astype(BF16), g_final.reshape(1, -1))


def _mixer(x2, batch, seq, tables, onehot, norm_g, w_in, conv_w, conv_b, wa, ba, wx, bx, lam,
           q_norm, w_uq, kv_norm, w_ukv, cmp_pos, cmp_w1, cmp_w2, gate_b, w_out):
    w_cat, w_t = _prep_w_in(w_in)
    lru_xy, q_m, k_m, qn, kc, vc, ksw, vt_sw, gates_t, vt_m = _in_proj(
        x2, norm_g, w_cat, w_t, tables, gate_b, q_norm, w_uq, kv_norm, w_ukv, batch, seq)
    a_out = _lru(lru_xy, conv_w, conv_b, wa, ba, wx, bx, lam, batch, seq)
    b3 = lambda t: t.reshape(batch, seq, t.shape[-1])
    heads = tuple(range(N_ATT_HEADS))
    b_out = _flash(b3(q_m), b3(k_m), vt_m, mode="causal", batch=batch, seq=seq, per_head_kv=True,
                   k_width=N_ATT_HEADS * LANES, k_col=0, vt_rows=N_ATT_HEADS * HEAD_V, vt_row=0, out_dtype=BF16,
                   tk=LONG_KEY_TILE)

    kcmp, vcmp_t = _compress(kc, vc, cmp_pos, cmp_w1, cmp_w2, batch, seq)
    qn3 = b3(qn)
    o_c, selbias = _select(qn3, kcmp, vcmp_t, gates_t, batch, seq)
    ksw3 = b3(ksw)
    nsa_kw = dict(batch=batch, seq=seq, per_head_kv=False, k_width=LANES,
                  vt_rows=NSA_KV_HEADS * HEAD_V, out_dtype=F32, gates_t=gates_t)
    o_s = _flash(qn3, ksw3, vt_sw, mode="select", k_col=0, vt_row=0, selbias=selbias,
                 onehot=onehot, gate_rows=tuple(3 * h + 1 for h in heads), tk=LONG_KEY_TILE, **nsa_kw)
    o_w = _window(qn3, ksw3, vt_sw, gates_t, batch=batch, seq=seq, k_col=1, vt_row=1,
                  gate_rows=tuple(3 * h + 2 for h in heads))
    T = batch * seq
    flat = lambda t: t.reshape(T, t.shape[-1])
    return _out_proj(x2, a_out, flat(b_out), flat(o_c), flat(o_s), flat(o_w), w_out)


def kernel(x, norm_mix, w_in, lru_conv_w, lru_conv_b, lru_wa, lru_ba, lru_wx, lru_bx, lru_lambda, mla_q_norm, mla_w_uq, mla_kv_norm, mla_w_ukv, nsa_cmp_pos, nsa_cmp_w1, nsa_cmp_w2, nsa_gate_b, w_out, norm_ffn, ffn_w_gate, ffn_w_up, ffn_w_down, router_w, moe_w_gate, moe_w_up, moe_w_down, norm_final):
    batch, seq, _ = x.shape
    depth = norm_mix.shape[0]
    tables = _rope_tables(seq)
    blk_id = np.arange(seq)[:, None] // SEL_LEN
    onehot = jnp.asarray((blk_id == np.arange(LANES)[None, :]).astype(np.float32)).astype(BF16)
    x2 = x.reshape(batch * seq, D_MODEL)
    for l in range(depth):
        x2 = _mixer(x2, batch, seq, tables, onehot, norm_mix[l], w_in[l], lru_conv_w[l], lru_conv_b[l],
                    lru_wa[l], lru_ba[l], lru_wx[l], lru_bx[l], lru_lambda[l], mla_q_norm[l], mla_w_uq[l],
                    mla_kv_norm[l], mla_w_ukv[l], nsa_cmp_pos[l], nsa_cmp_w1[l], nsa_cmp_w2[l],
                    nsa_gate_b[l], w_out[l])
        last = l == depth - 1
        j = l // 2
        if l % 2 == 0:
            x2 = _ffn(x2, norm_ffn[l], ffn_w_gate[j], ffn_w_up[j], ffn_w_down[j], norm_final, last)
        else:
            x2 = _moe(x2, norm_ffn[l], router_w[j], moe_w_gate[j], moe_w_up[j], moe_w_down[j], norm_final, last)
    if depth == 0:
        raise ValueError("depth must be positive")
    return x2.reshape(batch, seq, D_MODEL)
```

```python
import functools
import math

import numpy as np
import jax
import jax.numpy as jnp
from jax import lax
from jax.experimental import pallas as pl
from jax.experimental.pallas import tpu as pltpu

F32 = jnp.float32
BF16 = jnp.bfloat16

D_MODEL = 1024
LRU_WIDTH = 256
CONV_WIDTH = 4
LRU_C = 8.0
MLA_HEADS = 6
MLA_NOPE = 64
MLA_ROPE = 32
MLA_Q_RANK = 192
NSA_HEADS = 6
NSA_KV_HEADS = 2
NSA_GROUP = NSA_HEADS // NSA_KV_HEADS
NSA_HEAD_DIM = 64
CMP_LEN = 32
CMP_STRIDE = 16
CMP_HIDDEN = 128
SEL_LEN = 64
SEL_TOPK = 16
WINDOW = 512
FORCE_BONUS = 1.0e3
D_FF = 2816
N_EXPERTS = 8
D_FF_EXPERT = 1408
ROPE_THETA = 10000.0
EPS = 1e-6
NEG_INF = -1.0e30

LANES = 128
VMEM_LIMIT = 56 * 1024 * 1024

C_LRU = 0
C_MLA = 512
C_KPE_ROT = 1024
C_QN = 1152
C_KC = 1536
C_VC = 1664
C_KS = 1792
C_KW = 1920
IN_COLS_PAD = 2048
N_ATT_HEADS = 6
HEAD_V = 64
LONG_KEY_TILE = 1024
SUM_ROWS = 16
GATE_ROWS = 32
LOG2E = 1.4426950408889634


def _cparams(sem):
    return pltpu.CompilerParams(dimension_semantics=sem, vmem_limit_bytes=VMEM_LIMIT)


def _gelu_tanh(x):
    return 0.5 * x * (1.0 + jnp.tanh(math.sqrt(2.0 / math.pi) * (x + 0.044715 * (x * x * x))))


def _sigmoid(x):
    return 1.0 / (1.0 + jnp.exp(-x))


def _dot(a, b):
    return jnp.dot(a, b, preferred_element_type=F32)


def _dot_nt(a, b):
    return lax.dot_general(a, b, (((1,), (1,)), ((), ())), preferred_element_type=F32)


def _rot_cols(w, half):
    return jnp.concatenate([-w[:, half:], w[:, :half]], axis=1)


def _pad_cols(w, n):
    return jnp.pad(w, ((0, 0), (0, n - w.shape[1])))


def _prep_w_in(w_in):
    splits = np.cumsum([256, 256, 192, 128, 32, 384, 128, 128, 128, 128, 128, 128, 18])[:-1].tolist()
    (x_l, y_l, q_dn, kv_dn, k_pe, q_n, kc, vc, ks, vs, kw, vw, g_n) = jnp.split(w_in, splits, axis=1)
    z = lambda n: jnp.zeros((w_in.shape[0], n), w_in.dtype)

    def kpe_slot(w):
        return jnp.concatenate([z(64), w, z(32)], axis=1)

    def qn_slots(fn):
        out = []
        for p in range(NSA_GROUP):
            for h in (p, p + NSA_GROUP):
                out.append(fn(q_n[:, h * 64:(h + 1) * 64]))
        return jnp.concatenate(out, axis=1)

    cols = [x_l, y_l, _pad_cols(q_dn, 256), kv_dn, kpe_slot(k_pe), kpe_slot(_rot_cols(k_pe, 16)),
            qn_slots(lambda w: w), kc, vc, ks, kw]
    w = jnp.concatenate(cols, axis=1)
    assert w.shape[1] == IN_COLS_PAD
    w_t = jnp.concatenate([vs, vw, _pad_cols(g_n, GATE_ROWS)], axis=1).T
    return w.astype(BF16), w_t.astype(BF16)


def _rope_tables(seq):
    pos = jnp.arange(seq, dtype=F32)[:, None]
    inv64 = ROPE_THETA ** (-jnp.arange(32, dtype=F32) * 2.0 / 64)
    inv32 = ROPE_THETA ** (-jnp.arange(16, dtype=F32) * 2.0 / 32)
    a64 = pos * inv64[None, :]
    a32 = pos * inv32[None, :]
    c64, s64 = jnp.cos(a64), jnp.sin(a64)
    c32, s32 = jnp.cos(a32), jnp.sin(a32)
    cos_n = jnp.concatenate([c64, c64, c64, c64], axis=1)
    sin_n = jnp.concatenate([-s64, s64, -s64, s64], axis=1)
    one = jnp.ones((seq, 64), F32)
    cos_m = jnp.concatenate([one, c32, c32, one[:, :32]], axis=1)
    sin_m = jnp.concatenate([0 * one, s32, s32, 0 * one[:, :32]], axis=1)
    return cos_n, sin_n, cos_m, sin_m


def _in_proj_kernel(x_ref, g_ref, w_ref, wt_ref, cn_ref, sn_ref, cm_ref, sm_ref, gb_ref,
                    qnorm_ref, wqa_ref, wqb_ref, kvnorm_ref, wk_ref, wvt_ref,
                    lru_ref, qm_ref, km_ref, qn_ref, kc_ref, vc_ref, ksw_ref, vt_ref, gt_ref, vtm_ref):
    x = x_ref[...]
    ms = jnp.mean(x * x, axis=-1, keepdims=True)
    h = ((x * lax.rsqrt(ms + EPS)) * g_ref[...]).astype(BF16)

    def proj(c0, n):
        return _dot(h, w_ref[:, c0:c0 + n])

    cn, sn = cn_ref[...], sn_ref[...]
    lru_ref[...] = proj(C_LRU, 512)
    cm, sm = cm_ref[...], sm_ref[...]
    kpe = proj(C_MLA + 384, 128) * cm + proj(C_KPE_ROT, 128) * sm
    _mla_heads(proj(C_MLA, 256), proj(C_MLA + 256, 128), kpe, cm, sm, qnorm_ref, wqa_ref, wqb_ref, kvnorm_ref,
               wk_ref, wvt_ref, qm_ref, km_ref, vtm_ref)
    scale = NSA_HEAD_DIM ** -0.5 * LOG2E
    lane = lax.broadcasted_iota(jnp.int32, cn.shape, 1)
    low_half = lane < NSA_HEAD_DIM
    first_half = (lane % NSA_HEAD_DIM) < NSA_HEAD_DIM // 2

    def rope_n(c0):
        a = proj(c0, LANES)
        swapped = jnp.where(first_half, pltpu.roll(a, LANES - NSA_HEAD_DIM // 2, 1), pltpu.roll(a, NSA_HEAD_DIM // 2, 1))
        return a * cn + swapped * sn

    for p in range(NSA_GROUP):
        r = rope_n(C_QN + p * LANES) * scale
        hi_hd = p + NSA_GROUP
        qn_ref[:, p * LANES:(p + 1) * LANES] = jnp.where(low_half, r, 0.0).astype(BF16)
        qn_ref[:, hi_hd * LANES:(hi_hd + 1) * LANES] = jnp.where(low_half, 0.0, r).astype(BF16)
    kc_ref[...] = rope_n(C_KC)
    vc_ref[...] = proj(C_VC, 128)
    ksw_ref[:, 0:128] = rope_n(C_KS).astype(BF16)
    ksw_ref[:, 128:256] = rope_n(C_KW).astype(BF16)
    vt_ref[0] = _dot_nt(wt_ref[0:256, :], h).astype(BF16)
    gt_ref[0] = _sigmoid(_dot_nt(wt_ref[256:256 + GATE_ROWS, :], h) + gb_ref[...])


def _in_proj(x2, g, w_cat, w_t, tables, gate_b, q_norm, w_uq, kv_norm, w_ukv, batch, seq, tm=512):
    T = x2.shape[0]
    nt = T // tm
    npos = seq // tm
    cn, sn, cm, sm = tables
    wqa, wqb, wk, wvt = _prep_mla_weights(w_uq, w_ukv)
    row = lambda i: (i, 0)
    fixed = lambda i: (0, 0)
    posmap = lambda i: (i % npos, 0)
    tab_spec = pl.BlockSpec((tm, LANES), posmap)
    whole = lambda a: pl.BlockSpec(a.shape, fixed)
    out_shapes = (
        jax.ShapeDtypeStruct((T, 512), F32),
        jax.ShapeDtypeStruct((T, 768), BF16),
        jax.ShapeDtypeStruct((T, 768), BF16),
        jax.ShapeDtypeStruct((T, 768), BF16),
        jax.ShapeDtypeStruct((T, 128), F32),
        jax.ShapeDtypeStruct((T, 128), F32),
        jax.ShapeDtypeStruct((T, 256), BF16),
    )
    t_shapes = (jax.ShapeDtypeStruct((batch, 256, seq), BF16),
                jax.ShapeDtypeStruct((batch, GATE_ROWS, seq), F32),
                jax.ShapeDtypeStruct((batch, 384, seq), BF16))
    tmap = lambda i: (i // npos, 0, i % npos)
    gate_b_col = jnp.pad(gate_b, (0, GATE_ROWS - gate_b.shape[0])).reshape(GATE_ROWS, 1)
    q_norm_p = _pad_cols(q_norm.reshape(1, -1), 256)
    kv_norm_r = kv_norm.reshape(1, -1)
    return pl.pallas_call(
        _in_proj_kernel,
        grid=(nt,),
        in_specs=[pl.BlockSpec((tm, D_MODEL), row), pl.BlockSpec((1, D_MODEL), fixed),
                  pl.BlockSpec((D_MODEL, IN_COLS_PAD), fixed), pl.BlockSpec((256 + GATE_ROWS, D_MODEL), fixed),
                  tab_spec, tab_spec, tab_spec, tab_spec, pl.BlockSpec((GATE_ROWS, 1), fixed),
                  whole(q_norm_p), whole(wqa), whole(wqb), whole(kv_norm_r), whole(wk), whole(wvt)],
        out_specs=[pl.BlockSpec((tm, s.shape[1]), row) for s in out_shapes]
        + [pl.BlockSpec((1, s.shape[1], tm), tmap) for s in t_shapes],
        out_shape=out_shapes + t_shapes,
        compiler_params=_cparams(("parallel",)),
        name="in_proj",
    )(x2, g.reshape(1, -1), w_cat, w_t, cn, sn, cm, sm, gate_b_col, q_norm_p, wqa, wqb, kv_norm_r, wk, wvt)


def _lru_kernel(xy_ref, cw_ref, cb_ref, wa_ref, ba_ref, wx_ref, bx_ref, lam_ref, out_ref,
                xe_sc, h_sc, *, tt):
    t = pl.program_id(1)

    @pl.when(t == 0)
    def _():
        xe_sc[0:8, :] = jnp.zeros((8, LRU_WIDTH), F32)
        h_sc[...] = jnp.zeros_like(h_sc)

    x = xy_ref[0, :, 0:LRU_WIDTH]
    y = xy_ref[0, :, LRU_WIDTH:2 * LRU_WIDTH]
    xe_sc[8:8 + tt, :] = x
    xc = cb_ref[...]
    for k in range(CONV_WIDTH):
        off = 8 - (CONV_WIDTH - 1) + k
        xc = xc + xe_sc[off:off + tt, :] * cw_ref[k:k + 1, :]
    xe_sc[0:8, :] = x[tt - 8:tt, :]

    xb = xc.astype(BF16)
    r = _sigmoid(_dot(xb, wa_ref[...]) + ba_ref[...])
    i = _sigmoid(_dot(xb, wx_ref[...]) + bx_ref[...])
    log_a = (-LRU_C * r) * jax.nn.softplus(-lam_ref[...])
    a = jnp.exp(log_a)
    b = jnp.sqrt(jnp.tanh(-log_a) * (a * a + 1.0)) * (i * xc)

    row = lax.broadcasted_iota(jnp.int32, (tt, LRU_WIDTH), 0)
    k = 1
    while k < tt:
        keep = row >= k
        a_sh = jnp.where(keep, pltpu.roll(a, k, 0), 1.0)
        b_sh = jnp.where(keep, pltpu.roll(b, k, 0), 0.0)
        b = a * b_sh + b
        a = a * a_sh
        k *= 2
    h = b + a * h_sc[0:1, :]
    h_sc[0:1, :] = h[tt - 1:tt, :]
    out_ref[0] = h * _gelu_tanh(y)


def _block_diag_dense(w):
    n, c, d = w.shape
    z = jnp.zeros((c, d), w.dtype)
    return jnp.concatenate([jnp.concatenate([w[j] if i == j else z for i in range(n)], axis=1) for j in range(n)],
                           axis=0)


def _lru(lru_xy, conv_w, conv_b, wa, ba, wx, bx, lam, batch, seq, tt=512):
    xy = lru_xy.reshape(batch, seq, 512)
    fixed = lambda b, t: (0, 0)
    vec = pl.BlockSpec((1, LRU_WIDTH), fixed)
    mat = pl.BlockSpec((LRU_WIDTH, LRU_WIDTH), fixed)
    out = pl.pallas_call(
        functools.partial(_lru_kernel, tt=tt),
        grid=(batch, seq // tt),
        in_specs=[pl.BlockSpec((1, tt, 512), lambda b, t: (b, t, 0)),
                  pl.BlockSpec((CONV_WIDTH, LRU_WIDTH), fixed), vec, mat, vec, mat, vec, vec],
        out_specs=pl.BlockSpec((1, tt, LRU_WIDTH), lambda b, t: (b, t, 0)),
        out_shape=jax.ShapeDtypeStruct((batch, seq, LRU_WIDTH), F32),
        scratch_shapes=[pltpu.VMEM((tt + 8, LRU_WIDTH), F32), pltpu.VMEM((8, LRU_WIDTH), F32)],
        compiler_params=_cparams(("parallel", "arbitrary")),
        name="rg_lru",
    )(xy, conv_w, conv_b.reshape(1, -1), _block_diag_dense(wa).astype(BF16), ba.reshape(1, -1),
      _block_diag_dense(wx).astype(BF16), bx.reshape(1, -1), lam.reshape(1, -1))
    return out.reshape(batch * seq, LRU_WIDTH)


def _mla_heads(q_dn, kv_dn, kpe, cm, sm, qn_ref, wqa_ref, wqb_ref, kvn_ref, wk_ref, wvt_ref, q_ref, k_ref, vt_ref):
    ms = jnp.sum(q_dn * q_dn, axis=-1, keepdims=True) * (1.0 / MLA_Q_RANK)
    ql = ((q_dn * lax.rsqrt(ms + EPS)) * qn_ref[...]).astype(BF16)
    ms = jnp.mean(kv_dn * kv_dn, axis=-1, keepdims=True)
    kvl = ((kv_dn * lax.rsqrt(ms + EPS)) * kvn_ref[...]).astype(BF16)
    scale = (MLA_NOPE + MLA_ROPE) ** -0.5 * LOG2E
    for hd in range(MLA_HEADS):
        sl = slice(hd * LANES, (hd + 1) * LANES)
        q_ref[:, sl] = ((_dot(ql, wqa_ref[:, sl]) * cm + _dot(ql, wqb_ref[:, sl]) * sm) * scale).astype(BF16)
        k_ref[:, sl] = (_dot(kvl, wk_ref[:, sl]) + kpe).astype(BF16)
    vt_ref[0] = _dot_nt(wvt_ref[...], kvl).astype(BF16)


def _prep_mla_weights(w_uq, w_ukv):
    zq = jnp.zeros((MLA_Q_RANK, 32), w_uq.dtype)
    qa, qb, kk, vv = [], [], [], []
    for h in range(MLA_HEADS):
        wq = w_uq[:, h * 96:(h + 1) * 96]
        nope, ropew = wq[:, :64], wq[:, 64:]
        qa.append(jnp.concatenate([nope, ropew, zq], axis=1))
        qb.append(jnp.concatenate([0 * nope, _rot_cols(ropew, 16), zq], axis=1))
        wkv = w_ukv[:, h * 128:(h + 1) * 128]
        kk.append(_pad_cols(wkv[:, :64], LANES))
        vv.append(wkv[:, 64:])
    pad_rows = lambda w: jnp.pad(w, ((0, 256 - MLA_Q_RANK), (0, 0)))
    return (pad_rows(jnp.concatenate(qa, axis=1)).astype(BF16), pad_rows(jnp.concatenate(qb, axis=1)).astype(BF16),
            jnp.concatenate(kk, axis=1).astype(BF16), jnp.concatenate(vv, axis=1).T.astype(BF16))


PROBE_KEYS = 128
LAG_LIMIT = 8.0


def _flash_kernel(qt_ref, kt_ref, first_ref, last_ref, par_ref, *refs, mode, k_per_head, v_per_head, gate_rows,
                  tq, tk):
    refs = list(refs)
    q_ref, k_ref, vt_ref = refs[:3]
    o_ref, m_st, acc_st = refs[-8:-5]
    s_sc, mx_sc, flag_sm = refs[-5:-3], refs[-3:-1], refs[-1]
    extra = refs[3:-8]
    sb_ref, oh_ref = (extra[0], extra[1]) if mode == "select" else (None, None)
    gt_ref = extra[-1] if gate_rows is not None else None
    step = pl.program_id(1)
    qt = qt_ref[step]
    kt = kt_ref[step]
    rd = par_ref[step]
    wr = 1 - rd

    def visible(nk):
        kpos = kt * tk + lax.broadcasted_iota(jnp.int32, (nk, tq), 0)
        qpos = qt * tq + lax.broadcasted_iota(jnp.int32, (nk, tq), 1)
        return kpos <= qpos

    def score_tile(j, nk, valid):
        q = q_ref[0, :, j * LANES:(j + 1) * LANES]
        k = k_ref[0, 0:nk, j * LANES:(j + 1) * LANES] if k_per_head else k_ref[0, 0:nk, :]
        if mode == "select":
            q = jnp.concatenate([q, sb_ref[0, j // NSA_GROUP]], axis=1)
            k = jnp.concatenate([k, oh_ref[0:nk, :]], axis=1)
        s = _dot_nt(k, q)
        return s if valid is None else jnp.where(valid, s, NEG_INF)

    def weighted_values(j, nk, p):
        row = (j if v_per_head else j // NSA_GROUP) * HEAD_V
        vt = vt_ref[0, row:row + HEAD_V, 0:nk]
        return _dot(jnp.concatenate([vt, jnp.ones((SUM_ROWS, nk), BF16)], axis=0), p)

    def exact(nk, init):
        valid = visible(nk)

        def scores(j, slot):
            s = score_tile(j, nk, valid)
            s_sc[slot][0:nk, :] = s
            mx_sc[slot][...] = jnp.max(s, axis=0, keepdims=True)

        def accumulate(j, slot):
            if init:
                m_new = mx_sc[slot][...]
            else:
                m_prev = m_st[rd, j]
                m_new = jnp.maximum(m_prev, mx_sc[slot][...])
            p = jnp.exp2((s_sc[slot][0:nk, :] - m_new).astype(BF16))
            upd = weighted_values(j, nk, p)
            acc_st[wr, j] = upd if init else jnp.exp2(m_prev - m_new) * acc_st[rd, j] + upd
            m_st[wr, j] = m_new

        scores(0, 0)
        for j in range(N_ATT_HEADS):
            if j + 1 < N_ATT_HEADS:
                scores(j + 1, (j + 1) % 2)
            accumulate(j, j % 2)

    def lagged(nk, masked, init):
        valid = visible(nk) if masked else None
        rise = None
        for j in range(N_ATT_HEADS):
            if init:
                probe = score_tile(j, PROBE_KEYS, None if valid is None else valid[0:PROBE_KEYS])
                m_prev = jnp.max(probe, axis=0, keepdims=True)
            else:
                m_prev = m_st[rd, j]
            s = score_tile(j, nk, valid)
            p = jnp.exp2((s - m_prev).astype(BF16))
            tile_max = jnp.max(s, axis=0, keepdims=True)
            m_new = jnp.maximum(m_prev, tile_max)
            upd = weighted_values(j, nk, p)
            acc_st[wr, j] = (upd if init else acc_st[rd, j] + upd) * jnp.exp2(m_prev - m_new)
            m_st[wr, j] = m_new
            d = tile_max - m_prev
            rise = d if rise is None else jnp.maximum(rise, d)
        flag_sm[0] = (jnp.max(rise) > LAG_LIMIT).astype(jnp.int32)

    first = first_ref[step] == 1
    later = jnp.logical_not(first)
    first_q, last_q = qt * tq, qt * tq + (tq - 1)
    crosses = kt * tk + (tk - 1) > first_q
    clear = jnp.logical_not(crosses)
    flag_sm[0] = 0
    for init, when in ((True, first), (False, later)):
        pl.when(when & clear)(functools.partial(lagged, tk, False, init))
        if tk > tq:
            half_only = kt * tk + tk // 2 > last_q
            pl.when(when & crosses & half_only)(functools.partial(lagged, tk // 2, True, init))
            pl.when(when & crosses & jnp.logical_not(half_only))(functools.partial(lagged, tk, True, init))
        else:
            pl.when(when & crosses)(functools.partial(lagged, tk, True, init))
    redo = flag_sm[0] == 1
    pl.when(redo & first)(lambda: exact(tk, True))
    pl.when(redo & later)(lambda: exact(tk, False))

    @pl.when(last_ref[step] == 1)
    def _():
        outs = []
        for j in range(N_ATT_HEADS):
            o = acc_st[wr, j, 0:HEAD_V, :] / acc_st[wr, j, HEAD_V:HEAD_V + 1, :]
            if gate_rows is not None:
                o = o * gt_ref[0, gate_rows[j]:gate_rows[j] + 1, :]
            outs.append(o)
        o_ref[0] = jnp.concatenate(outs, axis=0).T.astype(o_ref.dtype)


def _pair_tables(nq, tq, tk):
    qts, kts, first, last, par = [], [], [], [], []
    for qt in range(nq):
        hi = (qt * tq + tq - 1) // tk
        for kt in range(hi + 1):
            qts.append(qt)
            kts.append(kt)
            first.append(1 if kt == 0 else 0)
            last.append(1 if kt == hi else 0)
            par.append(kt % 2)
    arr = lambda v: jnp.asarray(np.array(v, dtype=np.int32))
    return arr(qts), arr(kts), arr(first), arr(last), arr(par)


def _flash(q, k, vt, *, mode, batch, seq, per_head_kv, k_width, k_col, vt_rows, vt_row, out_dtype,
           selbias=None, onehot=None, gate_rows=None, gates_t=None, tq=512, tk=512):
    tk = min(tk, seq)
    nq = seq // tq
    tables = _pair_tables(nq, tq, tk)
    n_steps = int(tables[0].shape[0])
    qw = N_ATT_HEADS * LANES
    ow = N_ATT_HEADS * HEAD_V
    in_specs = [
        pl.BlockSpec((1, tq, qw), lambda b, s, qt, kt, f, l, p: (b, qt[s], 0)),
        pl.BlockSpec((1, tk, k_width), lambda b, s, qt, kt, f, l, p: (b, kt[s], k_col)),
        pl.BlockSpec((1, vt_rows, tk), lambda b, s, qt, kt, f, l, p: (b, vt_row, kt[s])),
    ]
    args = [q, k, vt]
    if mode == "select":
        in_specs.append(pl.BlockSpec((1, NSA_KV_HEADS, tq, LANES), lambda b, s, qt, kt, f, l, p: (b, 0, qt[s], 0)))
        in_specs.append(pl.BlockSpec((tk, LANES), lambda b, s, qt, kt, f, l, p: (kt[s], 0)))
        args += [selbias, onehot]
    if gate_rows is not None:
        in_specs.append(pl.BlockSpec((1, GATE_ROWS, tq), lambda b, s, qt, kt, f, l, p: (b, 0, qt[s])))
        args.append(gates_t)
    kern = functools.partial(_flash_kernel, mode=mode, k_per_head=per_head_kv, v_per_head=per_head_kv,
                             gate_rows=gate_rows, tq=tq, tk=tk)
    return pl.pallas_call(
        kern,
        grid_spec=pltpu.PrefetchScalarGridSpec(
            num_scalar_prefetch=5,
            grid=(batch, n_steps),
            in_specs=in_specs,
            out_specs=pl.BlockSpec((1, tq, ow), lambda b, s, qt, kt, f, l, p: (b, qt[s], 0)),
            scratch_shapes=[pltpu.VMEM((2, N_ATT_HEADS, 1, tq), F32),
                            pltpu.VMEM((2, N_ATT_HEADS, HEAD_V + SUM_ROWS, tq), F32),
                            pltpu.VMEM((tk, tq), F32), pltpu.VMEM((tk, tq), F32),
                            pltpu.VMEM((1, tq), F32), pltpu.VMEM((1, tq), F32),
                            pltpu.SMEM((1,), jnp.int32)],
        ),
        out_shape=jax.ShapeDtypeStruct((batch, seq, ow), out_dtype),
        compiler_params=_cparams(("parallel", "arbitrary")),
        name="flash_" + mode,
    )(*tables, *args)


def _window_kernel(q_ref, kp_ref, kc_ref, vp_ref, vc_ref, gt_ref, o_ref, s_a, s_b, mx_a, mx_b, *, tq, gate_rows):
    qt = pl.program_id(1)
    krow = lax.broadcasted_iota(jnp.int32, (tq, tq), 0)
    qcol = lax.broadcasted_iota(jnp.int32, (tq, tq), 1)
    valid_prev = (qt > 0) & (qcol - krow < WINDOW - tq)
    valid_own = krow <= qcol
    ones = jnp.ones((SUM_ROWS, tq), BF16)
    slots, maxes = (s_a, s_b), (mx_a, mx_b)

    def scores(j):
        q = q_ref[0, :, j * LANES:(j + 1) * LANES]
        s_prev = jnp.where(valid_prev, _dot_nt(kp_ref[0], q), NEG_INF)
        s_own = jnp.where(valid_own, _dot_nt(kc_ref[0], q), NEG_INF)
        slots[j % 2][0:tq, :] = s_prev
        slots[j % 2][tq:2 * tq, :] = s_own
        maxes[j % 2][...] = jnp.maximum(jnp.max(s_prev, axis=0, keepdims=True), jnp.max(s_own, axis=0, keepdims=True))

    def output(j):
        m = maxes[j % 2][...]
        g = j // NSA_GROUP
        acc = None
        for half, v_ref in enumerate((vp_ref, vc_ref)):
            p = jnp.exp2((slots[j % 2][half * tq:(half + 1) * tq, :] - m).astype(BF16))
            vt = v_ref[0, g * HEAD_V:(g + 1) * HEAD_V, :]
            part = _dot(jnp.concatenate([vt, ones], axis=0), p)
            acc = part if acc is None else acc + part
        return (acc[0:HEAD_V] / acc[HEAD_V:HEAD_V + 1]) * gt_ref[0, gate_rows[j]:gate_rows[j] + 1, :]

    outs = []
    scores(0)
    for j in range(N_ATT_HEADS):
        if j + 1 < N_ATT_HEADS:
            scores(j + 1)
        outs.append(output(j))
    o_ref[0] = jnp.concatenate(outs, axis=0).T.astype(o_ref.dtype)


def _window(q, k, vt, gates_t, *, batch, seq, k_col, vt_row, gate_rows, tq=512):
    assert tq == WINDOW
    qw = N_ATT_HEADS * LANES
    ow = N_ATT_HEADS * HEAD_V
    prev = lambda t: jnp.maximum(t - 1, 0)
    vrows = NSA_KV_HEADS * HEAD_V
    return pl.pallas_call(
        functools.partial(_window_kernel, tq=tq, gate_rows=gate_rows),
        grid=(batch, seq // tq),
        in_specs=[pl.BlockSpec((1, tq, qw), lambda b, t: (b, t, 0)),
                  pl.BlockSpec((1, tq, LANES), lambda b, t: (b, prev(t), k_col)),
                  pl.BlockSpec((1, tq, LANES), lambda b, t: (b, t, k_col)),
                  pl.BlockSpec((1, vrows, tq), lambda b, t: (b, vt_row, prev(t))),
                  pl.BlockSpec((1, vrows, tq), lambda b, t: (b, vt_row, t)),
                  pl.BlockSpec((1, GATE_ROWS, tq), lambda b, t: (b, 0, t))],
        out_specs=pl.BlockSpec((1, tq, ow), lambda b, t: (b, t, 0)),
        out_shape=jax.ShapeDtypeStruct((batch, seq, ow), F32),
        scratch_shapes=[pltpu.VMEM((2 * tq, tq), F32), pltpu.VMEM((2 * tq, tq), F32),
                        pltpu.VMEM((1, tq), F32), pltpu.VMEM((1, tq), F32)],
        compiler_params=_cparams(("parallel", "parallel")),
        name="nsa_window",
    )(q, k, k, vt, vt, gates_t)


def _compress_kernel(kc_ref, vc_ref, pos_ref, w1a_ref, w1b_ref, w2_ref, ko_ref, vo_ref, *, nchunk):
    row = lax.broadcasted_iota(jnp.int32, (nchunk, LANES), 0)
    for br, src in enumerate((kc_ref, vc_ref)):
        pa = pb = None
        for l in range(CMP_STRIDE):
            r = src[0, pl.ds(l, nchunk, stride=CMP_STRIDE), :]
            sl = slice(l * LANES, (l + 1) * LANES)
            ta = _dot((r + pos_ref[br, 0:1, sl]).astype(BF16), w1a_ref[br, sl, :])
            tb = _dot((r + pos_ref[br, 1:2, sl]).astype(BF16), w1b_ref[br, sl, :])
            pa = ta if pa is None else pa + ta
            pb = tb if pb is None else pb + tb
        hid = _gelu_tanh(pa + pltpu.roll(pb, nchunk - 1, 0))
        out = jnp.where(row < nchunk - 1, _dot(hid.astype(BF16), w2_ref[br]), 0.0)
        if br == 0:
            ko_ref[0] = out.astype(ko_ref.dtype)
        else:
            vo_ref[0] = out.T.astype(vo_ref.dtype)


def _prep_compress_weights(cmp_pos, cmp_w1, cmp_w2):
    half = CMP_LEN // 2
    pos, w1a, w1b, w2 = [], [], [], []
    for br in range(2):
        p = cmp_pos[br]
        tile = lambda ph: jnp.concatenate([ph, ph], axis=1).reshape(1, half * LANES)
        pos.append(jnp.concatenate([tile(p[:half]), tile(p[half:])], axis=0))
        w = cmp_w1[br].reshape(CMP_LEN, NSA_HEAD_DIM, CMP_HIDDEN)
        z = jnp.zeros_like(w[:half])

        def big(wh):
            g0 = jnp.concatenate([wh, z], axis=1)
            g1 = jnp.concatenate([z, wh], axis=1)
            return jnp.concatenate([g0.reshape(half * LANES, CMP_HIDDEN), g1.reshape(half * LANES, CMP_HIDDEN)], axis=1)

        w1a.append(big(w[:half]))
        w1b.append(big(w[half:]))
        w2.append(_block_diag_dense(jnp.stack([cmp_w2[br], cmp_w2[br]])))
    return (jnp.stack(pos), jnp.stack(w1a).astype(BF16), jnp.stack(w1b).astype(BF16), jnp.stack(w2).astype(BF16))


def _compress(kc, vc, cmp_pos, cmp_w1, cmp_w2, batch, seq):
    nchunk = seq // CMP_STRIDE
    pos, w1a, w1b, w2 = _prep_compress_weights(cmp_pos, cmp_w1, cmp_w2)
    blk = pl.BlockSpec((1, seq, LANES), lambda b: (b, 0, 0))
    full3 = lambda shape: pl.BlockSpec(shape, lambda b: (0, 0, 0))
    return pl.pallas_call(
        functools.partial(_compress_kernel, nchunk=nchunk),
        grid=(batch,),
        in_specs=[blk, blk, full3(pos.shape), full3(w1a.shape), full3(w1b.shape), full3(w2.shape)],
        out_specs=[pl.BlockSpec((1, nchunk, LANES), lambda b: (b, 0, 0)),
                   pl.BlockSpec((1, LANES, nchunk), lambda b: (b, 0, 0))],
        out_shape=(jax.ShapeDtypeStruct((batch, nchunk, LANES), BF16),
                   jax.ShapeDtypeStruct((batch, LANES, nchunk), BF16)),
        compiler_params=_cparams(("parallel",)),
        name="nsa_compress",
    )(kc.reshape(batch, seq, LANES), vc.reshape(batch, seq, LANES), pos, w1a, w1b, w2)


def _select_kernel(q_ref, kc_ref, vct_ref, ovt_ref, gt_ref, oc_ref, sb_ref, ot_sc, imp_sc, *, tq, nchunk, variants):
    nsel = LANES
    qt = pl.program_id(1)
    col_ok = (qt * tq + lax.broadcasted_iota(jnp.int32, (1, tq), 1) >= CMP_LEN - 1).astype(F32)

    def attend(nc):
        kc = kc_ref[0, 0:nc, :]
        qpos = qt * tq + lax.broadcasted_iota(jnp.int32, (nc, tq), 1)
        cend = lax.broadcasted_iota(jnp.int32, (nc, tq), 0) * CMP_STRIDE + (CMP_LEN - 1)
        cvalid = cend <= qpos
        ovt = ovt_ref[:, 0:nc]
        for g in range(NSA_KV_HEADS):
            vct = vct_ref[0, g * HEAD_V:(g + 1) * HEAD_V, 0:nc]
            psum = jnp.zeros((nc, tq), F32)
            for pj in range(NSA_GROUP):
                hd = g * NSA_GROUP + pj
                q = q_ref[0, :, hd * LANES:(hd + 1) * LANES]
                s = jnp.where(cvalid, _dot_nt(kc, q), NEG_INF)
                m = jnp.max(s, axis=0, keepdims=True)
                e = jnp.exp2(s - m)
                p = e * (col_ok / jnp.maximum(jnp.sum(e, axis=0, keepdims=True), 1e-30))
                ot_sc[hd * HEAD_V:(hd + 1) * HEAD_V, :] = _dot(vct, p.astype(BF16)) * gt_ref[0, 3 * hd:3 * hd + 1, :]
                psum = psum + p
            hi = psum.astype(BF16)
            r1 = psum - hi.astype(F32)
            mid = r1.astype(BF16)
            lo = (r1 - mid.astype(F32)).astype(BF16)
            imp_sc[g] = _dot(ovt, hi) + _dot(ovt, mid) + _dot(ovt, lo)

    needed = (qt * tq + tq - CMP_STRIDE) // CMP_STRIDE
    lo_bound = 0
    for nc in variants:
        pl.when((needed > lo_bound) & (needed <= nc) if nc != variants[-1] else needed > lo_bound)(
            functools.partial(attend, nc))
        lo_bound = nc

    blk = lax.broadcasted_iota(jnp.int32, (nsel, tq), 0)
    blkf = blk.astype(F32)
    qp = qt * tq + lax.broadcasted_iota(jnp.int32, (nsel, tq), 1)
    valid = blk * SEL_LEN <= qp
    forced = (blk == 0) | (blk == qp // SEL_LEN)
    for g in range(NSA_KV_HEADS):
        score = jnp.where(valid, imp_sc[g], -1.0)
        score0 = jnp.where(forced, score + FORCE_BONUS, score)
        score = score0
        for _ in range(min(SEL_TOPK, nsel)):
            m = jnp.max(score, axis=0, keepdims=True)
            idx = jnp.min(jnp.where(score == m, blkf, float(nsel)), axis=0, keepdims=True)
            score = jnp.where(blkf == idx, -jnp.inf, score)
        bias = jnp.where(score0 >= 0.0, jnp.where(score == -jnp.inf, 0.0, NEG_INF), NEG_INF)
        sb_ref[0, g] = bias.T.astype(sb_ref.dtype)
    oc_ref[0] = ot_sc[...].T


def _overlap_matrix_t(nchunk, nsel):
    n = np.arange(nchunk)[None, :]
    s = np.arange(nsel)[:, None]
    cs = n * CMP_STRIDE
    ov = (cs < s * SEL_LEN + SEL_LEN) & (cs + CMP_LEN - 1 >= s * SEL_LEN) & (n < nchunk - 1)
    return jnp.asarray(ov.astype(np.float32)).astype(BF16)


def _select(qn, kcmp, vcmp_t, gates_t, batch, seq, tq=1024):
    nchunk = seq // CMP_STRIDE
    nsel = LANES
    assert seq // SEL_LEN <= nsel
    qw = N_ATT_HEADS * LANES
    ow = N_ATT_HEADS * HEAD_V
    ovt = _overlap_matrix_t(nchunk, nsel)
    quarter = nchunk // 4
    variants = tuple(quarter * i for i in range(1, 5)) if quarter % LANES == 0 else (nchunk,)
    return pl.pallas_call(
        functools.partial(_select_kernel, tq=tq, nchunk=nchunk, variants=variants),
        grid=(batch, seq // tq),
        in_specs=[pl.BlockSpec((1, tq, qw), lambda b, t: (b, t, 0)),
                  pl.BlockSpec((1, nchunk, LANES), lambda b, t: (b, 0, 0)),
                  pl.BlockSpec((1, NSA_KV_HEADS * HEAD_V, nchunk), lambda b, t: (b, 0, 0)),
                  pl.BlockSpec((nsel, nchunk), lambda b, t: (0, 0)),
                  pl.BlockSpec((1, GATE_ROWS, tq), lambda b, t: (b, 0, t))],
        out_specs=[pl.BlockSpec((1, tq, ow), lambda b, t: (b, t, 0)),
                   pl.BlockSpec((1, NSA_KV_HEADS, tq, nsel), lambda b, t: (b, 0, t, 0))],
        out_shape=(jax.ShapeDtypeStruct((batch, seq, ow), F32),
                   jax.ShapeDtypeStruct((batch, NSA_KV_HEADS, seq, nsel), BF16)),
        scratch_shapes=[pltpu.VMEM((ow, tq), F32), pltpu.VMEM((NSA_KV_HEADS, nsel, tq), F32)],
        compiler_params=_cparams(("parallel", "parallel")),
        name="nsa_select",
    )(qn, kcmp, vcmp_t, ovt, gates_t)


def _out_proj_kernel(x_ref, a_ref, b_ref, oc_ref, os_ref, ow_ref, wa_ref, wb_ref, wc_ref, o_ref):
    acc = x_ref[...] + _dot(a_ref[...].astype(BF16), wa_ref[...])
    acc = acc + _dot(b_ref[...], wb_ref[...])
    c = (oc_ref[...] + os_ref[...]) + ow_ref[...]
    o_ref[...] = acc + _dot(c.astype(BF16), wc_ref[...])


def _out_proj(x2, a_out, b_out, o_c, o_s, o_w, w_out, tm=512):
    T = x2.shape[0]
    wa = w_out[0:256].astype(BF16)
    wb = w_out[256:640].astype(BF16)
    wc = w_out[640:1024].astype(BF16)
    row = lambda i: (i, 0)
    fixed = lambda i: (0, 0)
    act = lambda n: pl.BlockSpec((tm, n), row)
    return pl.pallas_call(
        _out_proj_kernel,
        grid=(T // tm,),
        in_specs=[act(D_MODEL), act(256), act(384), act(384), act(384), act(384),
                  pl.BlockSpec((256, D_MODEL), fixed), pl.BlockSpec((384, D_MODEL), fixed),
                  pl.BlockSpec((384, D_MODEL), fixed)],
        out_specs=pl.BlockSpec((tm, D_MODEL), row),
        out_shape=jax.ShapeDtypeStruct((T, D_MODEL), F32),
        compiler_params=_cparams(("parallel",)),
        name="out_proj",
    )(x2, a_out, b_out, o_c, o_s, o_w, wa, wb, wc)


def _final_norm(y, gf_ref):
    ms = jnp.mean(y * y, axis=-1, keepdims=True)
    return (y * lax.rsqrt(ms + EPS)) * gf_ref[...]


def _ffn_kernel(x_ref, g_ref, wg_ref, wu_ref, wd_ref, gf_ref, o_ref, h_sc, acc_sc, *, final_norm):
    f = pl.program_id(1)

    @pl.when(f == 0)
    def _():
        x = x_ref[...]
        ms = jnp.mean(x * x, axis=-1, keepdims=True)
        h_sc[...] = ((x * lax.rsqrt(ms + EPS)) * g_ref[...]).astype(BF16)
        acc_sc[...] = x

    h = h_sc[...]
    gte = _dot(h, wg_ref[...])
    up = _dot(h, wu_ref[...])
    act = (gte * _sigmoid(gte)) * up
    acc_sc[...] += _dot(act.astype(BF16), wd_ref[...])

    @pl.when(f == pl.num_programs(1) - 1)
    def _():
        y = acc_sc[...]
        o_ref[...] = _final_norm(y, gf_ref) if final_norm else y


def _ffn(x2, g, wg, wu, wd, g_final, final_norm, tm=512, tf=1408):
    T = x2.shape[0]
    nf = D_FF // tf
    return pl.pallas_call(
        functools.partial(_ffn_kernel, final_norm=final_norm),
        grid=(T // tm, nf),
        in_specs=[pl.BlockSpec((tm, D_MODEL), lambda i, f: (i, 0)), pl.BlockSpec((1, D_MODEL), lambda i, f: (0, 0)),
                  pl.BlockSpec((D_MODEL, tf), lambda i, f: (0, f)), pl.BlockSpec((D_MODEL, tf), lambda i, f: (0, f)),
                  pl.BlockSpec((tf, D_MODEL), lambda i, f: (f, 0)), pl.BlockSpec((1, D_MODEL), lambda i, f: (0, 0))],
        out_specs=pl.BlockSpec((tm, D_MODEL), lambda i, f: (i, 0)),
        out_shape=jax.ShapeDtypeStruct((T, D_MODEL), F32),
        scratch_shapes=[pltpu.VMEM((tm, D_MODEL), BF16), pltpu.VMEM((tm, D_MODEL), F32)],
        compiler_params=_cparams(("parallel", "arbitrary")),
        name="ffn_dense",
    )(x2, g.reshape(1, -1), wg.astype(BF16), wu.astype(BF16), wd.astype(BF16), g_final.reshape(1, -1))


MOE_TILE = 1024
MOE_CHUNK = 384
MOE_CHUNK_SMALL = 256
MOE_CHUNK_MID = 320


def _moe_kernel(x_ref, g_ref, rw_ref, tri_ref, wg_ref, wu_ref, wd_ref, gf_ref, o_ref,
                h_sc, gate_sc, key_sc, keyt_sc, acc_sc, *, final_norm, tm):
    e = pl.program_id(1)
    lane = lax.broadcasted_iota(jnp.int32, (tm, LANES), 1)

    @pl.when(e == 0)
    def _():
        x = x_ref[...]
        ms = jnp.mean(x * x, axis=-1, keepdims=True)
        hf = (x * lax.rsqrt(ms + EPS)) * g_ref[...]
        h_sc[...] = hf.astype(BF16)
        acc_sc[...] = x
        h_hi = hf.astype(BF16)
        h_lo = (hf - h_hi.astype(F32)).astype(BF16)
        logits = (_dot(h_hi, rw_ref[0]) + _dot(h_lo, rw_ref[0]) + _dot(h_hi, rw_ref[1]))
        logits = jnp.where(lane < N_EXPERTS, logits, -jnp.inf)
        m1 = jnp.max(logits, axis=1, keepdims=True)
        i1 = jnp.min(jnp.where(logits == m1, lane, LANES), axis=1, keepdims=True)
        rest = jnp.where(lane == i1, -jnp.inf, logits)
        m2 = jnp.max(rest, axis=1, keepdims=True)
        i2 = jnp.min(jnp.where(rest == m2, lane, LANES), axis=1, keepdims=True)
        e2 = jnp.exp(m2 - m1)
        den = 1.0 + e2
        gate_sc[...] = jnp.where(lane == i1, 1.0 / den, 0.0) + jnp.where(lane == i2, e2 / den, 0.0)
        routed = jnp.where((lane == i1) | (lane == i2), 1.0, 0.0)
        key = jnp.where(routed > 0.0, _dot(tri_ref[...], routed.astype(BF16)), -1.0)
        key_sc[...] = key
        keyt_sc[...] = key.T

    sel = lane == e
    kcol = jnp.max(jnp.where(sel, key_sc[...], -1.0), axis=1, keepdims=True)
    wcol = jnp.sum(jnp.where(sel, gate_sc[...], 0.0), axis=1, keepdims=True)
    krow = keyt_sc[pl.ds(e, 1), :]
    count = jnp.sum(jnp.where(kcol >= 0.0, 1.0, 0.0)).astype(jnp.int32)

    def chunk(rows, ci, carry):
        base = (ci * rows).astype(F32)
        row_id = lax.broadcasted_iota(jnp.int32, (rows, tm), 0).astype(F32)
        col_id = lax.broadcasted_iota(jnp.int32, (tm, rows), 1).astype(F32)
        gather = jnp.where(krow - base == row_id, 1.0, 0.0).astype(BF16)
        xg = _dot(gather, h_sc[...]).astype(BF16)
        gte = _dot(xg, wg_ref[0])
        up = _dot(xg, wu_ref[0])
        act = (gte * _sigmoid(gte)) * up
        y = _dot(act.astype(BF16), wd_ref[0])
        scatter = jnp.where(kcol - base == col_id, 1.0, 0.0).astype(BF16)
        acc_sc[...] += wcol * _dot(scatter, y.astype(BF16))
        return carry

    @pl.when((count > 0) & (count <= MOE_CHUNK_SMALL))
    def _():
        chunk(MOE_CHUNK_SMALL, jnp.int32(0), 0)

    @pl.when((count > MOE_CHUNK_SMALL) & (count <= MOE_CHUNK_MID))
    def _():
        chunk(MOE_CHUNK_MID, jnp.int32(0), 0)

    n_chunks = jnp.where(count > MOE_CHUNK_MID, (count + (MOE_CHUNK - 1)) // MOE_CHUNK, 0)
    lax.fori_loop(0, n_chunks, functools.partial(chunk, MOE_CHUNK), 0)

    @pl.when(e == pl.num_programs(1) - 1)
    def _():
        out = acc_sc[...]
        o_ref[...] = _final_norm(out, gf_ref) if final_norm else out


def _moe(x2, g, router_w, wg, wu, wd, g_final, final_norm, tm=MOE_TILE):
    T = x2.shape[0]
    rw = _pad_cols(router_w, LANES)
    rw_hi = rw.astype(BF16)
    rw_lo = (rw - rw_hi.astype(F32)).astype(BF16)
    rw2 = jnp.stack([rw_hi, rw_lo])
    tri = jnp.asarray(np.tril(np.ones((tm, tm), np.float32), -1)).astype(BF16)
    return pl.pallas_call(
        functools.partial(_moe_kernel, final_norm=final_norm, tm=tm),
        grid=(T // tm, N_EXPERTS),
        in_specs=[pl.BlockSpec((tm, D_MODEL), lambda i, e: (i, 0)), pl.BlockSpec((1, D_MODEL), lambda i, e: (0, 0)),
                  pl.BlockSpec((2, D_MODEL, LANES), lambda i, e: (0, 0, 0)),
                  pl.BlockSpec((tm, tm), lambda i, e: (0, 0)),
                  pl.BlockSpec((1, D_MODEL, D_FF_EXPERT), lambda i, e: (e, 0, 0)),
                  pl.BlockSpec((1, D_MODEL, D_FF_EXPERT), lambda i, e: (e, 0, 0)),
                  pl.BlockSpec((1, D_FF_EXPERT, D_MODEL), lambda i, e: (e, 0, 0)),
                  pl.BlockSpec((1, D_MODEL), lambda i, e: (0, 0))],
        out_specs=pl.BlockSpec((tm, D_MODEL), lambda i, e: (i, 0)),
        out_shape=jax.ShapeDtypeStruct((T, D_MODEL), F32),
        scratch_shapes=[pltpu.VMEM((tm, D_MODEL), BF16), pltpu.VMEM((tm, LANES), F32),
                        pltpu.VMEM((tm, LANES), F32), pltpu.VMEM((LANES, tm), F32),
                        pltpu.VMEM((tm, D_MODEL), F32)],
        compiler_params=_cparams(("parallel", "arbitrary")),
        name="moe_routed",
    )(x2, g.reshape(1, -1), rw2, tri, wg.astype(BF16), wu.astype(BF16), wd.astype(BF16), g_final.reshape(1, -1))


def _mixer(x2, batch, seq, tables, onehot, norm_g, w_in, conv_w, conv_b, wa, ba, wx, bx, lam,
           q_norm, w_uq, kv_norm, w_ukv, cmp_pos, cmp_w1, cmp_w2, gate_b, w_out):
    w_cat, w_t = _prep_w_in(w_in)
    lru_xy, q_m, k_m, qn, kc, vc, ksw, vt_sw, gates_t, vt_m = _in_proj(
        x2, norm_g, w_cat, w_t, tables, gate_b, q_norm, w_uq, kv_norm, w_ukv, batch, seq)
    a_out = _lru(lru_xy, conv_w, conv_b, wa, ba, wx, bx, lam, batch, seq)
    b3 = lambda t: t.reshape(batch, seq, t.shape[-1])
    heads = tuple(range(N_ATT_HEADS))
    b_out = _flash(b3(q_m), b3(k_m), vt_m, mode="causal", batch=batch, seq=seq, per_head_kv=True,
                   k_width=N_ATT_HEADS * LANES, k_col=0, vt_rows=N_ATT_HEADS * HEAD_V, vt_row=0, out_dtype=BF16,
                   tk=LONG_KEY_TILE)

    kcmp, vcmp_t = _compress(kc, vc, cmp_pos, cmp_w1, cmp_w2, batch, seq)
    qn3 = b3(qn)
    o_c, selbias = _select(qn3, kcmp, vcmp_t, gates_t, batch, seq)
    ksw3 = b3(ksw)
    nsa_kw = dict(batch=batch, seq=seq, per_head_kv=False, k_width=LANES,
                  vt_rows=NSA_KV_HEADS * HEAD_V, out_dtype=F32, gates_t=gates_t)
    o_s = _flash(qn3, ksw3, vt_sw, mode="select", k_col=0, vt_row=0, selbias=selbias,
                 onehot=onehot, gate_rows=tuple(3 * h + 1 for h in heads), tk=LONG_KEY_TILE, **nsa_kw)
    o_w = _window(qn3, ksw3, vt_sw, gates_t, batch=batch, seq=seq, k_col=1, vt_row=1,
                  gate_rows=tuple(3 * h + 2 for h in heads))
    T = batch * seq
    flat = lambda t: t.reshape(T, t.shape[-1])
    return _out_proj(x2, a_out, flat(b_out), flat(o_c), flat(o_s), flat(o_w), w_out)


def kernel(x, norm_mix, w_in, lru_conv_w, lru_conv_b, lru_wa, lru_ba, lru_wx, lru_bx, lru_lambda, mla_q_norm, mla_w_uq, mla_kv_norm, mla_w_ukv, nsa_cmp_pos, nsa_cmp_w1, nsa_cmp_w2, nsa_gate_b, w_out, norm_ffn, ffn_w_gate, ffn_w_up, ffn_w_down, router_w, moe_w_gate, moe_w_up, moe_w_down, norm_final):
    batch, seq, _ = x.shape
    depth = norm_mix.shape[0]
    tables = _rope_tables(seq)
    blk_id = np.arange(seq)[:, None] // SEL_LEN
    onehot = jnp.asarray((blk_id == np.arange(LANES)[None, :]).astype(np.float32)).astype(BF16)
    x2 = x.reshape(batch * seq, D_MODEL)
    for l in range(depth):
        x2 = _mixer(x2, batch, seq, tables, onehot, norm_mix[l], w_in[l], lru_conv_w[l], lru_conv_b[l],
                    lru_wa[l], lru_ba[l], lru_wx[l], lru_bx[l], lru_lambda[l], mla_q_norm[l], mla_w_uq[l],
                    mla_kv_norm[l], mla_w_ukv[l], nsa_cmp_pos[l], nsa_cmp_w1[l], nsa_cmp_w2[l],
                    nsa_gate_b[l], w_out[l])
        last = l == depth - 1
        j = l // 2
        if l % 2 == 0:
            x2 = _ffn(x2, norm_ffn[l], ffn_w_gate[j], ffn_w_up[j], ffn_w_down[j], norm_final, last)
        else:
            x2 = _moe(x2, norm_ffn[l], router_w[j], moe_w_gate[j], moe_w_up[j], moe_w_down[j], norm_final, last)
    if depth == 0:
        raise ValueError("depth must be positive")
    return x2.reshape(batch, seq, D_MODEL)
```
